```python
import math, functools
import jax, jax.numpy as jnp
from jax import lax
import numpy as np

D_MODEL = 1024
BATCH = 8
SEQ = 2048
DEPTH = 2
DEC_BATCH = 128
DEC_SEQ = 1
PAST_LEN = 2048
PAGE_SIZE = 128

N_META = 16
N_EVEN = (DEPTH + 1) // 2
N_ODD = DEPTH // 2
D_SSM = D_MODEL // 2
SSM_GROUP = 16
N_SSM_GROUPS = D_SSM // SSM_GROUP
SSM_STATE = 64
N_HEADS = 8
HEAD_DIM = 64
D_ATT = N_HEADS * HEAD_DIM
N_IDX_HEADS = 8
IDX_DIM = 64
TOPK_MAX = 256
QBLOCK = 64
ROPE_THETA = 10000.0
POOL_WINDOWS = (2, 4, 8, 16)
POOL_GROUP_DIM = D_MODEL // len(POOL_WINDOWS)
POOL_HIST = max(POOL_WINDOWS) - 1
D_FF = 2816
N_EXPERTS = 8
TOP_K_EXPERTS = 2
D_FF_EXPERT = 3584
EPS = 1e-6
SPLITS = (D_SSM, D_SSM + D_ATT, D_SSM + 2 * D_ATT, D_SSM + 3 * D_ATT,
          D_SSM + 3 * D_ATT + N_IDX_HEADS * IDX_DIM,
          D_SSM + 3 * D_ATT + N_IDX_HEADS * IDX_DIM + IDX_DIM)
D_IN0 = SPLITS[-1] + N_IDX_HEADS

kernel_name = "hybrid_s5_dsa_pool_moe_step"


def rmsnorm(x, g):
    x32 = x.astype(jnp.float32)
    y = x32 * lax.rsqrt(jnp.mean(x32 * x32, axis=-1, keepdims=True) + EPS)
    return (y * g.astype(jnp.float32)).astype(x.dtype)


def rope(x, pos):
    half = x.shape[-1] // 2
    inv = ROPE_THETA ** (-jnp.arange(half, dtype=jnp.float32) / half)
    ang = pos[:, None] * inv[None, :]
    cos = jnp.cos(ang)[None, :, None, :]
    sin = jnp.sin(ang)[None, :, None, :]
    x32 = x.astype(jnp.float32)
    x1, x2 = x32[..., :half], x32[..., half:]
    return jnp.concatenate([x1 * cos - x2 * sin, x2 * cos + x1 * sin], axis=-1).astype(x.dtype)


def gather_rows(rows, idx):
    return jax.vmap(lambda r, i: r[i])(rows, idx)


def swiglu(h, w_gate, w_up, w_down):
    return (jax.nn.silu(h @ w_gate) * (h @ w_up)) @ w_down


def mixer_inputs(h, w_in, q_norm, k_norm, pos):
    B, L, _ = h.shape
    z = h @ w_in
    u, q, k, v, qi, ki, wi = jnp.split(z, SPLITS, axis=-1)
    q = rope(rmsnorm(q.reshape(B, L, N_HEADS, HEAD_DIM), q_norm), pos)
    k = rope(rmsnorm(k.reshape(B, L, N_HEADS, HEAD_DIM), k_norm), pos)
    v = v.reshape(B, L, N_HEADS, HEAD_DIM)
    qi = rope(qi.reshape(B, L, N_IDX_HEADS, IDX_DIM), pos)
    ki = rope(ki[:, :, None, :], pos)[:, :, 0]
    return u, q, k, v, qi, ki, wi


def index_scores(qi, ki, wi):
    dots = jnp.einsum('bqhd,bsd->bqhs', qi, ki, preferred_element_type=jnp.float32) * (IDX_DIM ** -0.5)
    w = wi.astype(jnp.float32) * (N_IDX_HEADS ** -0.5)
    return jnp.einsum('bqh,bqhs->bqs', w, jax.nn.relu(dots))


def sparse_attend(q, kg, vg, valid):
    logits = jnp.einsum('bqhd,bqkhd->bhqk', q, kg, preferred_element_type=jnp.float32) * (HEAD_DIM ** -0.5)
    logits = jnp.where(valid[:, None], logits, -jnp.inf)
    p = jax.nn.softmax(logits, axis=-1)
    out = jnp.einsum('bhqk,bqkhd->bqhd', p.astype(vg.dtype), vg, preferred_element_type=jnp.float32)
    return out.astype(q.dtype)


def dsa_prompt(q, k, v, qi, ki, wi, topk):
    B, L = q.shape[:2]
    nblk = -(-L // QBLOCK)
    Lp = nblk * QBLOCK

    def blocks(a):
        a = jnp.pad(a, [(0, 0), (0, Lp - L)] + [(0, 0)] * (a.ndim - 2))
        return a.reshape((B, nblk, QBLOCK) + a.shape[2:]).swapaxes(0, 1)

    qpos = jnp.arange(Lp, dtype=jnp.int32).reshape(nblk, QBLOCK)
    kpos = jnp.arange(L, dtype=jnp.int32)

    def one_block(args):
        qb, qib, wib, pb = args
        s = index_scores(qib, ki, wib)
        s = jnp.where(kpos[None, None, :] <= pb[None, :, None], s, -jnp.inf)
        _, idx = lax.top_k(s, topk)
        valid = idx <= pb[None, :, None]
        return sparse_attend(qb, gather_rows(k, idx), gather_rows(v, idx), valid)

    out = lax.map(one_block, (blocks(q), blocks(qi), blocks(wi), qpos))
    return out.swapaxes(0, 1).reshape(B, Lp, N_HEADS, HEAD_DIM)[:, :L]


def dsa_sample(q, k, v, qi, ki, wi, cache_k, cache_v, cache_kidx, page_table, topk):
    DB, S = q.shape[:2]
    page_size = cache_k.shape[1]
    past = page_table.shape[1] * page_size
    ki_past = cache_kidx[page_table].reshape(DB, past, IDX_DIM)
    ki_all = jnp.concatenate([ki_past.astype(ki.dtype), ki], axis=1)
    s = index_scores(qi, ki_all, wi)
    qpos = past + jnp.arange(S, dtype=jnp.int32)
    kpos = jnp.arange(past + S, dtype=jnp.int32)
    s = jnp.where(kpos[None, None, :] <= qpos[None, :, None], s, -jnp.inf)
    _, idx = lax.top_k(s, topk)
    valid = idx <= qpos[None, :, None]
    in_past = (idx < past)[..., None, None]
    jp = jnp.minimum(idx, past - 1)
    row = gather_rows(page_table, jp // page_size) * page_size + jp % page_size
    jn = jnp.clip(idx - past, 0, S - 1)
    flat_k = cache_k.reshape(-1, N_HEADS, HEAD_DIM)
    flat_v = cache_v.reshape(-1, N_HEADS, HEAD_DIM)
    kg = jnp.where(in_past, flat_k[row].astype(k.dtype), gather_rows(k, jn))
    vg = jnp.where(in_past, flat_v[row].astype(v.dtype), gather_rows(v, jn))
    return sparse_attend(q, kg, vg, valid)


def s5_mixer(u, h0_re, h0_im, lam_re, lam_im, log_dt, b_re, b_im, c_re, c_im, d_skip, w_glu):
    f32 = jnp.float32
    B, L, _ = u.shape
    ug = u.astype(f32).reshape(B, L, N_SSM_GROUPS, SSM_GROUP)
    lam = lax.complex(lam_re.astype(f32), lam_im.astype(f32))
    dt = jnp.exp(log_dt.astype(f32))[:, None]
    lam_bar = jnp.exp(lam * dt)
    b_bar = ((lam_bar - 1.0) / lam)[..., None] * lax.complex(b_re.astype(f32), b_im.astype(f32))
    bu = jnp.einsum('blgc,gpc->blgp', ug.astype(jnp.complex64), b_bar)
    a = jnp.broadcast_to(lam_bar, bu.shape)

    def combine(e1, e2):
        a1, b1 = e1
        a2, b2 = e2
        return a1 * a2, a2 * b1 + b2

    a_cum, h = lax.associative_scan(combine, (a, bu), axis=1)
    h = h + a_cum * lax.complex(h0_re.astype(f32), h0_im.astype(f32))[:, None]
    c_mat = lax.complex(c_re.astype(f32), c_im.astype(f32))
    y = jnp.real(jnp.einsum('blgp,gcp->blgc', h, c_mat)) + d_skip.astype(f32).reshape(N_SSM_GROUPS, SSM_GROUP) * ug
    y = jax.nn.gelu(y.reshape(B, L, D_SSM))
    y = y * jax.nn.sigmoid(y @ w_glu.astype(f32))
    h_last = h[:, -1]
    return y.astype(u.dtype), jnp.real(h_last), jnp.imag(h_last)


def pool_mixer(u, hist, pos0, pool_w, pool_scale):
    f32 = jnp.float32
    B, S, _ = u.shape
    n_hist = hist.shape[1]
    ext = jnp.concatenate([hist.astype(u.dtype), u], axis=1)
    T = ext.shape[1]
    ext32 = ext.astype(f32)
    c = jnp.cumsum(ext32, axis=1)
    pos = pos0 + jnp.arange(S, dtype=f32)
    pooled = []
    for g, w in enumerate(POOL_WINDOWS):
        sl = slice(g * POOL_GROUP_DIM, (g + 1) * POOL_GROUP_DIM)
        cg = c[..., sl]
        lagged = jnp.pad(cg, ((0, 0), (w, 0), (0, 0)))[:, :T]
        mean = (cg - lagged)[:, n_hist:] / jnp.minimum(float(w), pos + 1.0)[None, :, None]
        pooled.append(mean - ext32[:, n_hist:, sl])
    pooled = jnp.stack(pooled, axis=2)
    mixed = jnp.einsum('bsgc,gcd->bsgd', pooled, pool_w.astype(f32)).reshape(B, S, D_MODEL)
    return (mixed * pool_scale.astype(f32)).astype(u.dtype), ext[:, T - POOL_HIST:]


def moe(h, router_w, w_gate, w_up, w_down):
    B, S, D = h.shape
    t = h.reshape(B * S, D)
    logits = (t @ router_w).astype(jnp.float32)
    top_v, top_i = lax.top_k(logits, TOP_K_EXPERTS)
    gates = jax.nn.softmax(top_v, axis=-1)
    dense_gate = jnp.sum(jax.nn.one_hot(top_i, N_EXPERTS, dtype=jnp.float32) * gates[..., None], axis=1)
    out = jnp.zeros((B * S, D), jnp.float32)
    for e in range(N_EXPERTS):
        out = out + dense_gate[:, e:e + 1] * swiglu(t, w_gate[e], w_up[e], w_down[e]).astype(jnp.float32)
    return out.astype(h.dtype).reshape(B, S, D)


def even_layer(x, pos, h0_re, h0_im, attend, norm_mix, w_in, q_norm, k_norm, lam_re, lam_im, log_dt,
               b_re, b_im, c_re, c_im, d_skip, w_glu, w_out, norm_ffn, w_gate, w_up, w_down):
    B, L, _ = x.shape
    h = rmsnorm(x, norm_mix)
    u, q, k, v, qi, ki, wi = mixer_inputs(h, w_in, q_norm, k_norm, pos)
    y_ssm, h_re, h_im = s5_mixer(u, h0_re, h0_im, lam_re, lam_im, log_dt, b_re, b_im, c_re, c_im, d_skip, w_glu)
    y_att = attend(q, k, v, qi, ki, wi)
    x = x + jnp.concatenate([y_ssm, y_att.reshape(B, L, D_ATT)], axis=-1) @ w_out
    x = x + swiglu(rmsnorm(x, norm_ffn), w_gate, w_up, w_down)
    return x, k, v, ki, h_re, h_im


def odd_layer(x, hist, pos0, norm_mix, pool_w, pool_scale, norm_ffn, router_w, w_gate, w_up, w_down):
    y, new_hist = pool_mixer(rmsnorm(x, norm_mix), hist, pos0, pool_w, pool_scale)
    x = x + y
    x = x + moe(rmsnorm(x, norm_ffn), router_w, w_gate, w_up, w_down)
    return x, new_hist


def setup_inputs(seed: int = 0) -> dict:
    key = jax.random.key(seed)
    ks = iter(jax.random.split(key, 48))
    f32 = jnp.float32

    def nrm(shape, scale=1.0):
        return jax.random.normal(next(ks), shape, f32) * scale

    n_pages = PAST_LEN // PAGE_SIZE
    n_used = DEC_BATCH * n_pages
    n_pool = (n_used * 5) // 4
    x_prompt = nrm((BATCH, SEQ, D_MODEL))
    x_sample = nrm((DEC_BATCH, DEC_SEQ, D_MODEL))
    cache_k = nrm((N_EVEN, n_pool, PAGE_SIZE, N_HEADS, HEAD_DIM))
    cache_v = nrm((N_EVEN, n_pool, PAGE_SIZE, N_HEADS, HEAD_DIM))
    cache_kidx = nrm((N_EVEN, n_pool, PAGE_SIZE, IDX_DIM))
    page_table = jax.random.permutation(next(ks), n_pool)[:n_used].reshape(DEC_BATCH, n_pages).astype(jnp.int32)
    state_ssm_re = nrm((N_EVEN, DEC_BATCH, N_SSM_GROUPS, SSM_STATE), 0.1)
    state_ssm_im = nrm((N_EVEN, DEC_BATCH, N_SSM_GROUPS, SSM_STATE), 0.1)
    state_pool = nrm((N_ODD, DEC_BATCH, POOL_HIST, D_MODEL))
    meta_tokens = nrm((N_META, D_MODEL))
    norm_mix0 = 1.0 + nrm((N_EVEN, D_MODEL), 0.02)
    w_in0 = nrm((N_EVEN, D_MODEL, D_IN0), D_MODEL ** -0.5)
    q_norm = 1.0 + nrm((N_EVEN, HEAD_DIM), 0.02)
    k_norm = 1.0 + nrm((N_EVEN, HEAD_DIM), 0.02)
    ssm_lambda_re = -0.5 + nrm((N_EVEN, N_SSM_GROUPS, SSM_STATE), 0.01)
    ssm_lambda_im = math.pi * jnp.arange(SSM_STATE, dtype=f32) + nrm((N_EVEN, N_SSM_GROUPS, SSM_STATE), 0.01)
    ssm_log_dt = jax.random.uniform(next(ks), (N_EVEN, N_SSM_GROUPS), f32, math.log(0.001), math.log(0.1))
    ssm_b_re = nrm((N_EVEN, N_SSM_GROUPS, SSM_STATE, SSM_GROUP), (2 * SSM_GROUP) ** -0.5)
    ssm_b_im = nrm((N_EVEN, N_SSM_GROUPS, SSM_STATE, SSM_GROUP), (2 * SSM_GROUP) ** -0.5)
    ssm_c_re = nrm((N_EVEN, N_SSM_GROUPS, SSM_GROUP, SSM_STATE), SSM_STATE ** -0.5)
    ssm_c_im = nrm((N_EVEN, N_SSM_GROUPS, SSM_GROUP, SSM_STATE), SSM_STATE ** -0.5)
    ssm_d = nrm((N_EVEN, D_SSM))
    ssm_w_glu = nrm((N_EVEN, D_SSM, D_SSM), D_SSM ** -0.5)
    w_out0 = nrm((N_EVEN, D_SSM + D_ATT, D_MODEL), (D_SSM + D_ATT) ** -0.5)
    norm_ffn0 = 1.0 + nrm((N_EVEN, D_MODEL), 0.02)
    ffn_w_gate = nrm((N_EVEN, D_MODEL, D_FF), D_MODEL ** -0.5)
    ffn_w_up = nrm((N_EVEN, D_MODEL, D_FF), D_MODEL ** -0.5)
    ffn_w_down = nrm((N_EVEN, D_FF, D_MODEL), D_FF ** -0.5)
    norm_mix1 = 1.0 + nrm((N_ODD, D_MODEL), 0.02)
    pool_w = nrm((N_ODD, len(POOL_WINDOWS), POOL_GROUP_DIM, POOL_GROUP_DIM), POOL_GROUP_DIM ** -0.5)
    pool_scale = 1.0 + nrm((N_ODD, D_MODEL), 0.02)
    norm_ffn1 = 1.0 + nrm((N_ODD, D_MODEL), 0.02)
    router_w = nrm((N_ODD, D_MODEL, N_EXPERTS), D_MODEL ** -0.5)
    moe_w_gate = nrm((N_ODD, N_EXPERTS, D_MODEL, D_FF_EXPERT), D_MODEL ** -0.5)
    moe_w_up = nrm((N_ODD, N_EXPERTS, D_MODEL, D_FF_EXPERT), D_MODEL ** -0.5)
    moe_w_down = nrm((N_ODD, N_EXPERTS, D_FF_EXPERT, D_MODEL), D_FF_EXPERT ** -0.5)
    return {"x_prompt": x_prompt, "x_sample": x_sample, "cache_k": cache_k, "cache_v": cache_v,
            "cache_kidx": cache_kidx, "page_table": page_table, "state_ssm_re": state_ssm_re,
            "state_ssm_im": state_ssm_im, "state_pool": state_pool, "meta_tokens": meta_tokens,
            "norm_mix0": norm_mix0, "w_in0": w_in0, "q_norm": q_norm, "k_norm": k_norm,
            "ssm_lambda_re": ssm_lambda_re, "ssm_lambda_im": ssm_lambda_im, "ssm_log_dt": ssm_log_dt,
            "ssm_b_re": ssm_b_re, "ssm_b_im": ssm_b_im, "ssm_c_re": ssm_c_re, "ssm_c_im": ssm_c_im,
            "ssm_d": ssm_d, "ssm_w_glu": ssm_w_glu, "w_out0": w_out0, "norm_ffn0": norm_ffn0,
            "ffn_w_gate": ffn_w_gate, "ffn_w_up": ffn_w_up, "ffn_w_down": ffn_w_down,
            "norm_mix1": norm_mix1, "pool_w": pool_w, "pool_scale": pool_scale, "norm_ffn1": norm_ffn1,
            "router_w": router_w, "moe_w_gate": moe_w_gate, "moe_w_up": moe_w_up, "moe_w_down": moe_w_down}


def reference(x_prompt, x_sample, cache_k, cache_v, cache_kidx, page_table, state_ssm_re, state_ssm_im,
              state_pool, meta_tokens, norm_mix0, w_in0, q_norm, k_norm, ssm_lambda_re, ssm_lambda_im,
              ssm_log_dt, ssm_b_re, ssm_b_im, ssm_c_re, ssm_c_im, ssm_d, ssm_w_glu, w_out0, norm_ffn0,
              ffn_w_gate, ffn_w_up, ffn_w_down, norm_mix1, pool_w, pool_scale, norm_ffn1, router_w,
              moe_w_gate, moe_w_up, moe_w_down):
    f32 = jnp.float32
    B = x_prompt.shape[0]
    S = x_sample.shape[1]
    past_len = page_table.shape[1] * cache_k.shape[2]
    meta = jnp.broadcast_to(meta_tokens.astype(x_prompt.dtype)[None], (B, N_META, D_MODEL))
    x_p = jnp.concatenate([meta, x_prompt], axis=1)
    x_s = x_sample
    L = x_p.shape[1]
    pos_p = jnp.arange(L, dtype=f32)
    pos_s = past_len + jnp.arange(S, dtype=f32)
    topk_p = min(TOPK_MAX, x_prompt.shape[1] // 4)
    topk_s = min(TOPK_MAX, (past_len + S) // 4)
    zero_h = jnp.zeros((B, N_SSM_GROUPS, SSM_STATE), f32)
    no_hist = jnp.zeros((B, 0, D_MODEL), x_p.dtype)
    attend_p = functools.partial(dsa_prompt, topk=topk_p)

    k_p, v_p, ki_p, hre_p, him_p, pool_p = [], [], [], [], [], []
    k_s, v_s, ki_s, hre_s, him_s, pool_s = [], [], [], [], [], []
    for layer in range(DEPTH):
        i = layer // 2
        if layer % 2 == 0:
            params = (norm_mix0[i], w_in0[i], q_norm[i], k_norm[i], ssm_lambda_re[i], ssm_lambda_im[i],
                      ssm_log_dt[i], ssm_b_re[i], ssm_b_im[i], ssm_c_re[i], ssm_c_im[i], ssm_d[i],
                      ssm_w_glu[i], w_out0[i], norm_ffn0[i], ffn_w_gate[i], ffn_w_up[i], ffn_w_down[i])
            attend_s = functools.partial(dsa_sample, cache_k=cache_k[i], cache_v=cache_v[i],
                                         cache_kidx=cache_kidx[i], page_table=page_table, topk=topk_s)
            x_p, kk, vv, kki, hre, him = even_layer(x_p, pos_p, zero_h, zero_h, attend_p, *params)
            k_p.append(kk); v_p.append(vv); ki_p.append(kki); hre_p.append(hre); him_p.append(him)
            x_s, kk, vv, kki, hre, him = even_layer(x_s, pos_s, state_ssm_re[i], state_ssm_im[i], attend_s, *params)
            k_s.append(kk); v_s.append(vv); ki_s.append(kki); hre_s.append(hre); him_s.append(him)
        else:
            params = (norm_mix1[i], pool_w[i], pool_scale[i], norm_ffn1[i], router_w[i],
                      moe_w_gate[i], moe_w_up[i], moe_w_down[i])
            x_p, hist = odd_layer(x_p, no_hist, 0, *params)
            pool_p.append(hist)
            x_s, hist = odd_layer(x_s, state_pool[i], past_len, *params)
            pool_s.append(hist)

    y_prompt = x_p[:, N_META:]
    y_sample = x_s
    k_prompt = jnp.stack(k_p); v_prompt = jnp.stack(v_p); kidx_prompt = jnp.stack(ki_p)
    ssm_re_prompt = jnp.stack(hre_p); ssm_im_prompt = jnp.stack(him_p); pool_prompt = jnp.stack(pool_p)
    k_sample = jnp.stack(k_s); v_sample = jnp.stack(v_s); kidx_sample = jnp.stack(ki_s)
    ssm_re_sample = jnp.stack(hre_s); ssm_im_sample = jnp.stack(him_s); pool_sample = jnp.stack(pool_s)
    return (y_prompt, y_sample, k_prompt, v_prompt, kidx_prompt, ssm_re_prompt, ssm_im_prompt, pool_prompt,
            k_sample, v_sample, kidx_sample, ssm_re_sample, ssm_im_sample, pool_sample)
```

```python
import functools
import math

import jax
import jax.numpy as jnp
from jax import lax
from jax.experimental import pallas as pl
from jax.experimental.pallas import tpu as pltpu

F32 = jnp.float32
BF16 = jnp.bfloat16
I32 = jnp.int32

N_META = 16
SSM_GROUP = 16
SSM_STATE = 64
N_HEADS = 8
HEAD_DIM = 64
N_IDX_HEADS = 8
IDX_DIM = 64
TOPK_MAX = 256
ROPE_THETA = 10000.0
POOL_WINDOWS = (2, 4, 8, 16)
POOL_HIST = max(POOL_WINDOWS) - 1
TOP_K_EXPERTS = 2
EPS = 1e-6

LANES = 128
SUBLANES = 8
VMEM_LIMIT = 56 * 1024 * 1024
BISECT_ITERS = 22
NEG_BIG = -1e30


def _params(sem):
    return pltpu.CompilerParams(dimension_semantics=sem, vmem_limit_bytes=VMEM_LIMIT)


def _row_block(n, cap, mult=2 * SUBLANES):
    best = None
    for d in range(mult, min(n, cap) + 1, mult):
        if n % d == 0:
            best = d
    assert best is not None, n
    return best


def _rms(x, g):
    return x * lax.rsqrt(jnp.mean(x * x, axis=-1, keepdims=True) + EPS) * g


def _dot(a, b):
    return jnp.dot(a, b, preferred_element_type=F32)


def _dot_nt(a, b):
    return lax.dot_general(a, b, (((1,), (1,)), ((), ())), preferred_element_type=F32)


def _const_spec(shape):
    nd = len(shape)
    return pl.BlockSpec(shape, lambda *_: (0,) * nd)


def _inproj_body(x_ref, g_ref, wu_ref, wq_ref, wk_ref, wv_ref, wqi_ref, wkw_ref, qn_ref, kn_ref, hm_ref,
                 cos_ref, sin_ref,
                 u_ref, q_ref, k_ref, v_ref, qi_ref, ki_ref, kw_ref, kb_ref, vb_ref, kib_ref):
    h = _rms(x_ref[...], g_ref[...]).astype(BF16)
    cos = cos_ref[...]
    sin = sin_ref[...]
    lane = lax.broadcasted_iota(I32, (1, LANES), 1)
    lo_half = (lane & (HEAD_DIM // 2)) == 0

    def rope(z):
        cols = []
        for j in range(z.shape[1] // LANES):
            zs = z[:, LANES * j:LANES * (j + 1)]
            partner = jnp.where(lo_half, pltpu.roll(zs, LANES - HEAD_DIM // 2, 1), pltpu.roll(zs, HEAD_DIM // 2, 1))
            cols.append(zs * cos + partner * sin)
        return cols[0] if len(cols) == 1 else jnp.concatenate(cols, axis=1)

    def head_norm(z, gn):
        ms = _dot((z * z).astype(BF16), hm_ref[...])
        return z * lax.rsqrt(ms + EPS) * gn

    u_ref[...] = _dot(h, wu_ref[...])
    q = rope(head_norm(_dot(h, wq_ref[...]), qn_ref[...]))
    q_ref[...] = (q * (HEAD_DIM ** -0.5)).astype(BF16)
    k = rope(head_norm(_dot(h, wk_ref[...]), kn_ref[...]))
    k_ref[...] = k
    kb_ref[...] = k.astype(BF16)
    v = _dot(h, wv_ref[...])
    v_ref[...] = v
    vb_ref[...] = v.astype(BF16)
    qi_ref[...] = (rope(_dot(h, wqi_ref[...])) * (IDX_DIM ** -0.5)).astype(BF16)
    zkw = _dot(h, wkw_ref[...])
    kir = rope(zkw)
    kw_ref[...] = jnp.where(lane < IDX_DIM, kir, zkw * (N_IDX_HEADS ** -0.5))
    ki_ref[...] = kir[:, :IDX_DIM]
    kib_ref[...] = jnp.where(lane < IDX_DIM, kir, pltpu.roll(kir, IDX_DIM, 1)).astype(BF16)


def _inproj(x2d, cos_t, sin_t, tab_blocks, tm, g, ws, qn, kn, hm):
    T, D = x2d.shape
    wu, wq, wk, wv, wqi, wkw = ws
    d_ssm, d_att, d_qi = wu.shape[1], wq.shape[1], wqi.shape[1]
    row = lambda w: pl.BlockSpec((tm, w), lambda i: (i, 0))
    tab = pl.BlockSpec((tm, LANES), lambda i: (i % tab_blocks, 0))
    out_shape = (
        jax.ShapeDtypeStruct((T, d_ssm), F32),
        jax.ShapeDtypeStruct((T, d_att), BF16),
        jax.ShapeDtypeStruct((T, d_att), F32),
        jax.ShapeDtypeStruct((T, d_att), F32),
        jax.ShapeDtypeStruct((T, d_qi), BF16),
        jax.ShapeDtypeStruct((T, IDX_DIM), F32),
        jax.ShapeDtypeStruct((T, LANES), F32),
        jax.ShapeDtypeStruct((T, d_att), BF16),
        jax.ShapeDtypeStruct((T, d_att), BF16),
        jax.ShapeDtypeStruct((T, LANES), BF16),
    )
    out_specs = (row(d_ssm), row(d_att), row(d_att), row(d_att), row(d_qi), row(IDX_DIM), row(LANES),
                 row(d_att), row(d_att), row(LANES))
    in_specs = [row(D), _const_spec(g.shape)] + [_const_spec(w.shape) for w in ws] + [
        _const_spec(qn.shape), _const_spec(kn.shape), _const_spec(hm.shape), tab, tab]
    return pl.pallas_call(
        _inproj_body, grid=(T // tm,), in_specs=in_specs, out_specs=out_specs, out_shape=out_shape,
        compiler_params=_params(("arbitrary",)), name="inproj",
    )(x2d, g, *ws, qn, kn, hm, cos_t, sin_t)


def _s5_prep_body(lr_ref, li_ref, ldt_ref, btr_ref, bti_ref, lbr_ref, lbi_ref, bbr_ref, bbi_ref):
    lr = lr_ref[...]
    li = li_ref[...]
    dt = jnp.exp(ldt_ref[...])
    mag = jnp.exp(lr * dt)
    lbr = mag * jnp.cos(li * dt)
    lbi = mag * jnp.sin(li * dt)
    lbr_ref[...] = lbr
    lbi_ref[...] = lbi
    den = lr * lr + li * li
    cr = ((lbr - 1.0) * lr + lbi * li) / den
    ci = (lbi * lr - (lbr - 1.0) * li) / den
    btr = btr_ref[...]
    bti = bti_ref[...]
    bbr_ref[...] = cr * btr - ci * bti
    bbi_ref[...] = cr * bti + ci * btr


def _s5_prep(lam_re, lam_im, log_dt, b_re, b_im):
    G, P = lam_re.shape
    C = b_re.shape[-1]
    btr = jnp.swapaxes(b_re, 1, 2)
    bti = jnp.swapaxes(b_im, 1, 2)
    sds = jax.ShapeDtypeStruct
    return pl.pallas_call(
        _s5_prep_body,
        out_shape=(sds((G, 1, P), F32), sds((G, 1, P), F32), sds((G, C, P), F32), sds((G, C, P), F32)),
        name="s5_prep",
    )(lam_re.reshape(G, 1, P), lam_im.reshape(G, 1, P), log_dt.reshape(G, 1, 1), btr, bti)


def _s5_body(u_ref, h0r_ref, h0i_ref, lbr_ref, lbi_ref, wb_ref, wc_ref, d_ref, wglu_ref,
             y_ref, hr_ref, hi_ref, bu_ref, st_ref, *, nb, steps, precise):
    n_slab = wb_ref.shape[0]
    cin = wb_ref.shape[1]
    sw = wb_ref.shape[2] // 2
    n_state = n_slab * sw

    @pl.when(pl.program_id(0) == 0)
    def _():
        st_ref[0] = h0r_ref[...]
        st_ref[1] = h0i_ref[...]

    u = u_ref[...]
    for i in range(n_slab):
        ui = u[:, cin * i:cin * (i + 1)]
        if precise:
            bu = jnp.dot(ui, wb_ref[i], preferred_element_type=F32, precision=lax.Precision.HIGHEST)
        else:
            bu = _dot(ui.astype(BF16), wb_ref[i].astype(BF16))
        bu_ref[:, sw * i:sw * (i + 1)] = bu[:, :sw]
        bu_ref[:, n_state + sw * i:n_state + sw * (i + 1)] = bu[:, sw:]

    for i in range(n_slab):
        re_cols = slice(sw * i, sw * (i + 1))
        im_cols = slice(n_state + sw * i, n_state + sw * (i + 1))
        lr = jnp.broadcast_to(lbr_ref[:, re_cols], (nb, sw))
        li = jnp.broadcast_to(lbi_ref[:, re_cols], (nb, sw))

        def step(t, carry, re_cols=re_cols, im_cols=im_cols, lr=lr, li=li):
            hr, hi = carry
            rows = pl.ds(pl.multiple_of(t * nb, nb), nb)
            nhr = lr * hr - li * hi + bu_ref[rows, re_cols]
            nhi = lr * hi + li * hr + bu_ref[rows, im_cols]
            bu_ref[rows, re_cols] = nhr
            bu_ref[rows, im_cols] = nhi
            return nhr, nhi

        hr, hi = lax.fori_loop(0, steps, step, (st_ref[0, :, re_cols], st_ref[1, :, re_cols]))
        st_ref[0, :, re_cols] = hr
        st_ref[1, :, re_cols] = hi

    ys = []
    for i in range(n_slab):
        h_re = bu_ref[:, sw * i:sw * (i + 1)].astype(BF16)
        h_im = bu_ref[:, n_state + sw * i:n_state + sw * (i + 1)].astype(BF16)
        ys.append(_dot(h_re, wc_ref[i, 0]) + _dot(h_im, wc_ref[i, 1]))
    y = jnp.concatenate(ys, axis=1) + d_ref[...] * u
    y = 0.5 * y * (1.0 + jnp.tanh(math.sqrt(2.0 / math.pi) * (y + 0.044715 * (y * y * y))))
    y = y * jax.nn.sigmoid(_dot(y.astype(BF16), wglu_ref[...]))
    y_ref[...] = y.astype(BF16)
    hr_ref[...] = st_ref[0]
    hi_ref[...] = st_ref[1]


def _s5(u_tm, h0_re, h0_im, lbr_flat, lbi_flat, wb, wc, d_skip, wglu, nb, steps, precise):
    rows_total, d_ssm = u_tm.shape
    n_state = h0_re.shape[1]
    rows = nb * steps
    sds = jax.ShapeDtypeStruct
    body = functools.partial(_s5_body, nb=nb, steps=steps, precise=precise)
    return pl.pallas_call(
        body, grid=(rows_total // rows,),
        in_specs=[pl.BlockSpec((rows, d_ssm), lambda c: (c, 0)), _const_spec(h0_re.shape), _const_spec(h0_im.shape),
                  _const_spec(lbr_flat.shape), _const_spec(lbi_flat.shape), _const_spec(wb.shape),
                  _const_spec(wc.shape), _const_spec(d_skip.shape), _const_spec(wglu.shape)],
        out_specs=(pl.BlockSpec((rows, d_ssm), lambda c: (c, 0)), _const_spec(h0_re.shape), _const_spec(h0_im.shape)),
        out_shape=(sds((rows_total, d_ssm), BF16), sds(h0_re.shape, F32), sds(h0_im.shape, F32)),
        scratch_shapes=[pltpu.VMEM((rows, 2 * n_state), F32), pltpu.VMEM((2, nb, n_state), F32)],
        compiler_params=_params(("arbitrary",)), name="s5_precise" if precise else "s5",
    )(u_tm, h0_re, h0_im, lbr_flat, lbi_flat, wb, wc, d_skip, wglu)


def _topk_bias(s_ref, n, kk):
    rows = s_ref.shape[0]
    cols = slice(0, n)
    pos = lax.broadcasted_iota(I32, (1, n), 1)

    def count(pred):
        return jnp.sum(jnp.where(pred(s_ref[:, cols]), 1.0, 0.0), axis=1, keepdims=True)

    def largest_below(hi):
        s = s_ref[:, cols]
        return jnp.max(jnp.where(s < hi, s, -jnp.inf), axis=1, keepdims=True)

    s0 = s_ref[:, cols]
    smax = jnp.max(s0, axis=1, keepdims=True)
    smin = jnp.min(jnp.where(s0 > -jnp.inf, s0, jnp.inf), axis=1, keepdims=True)

    def bisect(_, c):
        lo, hi = c
        mid = lo + (hi - lo) * 0.5
        ge = count(lambda s: s >= mid) >= kk
        return jnp.where(ge, mid, lo), jnp.where(ge, hi, mid)

    _, hi = lax.fori_loop(0, BISECT_ITERS, bisect, (smin, smax + (smax - smin) + 1.0))
    thr = largest_below(hi)
    cnt = count(lambda s: s >= thr)

    def short(c):
        return jnp.min(c[2] - kk) < 0.0

    def lower(c):
        hi, thr, cnt = c
        hi = jnp.where(cnt < kk, thr, hi)
        thr = largest_below(hi)
        return hi, thr, count(lambda s: s >= thr)

    _, thr, cnt = lax.while_loop(short, lower, (hi, thr, cnt))
    n_gt = count(lambda s: s > thr)
    need = kk - n_gt
    n_pos_bits = max(1, (n - 1).bit_length())

    def tie_search():
        def pos_step(i, cut):
            cand = cut | lax.shift_left(jnp.int32(1), n_pos_bits - 1 - i)
            below = count(lambda s: jnp.where(s == thr, pos, 2 ** 30) < cand)
            return jnp.where(below < need, cand, cut)

        return lax.fori_loop(0, n_pos_bits, pos_step, jnp.zeros((rows, 1), I32))

    cut = lax.cond(jnp.max(cnt - n_gt - need) > 0.0, tie_search, lambda: jnp.full((rows, 1), 2 ** 30, I32))
    s = s_ref[:, cols]
    keep = (s > thr) | ((s == thr) & (pos <= cut))
    s_ref[:, cols] = jnp.where(keep, 0.0, NEG_BIG)


def _dsa_prompt_block(q0, n, qim_ref, qm_ref, w_ref, kip_ref, kp_ref, vp_ref, s_ref, acc_ref, topk):
    tq = s_ref.shape[0]
    cols = slice(0, n)
    lane = lax.broadcasted_iota(I32, (1, LANES), 1)
    s_ref[:, cols] = jnp.zeros((tq, n), F32)

    def score_head(h, carry):
        d = jnp.maximum(_dot_nt(qim_ref[h], kip_ref[cols, :]), 0.0)
        s_ref[:, cols] += w_ref[h] * d
        return carry

    lax.fori_loop(0, N_IDX_HEADS, score_head, 0)
    qpos = q0 + lax.broadcasted_iota(I32, (tq, 1), 0)
    kpos = lax.broadcasted_iota(I32, (1, n), 1)
    s_ref[:, cols] = jnp.where(kpos <= qpos, s_ref[:, cols], -jnp.inf)
    _topk_bias(s_ref, n, jnp.minimum(qpos + 1, topk).astype(F32))

    def head_pair(sp, carry):
        outs = []
        for hh in range(2):
            lg = _dot_nt(qm_ref[2 * sp + hh], kp_ref[sp, cols, :]) + s_ref[:, cols]
            p = jnp.exp(lg - jnp.max(lg, axis=1, keepdims=True))
            den = jnp.sum(p, axis=1, keepdims=True)
            outs.append(_dot(p.astype(BF16), vp_ref[sp, cols, :]) / den)
        acc_ref[sp] = jnp.where(lane < HEAD_DIM, outs[0], outs[1])
        return carry

    lax.fori_loop(0, acc_ref.shape[0], head_pair, 0)


def _dsa_prompt_body(qi_ref, kw_ref, q_ref, kib_ref, kb_ref, vb_ref, o_ref,
                     kip_ref, kp_ref, vp_ref, qim_ref, qm_ref, w_ref, s_ref, acc_ref, *, seq, topk):
    tq = q_ref.shape[0]
    n_slab, lp, _ = kp_ref.shape
    j = pl.program_id(1)
    lane = lax.broadcasted_iota(I32, (1, LANES), 1)

    @pl.when(j == 0)
    def _():
        kip_ref[0:seq] = kib_ref[...]
        if lp > seq:
            kip_ref[seq:] = jnp.zeros((lp - seq, LANES), BF16)
        for sp in range(n_slab):
            sl = slice(LANES * sp, LANES * (sp + 1))
            kp_ref[sp, 0:seq] = kb_ref[:, sl]
            vp_ref[sp, 0:seq] = vb_ref[:, sl]
            if lp > seq:
                kp_ref[sp, seq:] = jnp.zeros((lp - seq, LANES), BF16)
                vp_ref[sp, seq:] = jnp.zeros((lp - seq, LANES), BF16)

    kw = kw_ref[...]
    for h in range(N_IDX_HEADS):
        head_lanes = (lane // IDX_DIM) == (h % 2)
        sl = slice(LANES * (h // 2), LANES * (h // 2 + 1))
        qim_ref[h] = jnp.where(head_lanes, qi_ref[:, sl], jnp.zeros((), BF16))
        qm_ref[h] = jnp.where(head_lanes, q_ref[:, sl], jnp.zeros((), BF16))
        w_ref[h] = kw[:, IDX_DIM + h:IDX_DIM + h + 1]

    for jj in range(seq // tq):
        n = min(-(-((jj + 1) * tq) // LANES) * LANES, lp)

        @pl.when(j == jj)
        def _(jj=jj, n=n):
            _dsa_prompt_block(jj * tq, n, qim_ref, qm_ref, w_ref, kip_ref, kp_ref, vp_ref, s_ref, acc_ref, topk)

    for sp in range(n_slab):
        o_ref[:, LANES * sp:LANES * (sp + 1)] = acc_ref[sp].astype(BF16)


def _dsa_prompt(qi, kw, q, kib, kb, vb, batch, seq, tq, topk):
    T, d_att = q.shape
    nq = seq // tq
    lp = -(-seq // LANES) * LANES
    n_slab = d_att // LANES
    qrow = lambda w: pl.BlockSpec((tq, w), lambda b, j: (b * nq + j, 0))
    full = lambda w: pl.BlockSpec((None, seq, w), lambda b, j: (b, 0, 0))
    body = functools.partial(_dsa_prompt_body, seq=seq, topk=topk)
    return pl.pallas_call(
        body, grid=(batch, nq),
        in_specs=[qrow(qi.shape[1]), qrow(LANES), qrow(d_att), full(LANES), full(d_att), full(d_att)],
        out_specs=qrow(d_att),
        out_shape=jax.ShapeDtypeStruct((T, d_att), BF16),
        scratch_shapes=[pltpu.VMEM((lp, LANES), BF16), pltpu.VMEM((n_slab, lp, LANES), BF16),
                        pltpu.VMEM((n_slab, lp, LANES), BF16),
                        pltpu.VMEM((N_IDX_HEADS, tq, LANES), BF16), pltpu.VMEM((N_HEADS, tq, LANES), BF16),
                        pltpu.VMEM((N_IDX_HEADS, tq, 1), F32),
                        pltpu.VMEM((tq, lp), F32), pltpu.VMEM((n_slab, tq, LANES), F32)],
        compiler_params=_params(("arbitrary", "arbitrary")), name="dsa_prompt",
    )(qi, kw, q, kib.reshape(batch, seq, LANES), kb.reshape(batch, seq, d_att), vb.reshape(batch, seq, d_att))


def _dsa_sample_scores_body(pt_ref, qi_ref, w_ref, kin_ref, *refs):
    del pt_ref
    page_refs, s_ref = refs[:-1], refs[-1]
    n_pages = len(page_refs)
    idx_dim, page = page_refs[0].shape
    heads = [slice(idx_dim * h, idx_dim * (h + 1)) for h in range(N_IDX_HEADS)]
    qcols = [jnp.broadcast_to(qi_ref[rows, :], (idx_dim, page)) for rows in heads]
    ws = [w_ref[h:h + 1, :] for h in range(N_IDX_HEADS)]

    def score(dot_of_head):
        s = None
        for h in range(N_IDX_HEADS):
            t = ws[h] * jnp.maximum(dot_of_head(h), 0.0)
            s = t if s is None else s + t
        return s

    for p in range(n_pages):
        kt = page_refs[p][...]
        s_ref[p:p + 1, :] = score(lambda h: jnp.sum(qcols[h] * kt, axis=0, keepdims=True))
    kin = kin_ref[...]
    s_new = score(lambda h: jnp.sum(qi_ref[heads[h], :] * kin, axis=0, keepdims=True))
    lane = lax.broadcasted_iota(I32, (1, page), 1)
    s_ref[n_pages:n_pages + 1, :] = jnp.where(lane == 0, s_new, 0.0)


def _dsa_sample_scores(page_table, qi_col, w3, ki_col, kidx_t):
    nb, n_pages = page_table.shape
    idx_dim, page = kidx_t.shape[1:]
    per = lambda shape: pl.BlockSpec((None,) + shape, lambda b, pt: (b, 0, 0))
    page_specs = [pl.BlockSpec((None, idx_dim, page), functools.partial(lambda b, pt, p: (pt[b, p], 0, 0), p=p))
                  for p in range(n_pages)]
    grid_spec = pltpu.PrefetchScalarGridSpec(
        num_scalar_prefetch=1, grid=(nb,),
        in_specs=[per(qi_col.shape[1:]), per(w3.shape[1:]), per(ki_col.shape[1:])] + page_specs,
        out_specs=per((n_pages + 1, page)))
    return pl.pallas_call(
        _dsa_sample_scores_body, grid_spec=grid_spec,
        out_shape=jax.ShapeDtypeStruct((nb, n_pages + 1, page), F32),
        compiler_params=_params(("arbitrary",)), name="dsa_sample_scores",
    )(page_table, qi_col, w3, ki_col, *([kidx_t] * n_pages))


def _dsa_sample_select_body(s_ref, b_ref, *, n_keys, topk):
    rows, width = s_ref.shape
    pos = lax.broadcasted_iota(I32, (1, width), 1)
    b_ref[...] = jnp.where(pos < n_keys, s_ref[...], -jnp.inf)
    _topk_bias(b_ref, width, jnp.full((rows, 1), float(min(topk, n_keys)), F32))


def _dsa_sample_select(scores, n_keys, topk):
    rows, width = scores.shape
    body = functools.partial(_dsa_sample_select_body, n_keys=n_keys, topk=topk)
    return pl.pallas_call(
        body, out_shape=jax.ShapeDtypeStruct((rows, width), F32),
        compiler_params=pltpu.CompilerParams(vmem_limit_bytes=VMEM_LIMIT), name="dsa_sample_select",
    )(scores)


def _dsa_sample_attend_body(pt_ref, q_ref, kn_ref, vn_ref, b_ref, *refs):
    del pt_ref
    o_ref, lg_ref = refs[-2], refs[-1]
    n_pages = (len(refs) - 2) // 2
    k_refs, v_refs = refs[:n_pages], refs[n_pages:2 * n_pages]
    n_heads, hd, page = k_refs[0].shape
    lane = lax.broadcasted_iota(I32, (1, page), 1)
    bias = b_ref[...]
    for h in range(n_heads):
        rows = slice(hd * h, hd * (h + 1))
        qc = jnp.broadcast_to(q_ref[rows, :], (hd, page))
        for p in range(n_pages):
            lg_ref[p:p + 1, :] = jnp.sum(qc * k_refs[p][h], axis=0, keepdims=True)
        lg_new = jnp.sum(q_ref[rows, :] * kn_ref[rows, :], axis=0, keepdims=True)
        lg_ref[n_pages:n_pages + 1, :] = jnp.where(lane == 0, lg_new, 0.0)
        lg = lg_ref[...] + bias
        m = jnp.max(jnp.max(lg, axis=1, keepdims=True), axis=0, keepdims=True)
        pr = jnp.exp(lg - m)
        den = jnp.sum(jnp.sum(pr, axis=1, keepdims=True), axis=0, keepdims=True)
        acc = jnp.zeros((hd, page), F32)
        for p in range(n_pages):
            acc = acc + pr[p:p + 1, :] * v_refs[p][h]
        out = jnp.sum(acc, axis=1, keepdims=True) + pr[n_pages:n_pages + 1, 0:1] * vn_ref[rows, :]
        o_ref[rows, :] = out / den


def _dsa_sample_attend(page_table, q_col, kn_col, vn_col, bias3, k_t, v_t):
    nb, n_pages = page_table.shape
    n_heads, hd, page = k_t.shape[1:]
    per = lambda shape: pl.BlockSpec((None,) + shape, lambda b, pt: (b, 0, 0))
    page_specs = [pl.BlockSpec((None, n_heads, hd, page), functools.partial(lambda b, pt, p: (pt[b, p], 0, 0, 0), p=p))
                  for p in range(n_pages)]
    grid_spec = pltpu.PrefetchScalarGridSpec(
        num_scalar_prefetch=1, grid=(nb,),
        in_specs=[per(q_col.shape[1:]), per(kn_col.shape[1:]), per(vn_col.shape[1:]), per(bias3.shape[1:])] + page_specs * 2,
        out_specs=per(q_col.shape[1:]),
        scratch_shapes=[pltpu.VMEM((n_pages + 1, page), F32)])
    return pl.pallas_call(
        _dsa_sample_attend_body, grid_spec=grid_spec,
        out_shape=jax.ShapeDtypeStruct(q_col.shape, F32),
        compiler_params=_params(("arbitrary",)), name="dsa_sample_attend",
    )(page_table, q_col, kn_col, vn_col, bias3, *([k_t] * n_pages), *([v_t] * n_pages))


def _swiglu(h, wg_ref, wu_ref, wd_ref, n_chunks):
    fc = wg_ref.shape[-1] // n_chunks
    y = None
    for c in range(n_chunks):
        cols = slice(fc * c, fc * (c + 1))
        gate = _dot(h, wg_ref[:, cols])
        act = (gate * jax.nn.sigmoid(gate) * _dot(h, wu_ref[:, cols])).astype(BF16)
        part = _dot(act, wd_ref[cols, :])
        y = part if y is None else y + part
    return y


def _outproj_ffn_body(x_ref, ys_ref, ya_ref, wos_ref, woa_ref, g_ref, wg_ref, wu_ref, wd_ref, o_ref):
    x = x_ref[...] + _dot(ys_ref[...], wos_ref[...]) + _dot(ya_ref[...], woa_ref[...])
    h = _rms(x, g_ref[...]).astype(BF16)
    o_ref[...] = x + _swiglu(h, wg_ref, wu_ref, wd_ref, 2)


def _outproj_ffn(x2d, ys, ya, wos, woa, g, wg, wu, wd, tm):
    T, D = x2d.shape
    row = lambda w: pl.BlockSpec((tm, w), lambda i: (i, 0))
    single = lambda a: pl.BlockSpec(a.shape, lambda i: (0,) * a.ndim, pipeline_mode=pl.Buffered(1))
    return pl.pallas_call(
        _outproj_ffn_body, grid=(T // tm,),
        in_specs=[row(D), row(ys.shape[1]), row(ya.shape[1]), single(wos), single(woa), single(g),
                  single(wg), single(wu), single(wd)],
        out_specs=row(D), out_shape=jax.ShapeDtypeStruct((T, D), F32),
        compiler_params=_params(("arbitrary",)), name="outproj_ffn",
    )(x2d, ys, ya, wos, woa, g, wg, wu, wd)


def _pool_mix(h, window_sum, divisor, pw_ref, scale):
    gd = h.shape[1] // len(POOL_WINDOWS)
    cols = []
    for g, w in enumerate(POOL_WINDOWS):
        pooled = window_sum(g, w) / divisor(w) - h[:, gd * g:gd * (g + 1)]
        cols.append(_dot(pooled.astype(BF16), pw_ref[g]))
    return jnp.concatenate(cols, axis=1) * scale


def _pool_prompt_body(x_ref, g_ref, pw_ref, sc_ref, o_ref, hist_ref, ext_ref):
    tm, D = x_ref.shape
    gd = D // len(POOL_WINDOWS)
    halo = POOL_HIST + 1
    j = pl.program_id(1)

    @pl.when(j == 0)
    def _():
        ext_ref[0:halo] = jnp.zeros((halo, D), F32)

    x = x_ref[...]
    h = _rms(x, g_ref[...])
    ext_ref[halo:] = h
    pos = (j * tm + lax.broadcasted_iota(I32, (tm, 1), 0)).astype(F32)

    def window_sum(g, w):
        acc = h[:, gd * g:gd * (g + 1)]
        for k in range(1, w):
            acc = acc + ext_ref[halo - k:halo - k + tm, gd * g:gd * (g + 1)]
        return acc

    mixed = _pool_mix(h, window_sum, lambda w: jnp.minimum(float(w), pos + 1.0), pw_ref, sc_ref[...])
    o_ref[...] = x + mixed
    ext_ref[0:halo] = ext_ref[tm:tm + halo]

    @pl.when(j == pl.num_programs(1) - 1)
    def _():
        hist_ref[...] = h[tm - POOL_HIST:, :]


def _pool_prompt(x2d, g, pw, scale, batch, seq, tm):
    T, D = x2d.shape
    nblk = seq // tm
    row = pl.BlockSpec((tm, D), lambda b, j: (b * nblk + j, 0))
    return pl.pallas_call(
        _pool_prompt_body, grid=(batch, nblk),
        in_specs=[row, _const_spec(g.shape), _const_spec(pw.shape), _const_spec(scale.shape)],
        out_specs=(row, pl.BlockSpec((None, POOL_HIST, D), lambda b, j: (b, 0, 0))),
        out_shape=(jax.ShapeDtypeStruct((T, D), F32), jax.ShapeDtypeStruct((batch, POOL_HIST, D), F32)),
        scratch_shapes=[pltpu.VMEM((tm + POOL_HIST + 1, D), F32)],
        compiler_params=_params(("arbitrary", "arbitrary")), name="pool_prompt",
    )(x2d, g, pw, scale)


def _pool_sample_body(x_ref, hist_ref, g_ref, pw_ref, sc_ref, o_ref, nh_ref):
    D = x_ref.shape[1]
    gd = D // len(POOL_WINDOWS)
    x = x_ref[...]
    h = _rms(x, g_ref[...])

    def window_sum(g, w):
        acc = h[:, gd * g:gd * (g + 1)]
        for k in range(1, w):
            acc = acc + hist_ref[POOL_HIST - k, :, gd * g:gd * (g + 1)]
        return acc

    mixed = _pool_mix(h, window_sum, float, pw_ref, sc_ref[...])
    o_ref[...] = x + mixed
    for i in range(POOL_HIST - 1):
        nh_ref[i] = hist_ref[i + 1]
    nh_ref[POOL_HIST - 1] = h


def _pool_sample(x2d, hist_t, g, pw, scale):
    sds = jax.ShapeDtypeStruct
    return pl.pallas_call(
        _pool_sample_body, out_shape=(sds(x2d.shape, F32), sds(hist_t.shape, F32)),
        compiler_params=pltpu.CompilerParams(vmem_limit_bytes=VMEM_LIMIT), name="pool_sample",
    )(x2d, hist_t, g, pw, scale)


def _router_body(x_ref, g_ref, wr_ref, h_ref, gate_ref, *, n_experts):
    h = _rms(x_ref[...], g_ref[...])
    h_ref[...] = h.astype(BF16)
    logits = jnp.dot(h, wr_ref[...], preferred_element_type=F32, precision=lax.Precision.HIGHEST)
    lane = lax.broadcasted_iota(I32, logits.shape, 1).astype(F32)
    logits = jnp.where(lane < n_experts, logits, -jnp.inf)
    v1 = jnp.max(logits, axis=1, keepdims=True)
    i1 = jnp.min(jnp.where(logits == v1, lane, float(LANES)), axis=1, keepdims=True)
    rest = jnp.where(lane == i1, -jnp.inf, logits)
    v2 = jnp.max(rest, axis=1, keepdims=True)
    i2 = jnp.min(jnp.where(rest == v2, lane, float(LANES)), axis=1, keepdims=True)
    e2 = jnp.exp(v2 - v1)
    den = 1.0 + e2
    gate_ref[...] = jnp.where(lane == i1, 1.0 / den, 0.0) + jnp.where(lane == i2, e2 / den, 0.0)


def _router(x2d, g, wr_pad, n_experts, tm):
    T, D = x2d.shape
    row = lambda w: pl.BlockSpec((tm, w), lambda i: (i, 0))
    body = functools.partial(_router_body, n_experts=n_experts)
    return pl.pallas_call(
        body, grid=(T // tm,),
        in_specs=[row(D), _const_spec(g.shape), _const_spec(wr_pad.shape)],
        out_specs=(row(D), row(LANES)),
        out_shape=(jax.ShapeDtypeStruct((T, D), BF16), jax.ShapeDtypeStruct((T, LANES), F32)),
        compiler_params=_params(("arbitrary",)), name="router",
    )(x2d, g, wr_pad)


def _moe_dense_body(acc_ref, h_ref, gate_ref, wg_ref, wu_ref, wd_ref, o_ref):
    e = pl.program_id(0)
    y = _swiglu(h_ref[...], wg_ref, wu_ref, wd_ref, 2)
    gates = gate_ref[...]
    lane = lax.broadcasted_iota(I32, gates.shape, 1)
    ge = jnp.sum(jnp.where(lane == e, gates, 0.0), axis=1, keepdims=True)
    o_ref[...] = acc_ref[...] + ge * y


def _moe_dense(x2d, h, gates, wg, wu, wd, tm):
    T, D = x2d.shape
    E, _, F = wg.shape
    row = lambda w: pl.BlockSpec((tm, w), lambda e, i: (i, 0))
    wspec = lambda a, b: pl.BlockSpec((None, a, b), lambda e, i: (e, 0, 0), pipeline_mode=pl.Buffered(1))
    return pl.pallas_call(
        _moe_dense_body, grid=(E, T // tm),
        in_specs=[row(D), row(D), row(LANES), wspec(D, F), wspec(D, F), wspec(F, D)],
        out_specs=row(D), out_shape=jax.ShapeDtypeStruct((T, D), F32),
        input_output_aliases={0: 0},
        compiler_params=_params(("arbitrary", "arbitrary")), name="moe_dense",
    )(x2d, h, gates, wg, wu, wd)


def _rope_tables(pos):
    half = HEAD_DIM // 2
    inv = ROPE_THETA ** (-jnp.arange(half, dtype=F32) / half)
    ang = pos[:, None] * inv[None, :]
    cos, sin = jnp.cos(ang), jnp.sin(ang)
    reps = LANES // HEAD_DIM
    return jnp.tile(jnp.concatenate([cos, cos], axis=1), (1, reps)), jnp.tile(jnp.concatenate([-sin, sin], axis=1), (1, reps))


def kernel(x_prompt, x_sample, cache_k, cache_v, cache_kidx, page_table, state_ssm_re, state_ssm_im, state_pool, meta_tokens, norm_mix0, w_in0, q_norm, k_norm, ssm_lambda_re, ssm_lambda_im, ssm_log_dt, ssm_b_re, ssm_b_im, ssm_c_re, ssm_c_im, ssm_d, ssm_w_glu, w_out0, norm_ffn0, ffn_w_gate, ffn_w_up, ffn_w_down, norm_mix1, pool_w, pool_scale, norm_ffn1, router_w, moe_w_gate, moe_w_up, moe_w_down):
    B, S, D = x_prompt.shape
    DB, DS, _ = x_sample.shape
    assert DS == 1, "one new token per sample sequence"
    L = S + N_META
    n_pool, page = cache_k.shape[1], cache_k.shape[2]
    n_pages = page_table.shape[1]
    past = n_pages * page
    d_att = N_HEADS * HEAD_DIM
    d_qi = N_IDX_HEADS * IDX_DIM
    G, P = ssm_lambda_re.shape[1:]
    d_ssm = G * SSM_GROUP
    n_state = G * P
    E = router_w.shape[-1]
    topk_p = min(TOPK_MAX, S // 4)
    topk_s = min(TOPK_MAX, (past + DS) // 4)
    tm = _row_block(L, 1024)
    assert tm >= POOL_HIST + 1

    x_p = jnp.concatenate([jnp.broadcast_to(meta_tokens[None], (B, N_META, D)), x_prompt], axis=1).reshape(B * L, D)
    x_s = x_sample.reshape(DB, D)

    w_in = w_in0[0]
    o1, o2, o3, o4, o5, o6 = d_ssm, d_ssm + d_att, d_ssm + 2 * d_att, d_ssm + 3 * d_att, d_ssm + 3 * d_att + d_qi, d_ssm + 3 * d_att + d_qi + IDX_DIM
    w_kw = jnp.pad(w_in[:, o5:], ((0, 0), (0, LANES - (w_in.shape[1] - o5))))
    ws = tuple(w.astype(BF16) for w in (w_in[:, :o1], w_in[:, o1:o2], w_in[:, o2:o3], w_in[:, o3:o4], w_in[:, o4:o5], w_kw))
    g_mix0 = norm_mix0[0].reshape(1, D)
    qn = jnp.tile(q_norm[0], N_HEADS).reshape(1, d_att)
    kn = jnp.tile(k_norm[0], N_HEADS).reshape(1, d_att)
    head_of = jnp.arange(d_att) // HEAD_DIM
    hm = jnp.where(head_of[:, None] == head_of[None, :], 1.0 / HEAD_DIM, 0.0).astype(BF16)
    cos_p, sin_p = _rope_tables(jnp.arange(L, dtype=F32))
    cos_s, sin_s = _rope_tables(jnp.full((DB,), float(past), F32))

    u_p, q_p, k_p, v_p, qi_p, ki_p, kw_p, kb_p, vb_p, kib_p = _inproj(x_p, cos_p, sin_p, L // tm, tm, g_mix0, ws, qn, kn, hm)
    u_s, q_s, k_s, v_s, qi_s, ki_s, kw_s, _, _, _ = _inproj(x_s, cos_s, sin_s, 1, DB, g_mix0, ws, qn, kn, hm)

    lbr, lbi, bbr, bbi = _s5_prep(ssm_lambda_re[0], ssm_lambda_im[0], ssm_log_dt[0], ssm_b_re[0], ssm_b_im[0])
    gps = LANES // SSM_GROUP
    n_slab = G // gps
    eye = jnp.eye(gps, dtype=F32)

    def b_slabs(bt):
        return jnp.einsum('sgcp,gh->sgchp', bt.reshape(n_slab, gps, SSM_GROUP, P), eye).reshape(n_slab, LANES, gps * P)

    def c_slabs(c):
        return jnp.einsum('sgcp,gh->sgphc', c.reshape(n_slab, gps, SSM_GROUP, P), eye).reshape(n_slab, gps * P, LANES)

    wb = jnp.concatenate([b_slabs(bbr), b_slabs(bbi)], axis=2)
    wc = jnp.stack([c_slabs(ssm_c_re[0]), -c_slabs(ssm_c_im[0])], axis=1).astype(BF16)
    lbr_f, lbi_f = lbr.reshape(1, n_state), lbi.reshape(1, n_state)
    d_skip = ssm_d[0].reshape(1, d_ssm)
    wglu = ssm_w_glu[0].astype(BF16)
    assert B % SUBLANES == 0, "the S5 scan keeps one batch row per sublane"
    steps = max(d for d in range(1, L + 1) if L % d == 0 and d * B <= 512 and (d * B) % (2 * SUBLANES) == 0)
    u_tm = u_p.reshape(B, L, d_ssm).swapaxes(0, 1).reshape(L * B, d_ssm)
    zero_h = jnp.zeros((B, n_state), F32)
    ys_tm, hre_p, him_p = _s5(u_tm, zero_h, zero_h, lbr_f, lbi_f, wb, wc, d_skip, wglu, B, steps, False)
    ys_p = ys_tm.reshape(L, B, d_ssm).swapaxes(0, 1).reshape(B * L, d_ssm)
    ys_s, hre_s, him_s = _s5(u_s, state_ssm_re[0].reshape(DB, n_state), state_ssm_im[0].reshape(DB, n_state),
                             lbr_f, lbi_f, wb, wc, d_skip, wglu, DB, 1, True)

    ya_p = _dsa_prompt(qi_p, kw_p, q_p, kib_p, kb_p, vb_p, B, L, tm, topk_p)
    assert page == LANES
    w3 = kw_s[:, IDX_DIM:IDX_DIM + N_IDX_HEADS].reshape(DB, N_IDX_HEADS, 1)
    scores = _dsa_sample_scores(page_table, qi_s.astype(F32).reshape(DB, d_qi, 1), w3, ki_s.reshape(DB, IDX_DIM, 1),
                                jnp.transpose(cache_kidx[0], (0, 2, 1)))
    bias = _dsa_sample_select(scores.reshape(DB, (n_pages + 1) * page), past + DS, topk_s)
    ya_s = _dsa_sample_attend(page_table, q_s.astype(F32).reshape(DB, d_att, 1), k_s.reshape(DB, d_att, 1),
                              v_s.reshape(DB, d_att, 1), bias.reshape(DB, n_pages + 1, page),
                              jnp.transpose(cache_k[0], (0, 2, 3, 1)), jnp.transpose(cache_v[0], (0, 2, 3, 1)))
    ya_s = ya_s.reshape(DB, d_att).astype(BF16)

    w_out = w_out0[0].astype(BF16)
    ffn = (w_out[:d_ssm], w_out[d_ssm:], norm_ffn0[0].reshape(1, D), ffn_w_gate[0].astype(BF16),
           ffn_w_up[0].astype(BF16), ffn_w_down[0].astype(BF16))
    x_p = _outproj_ffn(x_p, ys_p, ya_p, *ffn, tm)
    x_s = _outproj_ffn(x_s, ys_s, ya_s, *ffn, DB)

    g_mix1 = norm_mix1[0].reshape(1, D)
    pw = pool_w[0].astype(BF16)
    psc = pool_scale[0].reshape(1, D)
    x_p, pool_p = _pool_prompt(x_p, g_mix1, pw, psc, B, L, tm)
    x_s, hist_t = _pool_sample(x_s, state_pool[0].swapaxes(0, 1), g_mix1, pw, psc)
    pool_s = hist_t.swapaxes(0, 1)

    g_ffn1 = norm_ffn1[0].reshape(1, D)
    wr_pad = jnp.pad(router_w[0], ((0, 0), (0, LANES - E)))
    wg, wu, wd = moe_w_gate[0].astype(BF16), moe_w_up[0].astype(BF16), moe_w_down[0].astype(BF16)
    x_all = jnp.concatenate([x_p, x_s], axis=0)
    T_all = x_all.shape[0]
    tr = _row_block(T_all, 512)
    h_all, gates = _router(x_all, g_ffn1, wr_pad, E, tr)
    x_all = _moe_dense(x_all, h_all, gates, wg, wu, wd, tr)
    x_p, x_s = x_all[:B * L], x_all[B * L:]

    y_prompt = x_p.reshape(B, L, D)[:, N_META:]
    y_sample = x_s.reshape(DB, 1, D)
    k_prompt = k_p.reshape(1, B, L, N_HEADS, HEAD_DIM)
    v_prompt = v_p.reshape(1, B, L, N_HEADS, HEAD_DIM)
    kidx_prompt = ki_p.reshape(1, B, L, IDX_DIM)
    return (y_prompt, y_sample, k_prompt, v_prompt, kidx_prompt,
            hre_p.reshape(1, B, G, P), him_p.reshape(1, B, G, P), pool_p[None],
            k_s.reshape(1, DB, 1, N_HEADS, HEAD_DIM), v_s.reshape(1, DB, 1, N_HEADS, HEAD_DIM),
            ki_s.reshape(1, DB, 1, IDX_DIM), hre_s.reshape(1, DB, G, P), him_s.reshape(1, DB, G, P), pool_s[None])
```

```python
import functools
import math

import jax
import jax.numpy as jnp
from jax import lax
from jax.experimental import pallas as pl
from jax.experimental.pallas import tpu as pltpu

F32 = jnp.float32
BF16 = jnp.bfloat16
I32 = jnp.int32

N_META = 16
SSM_GROUP = 16
SSM_STATE = 64
N_HEADS = 8
HEAD_DIM = 64
N_IDX_HEADS = 8
IDX_DIM = 64
TOPK_MAX = 256
ROPE_THETA = 10000.0
POOL_WINDOWS = (2, 4, 8, 16)
POOL_HIST = max(POOL_WINDOWS) - 1
TOP_K_EXPERTS = 2
EPS = 1e-6

LANES = 128
SUBLANES = 8
VMEM_LIMIT = 56 * 1024 * 1024
MOE_TILE = 256
BISECT_ITERS = 22
NEG_BIG = -1e30


def _params(sem):
    return pltpu.CompilerParams(dimension_semantics=sem, vmem_limit_bytes=VMEM_LIMIT)


def _row_block(n, cap, mult=2 * SUBLANES):
    best = None
    for d in range(mult, min(n, cap) + 1, mult):
        if n % d == 0:
            best = d
    assert best is not None, n
    return best


def _rms(x, g):
    return x * lax.rsqrt(jnp.mean(x * x, axis=-1, keepdims=True) + EPS) * g


def _dot(a, b):
    return jnp.dot(a, b, preferred_element_type=F32)


def _dot_nt(a, b):
    return lax.dot_general(a, b, (((1,), (1,)), ((), ())), preferred_element_type=F32)


def _const_spec(shape):
    nd = len(shape)
    return pl.BlockSpec(shape, lambda *_: (0,) * nd)


def _inproj_body(x_ref, g_ref, wu_ref, wq_ref, wk_ref, wv_ref, wqi_ref, wkw_ref, qn_ref, kn_ref, hm_ref,
                 cos_ref, sin_ref,
                 u_ref, q_ref, k_ref, v_ref, qi_ref, ki_ref, kw_ref, kb_ref, vb_ref, kib_ref):
    h = _rms(x_ref[...], g_ref[...]).astype(BF16)
    cos = cos_ref[...]
    sin = sin_ref[...]
    lane = lax.broadcasted_iota(I32, (1, LANES), 1)
    lo_half = (lane & (HEAD_DIM // 2)) == 0

    def rope(z):
        cols = []
        for j in range(z.shape[1] // LANES):
            zs = z[:, LANES * j:LANES * (j + 1)]
            partner = jnp.where(lo_half, pltpu.roll(zs, LANES - HEAD_DIM // 2, 1), pltpu.roll(zs, HEAD_DIM // 2, 1))
            cols.append(zs * cos + partner * sin)
        return cols[0] if len(cols) == 1 else jnp.concatenate(cols, axis=1)

    def head_norm(z, gn):
        ms = _dot((z * z).astype(BF16), hm_ref[...])
        return z * lax.rsqrt(ms + EPS) * gn

    u_ref[...] = _dot(h, wu_ref[...])
    q = rope(head_norm(_dot(h, wq_ref[...]), qn_ref[...]))
    q_ref[...] = (q * (HEAD_DIM ** -0.5)).astype(BF16)
    k = rope(head_norm(_dot(h, wk_ref[...]), kn_ref[...]))
    k_ref[...] = k
    kb_ref[...] = k.astype(BF16)
    v = _dot(h, wv_ref[...])
    v_ref[...] = v
    vb_ref[...] = v.astype(BF16)
    qi_ref[...] = (rope(_dot(h, wqi_ref[...])) * (IDX_DIM ** -0.5)).astype(BF16)
    zkw = _dot(h, wkw_ref[...])
    kir = rope(zkw)
    kw_ref[...] = jnp.where(lane < IDX_DIM, kir, zkw * (N_IDX_HEADS ** -0.5))
    ki_ref[...] = kir[:, :IDX_DIM]
    kib_ref[...] = jnp.where(lane < IDX_DIM, kir, pltpu.roll(kir, IDX_DIM, 1)).astype(BF16)


def _inproj(x2d, cos_t, sin_t, tab_blocks, tm, g, ws, qn, kn, hm):
    T, D = x2d.shape
    wu, wq, wk, wv, wqi, wkw = ws
    d_ssm, d_att, d_qi = wu.shape[1], wq.shape[1], wqi.shape[1]
    row = lambda w: pl.BlockSpec((tm, w), lambda i: (i, 0))
    tab = pl.BlockSpec((tm, LANES), lambda i: (i % tab_blocks, 0))
    out_shape = (
        jax.ShapeDtypeStruct((T, d_ssm), F32),
        jax.ShapeDtypeStruct((T, d_att), BF16),
        jax.ShapeDtypeStruct((T, d_att), F32),
        jax.ShapeDtypeStruct((T, d_att), F32),
        jax.ShapeDtypeStruct((T, d_qi), BF16),
        jax.ShapeDtypeStruct((T, IDX_DIM), F32),
        jax.ShapeDtypeStruct((T, LANES), F32),
        jax.ShapeDtypeStruct((T, d_att), BF16),
        jax.ShapeDtypeStruct((T, d_att), BF16),
        jax.ShapeDtypeStruct((T, LANES), BF16),
    )
    out_specs = (row(d_ssm), row(d_att), row(d_att), row(d_att), row(d_qi), row(IDX_DIM), row(LANES),
                 row(d_att), row(d_att), row(LANES))
    in_specs = [row(D), _const_spec(g.shape)] + [_const_spec(w.shape) for w in ws] + [
        _const_spec(qn.shape), _const_spec(kn.shape), _const_spec(hm.shape), tab, tab]
    return pl.pallas_call(
        _inproj_body, grid=(T // tm,), in_specs=in_specs, out_specs=out_specs, out_shape=out_shape,
        compiler_params=_params(("arbitrary",)), name="inproj",
    )(x2d, g, *ws, qn, kn, hm, cos_t, sin_t)


def _s5_prep_body(lr_ref, li_ref, ldt_ref, btr_ref, bti_ref, lbr_ref, lbi_ref, bbr_ref, bbi_ref):
    lr = lr_ref[...]
    li = li_ref[...]
    dt = jnp.exp(ldt_ref[...])
    mag = jnp.exp(lr * dt)
    lbr = mag * jnp.cos(li * dt)
    lbi = mag * jnp.sin(li * dt)
    lbr_ref[...] = lbr
    lbi_ref[...] = lbi
    den = lr * lr + li * li
    cr = ((lbr - 1.0) * lr + lbi * li) / den
    ci = (lbi * lr - (lbr - 1.0) * li) / den
    btr = btr_ref[...]
    bti = bti_ref[...]
    bbr_ref[...] = cr * btr - ci * bti
    bbi_ref[...] = cr * bti + ci * btr


def _s5_prep(lam_re, lam_im, log_dt, b_re, b_im):
    G, P = lam_re.shape
    C = b_re.shape[-1]
    btr = jnp.swapaxes(b_re, 1, 2)
    bti = jnp.swapaxes(b_im, 1, 2)
    sds = jax.ShapeDtypeStruct
    return pl.pallas_call(
        _s5_prep_body,
        out_shape=(sds((G, 1, P), F32), sds((G, 1, P), F32), sds((G, C, P), F32), sds((G, C, P), F32)),
        name="s5_prep",
    )(lam_re.reshape(G, 1, P), lam_im.reshape(G, 1, P), log_dt.reshape(G, 1, 1), btr, bti)


def _s5_body(u_ref, h0r_ref, h0i_ref, lbr_ref, lbi_ref, wb_ref, wc_ref, d_ref, wglu_ref,
             y_ref, hr_ref, hi_ref, bu_ref, st_ref, *, nb, steps, precise):
    n_slab = wb_ref.shape[0]
    cin = wb_ref.shape[1]
    sw = wb_ref.shape[2] // 2
    n_state = n_slab * sw

    @pl.when(pl.program_id(0) == 0)
    def _():
        st_ref[0] = h0r_ref[...]
        st_ref[1] = h0i_ref[...]

    u = u_ref[...]
    for i in range(n_slab):
        ui = u[:, cin * i:cin * (i + 1)]
        if precise:
            bu = jnp.dot(ui, wb_ref[i], preferred_element_type=F32, precision=lax.Precision.HIGHEST)
        else:
            bu = _dot(ui.astype(BF16), wb_ref[i].astype(BF16))
        bu_ref[:, sw * i:sw * (i + 1)] = bu[:, :sw]
        bu_ref[:, n_state + sw * i:n_state + sw * (i + 1)] = bu[:, sw:]

    for i in range(n_slab):
        re_cols = slice(sw * i, sw * (i + 1))
        im_cols = slice(n_state + sw * i, n_state + sw * (i + 1))
        lr = jnp.broadcast_to(lbr_ref[:, re_cols], (nb, sw))
        li = jnp.broadcast_to(lbi_ref[:, re_cols], (nb, sw))

        def step(t, carry, re_cols=re_cols, im_cols=im_cols, lr=lr, li=li):
            hr, hi = carry
            rows = pl.ds(pl.multiple_of(t * nb, nb), nb)
            nhr = lr * hr - li * hi + bu_ref[rows, re_cols]
            nhi = lr * hi + li * hr + bu_ref[rows, im_cols]
            bu_ref[rows, re_cols] = nhr
            bu_ref[rows, im_cols] = nhi
            return nhr, nhi

        hr, hi = lax.fori_loop(0, steps, step, (st_ref[0, :, re_cols], st_ref[1, :, re_cols]))
        st_ref[0, :, re_cols] = hr
        st_ref[1, :, re_cols] = hi

    ys = []
    for i in range(n_slab):
        h_re = bu_ref[:, sw * i:sw * (i + 1)].astype(BF16)
        h_im = bu_ref[:, n_state + sw * i:n_state + sw * (i + 1)].astype(BF16)
        ys.append(_dot(h_re, wc_ref[i, 0]) + _dot(h_im, wc_ref[i, 1]))
    y = jnp.concatenate(ys, axis=1) + d_ref[...] * u
    y = 0.5 * y * (1.0 + jnp.tanh(math.sqrt(2.0 / math.pi) * (y + 0.044715 * (y * y * y))))
    y = y * jax.nn.sigmoid(_dot(y.astype(BF16), wglu_ref[...]))
    y_ref[...] = y.astype(BF16)
    hr_ref[...] = st_ref[0]
    hi_ref[...] = st_ref[1]


def _s5(u_tm, h0_re, h0_im, lbr_flat, lbi_flat, wb, wc, d_skip, wglu, nb, steps, precise):
    rows_total, d_ssm = u_tm.shape
    n_state = h0_re.shape[1]
    rows = nb * steps
    sds = jax.ShapeDtypeStruct
    body = functools.partial(_s5_body, nb=nb, steps=steps, precise=precise)
    return pl.pallas_call(
        body, grid=(rows_total // rows,),
        in_specs=[pl.BlockSpec((rows, d_ssm), lambda c: (c, 0)), _const_spec(h0_re.shape), _const_spec(h0_im.shape),
                  _const_spec(lbr_flat.shape), _const_spec(lbi_flat.shape), _const_spec(wb.shape),
                  _const_spec(wc.shape), _const_spec(d_skip.shape), _const_spec(wglu.shape)],
        out_specs=(pl.BlockSpec((rows, d_ssm), lambda c: (c, 0)), _const_spec(h0_re.shape), _const_spec(h0_im.shape)),
        out_shape=(sds((rows_total, d_ssm), BF16), sds(h0_re.shape, F32), sds(h0_im.shape, F32)),
        scratch_shapes=[pltpu.VMEM((rows, 2 * n_state), F32), pltpu.VMEM((2, nb, n_state), F32)],
        compiler_params=_params(("arbitrary",)), name="s5_precise" if precise else "s5",
    )(u_tm, h0_re, h0_im, lbr_flat, lbi_flat, wb, wc, d_skip, wglu)


def _topk_bias(s_ref, n, kk):
    rows = s_ref.shape[0]
    cols = slice(0, n)
    pos = lax.broadcasted_iota(I32, (1, n), 1)

    def count(pred):
        return jnp.sum(jnp.where(pred(s_ref[:, cols]), 1.0, 0.0), axis=1, keepdims=True)

    def largest_below(hi):
        s = s_ref[:, cols]
        return jnp.max(jnp.where(s < hi, s, -jnp.inf), axis=1, keepdims=True)

    s0 = s_ref[:, cols]
    smax = jnp.max(s0, axis=1, keepdims=True)
    smin = jnp.min(jnp.where(s0 > -jnp.inf, s0, jnp.inf), axis=1, keepdims=True)

    def bisect(_, c):
        lo, hi = c
        mid = lo + (hi - lo) * 0.5
        ge = count(lambda s: s >= mid) >= kk
        return jnp.where(ge, mid, lo), jnp.where(ge, hi, mid)

    _, hi = lax.fori_loop(0, BISECT_ITERS, bisect, (smin, smax + (smax - smin) + 1.0))
    thr = largest_below(hi)
    cnt = count(lambda s: s >= thr)

    def short(c):
        return jnp.min(c[2] - kk) < 0.0

    def lower(c):
        hi, thr, cnt = c
        hi = jnp.where(cnt < kk, thr, hi)
        thr = largest_below(hi)
        return hi, thr, count(lambda s: s >= thr)

    _, thr, cnt = lax.while_loop(short, lower, (hi, thr, cnt))
    n_gt = count(lambda s: s > thr)
    need = kk - n_gt
    n_pos_bits = max(1, (n - 1).bit_length())

    def tie_search():
        def pos_step(i, cut):
            cand = cut | lax.shift_left(jnp.int32(1), n_pos_bits - 1 - i)
            below = count(lambda s: jnp.where(s == thr, pos, 2 ** 30) < cand)
            return jnp.where(below < need, cand, cut)

        return lax.fori_loop(0, n_pos_bits, pos_step, jnp.zeros((rows, 1), I32))

    cut = lax.cond(jnp.max(cnt - n_gt - need) > 0.0, tie_search, lambda: jnp.full((rows, 1), 2 ** 30, I32))
    s = s_ref[:, cols]
    keep = (s > thr) | ((s == thr) & (pos <= cut))
    s_ref[:, cols] = jnp.where(keep, 0.0, NEG_BIG)


def _dsa_prompt_block(q0, n, qim_ref, qm_ref, w_ref, kip_ref, kp_ref, vp_ref, s_ref, acc_ref, topk):
    tq = s_ref.shape[0]
    cols = slice(0, n)
    lane = lax.broadcasted_iota(I32, (1, LANES), 1)
    s_ref[:, cols] = jnp.zeros((tq, n), F32)

    def score_head(h, carry):
        d = jnp.maximum(_dot_nt(qim_ref[h], kip_ref[cols, :]), 0.0)
        s_ref[:, cols] += w_ref[h] * d
        return carry

    lax.fori_loop(0, N_IDX_HEADS, score_head, 0)
    qpos = q0 + lax.broadcasted_iota(I32, (tq, 1), 0)
    kpos = lax.broadcasted_iota(I32, (1, n), 1)
    s_ref[:, cols] = jnp.where(kpos <= qpos, s_ref[:, cols], -jnp.inf)
    _topk_bias(s_ref, n, jnp.minimum(qpos + 1, topk).astype(F32))

    def head_pair(sp, carry):
        outs = []
        for hh in range(2):
            lg = _dot_nt(qm_ref[2 * sp + hh], kp_ref[sp, cols, :]) + s_ref[:, cols]
            p = jnp.exp(lg - jnp.max(lg, axis=1, keepdims=True))
            den = jnp.sum(p, axis=1, keepdims=True)
            outs.append(_dot(p.astype(BF16), vp_ref[sp, cols, :]) / den)
        acc_ref[sp] = jnp.where(lane < HEAD_DIM, outs[0], outs[1])
        return carry

    lax.fori_loop(0, acc_ref.shape[0], head_pair, 0)


def _dsa_prompt_body(qi_ref, kw_ref, q_ref, kib_ref, kb_ref, vb_ref, o_ref,
                     kip_ref, kp_ref, vp_ref, qim_ref, qm_ref, w_ref, s_ref, acc_ref, *, seq, topk):
    tq = q_ref.shape[0]
    n_slab, lp, _ = kp_ref.shape
    j = pl.program_id(1)
    lane = lax.broadcasted_iota(I32, (1, LANES), 1)

    @pl.when(j == 0)
    def _():
        kip_ref[0:seq] = kib_ref[...]
        if lp > seq:
            kip_ref[seq:] = jnp.zeros((lp - seq, LANES), BF16)
        for sp in range(n_slab):
            sl = slice(LANES * sp, LANES * (sp + 1))
            kp_ref[sp, 0:seq] = kb_ref[:, sl]
            vp_ref[sp, 0:seq] = vb_ref[:, sl]
            if lp > seq:
                kp_ref[sp, seq:] = jnp.zeros((lp - seq, LANES), BF16)
                vp_ref[sp, seq:] = jnp.zeros((lp - seq, LANES), BF16)

    kw = kw_ref[...]
    for h in range(N_IDX_HEADS):
        head_lanes = (lane // IDX_DIM) == (h % 2)
        sl = slice(LANES * (h // 2), LANES * (h // 2 + 1))
        qim_ref[h] = jnp.where(head_lanes, qi_ref[:, sl], jnp.zeros((), BF16))
        qm_ref[h] = jnp.where(head_lanes, q_ref[:, sl], jnp.zeros((), BF16))
        w_ref[h] = kw[:, IDX_DIM + h:IDX_DIM + h + 1]

    for jj in range(seq // tq):
        n = min(-(-((jj + 1) * tq) // LANES) * LANES, lp)

        @pl.when(j == jj)
        def _(jj=jj, n=n):
            _dsa_prompt_block(jj * tq, n, qim_ref, qm_ref, w_ref, kip_ref, kp_ref, vp_ref, s_ref, acc_ref, topk)

    for sp in range(n_slab):
        o_ref[:, LANES * sp:LANES * (sp + 1)] = acc_ref[sp].astype(BF16)


def _dsa_prompt(qi, kw, q, kib, kb, vb, batch, seq, tq, topk):
    T, d_att = q.shape
    nq = seq // tq
    lp = -(-seq // LANES) * LANES
    n_slab = d_att // LANES
    qrow = lambda w: pl.BlockSpec((tq, w), lambda b, j: (b * nq + j, 0))
    full = lambda w: pl.BlockSpec((None, seq, w), lambda b, j: (b, 0, 0))
    body = functools.partial(_dsa_prompt_body, seq=seq, topk=topk)
    return pl.pallas_call(
        body, grid=(batch, nq),
        in_specs=[qrow(qi.shape[1]), qrow(LANES), qrow(d_att), full(LANES), full(d_att), full(d_att)],
        out_specs=qrow(d_att),
        out_shape=jax.ShapeDtypeStruct((T, d_att), BF16),
        scratch_shapes=[pltpu.VMEM((lp, LANES), BF16), pltpu.VMEM((n_slab, lp, LANES), BF16),
                        pltpu.VMEM((n_slab, lp, LANES), BF16),
                        pltpu.VMEM((N_IDX_HEADS, tq, LANES), BF16), pltpu.VMEM((N_HEADS, tq, LANES), BF16),
                        pltpu.VMEM((N_IDX_HEADS, tq, 1), F32),
                        pltpu.VMEM((tq, lp), F32), pltpu.VMEM((n_slab, tq, LANES), F32)],
        compiler_params=_params(("arbitrary", "arbitrary")), name="dsa_prompt",
    )(qi, kw, q, kib.reshape(batch, seq, LANES), kb.reshape(batch, seq, d_att), vb.reshape(batch, seq, d_att))


def _dsa_sample_scores_body(pt_ref, qi_ref, w_ref, kin_ref, *refs):
    del pt_ref
    page_refs, s_ref = refs[:-1], refs[-1]
    n_pages = len(page_refs)
    idx_dim, page = page_refs[0].shape
    heads = [slice(idx_dim * h, idx_dim * (h + 1)) for h in range(N_IDX_HEADS)]
    qcols = [jnp.broadcast_to(qi_ref[rows, :], (idx_dim, page)) for rows in heads]
    ws = [w_ref[h:h + 1, :] for h in range(N_IDX_HEADS)]

    def score(dot_of_head):
        s = None
        for h in range(N_IDX_HEADS):
            t = ws[h] * jnp.maximum(dot_of_head(h), 0.0)
            s = t if s is None else s + t
        return s

    for p in range(n_pages):
        kt = page_refs[p][...]
        s_ref[p:p + 1, :] = score(lambda h: jnp.sum(qcols[h] * kt, axis=0, keepdims=True))
    kin = kin_ref[...]
    s_new = score(lambda h: jnp.sum(qi_ref[heads[h], :] * kin, axis=0, keepdims=True))
    lane = lax.broadcasted_iota(I32, (1, page), 1)
    s_ref[n_pages:n_pages + 1, :] = jnp.where(lane == 0, s_new, 0.0)


def _dsa_sample_scores(page_table, qi_col, w3, ki_col, kidx_t):
    nb, n_pages = page_table.shape
    idx_dim, page = kidx_t.shape[1:]
    per = lambda shape: pl.BlockSpec((None,) + shape, lambda b, pt: (b, 0, 0))
    page_specs = [pl.BlockSpec((None, idx_dim, page), functools.partial(lambda b, pt, p: (pt[b, p], 0, 0), p=p))
                  for p in range(n_pages)]
    grid_spec = pltpu.PrefetchScalarGridSpec(
        num_scalar_prefetch=1, grid=(nb,),
        in_specs=[per(qi_col.shape[1:]), per(w3.shape[1:]), per(ki_col.shape[1:])] + page_specs,
        out_specs=per((n_pages + 1, page)))
    return pl.pallas_call(
        _dsa_sample_scores_body, grid_spec=grid_spec,
        out_shape=jax.ShapeDtypeStruct((nb, n_pages + 1, page), F32),
        compiler_params=_params(("arbitrary",)), name="dsa_sample_scores",
    )(page_table, qi_col, w3, ki_col, *([kidx_t] * n_pages))


def _dsa_sample_select_body(s_ref, b_ref, *, n_keys, topk):
    rows, width = s_ref.shape
    pos = lax.broadcasted_iota(I32, (1, width), 1)
    b_ref[...] = jnp.where(pos < n_keys, s_ref[...], -jnp.inf)
    _topk_bias(b_ref, width, jnp.full((rows, 1), float(min(topk, n_keys)), F32))


def _dsa_sample_select(scores, n_keys, topk):
    rows, width = scores.shape
    body = functools.partial(_dsa_sample_select_body, n_keys=n_keys, topk=topk)
    return pl.pallas_call(
        body, out_shape=jax.ShapeDtypeStruct((rows, width), F32),
        compiler_params=pltpu.CompilerParams(vmem_limit_bytes=VMEM_LIMIT), name="dsa_sample_select",
    )(scores)


def _dsa_sample_attend_body(pt_ref, q_ref, kn_ref, vn_ref, b_ref, *refs):
    del pt_ref
    o_ref, lg_ref = refs[-2], refs[-1]
    n_pages = (len(refs) - 2) // 2
    k_refs, v_refs = refs[:n_pages], refs[n_pages:2 * n_pages]
    n_heads, hd, page = k_refs[0].shape
    lane = lax.broadcasted_iota(I32, (1, page), 1)
    bias = b_ref[...]
    for h in range(n_heads):
        rows = slice(hd * h, hd * (h + 1))
        qc = jnp.broadcast_to(q_ref[rows, :], (hd, page))
        for p in range(n_pages):
            lg_ref[p:p + 1, :] = jnp.sum(qc * k_refs[p][h], axis=0, keepdims=True)
        lg_new = jnp.sum(q_ref[rows, :] * kn_ref[rows, :], axis=0, keepdims=True)
        lg_ref[n_pages:n_pages + 1, :] = jnp.where(lane == 0, lg_new, 0.0)
        lg = lg_ref[...] + bias
        m = jnp.max(jnp.max(lg, axis=1, keepdims=True), axis=0, keepdims=True)
        pr = jnp.exp(lg - m)
        den = jnp.sum(jnp.sum(pr, axis=1, keepdims=True), axis=0, keepdims=True)
        acc = jnp.zeros((hd, page), F32)
        for p in range(n_pages):
            acc = acc + pr[p:p + 1, :] * v_refs[p][h]
        out = jnp.sum(acc, axis=1, keepdims=True) + pr[n_pages:n_pages + 1, 0:1] * vn_ref[rows, :]
        o_ref[rows, :] = out / den


def _dsa_sample_attend(page_table, q_col, kn_col, vn_col, bias3, k_t, v_t):
    nb, n_pages = page_table.shape
    n_heads, hd, page = k_t.shape[1:]
    per = lambda shape: pl.BlockSpec((None,) + shape, lambda b, pt: (b, 0, 0))
    page_specs = [pl.BlockSpec((None, n_heads, hd, page), functools.partial(lambda b, pt, p: (pt[b, p], 0, 0, 0), p=p))
                  for p in range(n_pages)]
    grid_spec = pltpu.PrefetchScalarGridSpec(
        num_scalar_prefetch=1, grid=(nb,),
        in_specs=[per(q_col.shape[1:]), per(kn_col.shape[1:]), per(vn_col.shape[1:]), per(bias3.shape[1:])] + page_specs * 2,
        out_specs=per(q_col.shape[1:]),
        scratch_shapes=[pltpu.VMEM((n_pages + 1, page), F32)])
    return pl.pallas_call(
        _dsa_sample_attend_body, grid_spec=grid_spec,
        out_shape=jax.ShapeDtypeStruct(q_col.shape, F32),
        compiler_params=_params(("arbitrary",)), name="dsa_sample_attend",
    )(page_table, q_col, kn_col, vn_col, bias3, *([k_t] * n_pages), *([v_t] * n_pages))


def _swiglu(h, wg_ref, wu_ref, wd_ref, n_chunks):
    fc = wg_ref.shape[-1] // n_chunks
    y = None
    for c in range(n_chunks):
        cols = slice(fc * c, fc * (c + 1))
        gate = _dot(h, wg_ref[:, cols])
        act = (gate * jax.nn.sigmoid(gate) * _dot(h, wu_ref[:, cols])).astype(BF16)
        part = _dot(act, wd_ref[cols, :])
        y = part if y is None else y + part
    return y


def _outproj_ffn_body(x_ref, ys_ref, ya_ref, wos_ref, woa_ref, g_ref, wg_ref, wu_ref, wd_ref, o_ref):
    x = x_ref[...] + _dot(ys_ref[...], wos_ref[...]) + _dot(ya_ref[...], woa_ref[...])
    h = _rms(x, g_ref[...]).astype(BF16)
    o_ref[...] = x + _swiglu(h, wg_ref, wu_ref, wd_ref, 2)


def _outproj_ffn(x2d, ys, ya, wos, woa, g, wg, wu, wd, tm):
    T, D = x2d.shape
    row = lambda w: pl.BlockSpec((tm, w), lambda i: (i, 0))
    single = lambda a: pl.BlockSpec(a.shape, lambda i: (0,) * a.ndim, pipeline_mode=pl.Buffered(1))
    return pl.pallas_call(
        _outproj_ffn_body, grid=(T // tm,),
        in_specs=[row(D), row(ys.shape[1]), row(ya.shape[1]), single(wos), single(woa), single(g),
                  single(wg), single(wu), single(wd)],
        out_specs=row(D), out_shape=jax.ShapeDtypeStruct((T, D), F32),
        compiler_params=_params(("arbitrary",)), name="outproj_ffn",
    )(x2d, ys, ya, wos, woa, g, wg, wu, wd)


def _pool_mix(h, window_sum, divisor, pw_ref, scale):
    gd = h.shape[1] // len(POOL_WINDOWS)
    cols = []
    for g, w in enumerate(POOL_WINDOWS):
        pooled = window_sum(g, w) / divisor(w) - h[:, gd * g:gd * (g + 1)]
        cols.append(_dot(pooled.astype(BF16), pw_ref[g]))
    return jnp.concatenate(cols, axis=1) * scale


def _pool_prompt_body(x_ref, g_ref, pw_ref, sc_ref, o_ref, hist_ref, ext_ref):
    tm, D = x_ref.shape
    gd = D // len(POOL_WINDOWS)
    halo = POOL_HIST + 1
    j = pl.program_id(1)

    @pl.when(j == 0)
    def _():
        ext_ref[0:halo] = jnp.zeros((halo, D), F32)

    x = x_ref[...]
    h = _rms(x, g_ref[...])
    ext_ref[halo:] = h
    pos = (j * tm + lax.broadcasted_iota(I32, (tm, 1), 0)).astype(F32)

    def window_sum(g, w):
        acc = h[:, gd * g:gd * (g + 1)]
        for k in range(1, w):
            acc = acc + ext_ref[halo - k:halo - k + tm, gd * g:gd * (g + 1)]
        return acc

    mixed = _pool_mix(h, window_sum, lambda w: jnp.minimum(float(w), pos + 1.0), pw_ref, sc_ref[...])
    o_ref[...] = x + mixed
    ext_ref[0:halo] = ext_ref[tm:tm + halo]

    @pl.when(j == pl.num_programs(1) - 1)
    def _():
        hist_ref[...] = h[tm - POOL_HIST:, :]


def _pool_prompt(x2d, g, pw, scale, batch, seq, tm):
    T, D = x2d.shape
    nblk = seq // tm
    row = pl.BlockSpec((tm, D), lambda b, j: (b * nblk + j, 0))
    return pl.pallas_call(
        _pool_prompt_body, grid=(batch, nblk),
        in_specs=[row, _const_spec(g.shape), _const_spec(pw.shape), _const_spec(scale.shape)],
        out_specs=(row, pl.BlockSpec((None, POOL_HIST, D), lambda b, j: (b, 0, 0))),
        out_shape=(jax.ShapeDtypeStruct((T, D), F32), jax.ShapeDtypeStruct((batch, POOL_HIST, D), F32)),
        scratch_shapes=[pltpu.VMEM((tm + POOL_HIST + 1, D), F32)],
        compiler_params=_params(("arbitrary", "arbitrary")), name="pool_prompt",
    )(x2d, g, pw, scale)


def _pool_sample_body(x_ref, hist_ref, g_ref, pw_ref, sc_ref, o_ref, nh_ref):
    D = x_ref.shape[1]
    gd = D // len(POOL_WINDOWS)
    x = x_ref[...]
    h = _rms(x, g_ref[...])

    def window_sum(g, w):
        acc = h[:, gd * g:gd * (g + 1)]
        for k in range(1, w):
            acc = acc + hist_ref[POOL_HIST - k, :, gd * g:gd * (g + 1)]
        return acc

    mixed = _pool_mix(h, window_sum, float, pw_ref, sc_ref[...])
    o_ref[...] = x + mixed
    for i in range(POOL_HIST - 1):
        nh_ref[i] = hist_ref[i + 1]
    nh_ref[POOL_HIST - 1] = h


def _pool_sample(x2d, hist_t, g, pw, scale):
    sds = jax.ShapeDtypeStruct
    return pl.pallas_call(
        _pool_sample_body, out_shape=(sds(x2d.shape, F32), sds(hist_t.shape, F32)),
        compiler_params=pltpu.CompilerParams(vmem_limit_bytes=VMEM_LIMIT), name="pool_sample",
    )(x2d, hist_t, g, pw, scale)


def _store_row_tiles(ref, x):
    for s in range(ref.shape[1]):
        ref[:, s, :] = x[:, LANES * s:LANES * (s + 1)]


def _load_row_tiles(ref):
    return jnp.concatenate([ref[:, s, :] for s in range(ref.shape[1])], axis=1)


def _router_body(x_ref, g_ref, wr_ref, h_ref, sel_ref, *, n_experts):
    h = _rms(x_ref[...], g_ref[...])
    _store_row_tiles(h_ref, h)
    logits = jnp.dot(h, wr_ref[...], preferred_element_type=F32, precision=lax.Precision.HIGHEST)
    lane = lax.broadcasted_iota(I32, logits.shape, 1).astype(F32)
    logits = jnp.where(lane < n_experts, logits, -jnp.inf)
    v1 = jnp.max(logits, axis=1, keepdims=True)
    i1 = jnp.min(jnp.where(logits == v1, lane, float(LANES)), axis=1, keepdims=True)
    rest = jnp.where(lane == i1, -jnp.inf, logits)
    v2 = jnp.max(rest, axis=1, keepdims=True)
    i2 = jnp.min(jnp.where(rest == v2, lane, float(LANES)), axis=1, keepdims=True)
    e2 = jnp.exp(v2 - v1)
    den = 1.0 + e2
    sel_ref[...] = jnp.where(lane == 0.0, i1, jnp.where(lane == 1.0, i2, jnp.where(lane == 2.0, 1.0 / den, e2 / den)))


def _router(x2d, g, wr_pad, n_experts, tm):
    T, D = x2d.shape
    row = lambda w: pl.BlockSpec((tm, w), lambda i: (i, 0))
    body = functools.partial(_router_body, n_experts=n_experts)
    return pl.pallas_call(
        body, grid=(T // tm,),
        in_specs=[row(D), _const_spec(g.shape), _const_spec(wr_pad.shape)],
        out_specs=(pl.BlockSpec((tm, D // LANES, LANES), lambda i: (i, 0, 0)), row(LANES)),
        out_shape=(jax.ShapeDtypeStruct((T, D // LANES, LANES), F32), jax.ShapeDtypeStruct((T, LANES), F32)),
        compiler_params=_params(("arbitrary",)), name="router",
    )(x2d, g, wr_pad)


def _row_gather(idx_ref, base, src_hbm, dst, sem):
    def issue(r, carry):
        pltpu.make_async_copy(src_hbm.at[idx_ref[base + r]], dst.at[r], sem).start()
        return carry

    lax.fori_loop(0, dst.shape[0], issue, 0)


def _row_gather_wait(src_hbm, dst, sem):
    pltpu.make_async_copy(src_hbm.at[pl.ds(0, dst.shape[0])], dst, sem).wait()


def _moe_group_body(te_ref, nu_ref, tok_ref, h_hbm, gate_ref, wg_ref, wu_ref, wd_ref, y_ref, xbuf, sem):
    del te_ref
    i = pl.program_id(0)
    tm = xbuf.shape[1]
    slot = i % 2

    @pl.when(i == 0)
    def _():
        _row_gather(tok_ref, 0, h_hbm, xbuf.at[0], sem.at[0])

    @pl.when(i + 1 < pl.num_programs(0))
    def _():
        _row_gather(tok_ref, (i + 1) * tm, h_hbm, xbuf.at[1 - slot], sem.at[1 - slot])

    _row_gather_wait(h_hbm, xbuf.at[slot], sem.at[slot])

    @pl.when(i < nu_ref[0])
    def _():
        x = _load_row_tiles(xbuf.at[slot]).astype(BF16)
        _store_row_tiles(y_ref, _swiglu(x, wg_ref, wu_ref, wd_ref, 2) * gate_ref[...])

    @pl.when(i >= nu_ref[0])
    def _():
        y_ref[...] = jnp.zeros(y_ref.shape, F32)


def _moe_group(tile_expert, n_used, src_tok, h_tiles, row_gate, wg, wu, wd, tm):
    P = src_tok.shape[0]
    _, D, F = wg.shape
    S = h_tiles.shape[1]
    wspec = lambda a, b: pl.BlockSpec((None, a, b), lambda i, te, nu, tok: (te[i], 0, 0), pipeline_mode=pl.Buffered(1))
    grid_spec = pltpu.PrefetchScalarGridSpec(
        num_scalar_prefetch=3, grid=(P // tm,),
        in_specs=[pl.BlockSpec(memory_space=pl.ANY), pl.BlockSpec((tm, 1), lambda i, te, nu, tok: (i, 0)),
                  wspec(D, F), wspec(D, F), wspec(F, D)],
        out_specs=pl.BlockSpec((tm, S, LANES), lambda i, te, nu, tok: (i, 0, 0)),
        scratch_shapes=[pltpu.VMEM((2, tm, S, LANES), F32), pltpu.SemaphoreType.DMA((2,))])
    return pl.pallas_call(
        _moe_group_body, grid_spec=grid_spec,
        out_shape=jax.ShapeDtypeStruct((P, S, LANES), F32),
        compiler_params=pltpu.CompilerParams(dimension_semantics=("arbitrary",), vmem_limit_bytes=VMEM_LIMIT,
                                             disable_bounds_checks=True),
        name="moe_group",
    )(tile_expert, n_used, src_tok, h_tiles, row_gate, wg, wu, wd)


def _moe_combine_body(p1_ref, p2_ref, x_ref, y_hbm, o_ref, buf, sem):
    i = pl.program_id(0)
    tm = x_ref.shape[0]
    slot = i % 2

    def gather(tile, s):
        _row_gather(p1_ref, tile * tm, y_hbm, buf.at[s, 0], sem.at[s, 0])
        _row_gather(p2_ref, tile * tm, y_hbm, buf.at[s, 1], sem.at[s, 1])

    @pl.when(i == 0)
    def _():
        gather(0, 0)

    @pl.when(i + 1 < pl.num_programs(0))
    def _():
        gather(i + 1, 1 - slot)

    _row_gather_wait(y_hbm, buf.at[slot, 0], sem.at[slot, 0])
    _row_gather_wait(y_hbm, buf.at[slot, 1], sem.at[slot, 1])
    o_ref[...] = x_ref[...] + _load_row_tiles(buf.at[slot, 0]) + _load_row_tiles(buf.at[slot, 1])


def _moe_combine(pos1, pos2, x2d, y_tiles, tm):
    T, D = x2d.shape
    S = y_tiles.shape[1]
    row = pl.BlockSpec((tm, D), lambda i, p1, p2: (i, 0))
    grid_spec = pltpu.PrefetchScalarGridSpec(
        num_scalar_prefetch=2, grid=(T // tm,),
        in_specs=[row, pl.BlockSpec(memory_space=pl.ANY)], out_specs=row,
        scratch_shapes=[pltpu.VMEM((2, 2, tm, S, LANES), F32), pltpu.SemaphoreType.DMA((2, 2))])
    return pl.pallas_call(
        _moe_combine_body, grid_spec=grid_spec, out_shape=jax.ShapeDtypeStruct((T, D), F32),
        compiler_params=pltpu.CompilerParams(dimension_semantics=("arbitrary",), vmem_limit_bytes=VMEM_LIMIT,
                                             disable_bounds_checks=True),
        name="moe_combine",
    )(pos1, pos2, x2d, y_tiles)


def _moe_plan(sel, n_experts, tm):
    T = sel.shape[0]
    experts = jnp.concatenate([sel[:, 0], sel[:, 1]]).astype(I32)
    gates = jnp.concatenate([sel[:, 2], sel[:, 3]])
    tokens = jnp.tile(jnp.arange(T, dtype=I32), 2)
    onehot = (experts[:, None] == jnp.arange(n_experts, dtype=I32)[None, :]).astype(I32)
    rank = jnp.cumsum(onehot, axis=0) - onehot
    counts = jnp.sum(onehot, axis=0)
    padded = (counts + tm - 1) // tm * tm
    ends = jnp.cumsum(padded)
    pos = jnp.sum(onehot * ((ends - padded)[None, :] + rank), axis=1)
    n_rows = (2 * T + n_experts * (tm - 1)) // tm * tm
    src_tok = jnp.zeros((n_rows,), I32).at[pos].set(tokens)
    row_gate = jnp.zeros((n_rows,), F32).at[pos].set(gates)
    tile_start = jnp.arange(n_rows // tm, dtype=I32) * tm
    tile_expert = jnp.minimum(jnp.sum((tile_start[:, None] >= ends[None, :]).astype(I32), axis=1), n_experts - 1)
    n_used = (ends[-1] // tm).astype(I32).reshape(1)
    return tile_expert, n_used, src_tok, row_gate.reshape(n_rows, 1), pos[:T], pos[T:]


def _rope_tables(pos):
    half = HEAD_DIM // 2
    inv = ROPE_THETA ** (-jnp.arange(half, dtype=F32) / half)
    ang = pos[:, None] * inv[None, :]
    cos, sin = jnp.cos(ang), jnp.sin(ang)
    reps = LANES // HEAD_DIM
    return jnp.tile(jnp.concatenate([cos, cos], axis=1), (1, reps)), jnp.tile(jnp.concatenate([-sin, sin], axis=1), (1, reps))


def kernel(x_prompt, x_sample, cache_k, cache_v, cache_kidx, page_table, state_ssm_re, state_ssm_im, state_pool, meta_tokens, norm_mix0, w_in0, q_norm, k_norm, ssm_lambda_re, ssm_lambda_im, ssm_log_dt, ssm_b_re, ssm_b_im, ssm_c_re, ssm_c_im, ssm_d, ssm_w_glu, w_out0, norm_ffn0, ffn_w_gate, ffn_w_up, ffn_w_down, norm_mix1, pool_w, pool_scale, norm_ffn1, router_w, moe_w_gate, moe_w_up, moe_w_down):
    B, S, D = x_prompt.shape
    DB, DS, _ = x_sample.shape
    assert DS == 1, "one new token per sample sequence"
    L = S + N_META
    n_pool, page = cache_k.shape[1], cache_k.shape[2]
    n_pages = page_table.shape[1]
    past = n_pages * page
    d_att = N_HEADS * HEAD_DIM
    d_qi = N_IDX_HEADS * IDX_DIM
    G, P = ssm_lambda_re.shape[1:]
    d_ssm = G * SSM_GROUP
    n_state = G * P
    E = router_w.shape[-1]
    topk_p = min(TOPK_MAX, S // 4)
    topk_s = min(TOPK_MAX, (past + DS) // 4)
    tm = _row_block(L, 1024)
    assert tm >= POOL_HIST + 1

    x_p = jnp.concatenate([jnp.broadcast_to(meta_tokens[None], (B, N_META, D)), x_prompt], axis=1).reshape(B * L, D)
    x_s = x_sample.reshape(DB, D)

    w_in = w_in0[0]
    o1, o2, o3, o4, o5, o6 = d_ssm, d_ssm + d_att, d_ssm + 2 * d_att, d_ssm + 3 * d_att, d_ssm + 3 * d_att + d_qi, d_ssm + 3 * d_att + d_qi + IDX_DIM
    w_kw = jnp.pad(w_in[:, o5:], ((0, 0), (0, LANES - (w_in.shape[1] - o5))))
    ws = tuple(w.astype(BF16) for w in (w_in[:, :o1], w_in[:, o1:o2], w_in[:, o2:o3], w_in[:, o3:o4], w_in[:, o4:o5], w_kw))
    g_mix0 = norm_mix0[0].reshape(1, D)
    qn = jnp.tile(q_norm[0], N_HEADS).reshape(1, d_att)
    kn = jnp.tile(k_norm[0], N_HEADS).reshape(1, d_att)
    head_of = jnp.arange(d_att) // HEAD_DIM
    hm = jnp.where(head_of[:, None] == head_of[None, :], 1.0 / HEAD_DIM, 0.0).astype(BF16)
    cos_p, sin_p = _rope_tables(jnp.arange(L, dtype=F32))
    cos_s, sin_s = _rope_tables(jnp.full((DB,), float(past), F32))

    u_p, q_p, k_p, v_p, qi_p, ki_p, kw_p, kb_p, vb_p, kib_p = _inproj(x_p, cos_p, sin_p, L // tm, tm, g_mix0, ws, qn, kn, hm)
    u_s, q_s, k_s, v_s, qi_s, ki_s, kw_s, _, _, _ = _inproj(x_s, cos_s, sin_s, 1, DB, g_mix0, ws, qn, kn, hm)

    lbr, lbi, bbr, bbi = _s5_prep(ssm_lambda_re[0], ssm_lambda_im[0], ssm_log_dt[0], ssm_b_re[0], ssm_b_im[0])
    gps = LANES // SSM_GROUP
    n_slab = G // gps
    eye = jnp.eye(gps, dtype=F32)

    def b_slabs(bt):
        return jnp.einsum('sgcp,gh->sgchp', bt.reshape(n_slab, gps, SSM_GROUP, P), eye).reshape(n_slab, LANES, gps * P)

    def c_slabs(c):
        return jnp.einsum('sgcp,gh->sgphc', c.reshape(n_slab, gps, SSM_GROUP, P), eye).reshape(n_slab, gps * P, LANES)

    wb = jnp.concatenate([b_slabs(bbr), b_slabs(bbi)], axis=2)
    wc = jnp.stack([c_slabs(ssm_c_re[0]), -c_slabs(ssm_c_im[0])], axis=1).astype(BF16)
    lbr_f, lbi_f = lbr.reshape(1, n_state), lbi.reshape(1, n_state)
    d_skip = ssm_d[0].reshape(1, d_ssm)
    wglu = ssm_w_glu[0].astype(BF16)
    assert B % SUBLANES == 0, "the S5 scan keeps one batch row per sublane"
    steps = max(d for d in range(1, L + 1) if L % d == 0 and d * B <= 512 and (d * B) % (2 * SUBLANES) == 0)
    u_tm = u_p.reshape(B, L, d_ssm).swapaxes(0, 1).reshape(L * B, d_ssm)
    zero_h = jnp.zeros((B, n_state), F32)
    ys_tm, hre_p, him_p = _s5(u_tm, zero_h, zero_h, lbr_f, lbi_f, wb, wc, d_skip, wglu, B, steps, False)
    ys_p = ys_tm.reshape(L, B, d_ssm).swapaxes(0, 1).reshape(B * L, d_ssm)
    ys_s, hre_s, him_s = _s5(u_s, state_ssm_re[0].reshape(DB, n_state), state_ssm_im[0].reshape(DB, n_state),
                             lbr_f, lbi_f, wb, wc, d_skip, wglu, DB, 1, True)

    ya_p = _dsa_prompt(qi_p, kw_p, q_p, kib_p, kb_p, vb_p, B, L, tm, topk_p)
    assert page == LANES
    w3 = kw_s[:, IDX_DIM:IDX_DIM + N_IDX_HEADS].reshape(DB, N_IDX_HEADS, 1)
    scores = _dsa_sample_scores(page_table, qi_s.astype(F32).reshape(DB, d_qi, 1), w3, ki_s.reshape(DB, IDX_DIM, 1),
                                jnp.transpose(cache_kidx[0], (0, 2, 1)))
    bias = _dsa_sample_select(scores.reshape(DB, (n_pages + 1) * page), past + DS, topk_s)
    ya_s = _dsa_sample_attend(page_table, q_s.astype(F32).reshape(DB, d_att, 1), k_s.reshape(DB, d_att, 1),
                              v_s.reshape(DB, d_att, 1), bias.reshape(DB, n_pages + 1, page),
                              jnp.transpose(cache_k[0], (0, 2, 3, 1)), jnp.transpose(cache_v[0], (0, 2, 3, 1)))
    ya_s = ya_s.reshape(DB, d_att).astype(BF16)

    w_out = w_out0[0].astype(BF16)
    ffn = (w_out[:d_ssm], w_out[d_ssm:], norm_ffn0[0].reshape(1, D), ffn_w_gate[0].astype(BF16),
           ffn_w_up[0].astype(BF16), ffn_w_down[0].astype(BF16))
    x_p = _outproj_ffn(x_p, ys_p, ya_p, *ffn, tm)
    x_s = _outproj_ffn(x_s, ys_s, ya_s, *ffn, DB)

    g_mix1 = norm_mix1[0].reshape(1, D)
    pw = pool_w[0].astype(BF16)
    psc = pool_scale[0].reshape(1, D)
    x_p, pool_p = _pool_prompt(x_p, g_mix1, pw, psc, B, L, tm)
    x_s, hist_t = _pool_sample(x_s, state_pool[0].swapaxes(0, 1), g_mix1, pw, psc)
    pool_s = hist_t.swapaxes(0, 1)

    g_ffn1 = norm_ffn1[0].reshape(1, D)
    wr_pad = jnp.pad(router_w[0], ((0, 0), (0, LANES - E)))
    wg, wu, wd = moe_w_gate[0].astype(BF16), moe_w_up[0].astype(BF16), moe_w_down[0].astype(BF16)
    x_all = jnp.concatenate([x_p, x_s], axis=0)
    T_all = x_all.shape[0]
    tr = _row_block(T_all, MOE_TILE)
    h_tiles, sel = _router(x_all, g_ffn1, wr_pad, E, tr)
    tile_expert, n_used, src_tok, row_gate, pos1, pos2 = _moe_plan(sel, E, MOE_TILE)
    y_tiles = _moe_group(tile_expert, n_used, src_tok, h_tiles, row_gate, wg, wu, wd, MOE_TILE)
    x_all = _moe_combine(pos1, pos2, x_all, y_tiles, tr)
    x_p, x_s = x_all[:B * L], x_all[B * L:]

    y_prompt = x_p.reshape(B, L, D)[:, N_META:]
    y_sample = x_s.reshape(DB, 1, D)
    k_prompt = k_p.reshape(1, B, L, N_HEADS, HEAD_DIM)
    v_prompt = v_p.reshape(1, B, L, N_HEADS, HEAD_DIM)
    kidx_prompt = ki_p.reshape(1, B, L, IDX_DIM)
    return (y_prompt, y_sample, k_prompt, v_prompt, kidx_prompt,
            hre_p.reshape(1, B, G, P), him_p.reshape(1, B, G, P), pool_p[None],
            k_s.reshape(1, DB, 1, N_HEADS, HEAD_DIM), v_s.reshape(1, DB, 1, N_HEADS, HEAD_DIM),
            ki_s.reshape(1, DB, 1, IDX_DIM), hre_s.reshape(1, DB, G, P), him_s.reshape(1, DB, G, P), pool_s[None])
```

```python
import functools
import math

import jax
import jax.numpy as jnp
from jax import lax
from jax.experimental import pallas as pl
from jax.experimental.pallas import tpu as pltpu

F32 = jnp.float32
BF16 = jnp.bfloat16
I32 = jnp.int32

N_META = 16
SSM_GROUP = 16
SSM_STATE = 64
N_HEADS = 8
HEAD_DIM = 64
N_IDX_HEADS = 8
IDX_DIM = 64
TOPK_MAX = 256
ROPE_THETA = 10000.0
POOL_WINDOWS = (2, 4, 8, 16)
POOL_HIST = max(POOL_WINDOWS) - 1
TOP_K_EXPERTS = 2
EPS = 1e-6

LANES = 128
SUBLANES = 8
VMEM_LIMIT = 56 * 1024 * 1024
MOE_TILE = 256
BISECT_ITERS = 22
NEG_BIG = -1e30


def _params(sem):
    return pltpu.CompilerParams(dimension_semantics=sem, vmem_limit_bytes=VMEM_LIMIT)


def _row_block(n, cap, mult=2 * SUBLANES):
    best = None
    for d in range(mult, min(n, cap) + 1, mult):
        if n % d == 0:
            best = d
    assert best is not None, n
    return best


def _rms(x, g):
    return x * lax.rsqrt(jnp.mean(x * x, axis=-1, keepdims=True) + EPS) * g


def _dot(a, b):
    return jnp.dot(a, b, preferred_element_type=F32)


def _dot_nt(a, b):
    return lax.dot_general(a, b, (((1,), (1,)), ((), ())), preferred_element_type=F32)


def _const_spec(shape):
    nd = len(shape)
    return pl.BlockSpec(shape, lambda *_: (0,) * nd)


def _inproj_body(x_ref, g_ref, wu_ref, wq_ref, wk_ref, wv_ref, wqi_ref, wkw_ref, qn_ref, kn_ref, hm_ref,
                 cos_ref, sin_ref,
                 u_ref, q_ref, k_ref, v_ref, qi_ref, ki_ref, kw_ref, kb_ref, vb_ref, kib_ref):
    h = _rms(x_ref[...], g_ref[...]).astype(BF16)
    cos = cos_ref[...]
    sin = sin_ref[...]
    lane = lax.broadcasted_iota(I32, (1, LANES), 1)
    lo_half = (lane & (HEAD_DIM // 2)) == 0

    def rope(z):
        cols = []
        for j in range(z.shape[1] // LANES):
            zs = z[:, LANES * j:LANES * (j + 1)]
            partner = jnp.where(lo_half, pltpu.roll(zs, LANES - HEAD_DIM // 2, 1), pltpu.roll(zs, HEAD_DIM // 2, 1))
            cols.append(zs * cos + partner * sin)
        return cols[0] if len(cols) == 1 else jnp.concatenate(cols, axis=1)

    def head_norm(z, gn):
        ms = _dot((z * z).astype(BF16), hm_ref[...])
        return z * lax.rsqrt(ms + EPS) * gn

    u_ref[...] = _dot(h, wu_ref[...])
    q = rope(head_norm(_dot(h, wq_ref[...]), qn_ref[...]))
    q_ref[...] = (q * (HEAD_DIM ** -0.5)).astype(BF16)
    k = rope(head_norm(_dot(h, wk_ref[...]), kn_ref[...]))
    k_ref[...] = k
    kb_ref[...] = k.astype(BF16)
    v = _dot(h, wv_ref[...])
    v_ref[...] = v
    vb_ref[...] = v.astype(BF16)
    qi_ref[...] = (rope(_dot(h, wqi_ref[...])) * (IDX_DIM ** -0.5)).astype(BF16)
    zkw = _dot(h, wkw_ref[...])
    kir = rope(zkw)
    kw_ref[...] = jnp.where(lane < IDX_DIM, kir, zkw * (N_IDX_HEADS ** -0.5))
    ki_ref[...] = kir[:, :IDX_DIM]
    kib_ref[...] = jnp.where(lane < IDX_DIM, kir, pltpu.roll(kir, IDX_DIM, 1)).astype(BF16)


def _inproj(x2d, cos_t, sin_t, tab_blocks, tm, g, ws, qn, kn, hm):
    T, D = x2d.shape
    wu, wq, wk, wv, wqi, wkw = ws
    d_ssm, d_att, d_qi = wu.shape[1], wq.shape[1], wqi.shape[1]
    row = lambda w: pl.BlockSpec((tm, w), lambda i: (i, 0))
    tab = pl.BlockSpec((tm, LANES), lambda i: (i % tab_blocks, 0))
    out_shape = (
        jax.ShapeDtypeStruct((T, d_ssm), F32),
        jax.ShapeDtypeStruct((T, d_att), BF16),
        jax.ShapeDtypeStruct((T, d_att), F32),
        jax.ShapeDtypeStruct((T, d_att), F32),
        jax.ShapeDtypeStruct((T, d_qi), BF16),
        jax.ShapeDtypeStruct((T, IDX_DIM), F32),
        jax.ShapeDtypeStruct((T, LANES), F32),
        jax.ShapeDtypeStruct((T, d_att), BF16),
        jax.ShapeDtypeStruct((T, d_att), BF16),
        jax.ShapeDtypeStruct((T, LANES), BF16),
    )
    out_specs = (row(d_ssm), row(d_att), row(d_att), row(d_att), row(d_qi), row(IDX_DIM), row(LANES),
                 row(d_att), row(d_att), row(LANES))
    in_specs = [row(D), _const_spec(g.shape)] + [_const_spec(w.shape) for w in ws] + [
        _const_spec(qn.shape), _const_spec(kn.shape), _const_spec(hm.shape), tab, tab]
    return pl.pallas_call(
        _inproj_body, grid=(T // tm,), in_specs=in_specs, out_specs=out_specs, out_shape=out_shape,
        compiler_params=_params(("arbitrary",)), name="inproj",
    )(x2d, g, *ws, qn, kn, hm, cos_t, sin_t)


def _s5_prep_body(lr_ref, li_ref, ldt_ref, btr_ref, bti_ref, lbr_ref, lbi_ref, bbr_ref, bbi_ref):
    lr = lr_ref[...]
    li = li_ref[...]
    dt = jnp.exp(ldt_ref[...])
    mag = jnp.exp(lr * dt)
    lbr = mag * jnp.cos(li * dt)
    lbi = mag * jnp.sin(li * dt)
    lbr_ref[...] = lbr
    lbi_ref[...] = lbi
    den = lr * lr + li * li
    cr = ((lbr - 1.0) * lr + lbi * li) / den
    ci = (lbi * lr - (lbr - 1.0) * li) / den
    btr = btr_ref[...]
    bti = bti_ref[...]
    bbr_ref[...] = cr * btr - ci * bti
    bbi_ref[...] = cr * bti + ci * btr


def _s5_prep(lam_re, lam_im, log_dt, b_re, b_im):
    G, P = lam_re.shape
    C = b_re.shape[-1]
    btr = jnp.swapaxes(b_re, 1, 2)
    bti = jnp.swapaxes(b_im, 1, 2)
    sds = jax.ShapeDtypeStruct
    return pl.pallas_call(
        _s5_prep_body,
        out_shape=(sds((G, 1, P), F32), sds((G, 1, P), F32), sds((G, C, P), F32), sds((G, C, P), F32)),
        name="s5_prep",
    )(lam_re.reshape(G, 1, P), lam_im.reshape(G, 1, P), log_dt.reshape(G, 1, 1), btr, bti)


def _s5_body(u_ref, h0r_ref, h0i_ref, lbr_ref, lbi_ref, wb_ref, wc_ref, d_ref, wglu_ref,
             y_ref, hr_ref, hi_ref, bu_ref, st_ref, *, nb, steps, precise):
    n_slab = wb_ref.shape[0]
    cin = wb_ref.shape[1]
    sw = wb_ref.shape[2] // 2
    n_state = n_slab * sw

    @pl.when(pl.program_id(0) == 0)
    def _():
        st_ref[0] = h0r_ref[...]
        st_ref[1] = h0i_ref[...]

    u = u_ref[...]
    for i in range(n_slab):
        ui = u[:, cin * i:cin * (i + 1)]
        if precise:
            bu = jnp.dot(ui, wb_ref[i], preferred_element_type=F32, precision=lax.Precision.HIGHEST)
        else:
            bu = _dot(ui.astype(BF16), wb_ref[i].astype(BF16))
        bu_ref[:, sw * i:sw * (i + 1)] = bu[:, :sw]
        bu_ref[:, n_state + sw * i:n_state + sw * (i + 1)] = bu[:, sw:]

    for i in range(n_slab):
        re_cols = slice(sw * i, sw * (i + 1))
        im_cols = slice(n_state + sw * i, n_state + sw * (i + 1))
        lr = jnp.broadcast_to(lbr_ref[:, re_cols], (nb, sw))
        li = jnp.broadcast_to(lbi_ref[:, re_cols], (nb, sw))

        def step(t, carry, re_cols=re_cols, im_cols=im_cols, lr=lr, li=li):
            hr, hi = carry
            rows = pl.ds(pl.multiple_of(t * nb, nb), nb)
            nhr = lr * hr - li * hi + bu_ref[rows, re_cols]
            nhi = lr * hi + li * hr + bu_ref[rows, im_cols]
            bu_ref[rows, re_cols] = nhr
            bu_ref[rows, im_cols] = nhi
            return nhr, nhi

        hr, hi = lax.fori_loop(0, steps, step, (st_ref[0, :, re_cols], st_ref[1, :, re_cols]))
        st_ref[0, :, re_cols] = hr
        st_ref[1, :, re_cols] = hi

    ys = []
    for i in range(n_slab):
        h_re = bu_ref[:, sw * i:sw * (i + 1)].astype(BF16)
        h_im = bu_ref[:, n_state + sw * i:n_state + sw * (i + 1)].astype(BF16)
        ys.append(_dot(h_re, wc_ref[i, 0]) + _dot(h_im, wc_ref[i, 1]))
    y = jnp.concatenate(ys, axis=1) + d_ref[...] * u
    y = 0.5 * y * (1.0 + jnp.tanh(math.sqrt(2.0 / math.pi) * (y + 0.044715 * (y * y * y))))
    y = y * jax.nn.sigmoid(_dot(y.astype(BF16), wglu_ref[...]))
    y_ref[...] = y.astype(BF16)
    hr_ref[...] = st_ref[0]
    hi_ref[...] = st_ref[1]


def _s5(u_tm, h0_re, h0_im, lbr_flat, lbi_flat, wb, wc, d_skip, wglu, nb, steps, precise):
    rows_total, d_ssm = u_tm.shape
    n_state = h0_re.shape[1]
    rows = nb * steps
    sds = jax.ShapeDtypeStruct
    body = functools.partial(_s5_body, nb=nb, steps=steps, precise=precise)
    return pl.pallas_call(
        body, grid=(rows_total // rows,),
        in_specs=[pl.BlockSpec((rows, d_ssm), lambda c: (c, 0)), _const_spec(h0_re.shape), _const_spec(h0_im.shape),
                  _const_spec(lbr_flat.shape), _const_spec(lbi_flat.shape), _const_spec(wb.shape),
                  _const_spec(wc.shape), _const_spec(d_skip.shape), _const_spec(wglu.shape)],
        out_specs=(pl.BlockSpec((rows, d_ssm), lambda c: (c, 0)), _const_spec(h0_re.shape), _const_spec(h0_im.shape)),
        out_shape=(sds((rows_total, d_ssm), BF16), sds(h0_re.shape, F32), sds(h0_im.shape, F32)),
        scratch_shapes=[pltpu.VMEM((rows, 2 * n_state), F32), pltpu.VMEM((2, nb, n_state), F32)],
        compiler_params=_params(("arbitrary",)), name="s5_precise" if precise else "s5",
    )(u_tm, h0_re, h0_im, lbr_flat, lbi_flat, wb, wc, d_skip, wglu)


def _topk_bias(s_ref, n, kk):
    rows = s_ref.shape[0]
    cols = slice(0, n)

    def count(pred):
        return jnp.sum(jnp.where(pred(s_ref[:, cols]), 1.0, 0.0), axis=1, keepdims=True)

    def largest_below(hi):
        s = s_ref[:, cols]
        return jnp.max(jnp.where(s < hi, s, -jnp.inf), axis=1, keepdims=True)

    s0 = s_ref[:, cols]
    smax = jnp.max(s0, axis=1, keepdims=True)
    smin = jnp.min(jnp.where(s0 > -jnp.inf, s0, jnp.inf), axis=1, keepdims=True)

    def bisect(_, c):
        lo, hi = c
        mid = lo + (hi - lo) * 0.5
        ge = count(lambda s: s >= mid) >= kk
        return jnp.where(ge, mid, lo), jnp.where(ge, hi, mid)

    _, hi = lax.fori_loop(0, BISECT_ITERS, bisect, (smin, smax + (smax - smin) + 1.0))
    thr = largest_below(hi)
    cnt = count(lambda s: s >= thr)

    def short(c):
        return jnp.min(c[2] - kk) < 0.0

    def lower(c):
        hi, thr, cnt = c
        hi = jnp.where(cnt < kk, thr, hi)
        thr = largest_below(hi)
        return hi, thr, count(lambda s: s >= thr)

    _, thr, _ = lax.while_loop(short, lower, (hi, thr, cnt))
    need = kk - count(lambda s: s > thr)
    blk = 2 * LANES
    tri = (lax.broadcasted_iota(I32, (blk, blk), 0) <= lax.broadcasted_iota(I32, (blk, blk), 1)).astype(BF16)
    seen = jnp.zeros((rows, 1), F32)
    for c0 in range(0, n, blk):
        w = min(blk, n - c0)
        s = s_ref[:, c0:c0 + w]
        tie = s == thr
        upto = seen + _dot(jnp.where(tie, 1.0, 0.0).astype(BF16), tri[:w, :w])
        seen = upto[:, w - 1:w]
        s_ref[:, c0:c0 + w] = jnp.where((s > thr) | (tie & (upto <= need)), 0.0, NEG_BIG)


def _dsa_prompt_block(q0, n, qim_ref, qm_ref, w_ref, kip_ref, kp_ref, vp_ref, s_ref, acc_ref, topk):
    tq = s_ref.shape[0]
    cols = slice(0, n)
    lane = lax.broadcasted_iota(I32, (1, LANES), 1)
    s_ref[:, cols] = jnp.zeros((tq, n), F32)

    def score_head(h, carry):
        d = jnp.maximum(_dot_nt(qim_ref[h], kip_ref[cols, :]), 0.0)
        s_ref[:, cols] += w_ref[h] * d
        return carry

    lax.fori_loop(0, N_IDX_HEADS, score_head, 0)
    qpos = q0 + lax.broadcasted_iota(I32, (tq, 1), 0)
    kpos = lax.broadcasted_iota(I32, (1, n), 1)
    s_ref[:, cols] = jnp.where(kpos <= qpos, s_ref[:, cols], -jnp.inf)
    _topk_bias(s_ref, n, jnp.minimum(qpos + 1, topk).astype(F32))

    def head_pair(sp, carry):
        outs = []
        for hh in range(2):
            lg = _dot_nt(qm_ref[2 * sp + hh], kp_ref[sp, cols, :]) + s_ref[:, cols]
            p = jnp.exp(lg - jnp.max(lg, axis=1, keepdims=True))
            den = jnp.sum(p, axis=1, keepdims=True)
            outs.append(_dot(p.astype(BF16), vp_ref[sp, cols, :]) / den)
        acc_ref[sp] = jnp.where(lane < HEAD_DIM, outs[0], outs[1])
        return carry

    lax.fori_loop(0, acc_ref.shape[0], head_pair, 0)


def _dsa_prompt_body(qi_ref, kw_ref, q_ref, kib_ref, kb_ref, vb_ref, o_ref,
                     kip_ref, kp_ref, vp_ref, qim_ref, qm_ref, w_ref, s_ref, acc_ref, *, seq, topk):
    tq = q_ref.shape[0]
    n_slab, lp, _ = kp_ref.shape
    j = pl.program_id(1)
    lane = lax.broadcasted_iota(I32, (1, LANES), 1)

    @pl.when(j == 0)
    def _():
        kip_ref[0:seq] = kib_ref[...]
        if lp > seq:
            kip_ref[seq:] = jnp.zeros((lp - seq, LANES), BF16)
        for sp in range(n_slab):
            sl = slice(LANES * sp, LANES * (sp + 1))
            kp_ref[sp, 0:seq] = kb_ref[:, sl]
            vp_ref[sp, 0:seq] = vb_ref[:, sl]
            if lp > seq:
                kp_ref[sp, seq:] = jnp.zeros((lp - seq, LANES), BF16)
                vp_ref[sp, seq:] = jnp.zeros((lp - seq, LANES), BF16)

    kw = kw_ref[...]
    for h in range(N_IDX_HEADS):
        head_lanes = (lane // IDX_DIM) == (h % 2)
        sl = slice(LANES * (h // 2), LANES * (h // 2 + 1))
        qim_ref[h] = jnp.where(head_lanes, qi_ref[:, sl], jnp.zeros((), BF16))
        qm_ref[h] = jnp.where(head_lanes, q_ref[:, sl], jnp.zeros((), BF16))
        w_ref[h] = kw[:, IDX_DIM + h:IDX_DIM + h + 1]

    for jj in range(seq // tq):
        n = min(-(-((jj + 1) * tq) // LANES) * LANES, lp)

        @pl.when(j == jj)
        def _(jj=jj, n=n):
            _dsa_prompt_block(jj * tq, n, qim_ref, qm_ref, w_ref, kip_ref, kp_ref, vp_ref, s_ref, acc_ref, topk)

    for sp in range(n_slab):
        o_ref[:, LANES * sp:LANES * (sp + 1)] = acc_ref[sp].astype(BF16)


def _dsa_prompt(qi, kw, q, kib, kb, vb, batch, seq, tq, topk):
    T, d_att = q.shape
    nq = seq // tq
    lp = -(-seq // LANES) * LANES
    n_slab = d_att // LANES
    qrow = lambda w: pl.BlockSpec((tq, w), lambda b, j: (b * nq + j, 0))
    full = lambda w: pl.BlockSpec((None, seq, w), lambda b, j: (b, 0, 0))
    body = functools.partial(_dsa_prompt_body, seq=seq, topk=topk)
    return pl.pallas_call(
        body, grid=(batch, nq),
        in_specs=[qrow(qi.shape[1]), qrow(LANES), qrow(d_att), full(LANES), full(d_att), full(d_att)],
        out_specs=qrow(d_att),
        out_shape=jax.ShapeDtypeStruct((T, d_att), BF16),
        scratch_shapes=[pltpu.VMEM((lp, LANES), BF16), pltpu.VMEM((n_slab, lp, LANES), BF16),
                        pltpu.VMEM((n_slab, lp, LANES), BF16),
                        pltpu.VMEM((N_IDX_HEADS, tq, LANES), BF16), pltpu.VMEM((N_HEADS, tq, LANES), BF16),
                        pltpu.VMEM((N_IDX_HEADS, tq, 1), F32),
                        pltpu.VMEM((tq, lp), F32), pltpu.VMEM((n_slab, tq, LANES), F32)],
        compiler_params=_params(("arbitrary", "arbitrary")), name="dsa_prompt",
    )(qi, kw, q, kib.reshape(batch, seq, LANES), kb.reshape(batch, seq, d_att), vb.reshape(batch, seq, d_att))


def _dsa_sample_scores_body(pt_ref, qi_ref, w_ref, kin_ref, *refs):
    del pt_ref
    page_refs, s_ref = refs[:-1], refs[-1]
    n_pages = len(page_refs)
    idx_dim, page = page_refs[0].shape
    heads = [slice(idx_dim * h, idx_dim * (h + 1)) for h in range(N_IDX_HEADS)]
    qcols = [jnp.broadcast_to(qi_ref[rows, :], (idx_dim, page)) for rows in heads]
    ws = [w_ref[h:h + 1, :] for h in range(N_IDX_HEADS)]

    def score(dot_of_head):
        s = None
        for h in range(N_IDX_HEADS):
            t = ws[h] * jnp.maximum(dot_of_head(h), 0.0)
            s = t if s is None else s + t
        return s

    for p in range(n_pages):
        kt = page_refs[p][...]
        s_ref[p:p + 1, :] = score(lambda h: jnp.sum(qcols[h] * kt, axis=0, keepdims=True))
    kin = kin_ref[...]
    s_new = score(lambda h: jnp.sum(qi_ref[heads[h], :] * kin, axis=0, keepdims=True))
    lane = lax.broadcasted_iota(I32, (1, page), 1)
    s_ref[n_pages:n_pages + 1, :] = jnp.where(lane == 0, s_new, 0.0)


def _dsa_sample_scores(page_table, qi_col, w3, ki_col, kidx_t):
    nb, n_pages = page_table.shape
    idx_dim, page = kidx_t.shape[1:]
    per = lambda shape: pl.BlockSpec((None,) + shape, lambda b, pt: (b, 0, 0))
    page_specs = [pl.BlockSpec((None, idx_dim, page), functools.partial(lambda b, pt, p: (pt[b, p], 0, 0), p=p))
                  for p in range(n_pages)]
    grid_spec = pltpu.PrefetchScalarGridSpec(
        num_scalar_prefetch=1, grid=(nb,),
        in_specs=[per(qi_col.shape[1:]), per(w3.shape[1:]), per(ki_col.shape[1:])] + page_specs,
        out_specs=per((n_pages + 1, page)))
    return pl.pallas_call(
        _dsa_sample_scores_body, grid_spec=grid_spec,
        out_shape=jax.ShapeDtypeStruct((nb, n_pages + 1, page), F32),
        compiler_params=_params(("arbitrary",)), name="dsa_sample_scores",
    )(page_table, qi_col, w3, ki_col, *([kidx_t] * n_pages))


def _dsa_sample_select_body(s_ref, b_ref, *, n_keys, topk):
    rows, width = s_ref.shape
    pos = lax.broadcasted_iota(I32, (1, width), 1)
    b_ref[...] = jnp.where(pos < n_keys, s_ref[...], -jnp.inf)
    _topk_bias(b_ref, width, jnp.full((rows, 1), float(min(topk, n_keys)), F32))


def _dsa_sample_select(scores, n_keys, topk):
    rows, width = scores.shape
    body = functools.partial(_dsa_sample_select_body, n_keys=n_keys, topk=topk)
    return pl.pallas_call(
        body, out_shape=jax.ShapeDtypeStruct((rows, width), F32),
        compiler_params=pltpu.CompilerParams(vmem_limit_bytes=VMEM_LIMIT), name="dsa_sample_select",
    )(scores)


def _dsa_sample_attend_body(pt_ref, q_ref, kn_ref, vn_ref, b_ref, *refs):
    del pt_ref
    o_ref, lg_ref = refs[-2], refs[-1]
    n_pages = (len(refs) - 2) // 2
    k_refs, v_refs = refs[:n_pages], refs[n_pages:2 * n_pages]
    n_heads, hd, page = k_refs[0].shape
    lane = lax.broadcasted_iota(I32, (1, page), 1)
    bias = b_ref[...]
    for h in range(n_heads):
        rows = slice(hd * h, hd * (h + 1))
        qc = jnp.broadcast_to(q_ref[rows, :], (hd, page))
        for p in range(n_pages):
            lg_ref[p:p + 1, :] = jnp.sum(qc * k_refs[p][h], axis=0, keepdims=True)
        lg_new = jnp.sum(q_ref[rows, :] * kn_ref[rows, :], axis=0, keepdims=True)
        lg_ref[n_pages:n_pages + 1, :] = jnp.where(lane == 0, lg_new, 0.0)
        lg = lg_ref[...] + bias
        m = jnp.max(jnp.max(lg, axis=1, keepdims=True), axis=0, keepdims=True)
        pr = jnp.exp(lg - m)
        den = jnp.sum(jnp.sum(pr, axis=1, keepdims=True), axis=0, keepdims=True)
        acc = jnp.zeros((hd, page), F32)
        for p in range(n_pages):
            acc = acc + pr[p:p + 1, :] * v_refs[p][h]
        out = jnp.sum(acc, axis=1, keepdims=True) + pr[n_pages:n_pages + 1, 0:1] * vn_ref[rows, :]
        o_ref[rows, :] = out / den


def _dsa_sample_attend(page_table, q_col, kn_col, vn_col, bias3, k_t, v_t):
    nb, n_pages = page_table.shape
    n_heads, hd, page = k_t.shape[1:]
    per = lambda shape: pl.BlockSpec((None,) + shape, lambda b, pt: (b, 0, 0))
    page_specs = [pl.BlockSpec((None, n_heads, hd, page), functools.partial(lambda b, pt, p: (pt[b, p], 0, 0, 0), p=p))
                  for p in range(n_pages)]
    grid_spec = pltpu.PrefetchScalarGridSpec(
        num_scalar_prefetch=1, grid=(nb,),
        in_specs=[per(q_col.shape[1:]), per(kn_col.shape[1:]), per(vn_col.shape[1:]), per(bias3.shape[1:])] + page_specs * 2,
        out_specs=per(q_col.shape[1:]),
        scratch_shapes=[pltpu.VMEM((n_pages + 1, page), F32)])
    return pl.pallas_call(
        _dsa_sample_attend_body, grid_spec=grid_spec,
        out_shape=jax.ShapeDtypeStruct(q_col.shape, F32),
        compiler_params=_params(("arbitrary",)), name="dsa_sample_attend",
    )(page_table, q_col, kn_col, vn_col, bias3, *([k_t] * n_pages), *([v_t] * n_pages))


def _swiglu(h, wg_ref, wu_ref, wd_ref, n_chunks):
    fc = wg_ref.shape[-1] // n_chunks
    y = None
    for c in range(n_chunks):
        cols = slice(fc * c, fc * (c + 1))
        gate = _dot(h, wg_ref[:, cols])
        act = (gate * jax.nn.sigmoid(gate) * _dot(h, wu_ref[:, cols])).astype(BF16)
        part = _dot(act, wd_ref[cols, :])
        y = part if y is None else y + part
    return y


def _outproj_ffn_body(x_ref, ys_ref, ya_ref, wos_ref, woa_ref, g_ref, wg_ref, wu_ref, wd_ref, o_ref):
    x = x_ref[...] + _dot(ys_ref[...], wos_ref[...]) + _dot(ya_ref[...], woa_ref[...])
    h = _rms(x, g_ref[...]).astype(BF16)
    o_ref[...] = x + _swiglu(h, wg_ref, wu_ref, wd_ref, 2)


def _outproj_ffn(x2d, ys, ya, wos, woa, g, wg, wu, wd, tm):
    T, D = x2d.shape
    row = lambda w: pl.BlockSpec((tm, w), lambda i: (i, 0))
    single = lambda a: pl.BlockSpec(a.shape, lambda i: (0,) * a.ndim, pipeline_mode=pl.Buffered(1))
    return pl.pallas_call(
        _outproj_ffn_body, grid=(T // tm,),
        in_specs=[row(D), row(ys.shape[1]), row(ya.shape[1]), single(wos), single(woa), single(g),
                  single(wg), single(wu), single(wd)],
        out_specs=row(D), out_shape=jax.ShapeDtypeStruct((T, D), F32),
        compiler_params=_params(("arbitrary",)), name="outproj_ffn",
    )(x2d, ys, ya, wos, woa, g, wg, wu, wd)


def _pool_mix(h, window_sum, divisor, pw_ref, scale):
    gd = h.shape[1] // len(POOL_WINDOWS)
    cols = []
    for g, w in enumerate(POOL_WINDOWS):
        pooled = window_sum(g, w) / divisor(w) - h[:, gd * g:gd * (g + 1)]
        cols.append(_dot(pooled.astype(BF16), pw_ref[g]))
    return jnp.concatenate(cols, axis=1) * scale


def _pool_prompt_body(x_ref, g_ref, pw_ref, sc_ref, o_ref, hist_ref, ext_ref):
    tm, D = x_ref.shape
    gd = D // len(POOL_WINDOWS)
    halo = POOL_HIST + 1
    j = pl.program_id(1)

    @pl.when(j == 0)
    def _():
        ext_ref[0:halo] = jnp.zeros((halo, D), F32)

    x = x_ref[...]
    h = _rms(x, g_ref[...])
    ext_ref[halo:] = h
    pos = (j * tm + lax.broadcasted_iota(I32, (tm, 1), 0)).astype(F32)

    def window_sum(g, w):
        acc = h[:, gd * g:gd * (g + 1)]
        for k in range(1, w):
            acc = acc + ext_ref[halo - k:halo - k + tm, gd * g:gd * (g + 1)]
        return acc

    mixed = _pool_mix(h, window_sum, lambda w: jnp.minimum(float(w), pos + 1.0), pw_ref, sc_ref[...])
    o_ref[...] = x + mixed
    ext_ref[0:halo] = ext_ref[tm:tm + halo]

    @pl.when(j == pl.num_programs(1) - 1)
    def _():
        hist_ref[...] = h[tm - POOL_HIST:, :]


def _pool_prompt(x2d, g, pw, scale, batch, seq, tm):
    T, D = x2d.shape
    nblk = seq // tm
    row = pl.BlockSpec((tm, D), lambda b, j: (b * nblk + j, 0))
    return pl.pallas_call(
        _pool_prompt_body, grid=(batch, nblk),
        in_specs=[row, _const_spec(g.shape), _const_spec(pw.shape), _const_spec(scale.shape)],
        out_specs=(row, pl.BlockSpec((None, POOL_HIST, D), lambda b, j: (b, 0, 0))),
        out_shape=(jax.ShapeDtypeStruct((T, D), F32), jax.ShapeDtypeStruct((batch, POOL_HIST, D), F32)),
        scratch_shapes=[pltpu.VMEM((tm + POOL_HIST + 1, D), F32)],
        compiler_params=_params(("arbitrary", "arbitrary")), name="pool_prompt",
    )(x2d, g, pw, scale)


def _pool_sample_body(x_ref, hist_ref, g_ref, pw_ref, sc_ref, o_ref, nh_ref):
    D = x_ref.shape[1]
    gd = D // len(POOL_WINDOWS)
    x = x_ref[...]
    h = _rms(x, g_ref[...])

    def window_sum(g, w):
        acc = h[:, gd * g:gd * (g + 1)]
        for k in range(1, w):
            acc = acc + hist_ref[POOL_HIST - k, :, gd * g:gd * (g + 1)]
        return acc

    mixed = _pool_mix(h, window_sum, float, pw_ref, sc_ref[...])
    o_ref[...] = x + mixed
    for i in range(POOL_HIST - 1):
        nh_ref[i] = hist_ref[i + 1]
    nh_ref[POOL_HIST - 1] = h


def _pool_sample(x2d, hist_t, g, pw, scale):
    sds = jax.ShapeDtypeStruct
    return pl.pallas_call(
        _pool_sample_body, out_shape=(sds(x2d.shape, F32), sds(hist_t.shape, F32)),
        compiler_params=pltpu.CompilerParams(vmem_limit_bytes=VMEM_LIMIT), name="pool_sample",
    )(x2d, hist_t, g, pw, scale)


def _store_row_tiles(ref, x):
    for s in range(ref.shape[1]):
        ref[:, s, :] = x[:, LANES * s:LANES * (s + 1)]


def _load_row_tiles(ref):
    return jnp.concatenate([ref[:, s, :] for s in range(ref.shape[1])], axis=1)


def _router_body(x_ref, g_ref, wr_ref, h_ref, sel_ref, *, n_experts):
    h = _rms(x_ref[...], g_ref[...])
    _store_row_tiles(h_ref, h)
    logits = jnp.dot(h, wr_ref[...], preferred_element_type=F32, precision=lax.Precision.HIGHEST)
    lane = lax.broadcasted_iota(I32, logits.shape, 1).astype(F32)
    logits = jnp.where(lane < n_experts, logits, -jnp.inf)
    v1 = jnp.max(logits, axis=1, keepdims=True)
    i1 = jnp.min(jnp.where(logits == v1, lane, float(LANES)), axis=1, keepdims=True)
    rest = jnp.where(lane == i1, -jnp.inf, logits)
    v2 = jnp.max(rest, axis=1, keepdims=True)
    i2 = jnp.min(jnp.where(rest == v2, lane, float(LANES)), axis=1, keepdims=True)
    e2 = jnp.exp(v2 - v1)
    den = 1.0 + e2
    sel_ref[...] = jnp.where(lane == 0.0, i1, jnp.where(lane == 1.0, i2, jnp.where(lane == 2.0, 1.0 / den, e2 / den)))


def _router(x2d, g, wr_pad, n_experts, tm):
    T, D = x2d.shape
    row = lambda w: pl.BlockSpec((tm, w), lambda i: (i, 0))
    body = functools.partial(_router_body, n_experts=n_experts)
    return pl.pallas_call(
        body, grid=(T // tm,),
        in_specs=[row(D), _const_spec(g.shape), _const_spec(wr_pad.shape)],
        out_specs=(pl.BlockSpec((tm, D // LANES, LANES), lambda i: (i, 0, 0)), row(LANES)),
        out_shape=(jax.ShapeDtypeStruct((T, D // LANES, LANES), F32), jax.ShapeDtypeStruct((T, LANES), F32)),
        compiler_params=_params(("arbitrary",)), name="router",
    )(x2d, g, wr_pad)


def _row_gather(idx_ref, base, src_hbm, dst, sem):
    def issue(r, carry):
        pltpu.make_async_copy(src_hbm.at[idx_ref[base + r]], dst.at[r], sem).start()
        return carry

    lax.fori_loop(0, dst.shape[0], issue, 0, unroll=8)


def _row_gather_wait(src_hbm, dst, sem):
    pltpu.make_async_copy(src_hbm.at[pl.ds(0, dst.shape[0])], dst, sem).wait()


def _moe_dispatch_body(p1_ref, p2_ref, ends_ref, h_hbm, x_hbm, zero_ref, sems, *, tm):
    i = pl.program_id(0)
    gt = zero_ref.shape[0]
    n_groups = ends_ref.shape[0]
    sem, zsem = sems.at[0], sems.at[1]

    def fill_tile(e, act):
        if e < n_groups:
            start = ends_ref[e] - gt
            exists = ends_ref[e] > (ends_ref[e - 1] if e else 0)
        else:
            start = ends_ref[n_groups - 1] + (e - n_groups) * gt
            exists = start < x_hbm.shape[0]

        @pl.when(exists)
        def _():
            act(pltpu.make_async_copy(zero_ref, x_hbm.at[pl.ds(pl.multiple_of(start, gt), gt)], zsem))

    @pl.when(i == 0)
    def _():
        zero_ref[...] = jnp.zeros(zero_ref.shape, F32)
        for e in range(2 * n_groups):
            fill_tile(e, lambda cp: cp.start())
        for e in range(2 * n_groups):
            fill_tile(e, lambda cp: cp.wait())

    def issue(r, carry):
        t = i * tm + r
        pltpu.make_async_copy(h_hbm.at[t], x_hbm.at[p1_ref[t]], sem).start()
        pltpu.make_async_copy(h_hbm.at[t], x_hbm.at[p2_ref[t]], sem).start()
        return carry

    lax.fori_loop(0, tm, issue, 0, unroll=8)
    for _ in range(2):
        pltpu.make_async_copy(h_hbm.at[pl.ds(0, tm)], x_hbm.at[pl.ds(0, tm)], sem).wait()


def _moe_dispatch(pos1, pos2, ends, h_tiles, n_rows, tm, group_tile):
    T, S, _ = h_tiles.shape
    grid_spec = pltpu.PrefetchScalarGridSpec(
        num_scalar_prefetch=3, grid=(T // tm,),
        in_specs=[pl.BlockSpec(memory_space=pl.ANY)], out_specs=pl.BlockSpec(memory_space=pl.ANY),
        scratch_shapes=[pltpu.VMEM((group_tile, S, LANES), F32), pltpu.SemaphoreType.DMA((2,))])
    return pl.pallas_call(
        functools.partial(_moe_dispatch_body, tm=tm), grid_spec=grid_spec,
        out_shape=jax.ShapeDtypeStruct((n_rows, S, LANES), F32),
        compiler_params=pltpu.CompilerParams(dimension_semantics=("arbitrary",), vmem_limit_bytes=VMEM_LIMIT,
                                             disable_bounds_checks=True),
        name="moe_dispatch",
    )(pos1, pos2, ends, h_tiles)


def _moe_group_body(te_ref, nu_ref, x_ref, wg_ref, wu_ref, wd_ref, y_ref):
    del te_ref
    i = pl.program_id(0)

    @pl.when(i < nu_ref[0])
    def _():
        _store_row_tiles(y_ref, _swiglu(_load_row_tiles(x_ref).astype(BF16), wg_ref, wu_ref, wd_ref, 2))

    @pl.when(i >= nu_ref[0])
    def _():
        y_ref[...] = jnp.zeros(y_ref.shape, F32)


def _moe_group(tile_expert, n_used, x_tiles, wg, wu, wd, tm):
    P, S, _ = x_tiles.shape
    _, D, F = wg.shape
    wspec = lambda a, b: pl.BlockSpec((None, a, b), lambda i, te, nu: (te[i], 0, 0), pipeline_mode=pl.Buffered(1))
    grid_spec = pltpu.PrefetchScalarGridSpec(
        num_scalar_prefetch=2, grid=(P // tm,),
        in_specs=[pl.BlockSpec((tm, S, LANES), lambda i, te, nu: (jnp.minimum(i, nu[0] - 1), 0, 0)),
                  wspec(D, F), wspec(D, F), wspec(F, D)],
        out_specs=pl.BlockSpec((tm, S, LANES), lambda i, te, nu: (i, 0, 0)))
    return pl.pallas_call(
        _moe_group_body, grid_spec=grid_spec, out_shape=jax.ShapeDtypeStruct((P, S, LANES), F32),
        compiler_params=_params(("arbitrary",)), name="moe_group",
    )(tile_expert, n_used, x_tiles, wg, wu, wd)


def _moe_combine_body(p1_ref, p2_ref, x_ref, sel_ref, y_hbm, o_ref, buf, sem):
    i = pl.program_id(0)
    tm = x_ref.shape[0]
    slot = i % 2

    def gather(tile, s):
        _row_gather(p1_ref, tile * tm, y_hbm, buf.at[s, 0], sem.at[s, 0])
        _row_gather(p2_ref, tile * tm, y_hbm, buf.at[s, 1], sem.at[s, 1])

    @pl.when(i == 0)
    def _():
        gather(0, 0)

    @pl.when(i + 1 < pl.num_programs(0))
    def _():
        gather(i + 1, 1 - slot)

    _row_gather_wait(y_hbm, buf.at[slot, 0], sem.at[slot, 0])
    _row_gather_wait(y_hbm, buf.at[slot, 1], sem.at[slot, 1])
    sel = sel_ref[...]
    o_ref[...] = (x_ref[...] + sel[:, 2:3] * _load_row_tiles(buf.at[slot, 0])
                  + sel[:, 3:4] * _load_row_tiles(buf.at[slot, 1]))


def _moe_combine(pos1, pos2, x2d, sel, y_tiles, tm):
    T, D = x2d.shape
    S = y_tiles.shape[1]
    row = lambda w: pl.BlockSpec((tm, w), lambda i, p1, p2: (i, 0))
    grid_spec = pltpu.PrefetchScalarGridSpec(
        num_scalar_prefetch=2, grid=(T // tm,),
        in_specs=[row(D), row(LANES), pl.BlockSpec(memory_space=pl.ANY)], out_specs=row(D),
        scratch_shapes=[pltpu.VMEM((2, 2, tm, S, LANES), F32), pltpu.SemaphoreType.DMA((2, 2))])
    return pl.pallas_call(
        _moe_combine_body, grid_spec=grid_spec, out_shape=jax.ShapeDtypeStruct((T, D), F32),
        compiler_params=pltpu.CompilerParams(dimension_semantics=("arbitrary",), vmem_limit_bytes=VMEM_LIMIT,
                                             disable_bounds_checks=True),
        name="moe_combine",
    )(pos1, pos2, x2d, sel, y_tiles)


def _moe_plan(sel, n_experts, tm):
    T = sel.shape[0]
    experts = jnp.concatenate([sel[:, 0], sel[:, 1]]).astype(I32)
    onehot = (experts[:, None] == jnp.arange(n_experts, dtype=I32)[None, :]).astype(I32)
    rank = jnp.cumsum(onehot, axis=0) - onehot
    counts = jnp.sum(onehot, axis=0)
    padded = (counts + tm - 1) // tm * tm
    ends = jnp.cumsum(padded)
    pos = jnp.sum(onehot * ((ends - padded)[None, :] + rank), axis=1)
    n_rows = (2 * T + n_experts * (tm - 1)) // tm * tm
    tile_start = jnp.arange(n_rows // tm, dtype=I32) * tm
    tile_expert = jnp.minimum(jnp.sum((tile_start[:, None] >= ends[None, :]).astype(I32), axis=1), n_experts - 1)
    n_used = (ends[-1] // tm).astype(I32).reshape(1)
    return tile_expert, n_used, ends.astype(I32), n_rows, pos[:T], pos[T:]


def _rope_tables(pos):
    half = HEAD_DIM // 2
    inv = ROPE_THETA ** (-jnp.arange(half, dtype=F32) / half)
    ang = pos[:, None] * inv[None, :]
    cos, sin = jnp.cos(ang), jnp.sin(ang)
    reps = LANES // HEAD_DIM
    return jnp.tile(jnp.concatenate([cos, cos], axis=1), (1, reps)), jnp.tile(jnp.concatenate([-sin, sin], axis=1), (1, reps))


def kernel(x_prompt, x_sample, cache_k, cache_v, cache_kidx, page_table, state_ssm_re, state_ssm_im, state_pool, meta_tokens, norm_mix0, w_in0, q_norm, k_norm, ssm_lambda_re, ssm_lambda_im, ssm_log_dt, ssm_b_re, ssm_b_im, ssm_c_re, ssm_c_im, ssm_d, ssm_w_glu, w_out0, norm_ffn0, ffn_w_gate, ffn_w_up, ffn_w_down, norm_mix1, pool_w, pool_scale, norm_ffn1, router_w, moe_w_gate, moe_w_up, moe_w_down):
    B, S, D = x_prompt.shape
    DB, DS, _ = x_sample.shape
    assert DS == 1, "one new token per sample sequence"
    L = S + N_META
    n_pool, page = cache_k.shape[1], cache_k.shape[2]
    n_pages = page_table.shape[1]
    past = n_pages * page
    d_att = N_HEADS * HEAD_DIM
    d_qi = N_IDX_HEADS * IDX_DIM
    G, P = ssm_lambda_re.shape[1:]
    d_ssm = G * SSM_GROUP
    n_state = G * P
    E = router_w.shape[-1]
    topk_p = min(TOPK_MAX, S // 4)
    topk_s = min(TOPK_MAX, (past + DS) // 4)
    tm = _row_block(L, 1024)
    assert tm >= POOL_HIST + 1

    x_p = jnp.concatenate([jnp.broadcast_to(meta_tokens[None], (B, N_META, D)), x_prompt], axis=1).reshape(B * L, D)
    x_s = x_sample.reshape(DB, D)

    w_in = w_in0[0]
    o1, o2, o3, o4, o5, o6 = d_ssm, d_ssm + d_att, d_ssm + 2 * d_att, d_ssm + 3 * d_att, d_ssm + 3 * d_att + d_qi, d_ssm + 3 * d_att + d_qi + IDX_DIM
    w_kw = jnp.pad(w_in[:, o5:], ((0, 0), (0, LANES - (w_in.shape[1] - o5))))
    ws = tuple(w.astype(BF16) for w in (w_in[:, :o1], w_in[:, o1:o2], w_in[:, o2:o3], w_in[:, o3:o4], w_in[:, o4:o5], w_kw))
    g_mix0 = norm_mix0[0].reshape(1, D)
    qn = jnp.tile(q_norm[0], N_HEADS).reshape(1, d_att)
    kn = jnp.tile(k_norm[0], N_HEADS).reshape(1, d_att)
    head_of = jnp.arange(d_att) // HEAD_DIM
    hm = jnp.where(head_of[:, None] == head_of[None, :], 1.0 / HEAD_DIM, 0.0).astype(BF16)
    cos_p, sin_p = _rope_tables(jnp.arange(L, dtype=F32))
    cos_s, sin_s = _rope_tables(jnp.full((DB,), float(past), F32))

    u_p, q_p, k_p, v_p, qi_p, ki_p, kw_p, kb_p, vb_p, kib_p = _inproj(x_p, cos_p, sin_p, L // tm, tm, g_mix0, ws, qn, kn, hm)
    u_s, q_s, k_s, v_s, qi_s, ki_s, kw_s, _, _, _ = _inproj(x_s, cos_s, sin_s, 1, DB, g_mix0, ws, qn, kn, hm)

    lbr, lbi, bbr, bbi = _s5_prep(ssm_lambda_re[0], ssm_lambda_im[0], ssm_log_dt[0], ssm_b_re[0], ssm_b_im[0])
    gps = LANES // SSM_GROUP
    n_slab = G // gps
    eye = jnp.eye(gps, dtype=F32)

    def b_slabs(bt):
        return jnp.einsum('sgcp,gh->sgchp', bt.reshape(n_slab, gps, SSM_GROUP, P), eye).reshape(n_slab, LANES, gps * P)

    def c_slabs(c):
        return jnp.einsum('sgcp,gh->sgphc', c.reshape(n_slab, gps, SSM_GROUP, P), eye).reshape(n_slab, gps * P, LANES)

    wb = jnp.concatenate([b_slabs(bbr), b_slabs(bbi)], axis=2)
    wc = jnp.stack([c_slabs(ssm_c_re[0]), -c_slabs(ssm_c_im[0])], axis=1).astype(BF16)
    lbr_f, lbi_f = lbr.reshape(1, n_state), lbi.reshape(1, n_state)
    d_skip = ssm_d[0].reshape(1, d_ssm)
    wglu = ssm_w_glu[0].astype(BF16)
    assert B % SUBLANES == 0, "the S5 scan keeps one batch row per sublane"
    steps = max(d for d in range(1, L + 1) if L % d == 0 and d * B <= 512 and (d * B) % (2 * SUBLANES) == 0)
    u_tm = u_p.reshape(B, L, d_ssm).swapaxes(0, 1).reshape(L * B, d_ssm)
    zero_h = jnp.zeros((B, n_state), F32)
    ys_tm, hre_p, him_p = _s5(u_tm, zero_h, zero_h, lbr_f, lbi_f, wb, wc, d_skip, wglu, B, steps, False)
    ys_p = ys_tm.reshape(L, B, d_ssm).swapaxes(0, 1).reshape(B * L, d_ssm)
    ys_s, hre_s, him_s = _s5(u_s, state_ssm_re[0].reshape(DB, n_state), state_ssm_im[0].reshape(DB, n_state),
                             lbr_f, lbi_f, wb, wc, d_skip, wglu, DB, 1, True)

    ya_p = _dsa_prompt(qi_p, kw_p, q_p, kib_p, kb_p, vb_p, B, L, tm, topk_p)
    assert page == LANES
    w3 = kw_s[:, IDX_DIM:IDX_DIM + N_IDX_HEADS].reshape(DB, N_IDX_HEADS, 1)
    scores = _dsa_sample_scores(page_table, qi_s.astype(F32).reshape(DB, d_qi, 1), w3, ki_s.reshape(DB, IDX_DIM, 1),
                                jnp.transpose(cache_kidx[0], (0, 2, 1)))
    bias = _dsa_sample_select(scores.reshape(DB, (n_pages + 1) * page), past + DS, topk_s)
    ya_s = _dsa_sample_attend(page_table, q_s.astype(F32).reshape(DB, d_att, 1), k_s.reshape(DB, d_att, 1),
                              v_s.reshape(DB, d_att, 1), bias.reshape(DB, n_pages + 1, page),
                              jnp.transpose(cache_k[0], (0, 2, 3, 1)), jnp.transpose(cache_v[0], (0, 2, 3, 1)))
    ya_s = ya_s.reshape(DB, d_att).astype(BF16)

    w_out = w_out0[0].astype(BF16)
    ffn = (w_out[:d_ssm], w_out[d_ssm:], norm_ffn0[0].reshape(1, D), ffn_w_gate[0].astype(BF16),
           ffn_w_up[0].astype(BF16), ffn_w_down[0].astype(BF16))
    x_p = _outproj_ffn(x_p, ys_p, ya_p, *ffn, tm)
    x_s = _outproj_ffn(x_s, ys_s, ya_s, *ffn, DB)

    g_mix1 = norm_mix1[0].reshape(1, D)
    pw = pool_w[0].astype(BF16)
    psc = pool_scale[0].reshape(1, D)
    x_p, pool_p = _pool_prompt(x_p, g_mix1, pw, psc, B, L, tm)
    x_s, hist_t = _pool_sample(x_s, state_pool[0].swapaxes(0, 1), g_mix1, pw, psc)
    pool_s = hist_t.swapaxes(0, 1)

    g_ffn1 = norm_ffn1[0].reshape(1, D)
    wr_pad = jnp.pad(router_w[0], ((0, 0), (0, LANES - E)))
    wg, wu, wd = moe_w_gate[0].astype(BF16), moe_w_up[0].astype(BF16), moe_w_down[0].astype(BF16)
    x_all = jnp.concatenate([x_p, x_s], axis=0)
    T_all = x_all.shape[0]
    tr = _row_block(T_all, MOE_TILE)
    h_tiles, sel = _router(x_all, g_ffn1, wr_pad, E, tr)
    tile_expert, n_used, ends, n_rows, pos1, pos2 = _moe_plan(sel, E, MOE_TILE)
    x_tiles = _moe_dispatch(pos1, pos2, ends, h_tiles, n_rows, tr, MOE_TILE)
    y_tiles = _moe_group(tile_expert, n_used, x_tiles, wg, wu, wd, MOE_TILE)
    x_all = _moe_combine(pos1, pos2, x_all, sel, y_tiles, tr)
    x_p, x_s = x_all[:B * L], x_all[B * L:]

    y_prompt = x_p.reshape(B, L, D)[:, N_META:]
    y_sample = x_s.reshape(DB, 1, D)
    k_prompt = k_p.reshape(1, B, L, N_HEADS, HEAD_DIM)
    v_prompt = v_p.reshape(1, B, L, N_HEADS, HEAD_DIM)
    kidx_prompt = ki_p.reshape(1, B, L, IDX_DIM)
    return (y_prompt, y_sample, k_prompt, v_prompt, kidx_prompt,
            hre_p.reshape(1, B, G, P), him_p.reshape(1, B, G, P), pool_p[None],
            k_s.reshape(1, DB, 1, N_HEADS, HEAD_DIM), v_s.reshape(1, DB, 1, N_HEADS, HEAD_DIM),
            ki_s.reshape(1, DB, 1, IDX_DIM), hre_s.reshape(1, DB, G, P), him_s.reshape(1, DB, G, P), pool_s[None])
```

```python
import functools
import math

import jax
import jax.numpy as jnp
from jax import lax
from jax.experimental import pallas as pl
from jax.experimental.pallas import tpu as pltpu

F32 = jnp.float32
BF16 = jnp.bfloat16
I32 = jnp.int32

N_META = 16
SSM_GROUP = 16
SSM_STATE = 64
N_HEADS = 8
HEAD_DIM = 64
N_IDX_HEADS = 8
IDX_DIM = 64
TOPK_MAX = 256
ROPE_THETA = 10000.0
POOL_WINDOWS = (2, 4, 8, 16)
POOL_HIST = max(POOL_WINDOWS) - 1
TOP_K_EXPERTS = 2
EPS = 1e-6

LANES = 128
SUBLANES = 8
VMEM_LIMIT = 56 * 1024 * 1024
MOE_TILE = 256
BISECT_ITERS = 16
NEG_BIG = -1e30


def _params(sem):
    return pltpu.CompilerParams(dimension_semantics=sem, vmem_limit_bytes=VMEM_LIMIT)


def _row_block(n, cap, mult=2 * SUBLANES):
    best = None
    for d in range(mult, min(n, cap) + 1, mult):
        if n % d == 0:
            best = d
    assert best is not None, n
    return best


def _rms(x, g):
    return x * lax.rsqrt(jnp.mean(x * x, axis=-1, keepdims=True) + EPS) * g


def _dot(a, b):
    return jnp.dot(a, b, preferred_element_type=F32)


def _dot_nt(a, b):
    return lax.dot_general(a, b, (((1,), (1,)), ((), ())), preferred_element_type=F32)


def _const_spec(shape):
    nd = len(shape)
    return pl.BlockSpec(shape, lambda *_: (0,) * nd)


def _inproj_body(x_ref, g_ref, wu_ref, wq_ref, wk_ref, wv_ref, wqi_ref, wkw_ref, qn_ref, kn_ref, hm_ref,
                 cos_ref, sin_ref,
                 u_ref, q_ref, k_ref, v_ref, qi_ref, ki_ref, kw_ref, kb_ref, vb_ref, kib_ref):
    h = _rms(x_ref[...], g_ref[...]).astype(BF16)
    cos = cos_ref[...]
    sin = sin_ref[...]
    lane = lax.broadcasted_iota(I32, (1, LANES), 1)
    lo_half = (lane & (HEAD_DIM // 2)) == 0

    def rope(z):
        cols = []
        for j in range(z.shape[1] // LANES):
            zs = z[:, LANES * j:LANES * (j + 1)]
            partner = jnp.where(lo_half, pltpu.roll(zs, LANES - HEAD_DIM // 2, 1), pltpu.roll(zs, HEAD_DIM // 2, 1))
            cols.append(zs * cos + partner * sin)
        return cols[0] if len(cols) == 1 else jnp.concatenate(cols, axis=1)

    def head_norm(z, gn):
        ms = _dot((z * z).astype(BF16), hm_ref[...])
        return z * lax.rsqrt(ms + EPS) * gn

    u_ref[...] = _dot(h, wu_ref[...])
    q = rope(head_norm(_dot(h, wq_ref[...]), qn_ref[...]))
    q_ref[...] = (q * (HEAD_DIM ** -0.5)).astype(BF16)
    k = rope(head_norm(_dot(h, wk_ref[...]), kn_ref[...]))
    k_ref[...] = k
    kb_ref[...] = k.astype(BF16)
    v = _dot(h, wv_ref[...])
    v_ref[...] = v
    vb_ref[...] = v.astype(BF16)
    qi_ref[...] = (rope(_dot(h, wqi_ref[...])) * (IDX_DIM ** -0.5)).astype(BF16)
    zkw = _dot(h, wkw_ref[...])
    kir = rope(zkw)
    kw_ref[...] = jnp.where(lane < IDX_DIM, kir, zkw * (N_IDX_HEADS ** -0.5))
    ki_ref[...] = kir[:, :IDX_DIM]
    kib_ref[...] = jnp.where(lane < IDX_DIM, kir, pltpu.roll(kir, IDX_DIM, 1)).astype(BF16)


def _inproj(x2d, cos_t, sin_t, tab_blocks, tm, g, ws, qn, kn, hm):
    T, D = x2d.shape
    wu, wq, wk, wv, wqi, wkw = ws
    d_ssm, d_att, d_qi = wu.shape[1], wq.shape[1], wqi.shape[1]
    row = lambda w: pl.BlockSpec((tm, w), lambda i: (i, 0))
    tab = pl.BlockSpec((tm, LANES), lambda i: (i % tab_blocks, 0))
    out_shape = (
        jax.ShapeDtypeStruct((T, d_ssm), F32),
        jax.ShapeDtypeStruct((T, d_att), BF16),
        jax.ShapeDtypeStruct((T, d_att), F32),
        jax.ShapeDtypeStruct((T, d_att), F32),
        jax.ShapeDtypeStruct((T, d_qi), BF16),
        jax.ShapeDtypeStruct((T, IDX_DIM), F32),
        jax.ShapeDtypeStruct((T, LANES), F32),
        jax.ShapeDtypeStruct((T, d_att), BF16),
        jax.ShapeDtypeStruct((T, d_att), BF16),
        jax.ShapeDtypeStruct((T, LANES), BF16),
    )
    out_specs = (row(d_ssm), row(d_att), row(d_att), row(d_att), row(d_qi), row(IDX_DIM), row(LANES),
                 row(d_att), row(d_att), row(LANES))
    in_specs = [row(D), _const_spec(g.shape)] + [_const_spec(w.shape) for w in ws] + [
        _const_spec(qn.shape), _const_spec(kn.shape), _const_spec(hm.shape), tab, tab]
    return pl.pallas_call(
        _inproj_body, grid=(T // tm,), in_specs=in_specs, out_specs=out_specs, out_shape=out_shape,
        compiler_params=_params(("arbitrary",)), name="inproj",
    )(x2d, g, *ws, qn, kn, hm, cos_t, sin_t)


def _s5_prep_body(lr_ref, li_ref, ldt_ref, btr_ref, bti_ref, lbr_ref, lbi_ref, bbr_ref, bbi_ref):
    lr = lr_ref[...]
    li = li_ref[...]
    dt = jnp.exp(ldt_ref[...])
    mag = jnp.exp(lr * dt)
    lbr = mag * jnp.cos(li * dt)
    lbi = mag * jnp.sin(li * dt)
    lbr_ref[...] = lbr
    lbi_ref[...] = lbi
    den = lr * lr + li * li
    cr = ((lbr - 1.0) * lr + lbi * li) / den
    ci = (lbi * lr - (lbr - 1.0) * li) / den
    btr = btr_ref[...]
    bti = bti_ref[...]
    bbr_ref[...] = cr * btr - ci * bti
    bbi_ref[...] = cr * bti + ci * btr


def _s5_prep(lam_re, lam_im, log_dt, b_re, b_im):
    G, P = lam_re.shape
    C = b_re.shape[-1]
    btr = jnp.swapaxes(b_re, 1, 2)
    bti = jnp.swapaxes(b_im, 1, 2)
    sds = jax.ShapeDtypeStruct
    return pl.pallas_call(
        _s5_prep_body,
        out_shape=(sds((G, 1, P), F32), sds((G, 1, P), F32), sds((G, C, P), F32), sds((G, C, P), F32)),
        name="s5_prep",
    )(lam_re.reshape(G, 1, P), lam_im.reshape(G, 1, P), log_dt.reshape(G, 1, 1), btr, bti)


def _s5_body(u_ref, h0r_ref, h0i_ref, lbr_ref, lbi_ref, wb_ref, wc_ref, d_ref, wglu_ref,
             y_ref, hr_ref, hi_ref, bu_ref, st_ref, *, nb, steps, precise):
    n_slab = wb_ref.shape[0]
    cin = wb_ref.shape[1]
    sw = wb_ref.shape[2] // 2
    n_state = n_slab * sw

    @pl.when(pl.program_id(0) == 0)
    def _():
        st_ref[0] = h0r_ref[...]
        st_ref[1] = h0i_ref[...]

    u = u_ref[...]
    for i in range(n_slab):
        ui = u[:, cin * i:cin * (i + 1)]
        if precise:
            bu = jnp.dot(ui, wb_ref[i], preferred_element_type=F32, precision=lax.Precision.HIGHEST)
        else:
            bu = _dot(ui.astype(BF16), wb_ref[i].astype(BF16))
        bu_ref[:, sw * i:sw * (i + 1)] = bu[:, :sw]
        bu_ref[:, n_state + sw * i:n_state + sw * (i + 1)] = bu[:, sw:]

    for i in range(n_slab):
        re_cols = slice(sw * i, sw * (i + 1))
        im_cols = slice(n_state + sw * i, n_state + sw * (i + 1))
        lr = jnp.broadcast_to(lbr_ref[:, re_cols], (nb, sw))
        li = jnp.broadcast_to(lbi_ref[:, re_cols], (nb, sw))

        def step(t, carry, re_cols=re_cols, im_cols=im_cols, lr=lr, li=li):
            hr, hi = carry
            rows = pl.ds(pl.multiple_of(t * nb, nb), nb)
            nhr = lr * hr - li * hi + bu_ref[rows, re_cols]
            nhi = lr * hi + li * hr + bu_ref[rows, im_cols]
            bu_ref[rows, re_cols] = nhr
            bu_ref[rows, im_cols] = nhi
            return nhr, nhi

        hr, hi = lax.fori_loop(0, steps, step, (st_ref[0, :, re_cols], st_ref[1, :, re_cols]))
        st_ref[0, :, re_cols] = hr
        st_ref[1, :, re_cols] = hi

    ys = []
    for i in range(n_slab):
        h_re = bu_ref[:, sw * i:sw * (i + 1)].astype(BF16)
        h_im = bu_ref[:, n_state + sw * i:n_state + sw * (i + 1)].astype(BF16)
        ys.append(_dot(h_re, wc_ref[i, 0]) + _dot(h_im, wc_ref[i, 1]))
    y = jnp.concatenate(ys, axis=1) + d_ref[...] * u
    y = 0.5 * y * (1.0 + jnp.tanh(math.sqrt(2.0 / math.pi) * (y + 0.044715 * (y * y * y))))
    y = y * jax.nn.sigmoid(_dot(y.astype(BF16), wglu_ref[...]))
    y_ref[...] = y.astype(BF16)
    hr_ref[...] = st_ref[0]
    hi_ref[...] = st_ref[1]


def _s5(u_tm, h0_re, h0_im, lbr_flat, lbi_flat, wb, wc, d_skip, wglu, nb, steps, precise):
    rows_total, d_ssm = u_tm.shape
    n_state = h0_re.shape[1]
    rows = nb * steps
    sds = jax.ShapeDtypeStruct
    body = functools.partial(_s5_body, nb=nb, steps=steps, precise=precise)
    return pl.pallas_call(
        body, grid=(rows_total // rows,),
        in_specs=[pl.BlockSpec((rows, d_ssm), lambda c: (c, 0)), _const_spec(h0_re.shape), _const_spec(h0_im.shape),
                  _const_spec(lbr_flat.shape), _const_spec(lbi_flat.shape), _const_spec(wb.shape),
                  _const_spec(wc.shape), _const_spec(d_skip.shape), _const_spec(wglu.shape)],
        out_specs=(pl.BlockSpec((rows, d_ssm), lambda c: (c, 0)), _const_spec(h0_re.shape), _const_spec(h0_im.shape)),
        out_shape=(sds((rows_total, d_ssm), BF16), sds(h0_re.shape, F32), sds(h0_im.shape, F32)),
        scratch_shapes=[pltpu.VMEM((rows, 2 * n_state), F32), pltpu.VMEM((2, nb, n_state), F32)],
        compiler_params=_params(("arbitrary",)), name="s5_precise" if precise else "s5",
    )(u_tm, h0_re, h0_im, lbr_flat, lbi_flat, wb, wc, d_skip, wglu)


def _topk_bias(s_ref, n, kk):
    rows = s_ref.shape[0]
    cols = slice(0, n)

    def count(pred):
        return jnp.sum(jnp.where(pred(s_ref[:, cols]), 1.0, 0.0), axis=1, keepdims=True)

    def largest_below(hi):
        s = s_ref[:, cols]
        return jnp.max(jnp.where(s < hi, s, -jnp.inf), axis=1, keepdims=True)

    s0 = s_ref[:, cols]
    smax = jnp.max(s0, axis=1, keepdims=True)
    smin = jnp.min(jnp.where(s0 > -jnp.inf, s0, jnp.inf), axis=1, keepdims=True)

    def bisect(_, c):
        lo, hi = c
        mid = lo + (hi - lo) * 0.5
        ge = count(lambda s: s >= mid) >= kk
        return jnp.where(ge, mid, lo), jnp.where(ge, hi, mid)

    _, hi = lax.fori_loop(0, BISECT_ITERS, bisect, (smin, smax + (smax - smin) + 1.0))
    thr = largest_below(hi)
    cnt = count(lambda s: s >= thr)

    def short(c):
        return jnp.min(c[2] - kk) < 0.0

    def lower(c):
        hi, thr, cnt = c
        hi = jnp.where(cnt < kk, thr, hi)
        thr = largest_below(hi)
        return hi, thr, count(lambda s: s >= thr)

    _, thr, _ = lax.while_loop(short, lower, (hi, thr, cnt))
    need = kk - count(lambda s: s > thr)
    blk = 2 * LANES
    tri = (lax.broadcasted_iota(I32, (blk, blk), 0) <= lax.broadcasted_iota(I32, (blk, blk), 1)).astype(BF16)
    seen = jnp.zeros((rows, 1), F32)
    for c0 in range(0, n, blk):
        w = min(blk, n - c0)
        s = s_ref[:, c0:c0 + w]
        tie = s == thr
        upto = seen + _dot(jnp.where(tie, 1.0, 0.0).astype(BF16), tri[:w, :w])
        seen = upto[:, w - 1:w]
        s_ref[:, c0:c0 + w] = jnp.where((s > thr) | (tie & (upto <= need)), 0.0, NEG_BIG)


def _dsa_prompt_block(q0, n, qim_ref, qm_ref, w_ref, kip_ref, kp_ref, vp_ref, s_ref, acc_ref, topk):
    tq = s_ref.shape[0]
    cols = slice(0, n)
    lane = lax.broadcasted_iota(I32, (1, LANES), 1)
    s_ref[:, cols] = jnp.zeros((tq, n), F32)

    def score_head(h, carry):
        d = jnp.maximum(_dot_nt(qim_ref[h], kip_ref[cols, :]), 0.0)
        s_ref[:, cols] += w_ref[h] * d
        return carry

    lax.fori_loop(0, N_IDX_HEADS, score_head, 0)
    qpos = q0 + lax.broadcasted_iota(I32, (tq, 1), 0)
    kpos = lax.broadcasted_iota(I32, (1, n), 1)
    s_ref[:, cols] = jnp.where(kpos <= qpos, s_ref[:, cols], -jnp.inf)
    _topk_bias(s_ref, n, jnp.minimum(qpos + 1, topk).astype(F32))

    def head_pair(sp, carry):
        outs = []
        for hh in range(2):
            lg = _dot_nt(qm_ref[2 * sp + hh], kp_ref[sp, cols, :]) + s_ref[:, cols]
            p = jnp.exp(lg - jnp.max(lg, axis=1, keepdims=True))
            pv = _dot(p.astype(BF16), vp_ref[sp, cols, :])
            outs.append(pv[:, :LANES] / pv[:, LANES:LANES + 1])
        acc_ref[sp] = jnp.where(lane < HEAD_DIM, outs[0], outs[1])
        return carry

    lax.fori_loop(0, acc_ref.shape[0], head_pair, 0)


def _dsa_prompt_body(qi_ref, kw_ref, q_ref, kib_ref, kb_ref, vb_ref, o_ref,
                     kip_ref, kp_ref, vp_ref, qim_ref, qm_ref, w_ref, s_ref, acc_ref, *, seq, topk):
    tq = q_ref.shape[0]
    n_slab, lp, _ = kp_ref.shape
    j = pl.program_id(1)
    lane = lax.broadcasted_iota(I32, (1, LANES), 1)

    @pl.when(j == 0)
    def _():
        kip_ref[0:seq] = kib_ref[...]
        if lp > seq:
            kip_ref[seq:] = jnp.zeros((lp - seq, LANES), BF16)
        for sp in range(n_slab):
            sl = slice(LANES * sp, LANES * (sp + 1))
            kp_ref[sp, 0:seq] = kb_ref[:, sl]
            vp_ref[sp, 0:seq, 0:LANES] = vb_ref[:, sl]
            vp_ref[sp, :, LANES:] = jnp.where(lane == 0, 1.0, 0.0).astype(BF16) + jnp.zeros((lp, LANES), BF16)
            if lp > seq:
                kp_ref[sp, seq:] = jnp.zeros((lp - seq, LANES), BF16)
                vp_ref[sp, seq:, 0:LANES] = jnp.zeros((lp - seq, LANES), BF16)

    kw = kw_ref[...]
    for h in range(N_IDX_HEADS):
        head_lanes = (lane // IDX_DIM) == (h % 2)
        sl = slice(LANES * (h // 2), LANES * (h // 2 + 1))
        qim_ref[h] = jnp.where(head_lanes, qi_ref[:, sl], jnp.zeros((), BF16))
        qm_ref[h] = jnp.where(head_lanes, q_ref[:, sl], jnp.zeros((), BF16))
        w_ref[h] = kw[:, IDX_DIM + h:IDX_DIM + h + 1]

    for jj in range(seq // tq):
        n = min(-(-((jj + 1) * tq) // LANES) * LANES, lp)

        @pl.when(j == jj)
        def _(jj=jj, n=n):
            _dsa_prompt_block(jj * tq, n, qim_ref, qm_ref, w_ref, kip_ref, kp_ref, vp_ref, s_ref, acc_ref, topk)

    for sp in range(n_slab):
        o_ref[:, LANES * sp:LANES * (sp + 1)] = acc_ref[sp].astype(BF16)


def _dsa_prompt(qi, kw, q, kib, kb, vb, batch, seq, tq, topk):
    T, d_att = q.shape
    nq = seq // tq
    lp = -(-seq // LANES) * LANES
    n_slab = d_att // LANES
    qrow = lambda w: pl.BlockSpec((tq, w), lambda b, j: (b * nq + j, 0))
    full = lambda w: pl.BlockSpec((None, seq, w), lambda b, j: (b, 0, 0))
    body = functools.partial(_dsa_prompt_body, seq=seq, topk=topk)
    return pl.pallas_call(
        body, grid=(batch, nq),
        in_specs=[qrow(qi.shape[1]), qrow(LANES), qrow(d_att), full(LANES), full(d_att), full(d_att)],
        out_specs=qrow(d_att),
        out_shape=jax.ShapeDtypeStruct((T, d_att), BF16),
        scratch_shapes=[pltpu.VMEM((lp, LANES), BF16), pltpu.VMEM((n_slab, lp, LANES), BF16),
                        pltpu.VMEM((n_slab, lp, 2 * LANES), BF16),
                        pltpu.VMEM((N_IDX_HEADS, tq, LANES), BF16), pltpu.VMEM((N_HEADS, tq, LANES), BF16),
                        pltpu.VMEM((N_IDX_HEADS, tq, 1), F32),
                        pltpu.VMEM((tq, lp), F32), pltpu.VMEM((n_slab, tq, LANES), F32)],
        compiler_params=_params(("arbitrary", "arbitrary")), name="dsa_prompt",
    )(qi, kw, q, kib.reshape(batch, seq, LANES), kb.reshape(batch, seq, d_att), vb.reshape(batch, seq, d_att))


def _dsa_sample_scores_body(pt_ref, qi_ref, w_ref, kin_ref, *refs):
    del pt_ref
    page_refs, s_ref = refs[:-1], refs[-1]
    n_pages = len(page_refs)
    idx_dim, page = page_refs[0].shape
    heads = [slice(idx_dim * h, idx_dim * (h + 1)) for h in range(N_IDX_HEADS)]
    qcols = [jnp.broadcast_to(qi_ref[rows, :], (idx_dim, page)) for rows in heads]
    ws = [w_ref[h:h + 1, :] for h in range(N_IDX_HEADS)]

    def score(dot_of_head):
        s = None
        for h in range(N_IDX_HEADS):
            t = ws[h] * jnp.maximum(dot_of_head(h), 0.0)
            s = t if s is None else s + t
        return s

    for p in range(n_pages):
        kt = page_refs[p][...]
        s_ref[p:p + 1, :] = score(lambda h: jnp.sum(qcols[h] * kt, axis=0, keepdims=True))
    kin = kin_ref[...]
    s_new = score(lambda h: jnp.sum(qi_ref[heads[h], :] * kin, axis=0, keepdims=True))
    lane = lax.broadcasted_iota(I32, (1, page), 1)
    s_ref[n_pages:n_pages + 1, :] = jnp.where(lane == 0, s_new, 0.0)


def _dsa_sample_scores(page_table, qi_col, w3, ki_col, kidx_t):
    nb, n_pages = page_table.shape
    idx_dim, page = kidx_t.shape[1:]
    per = lambda shape: pl.BlockSpec((None,) + shape, lambda b, pt: (b, 0, 0))
    page_specs = [pl.BlockSpec((None, idx_dim, page), functools.partial(lambda b, pt, p: (pt[b, p], 0, 0), p=p))
                  for p in range(n_pages)]
    grid_spec = pltpu.PrefetchScalarGridSpec(
        num_scalar_prefetch=1, grid=(nb,),
        in_specs=[per(qi_col.shape[1:]), per(w3.shape[1:]), per(ki_col.shape[1:])] + page_specs,
        out_specs=per((n_pages + 1, page)))
    return pl.pallas_call(
        _dsa_sample_scores_body, grid_spec=grid_spec,
        out_shape=jax.ShapeDtypeStruct((nb, n_pages + 1, page), F32),
        compiler_params=_params(("arbitrary",)), name="dsa_sample_scores",
    )(page_table, qi_col, w3, ki_col, *([kidx_t] * n_pages))


def _dsa_sample_select_body(s_ref, b_ref, *, n_keys, topk):
    rows, width = s_ref.shape
    pos = lax.broadcasted_iota(I32, (1, width), 1)
    b_ref[...] = jnp.where(pos < n_keys, s_ref[...], -jnp.inf)
    _topk_bias(b_ref, width, jnp.full((rows, 1), float(min(topk, n_keys)), F32))


def _dsa_sample_select(scores, n_keys, topk):
    rows, width = scores.shape
    body = functools.partial(_dsa_sample_select_body, n_keys=n_keys, topk=topk)
    return pl.pallas_call(
        body, out_shape=jax.ShapeDtypeStruct((rows, width), F32),
        compiler_params=pltpu.CompilerParams(vmem_limit_bytes=VMEM_LIMIT), name="dsa_sample_select",
    )(scores)


def _dsa_sample_attend_body(pt_ref, q_ref, kn_ref, vn_ref, b_ref, *refs):
    del pt_ref
    o_ref, lg_ref = refs[-2], refs[-1]
    n_pages = (len(refs) - 2) // 2
    k_refs, v_refs = refs[:n_pages], refs[n_pages:2 * n_pages]
    n_heads, hd, page = k_refs[0].shape
    lane = lax.broadcasted_iota(I32, (1, page), 1)
    bias = b_ref[...]
    for h in range(n_heads):
        rows = slice(hd * h, hd * (h + 1))
        qc = jnp.broadcast_to(q_ref[rows, :], (hd, page))
        for p in range(n_pages):
            lg_ref[p:p + 1, :] = jnp.sum(qc * k_refs[p][h], axis=0, keepdims=True)
        lg_new = jnp.sum(q_ref[rows, :] * kn_ref[rows, :], axis=0, keepdims=True)
        lg_ref[n_pages:n_pages + 1, :] = jnp.where(lane == 0, lg_new, 0.0)
        lg = lg_ref[...] + bias
        m = jnp.max(jnp.max(lg, axis=1, keepdims=True), axis=0, keepdims=True)
        pr = jnp.exp(lg - m)
        den = jnp.sum(jnp.sum(pr, axis=1, keepdims=True), axis=0, keepdims=True)
        acc = jnp.zeros((hd, page), F32)
        for p in range(n_pages):
            acc = acc + pr[p:p + 1, :] * v_refs[p][h]
        out = jnp.sum(acc, axis=1, keepdims=True) + pr[n_pages:n_pages + 1, 0:1] * vn_ref[rows, :]
        o_ref[rows, :] = out / den


def _dsa_sample_attend(page_table, q_col, kn_col, vn_col, bias3, k_t, v_t):
    nb, n_pages = page_table.shape
    n_heads, hd, page = k_t.shape[1:]
    per = lambda shape: pl.BlockSpec((None,) + shape, lambda b, pt: (b, 0, 0))
    page_specs = [pl.BlockSpec((None, n_heads, hd, page), functools.partial(lambda b, pt, p: (pt[b, p], 0, 0, 0), p=p))
                  for p in range(n_pages)]
    grid_spec = pltpu.PrefetchScalarGridSpec(
        num_scalar_prefetch=1, grid=(nb,),
        in_specs=[per(q_col.shape[1:]), per(kn_col.shape[1:]), per(vn_col.shape[1:]), per(bias3.shape[1:])] + page_specs * 2,
        out_specs=per(q_col.shape[1:]),
        scratch_shapes=[pltpu.VMEM((n_pages + 1, page), F32)])
    return pl.pallas_call(
        _dsa_sample_attend_body, grid_spec=grid_spec,
        out_shape=jax.ShapeDtypeStruct(q_col.shape, F32),
        compiler_params=_params(("arbitrary",)), name="dsa_sample_attend",
    )(page_table, q_col, kn_col, vn_col, bias3, *([k_t] * n_pages), *([v_t] * n_pages))


def _swiglu(h, wg_ref, wu_ref, wd_ref, n_chunks):
    fc = wg_ref.shape[-1] // n_chunks
    y = None
    for c in range(n_chunks):
        cols = slice(fc * c, fc * (c + 1))
        gate = _dot(h, wg_ref[:, cols])
        act = (gate * jax.nn.sigmoid(gate) * _dot(h, wu_ref[:, cols])).astype(BF16)
        part = _dot(act, wd_ref[cols, :])
        y = part if y is None else y + part
    return y


def _outproj_ffn_body(x_ref, ys_ref, ya_ref, wos_ref, woa_ref, g_ref, wg_ref, wu_ref, wd_ref, o_ref):
    x = x_ref[...] + _dot(ys_ref[...], wos_ref[...]) + _dot(ya_ref[...], woa_ref[...])
    h = _rms(x, g_ref[...]).astype(BF16)
    o_ref[...] = x + _swiglu(h, wg_ref, wu_ref, wd_ref, 2)


def _outproj_ffn(x2d, ys, ya, wos, woa, g, wg, wu, wd, tm):
    T, D = x2d.shape
    row = lambda w: pl.BlockSpec((tm, w), lambda i: (i, 0))
    single = lambda a: pl.BlockSpec(a.shape, lambda i: (0,) * a.ndim, pipeline_mode=pl.Buffered(1))
    return pl.pallas_call(
        _outproj_ffn_body, grid=(T // tm,),
        in_specs=[row(D), row(ys.shape[1]), row(ya.shape[1]), single(wos), single(woa), single(g),
                  single(wg), single(wu), single(wd)],
        out_specs=row(D), out_shape=jax.ShapeDtypeStruct((T, D), F32),
        compiler_params=_params(("arbitrary",)), name="outproj_ffn",
    )(x2d, ys, ya, wos, woa, g, wg, wu, wd)


def _pool_mix(h, window_sum, divisor, pw_ref, scale):
    gd = h.shape[1] // len(POOL_WINDOWS)
    cols = []
    for g, w in enumerate(POOL_WINDOWS):
        pooled = window_sum(g, w) / divisor(w) - h[:, gd * g:gd * (g + 1)]
        cols.append(_dot(pooled.astype(BF16), pw_ref[g]))
    return jnp.concatenate(cols, axis=1) * scale


def _pool_prompt_body(x_ref, g_ref, pw_ref, sc_ref, o_ref, hist_ref, ext_ref):
    tm, D = x_ref.shape
    gd = D // len(POOL_WINDOWS)
    halo = POOL_HIST + 1
    j = pl.program_id(1)

    @pl.when(j == 0)
    def _():
        ext_ref[0:halo] = jnp.zeros((halo, D), F32)

    x = x_ref[...]
    h = _rms(x, g_ref[...])
    ext_ref[halo:] = h
    pos = (j * tm + lax.broadcasted_iota(I32, (tm, 1), 0)).astype(F32)

    def window_sum(g, w):
        acc = h[:, gd * g:gd * (g + 1)]
        for k in range(1, w):
            acc = acc + ext_ref[halo - k:halo - k + tm, gd * g:gd * (g + 1)]
        return acc

    mixed = _pool_mix(h, window_sum, lambda w: jnp.minimum(float(w), pos + 1.0), pw_ref, sc_ref[...])
    o_ref[...] = x + mixed
    ext_ref[0:halo] = ext_ref[tm:tm + halo]

    @pl.when(j == pl.num_programs(1) - 1)
    def _():
        hist_ref[...] = h[tm - POOL_HIST:, :]


def _pool_prompt(x2d, g, pw, scale, batch, seq, tm):
    T, D = x2d.shape
    nblk = seq // tm
    row = pl.BlockSpec((tm, D), lambda b, j: (b * nblk + j, 0))
    return pl.pallas_call(
        _pool_prompt_body, grid=(batch, nblk),
        in_specs=[row, _const_spec(g.shape), _const_spec(pw.shape), _const_spec(scale.shape)],
        out_specs=(row, pl.BlockSpec((None, POOL_HIST, D), lambda b, j: (b, 0, 0))),
        out_shape=(jax.ShapeDtypeStruct((T, D), F32), jax.ShapeDtypeStruct((batch, POOL_HIST, D), F32)),
        scratch_shapes=[pltpu.VMEM((tm + POOL_HIST + 1, D), F32)],
        compiler_params=_params(("arbitrary", "arbitrary")), name="pool_prompt",
    )(x2d, g, pw, scale)


def _pool_sample_body(x_ref, hist_ref, g_ref, pw_ref, sc_ref, o_ref, nh_ref):
    D = x_ref.shape[1]
    gd = D // len(POOL_WINDOWS)
    x = x_ref[...]
    h = _rms(x, g_ref[...])

    def window_sum(g, w):
        acc = h[:, gd * g:gd * (g + 1)]
        for k in range(1, w):
            acc = acc + hist_ref[POOL_HIST - k, :, gd * g:gd * (g + 1)]
        return acc

    mixed = _pool_mix(h, window_sum, float, pw_ref, sc_ref[...])
    o_ref[...] = x + mixed
    for i in range(POOL_HIST - 1):
        nh_ref[i] = hist_ref[i + 1]
    nh_ref[POOL_HIST - 1] = h


def _pool_sample(x2d, hist_t, g, pw, scale):
    sds = jax.ShapeDtypeStruct
    return pl.pallas_call(
        _pool_sample_body, out_shape=(sds(x2d.shape, F32), sds(hist_t.shape, F32)),
        compiler_params=pltpu.CompilerParams(vmem_limit_bytes=VMEM_LIMIT), name="pool_sample",
    )(x2d, hist_t, g, pw, scale)


def _store_row_tiles(ref, x):
    for s in range(ref.shape[1]):
        ref[:, s, :] = x[:, LANES * s:LANES * (s + 1)]


def _load_row_tiles(ref):
    return jnp.concatenate([ref[:, s, :] for s in range(ref.shape[1])], axis=1)


def _router_body(x_ref, g_ref, wr_ref, h_ref, sel_ref, *, n_experts):
    h = _rms(x_ref[...], g_ref[...])
    _store_row_tiles(h_ref, h)
    logits = jnp.dot(h, wr_ref[...], preferred_element_type=F32, precision=lax.Precision.HIGHEST)
    lane = lax.broadcasted_iota(I32, logits.shape, 1).astype(F32)
    logits = jnp.where(lane < n_experts, logits, -jnp.inf)
    v1 = jnp.max(logits, axis=1, keepdims=True)
    i1 = jnp.min(jnp.where(logits == v1, lane, float(LANES)), axis=1, keepdims=True)
    rest = jnp.where(lane == i1, -jnp.inf, logits)
    v2 = jnp.max(rest, axis=1, keepdims=True)
    i2 = jnp.min(jnp.where(rest == v2, lane, float(LANES)), axis=1, keepdims=True)
    e2 = jnp.exp(v2 - v1)
    den = 1.0 + e2
    sel_ref[...] = jnp.where(lane == 0.0, i1, jnp.where(lane == 1.0, i2, jnp.where(lane == 2.0, 1.0 / den, e2 / den)))


def _router(x2d, g, wr_pad, n_experts, tm):
    T, D = x2d.shape
    row = lambda w: pl.BlockSpec((tm, w), lambda i: (i, 0))
    body = functools.partial(_router_body, n_experts=n_experts)
    return pl.pallas_call(
        body, grid=(T // tm,),
        in_specs=[row(D), _const_spec(g.shape), _const_spec(wr_pad.shape)],
        out_specs=(pl.BlockSpec((tm, D // LANES, LANES), lambda i: (i, 0, 0)), row(LANES)),
        out_shape=(jax.ShapeDtypeStruct((T, D // LANES, LANES), F32), jax.ShapeDtypeStruct((T, LANES), F32)),
        compiler_params=_params(("arbitrary",)), name="router",
    )(x2d, g, wr_pad)


def _row_gather(idx_ref, base, src_hbm, dst, sem):
    def issue(r, carry):
        pltpu.make_async_copy(src_hbm.at[idx_ref[base + r]], dst.at[r], sem).start()
        return carry

    lax.fori_loop(0, dst.shape[0], issue, 0, unroll=8)


def _row_gather_wait(src_hbm, dst, sem):
    pltpu.make_async_copy(src_hbm.at[pl.ds(0, dst.shape[0])], dst, sem).wait()


def _moe_dispatch_body(p1_ref, p2_ref, ends_ref, h_ref, x_hbm, zero_ref, stage, sems, *, tm):
    i = pl.program_id(0)
    last = pl.num_programs(0) - 1
    gt = zero_ref.shape[0]
    n_groups = ends_ref.shape[0]
    zsem = sems.at[2]
    slot = i % 2

    def fill_tile(e, act):
        if e < n_groups:
            start = ends_ref[e] - gt
            exists = ends_ref[e] > (ends_ref[e - 1] if e else 0)
        else:
            start = ends_ref[n_groups - 1] + (e - n_groups) * gt
            exists = start < x_hbm.shape[0]

        @pl.when(exists)
        def _():
            act(pltpu.make_async_copy(zero_ref, x_hbm.at[pl.ds(pl.multiple_of(start, gt), gt)], zsem))

    @pl.when(i == 0)
    def _():
        zero_ref[...] = jnp.zeros(zero_ref.shape, F32)
        for e in range(2 * n_groups):
            fill_tile(e, lambda cp: cp.start())
        for e in range(2 * n_groups):
            fill_tile(e, lambda cp: cp.wait())

    def wait_rows(s):
        for _ in range(2):
            pltpu.make_async_copy(stage.at[s], x_hbm.at[pl.ds(0, tm)], sems.at[s]).wait()

    @pl.when(i >= 2)
    def _():
        wait_rows(slot)

    stage[slot] = h_ref[...]

    def issue(r, carry):
        t = i * tm + r
        pltpu.make_async_copy(stage.at[slot, r], x_hbm.at[p1_ref[t]], sems.at[slot]).start()
        pltpu.make_async_copy(stage.at[slot, r], x_hbm.at[p2_ref[t]], sems.at[slot]).start()
        return carry

    lax.fori_loop(0, tm, issue, 0, unroll=8)

    @pl.when(i == last)
    def _():
        wait_rows(slot)

    @pl.when((i == last) & (i >= 1))
    def _():
        wait_rows(1 - slot)


def _moe_dispatch(pos1, pos2, ends, h_tiles, n_rows, tm, group_tile):
    T, S, _ = h_tiles.shape
    grid_spec = pltpu.PrefetchScalarGridSpec(
        num_scalar_prefetch=3, grid=(T // tm,),
        in_specs=[pl.BlockSpec((tm, S, LANES), lambda i, p1, p2, en: (i, 0, 0))],
        out_specs=pl.BlockSpec(memory_space=pl.ANY),
        scratch_shapes=[pltpu.VMEM((group_tile, S, LANES), F32), pltpu.VMEM((2, tm, S, LANES), F32),
                        pltpu.SemaphoreType.DMA((3,))])
    return pl.pallas_call(
        functools.partial(_moe_dispatch_body, tm=tm), grid_spec=grid_spec,
        out_shape=jax.ShapeDtypeStruct((n_rows, S, LANES), F32),
        compiler_params=pltpu.CompilerParams(dimension_semantics=("arbitrary",), vmem_limit_bytes=VMEM_LIMIT,
                                             disable_bounds_checks=True),
        name="moe_dispatch",
    )(pos1, pos2, ends, h_tiles)


def _moe_group_body(te_ref, nu_ref, x_ref, wg_ref, wu_ref, wd_ref, y_ref):
    del te_ref
    i = pl.program_id(0)

    @pl.when(i < nu_ref[0])
    def _():
        _store_row_tiles(y_ref, _swiglu(_load_row_tiles(x_ref).astype(BF16), wg_ref, wu_ref, wd_ref, 2))

    @pl.when(i >= nu_ref[0])
    def _():
        y_ref[...] = jnp.zeros(y_ref.shape, F32)


def _moe_group(tile_expert, n_used, x_tiles, wg, wu, wd, tm):
    P, S, _ = x_tiles.shape
    _, D, F = wg.shape
    wspec = lambda a, b: pl.BlockSpec((None, a, b), lambda i, te, nu: (te[i], 0, 0), pipeline_mode=pl.Buffered(1))
    grid_spec = pltpu.PrefetchScalarGridSpec(
        num_scalar_prefetch=2, grid=(P // tm,),
        in_specs=[pl.BlockSpec((tm, S, LANES), lambda i, te, nu: (jnp.minimum(i, nu[0] - 1), 0, 0)),
                  wspec(D, F), wspec(D, F), wspec(F, D)],
        out_specs=pl.BlockSpec((tm, S, LANES), lambda i, te, nu: (i, 0, 0)))
    return pl.pallas_call(
        _moe_group_body, grid_spec=grid_spec, out_shape=jax.ShapeDtypeStruct((P, S, LANES), F32),
        compiler_params=_params(("arbitrary",)), name="moe_group",
    )(tile_expert, n_used, x_tiles, wg, wu, wd)


def _moe_combine_body(p1_ref, p2_ref, x_ref, sel_ref, y_hbm, o_ref, buf, sem):
    i = pl.program_id(0)
    tm = x_ref.shape[0]
    slot = i % 2

    def gather(tile, s):
        _row_gather(p1_ref, tile * tm, y_hbm, buf.at[s, 0], sem.at[s, 0])
        _row_gather(p2_ref, tile * tm, y_hbm, buf.at[s, 1], sem.at[s, 1])

    @pl.when(i == 0)
    def _():
        gather(0, 0)

    @pl.when(i + 1 < pl.num_programs(0))
    def _():
        gather(i + 1, 1 - slot)

    _row_gather_wait(y_hbm, buf.at[slot, 0], sem.at[slot, 0])
    _row_gather_wait(y_hbm, buf.at[slot, 1], sem.at[slot, 1])
    sel = sel_ref[...]
    o_ref[...] = (x_ref[...] + sel[:, 2:3] * _load_row_tiles(buf.at[slot, 0])
                  + sel[:, 3:4] * _load_row_tiles(buf.at[slot, 1]))


def _moe_combine(pos1, pos2, x2d, sel, y_tiles, tm):
    T, D = x2d.shape
    S = y_tiles.shape[1]
    row = lambda w: pl.BlockSpec((tm, w), lambda i, p1, p2: (i, 0))
    grid_spec = pltpu.PrefetchScalarGridSpec(
        num_scalar_prefetch=2, grid=(T // tm,),
        in_specs=[row(D), row(LANES), pl.BlockSpec(memory_space=pl.ANY)], out_specs=row(D),
        scratch_shapes=[pltpu.VMEM((2, 2, tm, S, LANES), F32), pltpu.SemaphoreType.DMA((2, 2))])
    return pl.pallas_call(
        _moe_combine_body, grid_spec=grid_spec, out_shape=jax.ShapeDtypeStruct((T, D), F32),
        compiler_params=pltpu.CompilerParams(dimension_semantics=("arbitrary",), vmem_limit_bytes=VMEM_LIMIT,
                                             disable_bounds_checks=True),
        name="moe_combine",
    )(pos1, pos2, x2d, sel, y_tiles)


def _moe_plan(sel, n_experts, tm):
    T = sel.shape[0]
    experts = jnp.concatenate([sel[:, 0], sel[:, 1]]).astype(I32)
    onehot = (experts[:, None] == jnp.arange(n_experts, dtype=I32)[None, :]).astype(I32)
    rank = jnp.cumsum(onehot, axis=0) - onehot
    counts = jnp.sum(onehot, axis=0)
    padded = (counts + tm - 1) // tm * tm
    ends = jnp.cumsum(padded)
    pos = jnp.sum(onehot * ((ends - padded)[None, :] + rank), axis=1)
    n_rows = (2 * T + n_experts * (tm - 1)) // tm * tm
    tile_start = jnp.arange(n_rows // tm, dtype=I32) * tm
    tile_expert = jnp.minimum(jnp.sum((tile_start[:, None] >= ends[None, :]).astype(I32), axis=1), n_experts - 1)
    n_used = (ends[-1] // tm).astype(I32).reshape(1)
    return tile_expert, n_used, ends.astype(I32), n_rows, pos[:T], pos[T:]


def _rope_tables(pos):
    half = HEAD_DIM // 2
    inv = ROPE_THETA ** (-jnp.arange(half, dtype=F32) / half)
    ang = pos[:, None] * inv[None, :]
    cos, sin = jnp.cos(ang), jnp.sin(ang)
    reps = LANES // HEAD_DIM
    return jnp.tile(jnp.concatenate([cos, cos], axis=1), (1, reps)), jnp.tile(jnp.concatenate([-sin, sin], axis=1), (1, reps))


def kernel(x_prompt, x_sample, cache_k, cache_v, cache_kidx, page_table, state_ssm_re, state_ssm_im, state_pool, meta_tokens, norm_mix0, w_in0, q_norm, k_norm, ssm_lambda_re, ssm_lambda_im, ssm_log_dt, ssm_b_re, ssm_b_im, ssm_c_re, ssm_c_im, ssm_d, ssm_w_glu, w_out0, norm_ffn0, ffn_w_gate, ffn_w_up, ffn_w_down, norm_mix1, pool_w, pool_scale, norm_ffn1, router_w, moe_w_gate, moe_w_up, moe_w_down):
    B, S, D = x_prompt.shape
    DB, DS, _ = x_sample.shape
    assert DS == 1, "one new token per sample sequence"
    L = S + N_META
    n_pool, page = cache_k.shape[1], cache_k.shape[2]
    n_pages = page_table.shape[1]
    past = n_pages * page
    d_att = N_HEADS * HEAD_DIM
    d_qi = N_IDX_HEADS * IDX_DIM
    G, P = ssm_lambda_re.shape[1:]
    d_ssm = G * SSM_GROUP
    n_state = G * P
    E = router_w.shape[-1]
    topk_p = min(TOPK_MAX, S // 4)
    topk_s = min(TOPK_MAX, (past + DS) // 4)
    tm = _row_block(L, 1024)
    assert tm >= POOL_HIST + 1

    x_p = jnp.concatenate([jnp.broadcast_to(meta_tokens[None], (B, N_META, D)), x_prompt], axis=1).reshape(B * L, D)
    x_s = x_sample.reshape(DB, D)

    w_in = w_in0[0]
    o1, o2, o3, o4, o5, o6 = d_ssm, d_ssm + d_att, d_ssm + 2 * d_att, d_ssm + 3 * d_att, d_ssm + 3 * d_att + d_qi, d_ssm + 3 * d_att + d_qi + IDX_DIM
    w_kw = jnp.pad(w_in[:, o5:], ((0, 0), (0, LANES - (w_in.shape[1] - o5))))
    ws = tuple(w.astype(BF16) for w in (w_in[:, :o1], w_in[:, o1:o2], w_in[:, o2:o3], w_in[:, o3:o4], w_in[:, o4:o5], w_kw))
    g_mix0 = norm_mix0[0].reshape(1, D)
    qn = jnp.tile(q_norm[0], N_HEADS).reshape(1, d_att)
    kn = jnp.tile(k_norm[0], N_HEADS).reshape(1, d_att)
    head_of = jnp.arange(d_att) // HEAD_DIM
    hm = jnp.where(head_of[:, None] == head_of[None, :], 1.0 / HEAD_DIM, 0.0).astype(BF16)
    cos_p, sin_p = _rope_tables(jnp.arange(L, dtype=F32))
    cos_s, sin_s = _rope_tables(jnp.full((DB,), float(past), F32))

    u_p, q_p, k_p, v_p, qi_p, ki_p, kw_p, kb_p, vb_p, kib_p = _inproj(x_p, cos_p, sin_p, L // tm, tm, g_mix0, ws, qn, kn, hm)
    u_s, q_s, k_s, v_s, qi_s, ki_s, kw_s, _, _, _ = _inproj(x_s, cos_s, sin_s, 1, DB, g_mix0, ws, qn, kn, hm)

    lbr, lbi, bbr, bbi = _s5_prep(ssm_lambda_re[0], ssm_lambda_im[0], ssm_log_dt[0], ssm_b_re[0], ssm_b_im[0])
    gps = LANES // SSM_GROUP
    n_slab = G // gps
    eye = jnp.eye(gps, dtype=F32)

    def b_slabs(bt):
        return jnp.einsum('sgcp,gh->sgchp', bt.reshape(n_slab, gps, SSM_GROUP, P), eye).reshape(n_slab, LANES, gps * P)

    def c_slabs(c):
        return jnp.einsum('sgcp,gh->sgphc', c.reshape(n_slab, gps, SSM_GROUP, P), eye).reshape(n_slab, gps * P, LANES)

    wb = jnp.concatenate([b_slabs(bbr), b_slabs(bbi)], axis=2)
    wc = jnp.stack([c_slabs(ssm_c_re[0]), -c_slabs(ssm_c_im[0])], axis=1).astype(BF16)
    lbr_f, lbi_f = lbr.reshape(1, n_state), lbi.reshape(1, n_state)
    d_skip = ssm_d[0].reshape(1, d_ssm)
    wglu = ssm_w_glu[0].astype(BF16)
    assert B % SUBLANES == 0, "the S5 scan keeps one batch row per sublane"
    steps = max(d for d in range(1, L + 1) if L % d == 0 and d * B <= 512 and (d * B) % (2 * SUBLANES) == 0)
    u_tm = u_p.reshape(B, L, d_ssm).swapaxes(0, 1).reshape(L * B, d_ssm)
    zero_h = jnp.zeros((B, n_state), F32)
    ys_tm, hre_p, him_p = _s5(u_tm, zero_h, zero_h, lbr_f, lbi_f, wb, wc, d_skip, wglu, B, steps, False)
    ys_p = ys_tm.reshape(L, B, d_ssm).swapaxes(0, 1).reshape(B * L, d_ssm)
    ys_s, hre_s, him_s = _s5(u_s, state_ssm_re[0].reshape(DB, n_state), state_ssm_im[0].reshape(DB, n_state),
                             lbr_f, lbi_f, wb, wc, d_skip, wglu, DB, 1, True)

    ya_p = _dsa_prompt(qi_p, kw_p, q_p, kib_p, kb_p, vb_p, B, L, tm, topk_p)
    assert page == LANES
    w3 = kw_s[:, IDX_DIM:IDX_DIM + N_IDX_HEADS].reshape(DB, N_IDX_HEADS, 1)
    scores = _dsa_sample_scores(page_table, qi_s.astype(F32).reshape(DB, d_qi, 1), w3, ki_s.reshape(DB, IDX_DIM, 1),
                                jnp.transpose(cache_kidx[0], (0, 2, 1)))
    bias = _dsa_sample_select(scores.reshape(DB, (n_pages + 1) * page), past + DS, topk_s)
    ya_s = _dsa_sample_attend(page_table, q_s.astype(F32).reshape(DB, d_att, 1), k_s.reshape(DB, d_att, 1),
                              v_s.reshape(DB, d_att, 1), bias.reshape(DB, n_pages + 1, page),
                              jnp.transpose(cache_k[0], (0, 2, 3, 1)), jnp.transpose(cache_v[0], (0, 2, 3, 1)))
    ya_s = ya_s.reshape(DB, d_att).astype(BF16)

    w_out = w_out0[0].astype(BF16)
    ffn = (w_out[:d_ssm], w_out[d_ssm:], norm_ffn0[0].reshape(1, D), ffn_w_gate[0].astype(BF16),
           ffn_w_up[0].astype(BF16), ffn_w_down[0].astype(BF16))
    x_p = _outproj_ffn(x_p, ys_p, ya_p, *ffn, tm)
    x_s = _outproj_ffn(x_s, ys_s, ya_s, *ffn, DB)

    g_mix1 = norm_mix1[0].reshape(1, D)
    pw = pool_w[0].astype(BF16)
    psc = pool_scale[0].reshape(1, D)
    x_p, pool_p = _pool_prompt(x_p, g_mix1, pw, psc, B, L, tm)
    x_s, hist_t = _pool_sample(x_s, state_pool[0].swapaxes(0, 1), g_mix1, pw, psc)
    pool_s = hist_t.swapaxes(0, 1)

    g_ffn1 = norm_ffn1[0].reshape(1, D)
    wr_pad = jnp.pad(router_w[0], ((0, 0), (0, LANES - E)))
    wg, wu, wd = moe_w_gate[0].astype(BF16), moe_w_up[0].astype(BF16), moe_w_down[0].astype(BF16)
    x_all = jnp.concatenate([x_p, x_s], axis=0)
    T_all = x_all.shape[0]
    tr = _row_block(T_all, MOE_TILE)
    h_tiles, sel = _router(x_all, g_ffn1, wr_pad, E, tr)
    tile_expert, n_used, ends, n_rows, pos1, pos2 = _moe_plan(sel, E, MOE_TILE)
    x_tiles = _moe_dispatch(pos1, pos2, ends, h_tiles, n_rows, tr, MOE_TILE)
    y_tiles = _moe_group(tile_expert, n_used, x_tiles, wg, wu, wd, MOE_TILE)
    x_all = _moe_combine(pos1, pos2, x_all, sel, y_tiles, tr)
    x_p, x_s = x_all[:B * L], x_all[B * L:]

    y_prompt = x_p.reshape(B, L, D)[:, N_META:]
    y_sample = x_s.reshape(DB, 1, D)
    k_prompt = k_p.reshape(1, B, L, N_HEADS, HEAD_DIM)
    v_prompt = v_p.reshape(1, B, L, N_HEADS, HEAD_DIM)
    kidx_prompt = ki_p.reshape(1, B, L, IDX_DIM)
    return (y_prompt, y_sample, k_prompt, v_prompt, kidx_prompt,
            hre_p.reshape(1, B, G, P), him_p.reshape(1, B, G, P), pool_p[None],
            k_s.reshape(1, DB, 1, N_HEADS, HEAD_DIM), v_s.reshape(1, DB, 1, N_HEADS, HEAD_DIM),
            ki_s.reshape(1, DB, 1, IDX_DIM), hre_s.reshape(1, DB, G, P), him_s.reshape(1, DB, G, P), pool_s[None])
```

```python
import functools
import math

import jax
import jax.numpy as jnp
from jax import lax
from jax.experimental import pallas as pl
from jax.experimental.pallas import tpu as pltpu

F32 = jnp.float32
BF16 = jnp.bfloat16
I32 = jnp.int32

N_META = 16
SSM_GROUP = 16
SSM_STATE = 64
N_HEADS = 8
HEAD_DIM = 64
N_IDX_HEADS = 8
IDX_DIM = 64
TOPK_MAX = 256
ROPE_THETA = 10000.0
POOL_WINDOWS = (2, 4, 8, 16)
POOL_HIST = max(POOL_WINDOWS) - 1
TOP_K_EXPERTS = 2
EPS = 1e-6

LANES = 128
SUBLANES = 8
VMEM_LIMIT = 56 * 1024 * 1024
OUT_CHUNK = 2 * SUBLANES
MOE_TILE = 256
BISECT_ITERS = 16
NEG_BIG = -1e30


def _params(sem):
    return pltpu.CompilerParams(dimension_semantics=sem, vmem_limit_bytes=VMEM_LIMIT)


def _row_block(n, cap, mult=2 * SUBLANES):
    best = None
    for d in range(mult, min(n, cap) + 1, mult):
        if n % d == 0:
            best = d
    assert best is not None, n
    return best


def _rms(x, g):
    return x * lax.rsqrt(jnp.mean(x * x, axis=-1, keepdims=True) + EPS) * g


def _dot(a, b):
    return jnp.dot(a, b, preferred_element_type=F32)


def _dot_nt(a, b):
    return lax.dot_general(a, b, (((1,), (1,)), ((), ())), preferred_element_type=F32)


def _const_spec(shape):
    nd = len(shape)
    return pl.BlockSpec(shape, lambda *_: (0,) * nd)


def _inproj_body(x_ref, g_ref, wu_ref, wq_ref, wk_ref, wv_ref, wqi_ref, wkw_ref, qn_ref, kn_ref, hm_ref,
                 cos_ref, sin_ref,
                 u_ref, q_ref, k_ref, v_ref, qi_ref, ki_ref, kw_ref, kb_ref, vb_ref, kib_ref):
    h = _rms(x_ref[...], g_ref[...]).astype(BF16)
    cos = cos_ref[...]
    sin = sin_ref[...]
    lane = lax.broadcasted_iota(I32, (1, LANES), 1)
    lo_half = (lane & (HEAD_DIM // 2)) == 0

    def rope(z):
        cols = []
        for j in range(z.shape[1] // LANES):
            zs = z[:, LANES * j:LANES * (j + 1)]
            partner = jnp.where(lo_half, pltpu.roll(zs, LANES - HEAD_DIM // 2, 1), pltpu.roll(zs, HEAD_DIM // 2, 1))
            cols.append(zs * cos + partner * sin)
        return cols[0] if len(cols) == 1 else jnp.concatenate(cols, axis=1)

    def head_norm(z, gn):
        ms = _dot((z * z).astype(BF16), hm_ref[...])
        return z * lax.rsqrt(ms + EPS) * gn

    u_ref[...] = _dot(h, wu_ref[...])
    q = rope(head_norm(_dot(h, wq_ref[...]), qn_ref[...]))
    q_ref[...] = (q * (HEAD_DIM ** -0.5)).astype(BF16)
    k = rope(head_norm(_dot(h, wk_ref[...]), kn_ref[...]))
    k_ref[...] = k
    kb_ref[...] = k.astype(BF16)
    v = _dot(h, wv_ref[...])
    v_ref[...] = v
    vb_ref[...] = v.astype(BF16)
    qi_ref[...] = (rope(_dot(h, wqi_ref[...])) * (IDX_DIM ** -0.5)).astype(BF16)
    zkw = _dot(h, wkw_ref[...])
    kir = rope(zkw)
    kw_ref[...] = jnp.where(lane < IDX_DIM, kir, zkw * (N_IDX_HEADS ** -0.5))
    ki_ref[...] = kir[:, :IDX_DIM]
    kib_ref[...] = jnp.where(lane < IDX_DIM, kir, pltpu.roll(kir, IDX_DIM, 1)).astype(BF16)


def _inproj(x2d, cos_t, sin_t, tab_blocks, tm, g, ws, qn, kn, hm):
    T, D = x2d.shape
    wu, wq, wk, wv, wqi, wkw = ws
    d_ssm, d_att, d_qi = wu.shape[1], wq.shape[1], wqi.shape[1]
    row = lambda w: pl.BlockSpec((tm, w), lambda i: (i, 0))
    tab = pl.BlockSpec((tm, LANES), lambda i: (i % tab_blocks, 0))
    out_shape = (
        jax.ShapeDtypeStruct((T, d_ssm), F32),
        jax.ShapeDtypeStruct((T, d_att), BF16),
        jax.ShapeDtypeStruct((T, d_att), F32),
        jax.ShapeDtypeStruct((T, d_att), F32),
        jax.ShapeDtypeStruct((T, d_qi), BF16),
        jax.ShapeDtypeStruct((T, IDX_DIM), F32),
        jax.ShapeDtypeStruct((T, LANES), F32),
        jax.ShapeDtypeStruct((T, d_att), BF16),
        jax.ShapeDtypeStruct((T, d_att), BF16),
        jax.ShapeDtypeStruct((T, LANES), BF16),
    )
    out_specs = (row(d_ssm), row(d_att), row(d_att), row(d_att), row(d_qi), row(IDX_DIM), row(LANES),
                 row(d_att), row(d_att), row(LANES))
    in_specs = [row(D), _const_spec(g.shape)] + [_const_spec(w.shape) for w in ws] + [
        _const_spec(qn.shape), _const_spec(kn.shape), _const_spec(hm.shape), tab, tab]
    return pl.pallas_call(
        _inproj_body, grid=(T // tm,), in_specs=in_specs, out_specs=out_specs, out_shape=out_shape,
        compiler_params=_params(("arbitrary",)), name="inproj",
    )(x2d, g, *ws, qn, kn, hm, cos_t, sin_t)


def _s5_prep_body(lr_ref, li_ref, ldt_ref, btr_ref, bti_ref, lbr_ref, lbi_ref, bbr_ref, bbi_ref):
    lr = lr_ref[...]
    li = li_ref[...]
    dt = jnp.exp(ldt_ref[...])
    mag = jnp.exp(lr * dt)
    lbr = mag * jnp.cos(li * dt)
    lbi = mag * jnp.sin(li * dt)
    lbr_ref[...] = lbr
    lbi_ref[...] = lbi
    den = lr * lr + li * li
    cr = ((lbr - 1.0) * lr + lbi * li) / den
    ci = (lbi * lr - (lbr - 1.0) * li) / den
    btr = btr_ref[...]
    bti = bti_ref[...]
    bbr_ref[...] = cr * btr - ci * bti
    bbi_ref[...] = cr * bti + ci * btr


def _s5_prep(lam_re, lam_im, log_dt, b_re, b_im):
    G, P = lam_re.shape
    C = b_re.shape[-1]
    btr = jnp.swapaxes(b_re, 1, 2)
    bti = jnp.swapaxes(b_im, 1, 2)
    sds = jax.ShapeDtypeStruct
    return pl.pallas_call(
        _s5_prep_body,
        out_shape=(sds((G, 1, P), F32), sds((G, 1, P), F32), sds((G, C, P), F32), sds((G, C, P), F32)),
        name="s5_prep",
    )(lam_re.reshape(G, 1, P), lam_im.reshape(G, 1, P), log_dt.reshape(G, 1, 1), btr, bti)


def _s5_body(u_ref, h0r_ref, h0i_ref, lbr_ref, lbi_ref, wb_ref, wc_ref, d_ref, wglu_ref,
             y_ref, hr_ref, hi_ref, bu_ref, st_ref, *, nb, steps, precise):
    n_slab = wb_ref.shape[0]
    cin = wb_ref.shape[1]
    sw = wb_ref.shape[2] // 2
    n_state = n_slab * sw

    @pl.when(pl.program_id(0) == 0)
    def _():
        st_ref[0] = h0r_ref[...]
        st_ref[1] = h0i_ref[...]

    u = u_ref[...]
    for i in range(n_slab):
        ui = u[:, cin * i:cin * (i + 1)]
        if precise:
            bu = jnp.dot(ui, wb_ref[i], preferred_element_type=F32, precision=lax.Precision.HIGHEST)
        else:
            bu = _dot(ui.astype(BF16), wb_ref[i].astype(BF16))
        bu_ref[:, sw * i:sw * (i + 1)] = bu[:, :sw]
        bu_ref[:, n_state + sw * i:n_state + sw * (i + 1)] = bu[:, sw:]

    for i in range(n_slab):
        re_cols = slice(sw * i, sw * (i + 1))
        im_cols = slice(n_state + sw * i, n_state + sw * (i + 1))
        lr = jnp.broadcast_to(lbr_ref[:, re_cols], (nb, sw))
        li = jnp.broadcast_to(lbi_ref[:, re_cols], (nb, sw))

        def step(t, carry, re_cols=re_cols, im_cols=im_cols, lr=lr, li=li):
            hr, hi = carry
            rows = pl.ds(pl.multiple_of(t * nb, nb), nb)
            nhr = lr * hr - li * hi + bu_ref[rows, re_cols]
            nhi = lr * hi + li * hr + bu_ref[rows, im_cols]
            bu_ref[rows, re_cols] = nhr
            bu_ref[rows, im_cols] = nhi
            return nhr, nhi

        hr, hi = lax.fori_loop(0, steps, step, (st_ref[0, :, re_cols], st_ref[1, :, re_cols]))
        st_ref[0, :, re_cols] = hr
        st_ref[1, :, re_cols] = hi

    ys = []
    for i in range(n_slab):
        h_re = bu_ref[:, sw * i:sw * (i + 1)].astype(BF16)
        h_im = bu_ref[:, n_state + sw * i:n_state + sw * (i + 1)].astype(BF16)
        ys.append(_dot(h_re, wc_ref[i, 0]) + _dot(h_im, wc_ref[i, 1]))
    y = jnp.concatenate(ys, axis=1) + d_ref[...] * u
    y = 0.5 * y * (1.0 + jnp.tanh(math.sqrt(2.0 / math.pi) * (y + 0.044715 * (y * y * y))))
    y = y * jax.nn.sigmoid(_dot(y.astype(BF16), wglu_ref[...]))
    y_ref[...] = y.astype(BF16)
    hr_ref[...] = st_ref[0]
    hi_ref[...] = st_ref[1]


def _s5(u_tm, h0_re, h0_im, lbr_flat, lbi_flat, wb, wc, d_skip, wglu, nb, steps, precise):
    rows_total, d_ssm = u_tm.shape
    n_state = h0_re.shape[1]
    rows = nb * steps
    sds = jax.ShapeDtypeStruct
    body = functools.partial(_s5_body, nb=nb, steps=steps, precise=precise)
    return pl.pallas_call(
        body, grid=(rows_total // rows,),
        in_specs=[pl.BlockSpec((rows, d_ssm), lambda c: (c, 0)), _const_spec(h0_re.shape), _const_spec(h0_im.shape),
                  _const_spec(lbr_flat.shape), _const_spec(lbi_flat.shape), _const_spec(wb.shape),
                  _const_spec(wc.shape), _const_spec(d_skip.shape), _const_spec(wglu.shape)],
        out_specs=(pl.BlockSpec((rows, d_ssm), lambda c: (c, 0)), _const_spec(h0_re.shape), _const_spec(h0_im.shape)),
        out_shape=(sds((rows_total, d_ssm), BF16), sds(h0_re.shape, F32), sds(h0_im.shape, F32)),
        scratch_shapes=[pltpu.VMEM((rows, 2 * n_state), F32), pltpu.VMEM((2, nb, n_state), F32)],
        compiler_params=_params(("arbitrary",)), name="s5_precise" if precise else "s5",
    )(u_tm, h0_re, h0_im, lbr_flat, lbi_flat, wb, wc, d_skip, wglu)


def _topk_bias(s_ref, n, kk):
    rows = s_ref.shape[0]
    cols = slice(0, n)

    def count(pred):
        return jnp.sum(jnp.where(pred(s_ref[:, cols]), 1.0, 0.0), axis=1, keepdims=True)

    def largest_below(hi):
        s = s_ref[:, cols]
        return jnp.max(jnp.where(s < hi, s, -jnp.inf), axis=1, keepdims=True)

    s0 = s_ref[:, cols]
    smax = jnp.max(s0, axis=1, keepdims=True)
    smin = jnp.min(jnp.where(s0 > -jnp.inf, s0, jnp.inf), axis=1, keepdims=True)

    def bisect(_, c):
        lo, hi = c
        mid = lo + (hi - lo) * 0.5
        ge = count(lambda s: s >= mid) >= kk
        return jnp.where(ge, mid, lo), jnp.where(ge, hi, mid)

    _, hi = lax.fori_loop(0, BISECT_ITERS, bisect, (smin, smax + (smax - smin) + 1.0))
    thr = largest_below(hi)
    cnt = count(lambda s: s >= thr)

    def short(c):
        return jnp.min(c[2] - kk) < 0.0

    def lower(c):
        hi, thr, cnt = c
        hi = jnp.where(cnt < kk, thr, hi)
        thr = largest_below(hi)
        return hi, thr, count(lambda s: s >= thr)

    _, thr, _ = lax.while_loop(short, lower, (hi, thr, cnt))
    need = kk - count(lambda s: s > thr)
    blk = 2 * LANES
    tri = (lax.broadcasted_iota(I32, (blk, blk), 0) <= lax.broadcasted_iota(I32, (blk, blk), 1)).astype(BF16)
    seen = jnp.zeros((rows, 1), F32)
    for c0 in range(0, n, blk):
        w = min(blk, n - c0)
        s = s_ref[:, c0:c0 + w]
        tie = s == thr
        upto = seen + _dot(jnp.where(tie, 1.0, 0.0).astype(BF16), tri[:w, :w])
        seen = upto[:, w - 1:w]
        s_ref[:, c0:c0 + w] = jnp.where((s > thr) | (tie & (upto <= need)), 0.0, NEG_BIG)


def _dsa_prompt_block(q0, n, qim_ref, qm_ref, w_ref, kip_ref, kp_ref, vp_ref, s_ref, acc_ref, topk):
    tq = s_ref.shape[0]
    cols = slice(0, n)
    lane = lax.broadcasted_iota(I32, (1, LANES), 1)
    s_ref[:, cols] = jnp.zeros((tq, n), F32)

    def score_head(h, carry):
        d = jnp.maximum(_dot_nt(qim_ref[h], kip_ref[cols, :]), 0.0)
        s_ref[:, cols] += w_ref[h] * d
        return carry

    lax.fori_loop(0, N_IDX_HEADS, score_head, 0)
    qpos = q0 + lax.broadcasted_iota(I32, (tq, 1), 0)
    kpos = lax.broadcasted_iota(I32, (1, n), 1)
    s_ref[:, cols] = jnp.where(kpos <= qpos, s_ref[:, cols], -jnp.inf)
    _topk_bias(s_ref, n, jnp.minimum(qpos + 1, topk).astype(F32))

    def head_pair(sp, carry):
        outs = []
        for hh in range(2):
            lg = _dot_nt(qm_ref[2 * sp + hh], kp_ref[sp, cols, :]) + s_ref[:, cols]
            p = jnp.exp(lg - jnp.max(lg, axis=1, keepdims=True))
            pv = _dot(p.astype(BF16), vp_ref[sp, cols, :])
            outs.append(pv[:, :LANES] / pv[:, LANES:LANES + 1])
        acc_ref[sp] = jnp.where(lane < HEAD_DIM, outs[0], outs[1])
        return carry

    lax.fori_loop(0, acc_ref.shape[0], head_pair, 0)


def _dsa_prompt_body(qi_ref, kw_ref, q_ref, kib_ref, kb_ref, vb_ref, o_ref,
                     kip_ref, kp_ref, vp_ref, qim_ref, qm_ref, w_ref, s_ref, acc_ref, *, seq, topk):
    tq = q_ref.shape[0]
    n_slab, lp, _ = kp_ref.shape
    j = pl.program_id(1)
    lane = lax.broadcasted_iota(I32, (1, LANES), 1)

    @pl.when(j == 0)
    def _():
        kip_ref[0:seq] = kib_ref[...]
        if lp > seq:
            kip_ref[seq:] = jnp.zeros((lp - seq, LANES), BF16)
        for sp in range(n_slab):
            sl = slice(LANES * sp, LANES * (sp + 1))
            kp_ref[sp, 0:seq] = kb_ref[:, sl]
            vp_ref[sp, 0:seq, 0:LANES] = vb_ref[:, sl]
            vp_ref[sp, :, LANES:] = jnp.where(lane == 0, 1.0, 0.0).astype(BF16) + jnp.zeros((lp, LANES), BF16)
            if lp > seq:
                kp_ref[sp, seq:] = jnp.zeros((lp - seq, LANES), BF16)
                vp_ref[sp, seq:, 0:LANES] = jnp.zeros((lp - seq, LANES), BF16)

    kw = kw_ref[...]
    for h in range(N_IDX_HEADS):
        head_lanes = (lane // IDX_DIM) == (h % 2)
        sl = slice(LANES * (h // 2), LANES * (h // 2 + 1))
        qim_ref[h] = jnp.where(head_lanes, qi_ref[:, sl], jnp.zeros((), BF16))
        qm_ref[h] = jnp.where(head_lanes, q_ref[:, sl], jnp.zeros((), BF16))
        w_ref[h] = kw[:, IDX_DIM + h:IDX_DIM + h + 1]

    for jj in range(seq // tq):
        n = min(-(-((jj + 1) * tq) // LANES) * LANES, lp)

        @pl.when(j == jj)
        def _(jj=jj, n=n):
            _dsa_prompt_block(jj * tq, n, qim_ref, qm_ref, w_ref, kip_ref, kp_ref, vp_ref, s_ref, acc_ref, topk)

    for sp in range(n_slab):
        o_ref[:, LANES * sp:LANES * (sp + 1)] = acc_ref[sp].astype(BF16)


def _dsa_prompt(qi, kw, q, kib, kb, vb, batch, seq, tq, topk):
    T, d_att = q.shape
    nq = seq // tq
    lp = -(-seq // LANES) * LANES
    n_slab = d_att // LANES
    qrow = lambda w: pl.BlockSpec((tq, w), lambda b, j: (b * nq + j, 0))
    full = lambda w: pl.BlockSpec((None, seq, w), lambda b, j: (b, 0, 0))
    body = functools.partial(_dsa_prompt_body, seq=seq, topk=topk)
    return pl.pallas_call(
        body, grid=(batch, nq),
        in_specs=[qrow(qi.shape[1]), qrow(LANES), qrow(d_att), full(LANES), full(d_att), full(d_att)],
        out_specs=qrow(d_att),
        out_shape=jax.ShapeDtypeStruct((T, d_att), BF16),
        scratch_shapes=[pltpu.VMEM((lp, LANES), BF16), pltpu.VMEM((n_slab, lp, LANES), BF16),
                        pltpu.VMEM((n_slab, lp, 2 * LANES), BF16),
                        pltpu.VMEM((N_IDX_HEADS, tq, LANES), BF16), pltpu.VMEM((N_HEADS, tq, LANES), BF16),
                        pltpu.VMEM((N_IDX_HEADS, tq, 1), F32),
                        pltpu.VMEM((tq, lp), F32), pltpu.VMEM((n_slab, tq, LANES), F32)],
        compiler_params=_params(("arbitrary", "arbitrary")), name="dsa_prompt",
    )(qi, kw, q, kib.reshape(batch, seq, LANES), kb.reshape(batch, seq, d_att), vb.reshape(batch, seq, d_att))


def _dsa_sample_scores_body(pt_ref, qi_ref, w_ref, kin_ref, *refs):
    del pt_ref
    page_refs, s_ref = refs[:-1], refs[-1]
    n_pages = len(page_refs)
    idx_dim, page = page_refs[0].shape
    heads = [slice(idx_dim * h, idx_dim * (h + 1)) for h in range(N_IDX_HEADS)]
    qcols = [jnp.broadcast_to(qi_ref[rows, :], (idx_dim, page)) for rows in heads]
    ws = [w_ref[h:h + 1, :] for h in range(N_IDX_HEADS)]

    def score(dot_of_head):
        s = None
        for h in range(N_IDX_HEADS):
            t = ws[h] * jnp.maximum(dot_of_head(h), 0.0)
            s = t if s is None else s + t
        return s

    for p in range(n_pages):
        kt = page_refs[p][...]
        s_ref[p:p + 1, :] = score(lambda h: jnp.sum(qcols[h] * kt, axis=0, keepdims=True))
    kin = kin_ref[...]
    s_new = score(lambda h: jnp.sum(qi_ref[heads[h], :] * kin, axis=0, keepdims=True))
    lane = lax.broadcasted_iota(I32, (1, page), 1)
    s_ref[n_pages:n_pages + 1, :] = jnp.where(lane == 0, s_new, 0.0)


def _dsa_sample_scores(page_table, qi_col, w3, ki_col, kidx_t):
    nb, n_pages = page_table.shape
    idx_dim, page = kidx_t.shape[1:]
    per = lambda shape: pl.BlockSpec((None,) + shape, lambda b, pt: (b, 0, 0))
    page_specs = [pl.BlockSpec((None, idx_dim, page), functools.partial(lambda b, pt, p: (pt[b, p], 0, 0), p=p))
                  for p in range(n_pages)]
    grid_spec = pltpu.PrefetchScalarGridSpec(
        num_scalar_prefetch=1, grid=(nb,),
        in_specs=[per(qi_col.shape[1:]), per(w3.shape[1:]), per(ki_col.shape[1:])] + page_specs,
        out_specs=per((n_pages + 1, page)))
    return pl.pallas_call(
        _dsa_sample_scores_body, grid_spec=grid_spec,
        out_shape=jax.ShapeDtypeStruct((nb, n_pages + 1, page), F32),
        compiler_params=_params(("arbitrary",)), name="dsa_sample_scores",
    )(page_table, qi_col, w3, ki_col, *([kidx_t] * n_pages))


def _dsa_sample_select_body(s_ref, b_ref, *, n_keys, topk):
    rows, width = s_ref.shape
    pos = lax.broadcasted_iota(I32, (1, width), 1)
    b_ref[...] = jnp.where(pos < n_keys, s_ref[...], -jnp.inf)
    _topk_bias(b_ref, width, jnp.full((rows, 1), float(min(topk, n_keys)), F32))


def _dsa_sample_select(scores, n_keys, topk):
    rows, width = scores.shape
    body = functools.partial(_dsa_sample_select_body, n_keys=n_keys, topk=topk)
    return pl.pallas_call(
        body, out_shape=jax.ShapeDtypeStruct((rows, width), F32),
        compiler_params=pltpu.CompilerParams(vmem_limit_bytes=VMEM_LIMIT), name="dsa_sample_select",
    )(scores)


def _dsa_sample_attend_body(pt_ref, q_ref, kn_ref, vn_ref, b_ref, *refs):
    del pt_ref
    o_ref, lg_ref = refs[-2], refs[-1]
    n_pages = (len(refs) - 2) // 2
    k_refs, v_refs = refs[:n_pages], refs[n_pages:2 * n_pages]
    n_heads, hd, page = k_refs[0].shape
    lane = lax.broadcasted_iota(I32, (1, page), 1)
    bias = b_ref[...]
    for h in range(n_heads):
        rows = slice(hd * h, hd * (h + 1))
        qc = jnp.broadcast_to(q_ref[rows, :], (hd, page))
        for p in range(n_pages):
            lg_ref[p:p + 1, :] = jnp.sum(qc * k_refs[p][h], axis=0, keepdims=True)
        lg_new = jnp.sum(q_ref[rows, :] * kn_ref[rows, :], axis=0, keepdims=True)
        lg_ref[n_pages:n_pages + 1, :] = jnp.where(lane == 0, lg_new, 0.0)
        lg = lg_ref[...] + bias
        m = jnp.max(jnp.max(lg, axis=1, keepdims=True), axis=0, keepdims=True)
        pr = jnp.exp(lg - m)
        den = jnp.sum(jnp.sum(pr, axis=1, keepdims=True), axis=0, keepdims=True)
        acc = jnp.zeros((hd, page), F32)
        for p in range(n_pages):
            acc = acc + pr[p:p + 1, :] * v_refs[p][h]
        out = jnp.sum(acc, axis=1, keepdims=True) + pr[n_pages:n_pages + 1, 0:1] * vn_ref[rows, :]
        o_ref[rows, :] = out / den


def _dsa_sample_attend(page_table, q_col, kn_col, vn_col, bias3, k_t, v_t):
    nb, n_pages = page_table.shape
    n_heads, hd, page = k_t.shape[1:]
    per = lambda shape: pl.BlockSpec((None,) + shape, lambda b, pt: (b, 0, 0))
    page_specs = [pl.BlockSpec((None, n_heads, hd, page), functools.partial(lambda b, pt, p: (pt[b, p], 0, 0, 0), p=p))
                  for p in range(n_pages)]
    grid_spec = pltpu.PrefetchScalarGridSpec(
        num_scalar_prefetch=1, grid=(nb,),
        in_specs=[per(q_col.shape[1:]), per(kn_col.shape[1:]), per(vn_col.shape[1:]), per(bias3.shape[1:])] + page_specs * 2,
        out_specs=per(q_col.shape[1:]),
        scratch_shapes=[pltpu.VMEM((n_pages + 1, page), F32)])
    return pl.pallas_call(
        _dsa_sample_attend_body, grid_spec=grid_spec,
        out_shape=jax.ShapeDtypeStruct(q_col.shape, F32),
        compiler_params=_params(("arbitrary",)), name="dsa_sample_attend",
    )(page_table, q_col, kn_col, vn_col, bias3, *([k_t] * n_pages), *([v_t] * n_pages))


def _swiglu(h, wg_ref, wu_ref, wd_ref, n_chunks):
    fc = wg_ref.shape[-1] // n_chunks
    y = None
    for c in range(n_chunks):
        cols = slice(fc * c, fc * (c + 1))
        gate = _dot(h, wg_ref[:, cols])
        act = (gate * jax.nn.sigmoid(gate) * _dot(h, wu_ref[:, cols])).astype(BF16)
        part = _dot(act, wd_ref[cols, :])
        y = part if y is None else y + part
    return y


def _outproj_ffn_body(x_ref, ys_ref, ya_ref, wos_ref, woa_ref, g_ref, wg_ref, wu_ref, wd_ref, o_ref):
    x = x_ref[...] + _dot(ys_ref[...], wos_ref[...]) + _dot(ya_ref[...], woa_ref[...])
    h = _rms(x, g_ref[...]).astype(BF16)
    o_ref[...] = x + _swiglu(h, wg_ref, wu_ref, wd_ref, 2)


def _outproj_ffn(x2d, ys, ya, wos, woa, g, wg, wu, wd, tm):
    T, D = x2d.shape
    row = lambda w: pl.BlockSpec((tm, w), lambda i: (i, 0))
    single = lambda a: pl.BlockSpec(a.shape, lambda i: (0,) * a.ndim, pipeline_mode=pl.Buffered(1))
    return pl.pallas_call(
        _outproj_ffn_body, grid=(T // tm,),
        in_specs=[row(D), row(ys.shape[1]), row(ya.shape[1]), single(wos), single(woa), single(g),
                  single(wg), single(wu), single(wd)],
        out_specs=row(D), out_shape=jax.ShapeDtypeStruct((T, D), F32),
        compiler_params=_params(("arbitrary",)), name="outproj_ffn",
    )(x2d, ys, ya, wos, woa, g, wg, wu, wd)


def _pool_mix(h, window_sum, divisor, pw_ref, scale):
    gd = h.shape[1] // len(POOL_WINDOWS)
    cols = []
    for g, w in enumerate(POOL_WINDOWS):
        pooled = window_sum(g, w) / divisor(w) - h[:, gd * g:gd * (g + 1)]
        cols.append(_dot(pooled.astype(BF16), pw_ref[g]))
    return jnp.concatenate(cols, axis=1) * scale


def _pool_prompt_body(x_ref, g_ref, pw_ref, sc_ref, o_ref, hist_ref, ext_ref):
    tm, D = x_ref.shape
    gd = D // len(POOL_WINDOWS)
    halo = POOL_HIST + 1
    j = pl.program_id(1)

    @pl.when(j == 0)
    def _():
        ext_ref[0:halo] = jnp.zeros((halo, D), F32)

    x = x_ref[...]
    h = _rms(x, g_ref[...])
    ext_ref[halo:] = h
    pos = (j * tm + lax.broadcasted_iota(I32, (tm, 1), 0)).astype(F32)

    def window_sum(g, w):
        acc = h[:, gd * g:gd * (g + 1)]
        for k in range(1, w):
            acc = acc + ext_ref[halo - k:halo - k + tm, gd * g:gd * (g + 1)]
        return acc

    mixed = _pool_mix(h, window_sum, lambda w: jnp.minimum(float(w), pos + 1.0), pw_ref, sc_ref[...])
    o_ref[...] = x + mixed
    ext_ref[0:halo] = ext_ref[tm:tm + halo]

    @pl.when(j == pl.num_programs(1) - 1)
    def _():
        hist_ref[...] = h[tm - POOL_HIST:, :]


def _pool_prompt(x2d, g, pw, scale, batch, seq, tm):
    T, D = x2d.shape
    nblk = seq // tm
    row = pl.BlockSpec((tm, D), lambda b, j: (b * nblk + j, 0))
    return pl.pallas_call(
        _pool_prompt_body, grid=(batch, nblk),
        in_specs=[row, _const_spec(g.shape), _const_spec(pw.shape), _const_spec(scale.shape)],
        out_specs=(row, pl.BlockSpec((None, POOL_HIST, D), lambda b, j: (b, 0, 0))),
        out_shape=(jax.ShapeDtypeStruct((T, D), F32), jax.ShapeDtypeStruct((batch, POOL_HIST, D), F32)),
        scratch_shapes=[pltpu.VMEM((tm + POOL_HIST + 1, D), F32)],
        compiler_params=_params(("arbitrary", "arbitrary")), name="pool_prompt",
    )(x2d, g, pw, scale)


def _pool_sample_body(x_ref, hist_ref, g_ref, pw_ref, sc_ref, o_ref, nh_ref):
    D = x_ref.shape[1]
    gd = D // len(POOL_WINDOWS)
    x = x_ref[...]
    h = _rms(x, g_ref[...])

    def window_sum(g, w):
        acc = h[:, gd * g:gd * (g + 1)]
        for k in range(1, w):
            acc = acc + hist_ref[POOL_HIST - k, :, gd * g:gd * (g + 1)]
        return acc

    mixed = _pool_mix(h, window_sum, float, pw_ref, sc_ref[...])
    o_ref[...] = x + mixed
    for i in range(POOL_HIST - 1):
        nh_ref[i] = hist_ref[i + 1]
    nh_ref[POOL_HIST - 1] = h


def _pool_sample(x2d, hist_t, g, pw, scale):
    sds = jax.ShapeDtypeStruct
    return pl.pallas_call(
        _pool_sample_body, out_shape=(sds(x2d.shape, F32), sds(hist_t.shape, F32)),
        compiler_params=pltpu.CompilerParams(vmem_limit_bytes=VMEM_LIMIT), name="pool_sample",
    )(x2d, hist_t, g, pw, scale)


def _store_row_tiles(ref, x):
    for s in range(ref.shape[1]):
        ref[:, s, :] = x[:, LANES * s:LANES * (s + 1)]


def _load_row_tiles(ref):
    return jnp.concatenate([ref[:, s, :] for s in range(ref.shape[1])], axis=1)


def _router_body(x_ref, g_ref, wr_ref, h_ref, sel_ref, *, n_experts):
    h = _rms(x_ref[...], g_ref[...])
    _store_row_tiles(h_ref, h)
    w = wr_ref[...]
    h_hi, w_hi = h.astype(BF16), w.astype(BF16)
    h_lo, w_lo = (h - h_hi.astype(F32)).astype(BF16), (w - w_hi.astype(F32)).astype(BF16)
    logits = _dot(h_hi, w_hi) + (_dot(h_hi, w_lo) + _dot(h_lo, w_hi))
    lane = lax.broadcasted_iota(I32, logits.shape, 1).astype(F32)
    logits = jnp.where(lane < n_experts, logits, -jnp.inf)
    v1 = jnp.max(logits, axis=1, keepdims=True)
    i1 = jnp.min(jnp.where(logits == v1, lane, float(LANES)), axis=1, keepdims=True)
    rest = jnp.where(lane == i1, -jnp.inf, logits)
    v2 = jnp.max(rest, axis=1, keepdims=True)
    i2 = jnp.min(jnp.where(rest == v2, lane, float(LANES)), axis=1, keepdims=True)
    e2 = jnp.exp(v2 - v1)
    den = 1.0 + e2
    sel_ref[...] = jnp.where(lane == 0.0, i1, jnp.where(lane == 1.0, i2, jnp.where(lane == 2.0, 1.0 / den, e2 / den)))


def _router(x2d, g, wr_pad, n_experts, tm):
    T, D = x2d.shape
    row = lambda w: pl.BlockSpec((tm, w), lambda i: (i, 0))
    body = functools.partial(_router_body, n_experts=n_experts)
    return pl.pallas_call(
        body, grid=(T // tm,),
        in_specs=[row(D), _const_spec(g.shape), _const_spec(wr_pad.shape)],
        out_specs=(pl.BlockSpec((tm, D // LANES, LANES), lambda i: (i, 0, 0)), row(LANES)),
        out_shape=(jax.ShapeDtypeStruct((T, D // LANES, LANES), F32), jax.ShapeDtypeStruct((T, LANES), F32)),
        compiler_params=_params(("arbitrary",)), name="router",
    )(x2d, g, wr_pad)


def _row_gather(idx_ref, base, src_hbm, dst, sem):
    def issue(r, carry):
        pltpu.make_async_copy(src_hbm.at[idx_ref[base + r]], dst.at[r], sem).start()
        return carry

    lax.fori_loop(0, dst.shape[0], issue, 0, unroll=8)


def _row_gather_wait(src_hbm, dst, sem):
    pltpu.make_async_copy(src_hbm.at[pl.ds(0, dst.shape[0])], dst, sem).wait()


def _moe_dispatch_body(p1_ref, p2_ref, ends_ref, h_ref, x_hbm, zero_ref, stage, sems, *, tm):
    i = pl.program_id(0)
    last = pl.num_programs(0) - 1
    gt = zero_ref.shape[0]
    n_groups = ends_ref.shape[0]
    zsem = sems.at[2]
    slot = i % 2

    def fill_tile(e, act):
        if e < n_groups:
            start = ends_ref[e] - gt
            exists = ends_ref[e] > (ends_ref[e - 1] if e else 0)
        else:
            start = ends_ref[n_groups - 1] + (e - n_groups) * gt
            exists = start < x_hbm.shape[0]

        @pl.when(exists)
        def _():
            act(pltpu.make_async_copy(zero_ref, x_hbm.at[pl.ds(pl.multiple_of(start, gt), gt)], zsem))

    @pl.when(i == 0)
    def _():
        zero_ref[...] = jnp.zeros(zero_ref.shape, F32)
        for e in range(2 * n_groups):
            fill_tile(e, lambda cp: cp.start())
        for e in range(2 * n_groups):
            fill_tile(e, lambda cp: cp.wait())

    def wait_rows(s):
        for _ in range(2):
            pltpu.make_async_copy(stage.at[s], x_hbm.at[pl.ds(0, tm)], sems.at[s]).wait()

    @pl.when(i >= 2)
    def _():
        wait_rows(slot)

    stage[slot] = h_ref[...]

    def issue(r, carry):
        t = i * tm + r
        pltpu.make_async_copy(stage.at[slot, r], x_hbm.at[p1_ref[t]], sems.at[slot]).start()
        pltpu.make_async_copy(stage.at[slot, r], x_hbm.at[p2_ref[t]], sems.at[slot]).start()
        return carry

    lax.fori_loop(0, tm, issue, 0, unroll=8)

    @pl.when(i == last)
    def _():
        wait_rows(slot)

    @pl.when((i == last) & (i >= 1))
    def _():
        wait_rows(1 - slot)


def _moe_dispatch(pos1, pos2, ends, h_tiles, n_rows, tm, group_tile):
    T, S, _ = h_tiles.shape
    grid_spec = pltpu.PrefetchScalarGridSpec(
        num_scalar_prefetch=3, grid=(T // tm,),
        in_specs=[pl.BlockSpec((tm, S, LANES), lambda i, p1, p2, en: (i, 0, 0))],
        out_specs=pl.BlockSpec(memory_space=pl.ANY),
        scratch_shapes=[pltpu.VMEM((group_tile, S, LANES), F32), pltpu.VMEM((2, tm, S, LANES), F32),
                        pltpu.SemaphoreType.DMA((3,))])
    return pl.pallas_call(
        functools.partial(_moe_dispatch_body, tm=tm), grid_spec=grid_spec,
        out_shape=jax.ShapeDtypeStruct((n_rows, S, LANES), F32),
        compiler_params=pltpu.CompilerParams(dimension_semantics=("arbitrary",), vmem_limit_bytes=VMEM_LIMIT,
                                             disable_bounds_checks=True),
        name="moe_dispatch",
    )(pos1, pos2, ends, h_tiles)


def _moe_group_body(te_ref, nu_ref, x_ref, wg_ref, wu_ref, wd_ref, y_ref):
    del te_ref
    i = pl.program_id(0)

    @pl.when(i < nu_ref[0])
    def _():
        _store_row_tiles(y_ref, _swiglu(_load_row_tiles(x_ref).astype(BF16), wg_ref, wu_ref, wd_ref, 2))

    @pl.when(i >= nu_ref[0])
    def _():
        y_ref[...] = jnp.zeros(y_ref.shape, F32)


def _moe_group(tile_expert, n_used, x_tiles, wg, wu, wd, tm):
    P, S, _ = x_tiles.shape
    _, D, F = wg.shape
    wspec = lambda a, b: pl.BlockSpec((None, a, b), lambda i, te, nu: (te[i], 0, 0), pipeline_mode=pl.Buffered(1))
    grid_spec = pltpu.PrefetchScalarGridSpec(
        num_scalar_prefetch=2, grid=(P // tm,),
        in_specs=[pl.BlockSpec((tm, S, LANES), lambda i, te, nu: (jnp.minimum(i, nu[0] - 1), 0, 0)),
                  wspec(D, F), wspec(D, F), wspec(F, D)],
        out_specs=pl.BlockSpec((tm, S, LANES), lambda i, te, nu: (i, 0, 0)))
    return pl.pallas_call(
        _moe_group_body, grid_spec=grid_spec, out_shape=jax.ShapeDtypeStruct((P, S, LANES), F32),
        compiler_params=_params(("arbitrary",)), name="moe_group",
    )(tile_expert, n_used, x_tiles, wg, wu, wd)


def _moe_combine_body(p1_ref, p2_ref, dst_ref, x_ref, sel_ref, y_hbm, ya_hbm, yb_hbm, buf, obuf, sem, osem):
    i = pl.program_id(0)
    last = pl.num_programs(0) - 1
    tm = x_ref.shape[0]
    n_chunks = tm // OUT_CHUNK
    rows_a = ya_hbm.shape[0]
    slot = i % 2

    def out_copies(step, s, act):
        for c in range(n_chunks):
            dst = dst_ref[step * n_chunks + c]
            src = obuf.at[s, pl.ds(c * OUT_CHUNK, OUT_CHUNK)]

            @pl.when((dst >= 0) & (dst < rows_a))
            def _():
                act(pltpu.make_async_copy(src, ya_hbm.at[pl.ds(pl.multiple_of(dst, OUT_CHUNK), OUT_CHUNK)], osem.at[s]))

            @pl.when(dst >= rows_a)
            def _():
                off = pl.multiple_of(dst - rows_a, OUT_CHUNK)
                act(pltpu.make_async_copy(src, yb_hbm.at[pl.ds(off, OUT_CHUNK)], osem.at[s]))

    def gather(tile, s):
        _row_gather(p1_ref, tile * tm, y_hbm, buf.at[s, 0], sem.at[s, 0])
        _row_gather(p2_ref, tile * tm, y_hbm, buf.at[s, 1], sem.at[s, 1])

    @pl.when(i == 0)
    def _():
        gather(0, 0)

    @pl.when(i + 1 < pl.num_programs(0))
    def _():
        gather(i + 1, 1 - slot)

    _row_gather_wait(y_hbm, buf.at[slot, 0], sem.at[slot, 0])
    _row_gather_wait(y_hbm, buf.at[slot, 1], sem.at[slot, 1])
    @pl.when(i >= 2)
    def _():
        out_copies(i - 2, slot, lambda cp: cp.wait())

    sel = sel_ref[...]
    obuf[slot] = (x_ref[...] + sel[:, 2:3] * _load_row_tiles(buf.at[slot, 0])
                  + sel[:, 3:4] * _load_row_tiles(buf.at[slot, 1]))
    out_copies(i, slot, lambda cp: cp.start())

    @pl.when(i == last)
    def _():
        out_copies(i, slot, lambda cp: cp.wait())

    @pl.when((i == last) & (i >= 1))
    def _():
        out_copies(i - 1, 1 - slot, lambda cp: cp.wait())


def _moe_combine(pos1, pos2, chunk_dst, x2d, sel, y_tiles, tm, rows_a, rows_b):
    T, D = x2d.shape
    S = y_tiles.shape[1]
    assert tm % OUT_CHUNK == 0 and rows_a % OUT_CHUNK == 0 and rows_b % OUT_CHUNK == 0
    row = lambda w: pl.BlockSpec((tm, w), lambda i, p1, p2, cd: (i, 0))
    hbm = pl.BlockSpec(memory_space=pl.ANY)
    grid_spec = pltpu.PrefetchScalarGridSpec(
        num_scalar_prefetch=3, grid=(T // tm,),
        in_specs=[row(D), row(LANES), hbm], out_specs=(hbm, hbm),
        scratch_shapes=[pltpu.VMEM((2, 2, tm, S, LANES), F32), pltpu.VMEM((2, tm, D), F32),
                        pltpu.SemaphoreType.DMA((2, 2)), pltpu.SemaphoreType.DMA((2,))])
    return pl.pallas_call(
        _moe_combine_body, grid_spec=grid_spec,
        out_shape=(jax.ShapeDtypeStruct((rows_a, D), F32), jax.ShapeDtypeStruct((rows_b, D), F32)),
        compiler_params=pltpu.CompilerParams(dimension_semantics=("arbitrary",), vmem_limit_bytes=VMEM_LIMIT,
                                             disable_bounds_checks=True),
        name="moe_combine",
    )(pos1, pos2, chunk_dst, x2d, sel, y_tiles)


def _moe_plan(sel, n_experts, tm):
    T = sel.shape[0]
    experts = jnp.concatenate([sel[:, 0], sel[:, 1]]).astype(I32)
    onehot = (experts[:, None] == jnp.arange(n_experts, dtype=I32)[None, :]).astype(I32)
    rank = jnp.cumsum(onehot, axis=0) - onehot
    counts = jnp.sum(onehot, axis=0)
    padded = (counts + tm - 1) // tm * tm
    ends = jnp.cumsum(padded)
    pos = jnp.sum(onehot * ((ends - padded)[None, :] + rank), axis=1)
    n_rows = (2 * T + n_experts * (tm - 1)) // tm * tm
    tile_start = jnp.arange(n_rows // tm, dtype=I32) * tm
    tile_expert = jnp.minimum(jnp.sum((tile_start[:, None] >= ends[None, :]).astype(I32), axis=1), n_experts - 1)
    n_used = (ends[-1] // tm).astype(I32).reshape(1)
    return tile_expert, n_used, ends.astype(I32), n_rows, pos[:T], pos[T:]


def _rope_tables(pos):
    half = HEAD_DIM // 2
    inv = ROPE_THETA ** (-jnp.arange(half, dtype=F32) / half)
    ang = pos[:, None] * inv[None, :]
    cos, sin = jnp.cos(ang), jnp.sin(ang)
    reps = LANES // HEAD_DIM
    return jnp.tile(jnp.concatenate([cos, cos], axis=1), (1, reps)), jnp.tile(jnp.concatenate([-sin, sin], axis=1), (1, reps))


def kernel(x_prompt, x_sample, cache_k, cache_v, cache_kidx, page_table, state_ssm_re, state_ssm_im, state_pool, meta_tokens, norm_mix0, w_in0, q_norm, k_norm, ssm_lambda_re, ssm_lambda_im, ssm_log_dt, ssm_b_re, ssm_b_im, ssm_c_re, ssm_c_im, ssm_d, ssm_w_glu, w_out0, norm_ffn0, ffn_w_gate, ffn_w_up, ffn_w_down, norm_mix1, pool_w, pool_scale, norm_ffn1, router_w, moe_w_gate, moe_w_up, moe_w_down):
    B, S, D = x_prompt.shape
    DB, DS, _ = x_sample.shape
    assert DS == 1, "one new token per sample sequence"
    L = S + N_META
    n_pool, page = cache_k.shape[1], cache_k.shape[2]
    n_pages = page_table.shape[1]
    past = n_pages * page
    d_att = N_HEADS * HEAD_DIM
    d_qi = N_IDX_HEADS * IDX_DIM
    G, P = ssm_lambda_re.shape[1:]
    d_ssm = G * SSM_GROUP
    n_state = G * P
    E = router_w.shape[-1]
    topk_p = min(TOPK_MAX, S // 4)
    topk_s = min(TOPK_MAX, (past + DS) // 4)
    tm = _row_block(L, 1024)
    assert tm >= POOL_HIST + 1

    x_p = jnp.concatenate([jnp.broadcast_to(meta_tokens[None], (B, N_META, D)), x_prompt], axis=1).reshape(B * L, D)
    x_s = x_sample.reshape(DB, D)

    w_in = w_in0[0]
    o1, o2, o3, o4, o5, o6 = d_ssm, d_ssm + d_att, d_ssm + 2 * d_att, d_ssm + 3 * d_att, d_ssm + 3 * d_att + d_qi, d_ssm + 3 * d_att + d_qi + IDX_DIM
    w_kw = jnp.pad(w_in[:, o5:], ((0, 0), (0, LANES - (w_in.shape[1] - o5))))
    ws = tuple(w.astype(BF16) for w in (w_in[:, :o1], w_in[:, o1:o2], w_in[:, o2:o3], w_in[:, o3:o4], w_in[:, o4:o5], w_kw))
    g_mix0 = norm_mix0[0].reshape(1, D)
    qn = jnp.tile(q_norm[0], N_HEADS).reshape(1, d_att)
    kn = jnp.tile(k_norm[0], N_HEADS).reshape(1, d_att)
    head_of = jnp.arange(d_att) // HEAD_DIM
    hm = jnp.where(head_of[:, None] == head_of[None, :], 1.0 / HEAD_DIM, 0.0).astype(BF16)
    cos_p, sin_p = _rope_tables(jnp.arange(L, dtype=F32))
    cos_s, sin_s = _rope_tables(jnp.full((DB,), float(past), F32))

    u_p, q_p, k_p, v_p, qi_p, ki_p, kw_p, kb_p, vb_p, kib_p = _inproj(x_p, cos_p, sin_p, L // tm, tm, g_mix0, ws, qn, kn, hm)
    u_s, q_s, k_s, v_s, qi_s, ki_s, kw_s, _, _, _ = _inproj(x_s, cos_s, sin_s, 1, DB, g_mix0, ws, qn, kn, hm)

    lbr, lbi, bbr, bbi = _s5_prep(ssm_lambda_re[0], ssm_lambda_im[0], ssm_log_dt[0], ssm_b_re[0], ssm_b_im[0])
    gps = LANES // SSM_GROUP
    n_slab = G // gps
    eye = jnp.eye(gps, dtype=F32)

    def b_slabs(bt):
        return jnp.einsum('sgcp,gh->sgchp', bt.reshape(n_slab, gps, SSM_GROUP, P), eye).reshape(n_slab, LANES, gps * P)

    def c_slabs(c):
        return jnp.einsum('sgcp,gh->sgphc', c.reshape(n_slab, gps, SSM_GROUP, P), eye).reshape(n_slab, gps * P, LANES)

    wb = jnp.concatenate([b_slabs(bbr), b_slabs(bbi)], axis=2)
    wc = jnp.stack([c_slabs(ssm_c_re[0]), -c_slabs(ssm_c_im[0])], axis=1).astype(BF16)
    lbr_f, lbi_f = lbr.reshape(1, n_state), lbi.reshape(1, n_state)
    d_skip = ssm_d[0].reshape(1, d_ssm)
    wglu = ssm_w_glu[0].astype(BF16)
    assert B % SUBLANES == 0, "the S5 scan keeps one batch row per sublane"
    steps = max(d for d in range(1, L + 1) if L % d == 0 and d * B <= 512 and (d * B) % (2 * SUBLANES) == 0)
    u_tm = u_p.reshape(B, L, d_ssm).swapaxes(0, 1).reshape(L * B, d_ssm)
    zero_h = jnp.zeros((B, n_state), F32)
    ys_tm, hre_p, him_p = _s5(u_tm, zero_h, zero_h, lbr_f, lbi_f, wb, wc, d_skip, wglu, B, steps, False)
    ys_p = ys_tm.reshape(L, B, d_ssm).swapaxes(0, 1).reshape(B * L, d_ssm)
    ys_s, hre_s, him_s = _s5(u_s, state_ssm_re[0].reshape(DB, n_state), state_ssm_im[0].reshape(DB, n_state),
                             lbr_f, lbi_f, wb, wc, d_skip, wglu, DB, 1, True)

    ya_p = _dsa_prompt(qi_p, kw_p, q_p, kib_p, kb_p, vb_p, B, L, tm, topk_p)
    assert page == LANES
    w3 = kw_s[:, IDX_DIM:IDX_DIM + N_IDX_HEADS].reshape(DB, N_IDX_HEADS, 1)
    scores = _dsa_sample_scores(page_table, qi_s.astype(F32).reshape(DB, d_qi, 1), w3, ki_s.reshape(DB, IDX_DIM, 1),
                                jnp.transpose(cache_kidx[0], (0, 2, 1)))
    bias = _dsa_sample_select(scores.reshape(DB, (n_pages + 1) * page), past + DS, topk_s)
    ya_s = _dsa_sample_attend(page_table, q_s.astype(F32).reshape(DB, d_att, 1), k_s.reshape(DB, d_att, 1),
                              v_s.reshape(DB, d_att, 1), bias.reshape(DB, n_pages + 1, page),
                              jnp.transpose(cache_k[0], (0, 2, 3, 1)), jnp.transpose(cache_v[0], (0, 2, 3, 1)))
    ya_s = ya_s.reshape(DB, d_att).astype(BF16)

    w_out = w_out0[0].astype(BF16)
    ffn = (w_out[:d_ssm], w_out[d_ssm:], norm_ffn0[0].reshape(1, D), ffn_w_gate[0].astype(BF16),
           ffn_w_up[0].astype(BF16), ffn_w_down[0].astype(BF16))
    x_p = _outproj_ffn(x_p, ys_p, ya_p, *ffn, tm)
    x_s = _outproj_ffn(x_s, ys_s, ya_s, *ffn, DB)

    g_mix1 = norm_mix1[0].reshape(1, D)
    pw = pool_w[0].astype(BF16)
    psc = pool_scale[0].reshape(1, D)
    T_all = B * L + DB
    x_p, pool_p = _pool_prompt(x_p, g_mix1, pw, psc, B, L, tm)
    x_s, hist_t = _pool_sample(x_s, state_pool[0].swapaxes(0, 1), g_mix1, pw, psc)
    pool_s = hist_t.swapaxes(0, 1)
    x_all = jnp.concatenate([x_p, x_s], axis=0)

    g_ffn1 = norm_ffn1[0].reshape(1, D)
    wr_pad = jnp.pad(router_w[0], ((0, 0), (0, LANES - E)))
    wg, wu, wd = moe_w_gate[0].astype(BF16), moe_w_up[0].astype(BF16), moe_w_down[0].astype(BF16)
    tr = _row_block(T_all, MOE_TILE)
    h_tiles, sel = _router(x_all, g_ffn1, wr_pad, E, tr)
    tile_expert, n_used, ends, n_rows, pos1, pos2 = _moe_plan(sel, E, MOE_TILE)
    x_tiles = _moe_dispatch(pos1, pos2, ends, h_tiles, n_rows, tr, MOE_TILE)
    y_tiles = _moe_group(tile_expert, n_used, x_tiles, wg, wu, wd, MOE_TILE)
    assert N_META % OUT_CHUNK == 0 and L % OUT_CHUNK == 0 and DB % OUT_CHUNK == 0
    row0 = jnp.arange(T_all // OUT_CHUNK, dtype=I32) * OUT_CHUNK
    in_seq = row0 % L
    chunk_dst = jnp.where(row0 >= B * L, row0 - B * N_META,
                          jnp.where(in_seq < N_META, -1, (row0 // L) * S + in_seq - N_META))
    y_prompt, y_sample = _moe_combine(pos1, pos2, chunk_dst, x_all, sel, y_tiles, tr, B * S, DB)
    y_prompt = y_prompt.reshape(B, S, D)
    y_sample = y_sample.reshape(DB, 1, D)
    k_prompt = k_p.reshape(1, B, L, N_HEADS, HEAD_DIM)
    v_prompt = v_p.reshape(1, B, L, N_HEADS, HEAD_DIM)
    kidx_prompt = ki_p.reshape(1, B, L, IDX_DIM)
    return (y_prompt, y_sample, k_prompt, v_prompt, kidx_prompt,
            hre_p.reshape(1, B, G, P), him_p.reshape(1, B, G, P), pool_p[None],
            k_s.reshape(1, DB, 1, N_HEADS, HEAD_DIM), v_s.reshape(1, DB, 1, N_HEADS, HEAD_DIM),
            ki_s.reshape(1, DB, 1, IDX_DIM), hre_s.reshape(1, DB, G, P), him_s.reshape(1, DB, G, P), pool_s[None])
```

```python
import functools
import math

import jax
import jax.numpy as jnp
from jax import lax
from jax.experimental import pallas as pl
from jax.experimental.pallas import tpu as pltpu

F32 = jnp.float32
BF16 = jnp.bfloat16
I32 = jnp.int32

N_META = 16
SSM_GROUP = 16
SSM_STATE = 64
N_HEADS = 8
HEAD_DIM = 64
N_IDX_HEADS = 8
IDX_DIM = 64
TOPK_MAX = 256
ROPE_THETA = 10000.0
POOL_WINDOWS = (2, 4, 8, 16)
POOL_HIST = max(POOL_WINDOWS) - 1
TOP_K_EXPERTS = 2
EPS = 1e-6

LANES = 128
SUBLANES = 8
VMEM_LIMIT = 56 * 1024 * 1024
OUT_CHUNK = 2 * SUBLANES
MOE_TILE = 256
BISECT_ITERS = 16
NEG_BIG = -1e30


def _params(sem):
    return pltpu.CompilerParams(dimension_semantics=sem, vmem_limit_bytes=VMEM_LIMIT)


def _row_block(n, cap, mult=2 * SUBLANES):
    best = None
    for d in range(mult, min(n, cap) + 1, mult):
        if n % d == 0:
            best = d
    assert best is not None, n
    return best


def _rms(x, g):
    return x * lax.rsqrt(jnp.mean(x * x, axis=-1, keepdims=True) + EPS) * g


def _dot(a, b):
    return jnp.dot(a, b, preferred_element_type=F32)


def _dot_nt(a, b):
    return lax.dot_general(a, b, (((1,), (1,)), ((), ())), preferred_element_type=F32)


def _const_spec(shape):
    nd = len(shape)
    return pl.BlockSpec(shape, lambda *_: (0,) * nd)


def _inproj_body(x_ref, g_ref, wu_ref, wq_ref, wk_ref, wv_ref, wqi_ref, wkw_ref, qn_ref, kn_ref, hm_ref,
                 cos_ref, sin_ref,
                 u_ref, q_ref, k_ref, v_ref, qi_ref, ki_ref, kw_ref, kb_ref, vb_ref, kib_ref, *, transposed):
    def store_kv(ref, z):
        if not transposed:
            ref[...] = z
            return
        per_slab = LANES // HEAD_DIM
        for sp in range(z.shape[1] // LANES):
            zt = z[:, LANES * sp:LANES * (sp + 1)].T
            ref[per_slab * sp:per_slab * (sp + 1)] = zt.reshape(per_slab, HEAD_DIM, z.shape[0])

    h = _rms(x_ref[...], g_ref[...]).astype(BF16)
    cos = cos_ref[...]
    sin = sin_ref[...]
    lane = lax.broadcasted_iota(I32, (1, LANES), 1)
    lo_half = (lane & (HEAD_DIM // 2)) == 0

    def rope(z):
        cols = []
        for j in range(z.shape[1] // LANES):
            zs = z[:, LANES * j:LANES * (j + 1)]
            partner = jnp.where(lo_half, pltpu.roll(zs, LANES - HEAD_DIM // 2, 1), pltpu.roll(zs, HEAD_DIM // 2, 1))
            cols.append(zs * cos + partner * sin)
        return cols[0] if len(cols) == 1 else jnp.concatenate(cols, axis=1)

    def head_norm(z, gn):
        ms = _dot((z * z).astype(BF16), hm_ref[...])
        return z * lax.rsqrt(ms + EPS) * gn

    u_ref[...] = _dot(h, wu_ref[...])
    q = rope(head_norm(_dot(h, wq_ref[...]), qn_ref[...]))
    q_ref[...] = (q * (HEAD_DIM ** -0.5)).astype(BF16)
    k = rope(head_norm(_dot(h, wk_ref[...]), kn_ref[...]))
    store_kv(k_ref, k)
    kb_ref[...] = k.astype(BF16)
    v = _dot(h, wv_ref[...])
    store_kv(v_ref, v)
    vb_ref[...] = v.astype(BF16)
    qi_ref[...] = (rope(_dot(h, wqi_ref[...])) * (IDX_DIM ** -0.5)).astype(BF16)
    zkw = _dot(h, wkw_ref[...])
    kir = rope(zkw)
    kw_ref[...] = jnp.where(lane < IDX_DIM, kir, zkw * (N_IDX_HEADS ** -0.5))
    ki_ref[...] = kir.T[:IDX_DIM] if transposed else kir[:, :IDX_DIM]
    kib_ref[...] = jnp.where(lane < IDX_DIM, kir, pltpu.roll(kir, IDX_DIM, 1)).astype(BF16)


def _inproj(x3, cos_t, sin_t, tm, out_rows, g, ws, qn, kn, hm, transposed):
    nb, rows, D = x3.shape
    wu, wq, wk, wv, wqi, wkw = ws
    d_ssm, d_att, d_qi = wu.shape[1], wq.shape[1], wqi.shape[1]
    row = lambda w: pl.BlockSpec((None, tm, w), lambda b, j: (b, j, 0))
    tab = pl.BlockSpec((tm, LANES), lambda b, j: (j, 0))
    sds = lambda w, dt: jax.ShapeDtypeStruct((nb, out_rows, w), dt)
    if transposed:
        kv_shape = jax.ShapeDtypeStruct((nb, N_HEADS, HEAD_DIM, out_rows), F32)
        kv_spec = pl.BlockSpec((None, N_HEADS, HEAD_DIM, tm), lambda b, j: (b, 0, 0, j))
        ki_shape = jax.ShapeDtypeStruct((nb, IDX_DIM, out_rows), F32)
        ki_spec = pl.BlockSpec((None, IDX_DIM, tm), lambda b, j: (b, 0, j))
    else:
        kv_shape, kv_spec, ki_shape, ki_spec = sds(d_att, F32), row(d_att), sds(IDX_DIM, F32), row(IDX_DIM)
    out_shape = (
        sds(d_ssm, F32),
        sds(d_att, BF16),
        kv_shape,
        kv_shape,
        sds(d_qi, BF16),
        ki_shape,
        sds(LANES, F32),
        sds(d_att, BF16),
        sds(d_att, BF16),
        sds(LANES, BF16),
    )
    out_specs = (row(d_ssm), row(d_att), kv_spec, kv_spec, row(d_qi), ki_spec, row(LANES),
                 row(d_att), row(d_att), row(LANES))
    in_specs = [row(D), _const_spec(g.shape)] + [_const_spec(w.shape) for w in ws] + [
        _const_spec(qn.shape), _const_spec(kn.shape), _const_spec(hm.shape), tab, tab]
    return pl.pallas_call(
        functools.partial(_inproj_body, transposed=transposed), grid=(nb, rows // tm),
        in_specs=in_specs, out_specs=out_specs, out_shape=out_shape,
        compiler_params=_params(("arbitrary", "arbitrary")), name="inproj_t" if transposed else "inproj",
    )(x3, g, *ws, qn, kn, hm, cos_t, sin_t)


def _s5_prep_body(lr_ref, li_ref, ldt_ref, btr_ref, bti_ref, lbr_ref, lbi_ref, bbr_ref, bbi_ref):
    lr = lr_ref[...]
    li = li_ref[...]
    dt = jnp.exp(ldt_ref[...])
    mag = jnp.exp(lr * dt)
    lbr = mag * jnp.cos(li * dt)
    lbi = mag * jnp.sin(li * dt)
    lbr_ref[...] = lbr
    lbi_ref[...] = lbi
    den = lr * lr + li * li
    cr = ((lbr - 1.0) * lr + lbi * li) / den
    ci = (lbi * lr - (lbr - 1.0) * li) / den
    btr = btr_ref[...]
    bti = bti_ref[...]
    bbr_ref[...] = cr * btr - ci * bti
    bbi_ref[...] = cr * bti + ci * btr


def _s5_prep(lam_re, lam_im, log_dt, b_re, b_im):
    G, P = lam_re.shape
    C = b_re.shape[-1]
    btr = jnp.swapaxes(b_re, 1, 2)
    bti = jnp.swapaxes(b_im, 1, 2)
    sds = jax.ShapeDtypeStruct
    return pl.pallas_call(
        _s5_prep_body,
        out_shape=(sds((G, 1, P), F32), sds((G, 1, P), F32), sds((G, C, P), F32), sds((G, C, P), F32)),
        name="s5_prep",
    )(lam_re.reshape(G, 1, P), lam_im.reshape(G, 1, P), log_dt.reshape(G, 1, 1), btr, bti)


def _s5_body(u_ref, h0r_ref, h0i_ref, lbr_ref, lbi_ref, wb_ref, wc_ref, d_ref, wglu_ref,
             y_ref, hr_ref, hi_ref, bu_ref, st_ref, *, nb, steps, precise):
    n_slab = wb_ref.shape[0]
    cin = wb_ref.shape[1]
    sw = wb_ref.shape[2] // 2
    n_state = n_slab * sw

    @pl.when(pl.program_id(0) == 0)
    def _():
        st_ref[0] = h0r_ref[...]
        st_ref[1] = h0i_ref[...]

    u = u_ref[...]
    for i in range(n_slab):
        ui = u[:, cin * i:cin * (i + 1)]
        if precise:
            bu = jnp.dot(ui, wb_ref[i], preferred_element_type=F32, precision=lax.Precision.HIGHEST)
        else:
            bu = _dot(ui.astype(BF16), wb_ref[i].astype(BF16))
        bu_ref[:, sw * i:sw * (i + 1)] = bu[:, :sw]
        bu_ref[:, n_state + sw * i:n_state + sw * (i + 1)] = bu[:, sw:]

    for i in range(n_slab):
        re_cols = slice(sw * i, sw * (i + 1))
        im_cols = slice(n_state + sw * i, n_state + sw * (i + 1))
        lr = jnp.broadcast_to(lbr_ref[:, re_cols], (nb, sw))
        li = jnp.broadcast_to(lbi_ref[:, re_cols], (nb, sw))

        def step(t, carry, re_cols=re_cols, im_cols=im_cols, lr=lr, li=li):
            hr, hi = carry
            rows = pl.ds(pl.multiple_of(t * nb, nb), nb)
            nhr = lr * hr - li * hi + bu_ref[rows, re_cols]
            nhi = lr * hi + li * hr + bu_ref[rows, im_cols]
            bu_ref[rows, re_cols] = nhr
            bu_ref[rows, im_cols] = nhi
            return nhr, nhi

        hr, hi = lax.fori_loop(0, steps, step, (st_ref[0, :, re_cols], st_ref[1, :, re_cols]))
        st_ref[0, :, re_cols] = hr
        st_ref[1, :, re_cols] = hi

    ys = []
    for i in range(n_slab):
        h_re = bu_ref[:, sw * i:sw * (i + 1)].astype(BF16)
        h_im = bu_ref[:, n_state + sw * i:n_state + sw * (i + 1)].astype(BF16)
        ys.append(_dot(h_re, wc_ref[i, 0]) + _dot(h_im, wc_ref[i, 1]))
    y = jnp.concatenate(ys, axis=1) + d_ref[...] * u
    y = 0.5 * y * (1.0 + jnp.tanh(math.sqrt(2.0 / math.pi) * (y + 0.044715 * (y * y * y))))
    y = y * jax.nn.sigmoid(_dot(y.astype(BF16), wglu_ref[...]))
    y_ref[...] = y.astype(BF16)
    hr_ref[...] = st_ref[0]
    hi_ref[...] = st_ref[1]


def _s5(u_tm, h0_re, h0_im, lbr_flat, lbi_flat, wb, wc, d_skip, wglu, nb, steps, precise):
    rows_total, d_ssm = u_tm.shape
    n_state = h0_re.shape[1]
    rows = nb * steps
    sds = jax.ShapeDtypeStruct
    body = functools.partial(_s5_body, nb=nb, steps=steps, precise=precise)
    return pl.pallas_call(
        body, grid=(rows_total // rows,),
        in_specs=[pl.BlockSpec((rows, d_ssm), lambda c: (c, 0)), _const_spec(h0_re.shape), _const_spec(h0_im.shape),
                  _const_spec(lbr_flat.shape), _const_spec(lbi_flat.shape), _const_spec(wb.shape),
                  _const_spec(wc.shape), _const_spec(d_skip.shape), _const_spec(wglu.shape)],
        out_specs=(pl.BlockSpec((rows, d_ssm), lambda c: (c, 0)), _const_spec(h0_re.shape), _const_spec(h0_im.shape)),
        out_shape=(sds((rows_total, d_ssm), BF16), sds(h0_re.shape, F32), sds(h0_im.shape, F32)),
        scratch_shapes=[pltpu.VMEM((rows, 2 * n_state), F32), pltpu.VMEM((2, nb, n_state), F32)],
        compiler_params=_params(("arbitrary",)), name="s5_precise" if precise else "s5",
    )(u_tm, h0_re, h0_im, lbr_flat, lbi_flat, wb, wc, d_skip, wglu)


def _topk_bias(s_ref, n, kk):
    rows = s_ref.shape[0]
    cols = slice(0, n)

    def count(pred):
        return jnp.sum(jnp.where(pred(s_ref[:, cols]), 1.0, 0.0), axis=1, keepdims=True)

    def largest_below(hi):
        s = s_ref[:, cols]
        return jnp.max(jnp.where(s < hi, s, -jnp.inf), axis=1, keepdims=True)

    s0 = s_ref[:, cols]
    smax = jnp.max(s0, axis=1, keepdims=True)
    smin = jnp.min(jnp.where(s0 > -jnp.inf, s0, jnp.inf), axis=1, keepdims=True)

    def bisect(_, c):
        lo, hi = c
        mid = lo + (hi - lo) * 0.5
        ge = count(lambda s: s >= mid) >= kk
        return jnp.where(ge, mid, lo), jnp.where(ge, hi, mid)

    _, hi = lax.fori_loop(0, BISECT_ITERS, bisect, (smin, smax + (smax - smin) + 1.0))
    thr = largest_below(hi)
    cnt = count(lambda s: s >= thr)

    def short(c):
        return jnp.min(c[2] - kk) < 0.0

    def lower(c):
        hi, thr, cnt = c
        hi = jnp.where(cnt < kk, thr, hi)
        thr = largest_below(hi)
        return hi, thr, count(lambda s: s >= thr)

    _, thr, _ = lax.while_loop(short, lower, (hi, thr, cnt))
    need = kk - count(lambda s: s > thr)
    blk = 2 * LANES
    tri = (lax.broadcasted_iota(I32, (blk, blk), 0) <= lax.broadcasted_iota(I32, (blk, blk), 1)).astype(BF16)
    seen = jnp.zeros((rows, 1), F32)
    for c0 in range(0, n, blk):
        w = min(blk, n - c0)
        s = s_ref[:, c0:c0 + w]
        tie = s == thr
        upto = seen + _dot(jnp.where(tie, 1.0, 0.0).astype(BF16), tri[:w, :w])
        seen = upto[:, w - 1:w]
        s_ref[:, c0:c0 + w] = jnp.where((s > thr) | (tie & (upto <= need)), 0.0, NEG_BIG)


def _dsa_prompt_block(q0, n, qim_ref, qm_ref, w_ref, kip_ref, kp_ref, vp_ref, s_ref, acc_ref, topk):
    tq = s_ref.shape[0]
    cols = slice(0, n)
    lane = lax.broadcasted_iota(I32, (1, LANES), 1)
    s_ref[:, cols] = jnp.zeros((tq, n), F32)

    def score_head(h, carry):
        d = jnp.maximum(_dot_nt(qim_ref[h], kip_ref[cols, :]), 0.0)
        s_ref[:, cols] += w_ref[h] * d
        return carry

    lax.fori_loop(0, N_IDX_HEADS, score_head, 0)
    qpos = q0 + lax.broadcasted_iota(I32, (tq, 1), 0)
    kpos = lax.broadcasted_iota(I32, (1, n), 1)
    s_ref[:, cols] = jnp.where(kpos <= qpos, s_ref[:, cols], -jnp.inf)
    _topk_bias(s_ref, n, jnp.minimum(qpos + 1, topk).astype(F32))

    def head_pair(sp, carry):
        outs = []
        for hh in range(2):
            lg = _dot_nt(qm_ref[2 * sp + hh], kp_ref[sp, cols, :]) + s_ref[:, cols]
            p = jnp.exp(lg - jnp.max(lg, axis=1, keepdims=True))
            pv = _dot(p.astype(BF16), vp_ref[sp, cols, :])
            outs.append(pv[:, :LANES] / pv[:, LANES:LANES + 1])
        acc_ref[sp] = jnp.where(lane < HEAD_DIM, outs[0], outs[1])
        return carry

    lax.fori_loop(0, acc_ref.shape[0], head_pair, 0)


def _dsa_prompt_body(qi_ref, kw_ref, q_ref, kib_ref, kb_ref, vb_ref, o_ref,
                     kip_ref, kp_ref, vp_ref, qim_ref, qm_ref, w_ref, s_ref, acc_ref, *, seq, topk):
    tq = q_ref.shape[0]
    n_slab, lp, _ = kp_ref.shape
    j = pl.program_id(1)
    lane = lax.broadcasted_iota(I32, (1, LANES), 1)

    @pl.when(j == 0)
    def _():
        kip_ref[0:seq] = kib_ref[...]
        if lp > seq:
            kip_ref[seq:] = jnp.zeros((lp - seq, LANES), BF16)
        for sp in range(n_slab):
            sl = slice(LANES * sp, LANES * (sp + 1))
            kp_ref[sp, 0:seq] = kb_ref[:, sl]
            vp_ref[sp, 0:seq, 0:LANES] = vb_ref[:, sl]
            vp_ref[sp, :, LANES:] = jnp.where(lane == 0, 1.0, 0.0).astype(BF16) + jnp.zeros((lp, LANES), BF16)
            if lp > seq:
                kp_ref[sp, seq:] = jnp.zeros((lp - seq, LANES), BF16)
                vp_ref[sp, seq:, 0:LANES] = jnp.zeros((lp - seq, LANES), BF16)

    kw = kw_ref[...]
    for h in range(N_IDX_HEADS):
        head_lanes = (lane // IDX_DIM) == (h % 2)
        sl = slice(LANES * (h // 2), LANES * (h // 2 + 1))
        qim_ref[h] = jnp.where(head_lanes, qi_ref[:, sl], jnp.zeros((), BF16))
        qm_ref[h] = jnp.where(head_lanes, q_ref[:, sl], jnp.zeros((), BF16))
        w_ref[h] = kw[:, IDX_DIM + h:IDX_DIM + h + 1]

    for jj in range(seq // tq):
        n = min(-(-((jj + 1) * tq) // LANES) * LANES, lp)

        @pl.when(j == jj)
        def _(jj=jj, n=n):
            _dsa_prompt_block(jj * tq, n, qim_ref, qm_ref, w_ref, kip_ref, kp_ref, vp_ref, s_ref, acc_ref, topk)

    for sp in range(n_slab):
        o_ref[:, LANES * sp:LANES * (sp + 1)] = acc_ref[sp].astype(BF16)


def _dsa_prompt(qi, kw, q, kib, kb, vb, batch, seq, tq, topk):
    T, d_att = q.shape
    nq = seq // tq
    lp = -(-seq // LANES) * LANES
    n_slab = d_att // LANES
    qrow = lambda w: pl.BlockSpec((tq, w), lambda b, j: (b * nq + j, 0))
    full = lambda w: pl.BlockSpec((None, seq, w), lambda b, j: (b, 0, 0))
    body = functools.partial(_dsa_prompt_body, seq=seq, topk=topk)
    return pl.pallas_call(
        body, grid=(batch, nq),
        in_specs=[qrow(qi.shape[1]), qrow(LANES), qrow(d_att), full(LANES), full(d_att), full(d_att)],
        out_specs=qrow(d_att),
        out_shape=jax.ShapeDtypeStruct((T, d_att), BF16),
        scratch_shapes=[pltpu.VMEM((lp, LANES), BF16), pltpu.VMEM((n_slab, lp, LANES), BF16),
                        pltpu.VMEM((n_slab, lp, 2 * LANES), BF16),
                        pltpu.VMEM((N_IDX_HEADS, tq, LANES), BF16), pltpu.VMEM((N_HEADS, tq, LANES), BF16),
                        pltpu.VMEM((N_IDX_HEADS, tq, 1), F32),
                        pltpu.VMEM((tq, lp), F32), pltpu.VMEM((n_slab, tq, LANES), F32)],
        compiler_params=_params(("arbitrary", "arbitrary")), name="dsa_prompt",
    )(qi, kw, q, kib.reshape(batch, seq, LANES), kb.reshape(batch, seq, d_att), vb.reshape(batch, seq, d_att))


def _dsa_sample_scores_body(pt_ref, qi_ref, w_ref, kin_ref, *refs):
    del pt_ref
    page_refs, s_ref = refs[:-1], refs[-1]
    n_pages = len(page_refs)
    idx_dim, page = page_refs[0].shape
    heads = [slice(idx_dim * h, idx_dim * (h + 1)) for h in range(N_IDX_HEADS)]
    qcols = [jnp.broadcast_to(qi_ref[rows, :], (idx_dim, page)) for rows in heads]
    ws = [w_ref[h:h + 1, :] for h in range(N_IDX_HEADS)]

    def score(dot_of_head):
        s = None
        for h in range(N_IDX_HEADS):
            t = ws[h] * jnp.maximum(dot_of_head(h), 0.0)
            s = t if s is None else s + t
        return s

    for p in range(n_pages):
        kt = page_refs[p][...]
        s_ref[p:p + 1, :] = score(lambda h: jnp.sum(qcols[h] * kt, axis=0, keepdims=True))
    kin = kin_ref[...]
    s_new = score(lambda h: jnp.sum(qi_ref[heads[h], :] * kin, axis=0, keepdims=True))
    lane = lax.broadcasted_iota(I32, (1, page), 1)
    s_ref[n_pages:n_pages + 1, :] = jnp.where(lane == 0, s_new, 0.0)


def _dsa_sample_scores(page_table, qi_col, w3, ki_col, kidx_t):
    nb, n_pages = page_table.shape
    idx_dim, page = kidx_t.shape[1:]
    per = lambda shape: pl.BlockSpec((None,) + shape, lambda b, pt: (b, 0, 0))
    page_specs = [pl.BlockSpec((None, idx_dim, page), functools.partial(lambda b, pt, p: (pt[b, p], 0, 0), p=p))
                  for p in range(n_pages)]
    grid_spec = pltpu.PrefetchScalarGridSpec(
        num_scalar_prefetch=1, grid=(nb,),
        in_specs=[per(qi_col.shape[1:]), per(w3.shape[1:]), per(ki_col.shape[1:])] + page_specs,
        out_specs=per((n_pages + 1, page)))
    return pl.pallas_call(
        _dsa_sample_scores_body, grid_spec=grid_spec,
        out_shape=jax.ShapeDtypeStruct((nb, n_pages + 1, page), F32),
        compiler_params=_params(("arbitrary",)), name="dsa_sample_scores",
    )(page_table, qi_col, w3, ki_col, *([kidx_t] * n_pages))


def _dsa_sample_select_body(s_ref, b_ref, *, n_keys, topk):
    rows, width = s_ref.shape
    pos = lax.broadcasted_iota(I32, (1, width), 1)
    b_ref[...] = jnp.where(pos < n_keys, s_ref[...], -jnp.inf)
    _topk_bias(b_ref, width, jnp.full((rows, 1), float(min(topk, n_keys)), F32))


def _dsa_sample_select(scores, n_keys, topk):
    rows, width = scores.shape
    body = functools.partial(_dsa_sample_select_body, n_keys=n_keys, topk=topk)
    return pl.pallas_call(
        body, out_shape=jax.ShapeDtypeStruct((rows, width), F32),
        compiler_params=pltpu.CompilerParams(vmem_limit_bytes=VMEM_LIMIT), name="dsa_sample_select",
    )(scores)


def _dsa_sample_attend_body(pt_ref, q_ref, kn_ref, vn_ref, b_ref, *refs):
    del pt_ref
    o_ref, lg_ref = refs[-2], refs[-1]
    n_pages = (len(refs) - 2) // 2
    k_refs, v_refs = refs[:n_pages], refs[n_pages:2 * n_pages]
    n_heads, hd, page = k_refs[0].shape
    lane = lax.broadcasted_iota(I32, (1, page), 1)
    bias = b_ref[...]
    for h in range(n_heads):
        rows = slice(hd * h, hd * (h + 1))
        qc = jnp.broadcast_to(q_ref[rows, :], (hd, page))
        for p in range(n_pages):
            lg_ref[p:p + 1, :] = jnp.sum(qc * k_refs[p][h], axis=0, keepdims=True)
        lg_new = jnp.sum(q_ref[rows, :] * kn_ref[rows, :], axis=0, keepdims=True)
        lg_ref[n_pages:n_pages + 1, :] = jnp.where(lane == 0, lg_new, 0.0)
        lg = lg_ref[...] + bias
        m = jnp.max(jnp.max(lg, axis=1, keepdims=True), axis=0, keepdims=True)
        pr = jnp.exp(lg - m)
        den = jnp.sum(jnp.sum(pr, axis=1, keepdims=True), axis=0, keepdims=True)
        acc = jnp.zeros((hd, page), F32)
        for p in range(n_pages):
            acc = acc + pr[p:p + 1, :] * v_refs[p][h]
        out = jnp.sum(acc, axis=1, keepdims=True) + pr[n_pages:n_pages + 1, 0:1] * vn_ref[rows, :]
        o_ref[rows, :] = out / den


def _dsa_sample_attend(page_table, q_col, kn_col, vn_col, bias3, k_t, v_t):
    nb, n_pages = page_table.shape
    n_heads, hd, page = k_t.shape[1:]
    per = lambda shape: pl.BlockSpec((None,) + shape, lambda b, pt: (b, 0, 0))
    page_specs = [pl.BlockSpec((None, n_heads, hd, page), functools.partial(lambda b, pt, p: (pt[b, p], 0, 0, 0), p=p))
                  for p in range(n_pages)]
    grid_spec = pltpu.PrefetchScalarGridSpec(
        num_scalar_prefetch=1, grid=(nb,),
        in_specs=[per(q_col.shape[1:]), per(kn_col.shape[1:]), per(vn_col.shape[1:]), per(bias3.shape[1:])] + page_specs * 2,
        out_specs=per(q_col.shape[1:]),
        scratch_shapes=[pltpu.VMEM((n_pages + 1, page), F32)])
    return pl.pallas_call(
        _dsa_sample_attend_body, grid_spec=grid_spec,
        out_shape=jax.ShapeDtypeStruct(q_col.shape, F32),
        compiler_params=_params(("arbitrary",)), name="dsa_sample_attend",
    )(page_table, q_col, kn_col, vn_col, bias3, *([k_t] * n_pages), *([v_t] * n_pages))


def _swiglu(h, wg_ref, wu_ref, wd_ref, n_chunks):
    fc = wg_ref.shape[-1] // n_chunks
    y = None
    for c in range(n_chunks):
        cols = slice(fc * c, fc * (c + 1))
        gate = _dot(h, wg_ref[:, cols])
        act = (gate * jax.nn.sigmoid(gate) * _dot(h, wu_ref[:, cols])).astype(BF16)
        part = _dot(act, wd_ref[cols, :])
        y = part if y is None else y + part
    return y


def _outproj_ffn_body(x_ref, ys_ref, ya_ref, wos_ref, woa_ref, g_ref, wg_ref, wu_ref, wd_ref, o_ref):
    x = x_ref[...] + _dot(ys_ref[...], wos_ref[...]) + _dot(ya_ref[...], woa_ref[...])
    h = _rms(x, g_ref[...]).astype(BF16)
    o_ref[...] = x + _swiglu(h, wg_ref, wu_ref, wd_ref, 2)


def _outproj_ffn(x3, ys, ya, wos, woa, g, wg, wu, wd, seq, tm):
    nb, _, D = x3.shape
    nblk = seq // tm
    row = lambda w: pl.BlockSpec((tm, w), lambda b, j: (b * nblk + j, 0))
    single = lambda a: pl.BlockSpec(a.shape, lambda b, j: (0,) * a.ndim, pipeline_mode=pl.Buffered(1))
    return pl.pallas_call(
        _outproj_ffn_body, grid=(nb, nblk),
        in_specs=[pl.BlockSpec((None, tm, D), lambda b, j: (b, j, 0)), row(ys.shape[1]), row(ya.shape[1]),
                  single(wos), single(woa), single(g), single(wg), single(wu), single(wd)],
        out_specs=row(D), out_shape=jax.ShapeDtypeStruct((nb * seq, D), F32),
        compiler_params=_params(("arbitrary", "arbitrary")), name="outproj_ffn",
    )(x3, ys, ya, wos, woa, g, wg, wu, wd)


def _pool_mix(h, window_sum, divisor, pw_ref, scale):
    gd = h.shape[1] // len(POOL_WINDOWS)
    cols = []
    for g, w in enumerate(POOL_WINDOWS):
        pooled = window_sum(g, w) / divisor(w) - h[:, gd * g:gd * (g + 1)]
        cols.append(_dot(pooled.astype(BF16), pw_ref[g]))
    return jnp.concatenate(cols, axis=1) * scale


def _pool_prompt_body(x_ref, g_ref, pw_ref, sc_ref, o_ref, hist_ref, ext_ref):
    tm, D = x_ref.shape
    gd = D // len(POOL_WINDOWS)
    halo = POOL_HIST + 1
    j = pl.program_id(1)

    @pl.when(j == 0)
    def _():
        ext_ref[0:halo] = jnp.zeros((halo, D), F32)

    x = x_ref[...]
    h = _rms(x, g_ref[...])
    ext_ref[halo:] = h
    pos = (j * tm + lax.broadcasted_iota(I32, (tm, 1), 0)).astype(F32)

    def window_sum(g, w):
        acc = h[:, gd * g:gd * (g + 1)]
        for k in range(1, w):
            acc = acc + ext_ref[halo - k:halo - k + tm, gd * g:gd * (g + 1)]
        return acc

    mixed = _pool_mix(h, window_sum, lambda w: jnp.minimum(float(w), pos + 1.0), pw_ref, sc_ref[...])
    o_ref[...] = x + mixed
    ext_ref[0:halo] = ext_ref[tm:tm + halo]

    @pl.when(j == pl.num_programs(1) - 1)
    def _():
        hist_ref[...] = h[tm - POOL_HIST:, :]


def _pool_prompt(x2d, g, pw, scale, batch, seq, tm):
    T, D = x2d.shape
    nblk = seq // tm
    row = pl.BlockSpec((tm, D), lambda b, j: (b * nblk + j, 0))
    return pl.pallas_call(
        _pool_prompt_body, grid=(batch, nblk),
        in_specs=[row, _const_spec(g.shape), _const_spec(pw.shape), _const_spec(scale.shape)],
        out_specs=(row, pl.BlockSpec((None, POOL_HIST, D), lambda b, j: (b, 0, 0))),
        out_shape=(jax.ShapeDtypeStruct((T, D), F32), jax.ShapeDtypeStruct((batch, POOL_HIST, D), F32)),
        scratch_shapes=[pltpu.VMEM((tm + POOL_HIST + 1, D), F32)],
        compiler_params=_params(("arbitrary", "arbitrary")), name="pool_prompt",
    )(x2d, g, pw, scale)


def _pool_sample_body(x_ref, hist_ref, g_ref, pw_ref, sc_ref, o_ref, nh_ref):
    D = x_ref.shape[1]
    gd = D // len(POOL_WINDOWS)
    x = x_ref[...]
    h = _rms(x, g_ref[...])

    def window_sum(g, w):
        acc = h[:, gd * g:gd * (g + 1)]
        for k in range(1, w):
            acc = acc + hist_ref[POOL_HIST - k, :, gd * g:gd * (g + 1)]
        return acc

    mixed = _pool_mix(h, window_sum, float, pw_ref, sc_ref[...])
    o_ref[...] = x + mixed
    for i in range(POOL_HIST - 1):
        nh_ref[i] = hist_ref[i + 1]
    nh_ref[POOL_HIST - 1] = h


def _pool_sample(x2d, hist_t, g, pw, scale):
    sds = jax.ShapeDtypeStruct
    return pl.pallas_call(
        _pool_sample_body, out_shape=(sds(x2d.shape, F32), sds(hist_t.shape, F32)),
        compiler_params=pltpu.CompilerParams(vmem_limit_bytes=VMEM_LIMIT), name="pool_sample",
    )(x2d, hist_t, g, pw, scale)


def _store_row_tiles(ref, x):
    for s in range(ref.shape[1]):
        ref[:, s, :] = x[:, LANES * s:LANES * (s + 1)]


def _load_row_tiles(ref):
    return jnp.concatenate([ref[:, s, :] for s in range(ref.shape[1])], axis=1)


def _router_body(x_ref, g_ref, wr_ref, h_ref, sel_ref, *, n_experts):
    h = _rms(x_ref[...], g_ref[...])
    _store_row_tiles(h_ref, h)
    w = wr_ref[...]
    h_hi, w_hi = h.astype(BF16), w.astype(BF16)
    h_lo, w_lo = (h - h_hi.astype(F32)).astype(BF16), (w - w_hi.astype(F32)).astype(BF16)
    logits = _dot(h_hi, w_hi) + (_dot(h_hi, w_lo) + _dot(h_lo, w_hi))
    lane = lax.broadcasted_iota(I32, logits.shape, 1).astype(F32)
    logits = jnp.where(lane < n_experts, logits, -jnp.inf)
    v1 = jnp.max(logits, axis=1, keepdims=True)
    i1 = jnp.min(jnp.where(logits == v1, lane, float(LANES)), axis=1, keepdims=True)
    rest = jnp.where(lane == i1, -jnp.inf, logits)
    v2 = jnp.max(rest, axis=1, keepdims=True)
    i2 = jnp.min(jnp.where(rest == v2, lane, float(LANES)), axis=1, keepdims=True)
    e2 = jnp.exp(v2 - v1)
    den = 1.0 + e2
    sel_ref[...] = jnp.where(lane == 0.0, i1, jnp.where(lane == 1.0, i2, jnp.where(lane == 2.0, 1.0 / den, e2 / den)))


def _router(x2d, g, wr_pad, n_experts, tm):
    T, D = x2d.shape
    row = lambda w: pl.BlockSpec((tm, w), lambda i: (i, 0))
    body = functools.partial(_router_body, n_experts=n_experts)
    return pl.pallas_call(
        body, grid=(T // tm,),
        in_specs=[row(D), _const_spec(g.shape), _const_spec(wr_pad.shape)],
        out_specs=(pl.BlockSpec((tm, D // LANES, LANES), lambda i: (i, 0, 0)), row(LANES)),
        out_shape=(jax.ShapeDtypeStruct((T, D // LANES, LANES), F32), jax.ShapeDtypeStruct((T, LANES), F32)),
        compiler_params=_params(("arbitrary",)), name="router",
    )(x2d, g, wr_pad)


def _row_gather(idx_ref, base, src_hbm, dst, sem):
    def issue(r, carry):
        pltpu.make_async_copy(src_hbm.at[idx_ref[base + r]], dst.at[r], sem).start()
        return carry

    lax.fori_loop(0, dst.shape[0], issue, 0, unroll=8)


def _row_gather_wait(src_hbm, dst, sem):
    pltpu.make_async_copy(src_hbm.at[pl.ds(0, dst.shape[0])], dst, sem).wait()


def _moe_dispatch_body(p1_ref, p2_ref, ends_ref, h_ref, x_hbm, zero_ref, stage, sems, *, tm):
    i = pl.program_id(0)
    last = pl.num_programs(0) - 1
    gt = zero_ref.shape[0]
    n_groups = ends_ref.shape[0]
    zsem = sems.at[2]
    slot = i % 2

    def fill_tile(e, act):
        if e < n_groups:
            start = ends_ref[e] - gt
            exists = ends_ref[e] > (ends_ref[e - 1] if e else 0)
        else:
            start = ends_ref[n_groups - 1] + (e - n_groups) * gt
            exists = start < x_hbm.shape[0]

        @pl.when(exists)
        def _():
            act(pltpu.make_async_copy(zero_ref, x_hbm.at[pl.ds(pl.multiple_of(start, gt), gt)], zsem))

    @pl.when(i == 0)
    def _():
        zero_ref[...] = jnp.zeros(zero_ref.shape, F32)
        for e in range(2 * n_groups):
            fill_tile(e, lambda cp: cp.start())
        for e in range(2 * n_groups):
            fill_tile(e, lambda cp: cp.wait())

    def wait_rows(s):
        for _ in range(2):
            pltpu.make_async_copy(stage.at[s], x_hbm.at[pl.ds(0, tm)], sems.at[s]).wait()

    @pl.when(i >= 2)
    def _():
        wait_rows(slot)

    stage[slot] = h_ref[...]

    def issue(r, carry):
        t = i * tm + r
        pltpu.make_async_copy(stage.at[slot, r], x_hbm.at[p1_ref[t]], sems.at[slot]).start()
        pltpu.make_async_copy(stage.at[slot, r], x_hbm.at[p2_ref[t]], sems.at[slot]).start()
        return carry

    lax.fori_loop(0, tm, issue, 0, unroll=8)

    @pl.when(i == last)
    def _():
        wait_rows(slot)

    @pl.when((i == last) & (i >= 1))
    def _():
        wait_rows(1 - slot)


def _moe_dispatch(pos1, pos2, ends, h_tiles, n_rows, tm, group_tile):
    T, S, _ = h_tiles.shape
    grid_spec = pltpu.PrefetchScalarGridSpec(
        num_scalar_prefetch=3, grid=(T // tm,),
        in_specs=[pl.BlockSpec((tm, S, LANES), lambda i, p1, p2, en: (i, 0, 0))],
        out_specs=pl.BlockSpec(memory_space=pl.ANY),
        scratch_shapes=[pltpu.VMEM((group_tile, S, LANES), F32), pltpu.VMEM((2, tm, S, LANES), F32),
                        pltpu.SemaphoreType.DMA((3,))])
    return pl.pallas_call(
        functools.partial(_moe_dispatch_body, tm=tm), grid_spec=grid_spec,
        out_shape=jax.ShapeDtypeStruct((n_rows, S, LANES), F32),
        compiler_params=pltpu.CompilerParams(dimension_semantics=("arbitrary",), vmem_limit_bytes=VMEM_LIMIT,
                                             disable_bounds_checks=True),
        name="moe_dispatch",
    )(pos1, pos2, ends, h_tiles)


def _moe_group_body(te_ref, nu_ref, x_ref, wg_ref, wu_ref, wd_ref, y_ref):
    del te_ref
    i = pl.program_id(0)

    @pl.when(i < nu_ref[0])
    def _():
        _store_row_tiles(y_ref, _swiglu(_load_row_tiles(x_ref).astype(BF16), wg_ref, wu_ref, wd_ref, 2))

    @pl.when(i >= nu_ref[0])
    def _():
        y_ref[...] = jnp.zeros(y_ref.shape, F32)


def _moe_group(tile_expert, n_used, x_tiles, wg, wu, wd, tm):
    P, S, _ = x_tiles.shape
    _, D, F = wg.shape
    wspec = lambda a, b: pl.BlockSpec((None, a, b), lambda i, te, nu: (te[i], 0, 0), pipeline_mode=pl.Buffered(1))
    grid_spec = pltpu.PrefetchScalarGridSpec(
        num_scalar_prefetch=2, grid=(P // tm,),
        in_specs=[pl.BlockSpec((tm, S, LANES), lambda i, te, nu: (jnp.minimum(i, nu[0] - 1), 0, 0)),
                  wspec(D, F), wspec(D, F), wspec(F, D)],
        out_specs=pl.BlockSpec((tm, S, LANES), lambda i, te, nu: (i, 0, 0)))
    return pl.pallas_call(
        _moe_group_body, grid_spec=grid_spec, out_shape=jax.ShapeDtypeStruct((P, S, LANES), F32),
        compiler_params=_params(("arbitrary",)), name="moe_group",
    )(tile_expert, n_used, x_tiles, wg, wu, wd)


def _moe_combine_body(p1_ref, p2_ref, dst_ref, x_ref, sel_ref, y_hbm, ya_hbm, yb_hbm, buf, obuf, sem, osem):
    i = pl.program_id(0)
    last = pl.num_programs(0) - 1
    tm = x_ref.shape[0]
    n_chunks = tm // OUT_CHUNK
    rows_a = ya_hbm.shape[0]
    slot = i % 2

    def out_copies(step, s, act):
        for c in range(n_chunks):
            dst = dst_ref[step * n_chunks + c]
            src = obuf.at[s, pl.ds(c * OUT_CHUNK, OUT_CHUNK)]

            @pl.when((dst >= 0) & (dst < rows_a))
            def _():
                act(pltpu.make_async_copy(src, ya_hbm.at[pl.ds(pl.multiple_of(dst, OUT_CHUNK), OUT_CHUNK)], osem.at[s]))

            @pl.when(dst >= rows_a)
            def _():
                off = pl.multiple_of(dst - rows_a, OUT_CHUNK)
                act(pltpu.make_async_copy(src, yb_hbm.at[pl.ds(off, OUT_CHUNK)], osem.at[s]))

    def gather(tile, s):
        _row_gather(p1_ref, tile * tm, y_hbm, buf.at[s, 0], sem.at[s, 0])
        _row_gather(p2_ref, tile * tm, y_hbm, buf.at[s, 1], sem.at[s, 1])

    @pl.when(i == 0)
    def _():
        gather(0, 0)

    @pl.when(i + 1 < pl.num_programs(0))
    def _():
        gather(i + 1, 1 - slot)

    _row_gather_wait(y_hbm, buf.at[slot, 0], sem.at[slot, 0])
    _row_gather_wait(y_hbm, buf.at[slot, 1], sem.at[slot, 1])
    @pl.when(i >= 2)
    def _():
        out_copies(i - 2, slot, lambda cp: cp.wait())

    sel = sel_ref[...]
    obuf[slot] = (x_ref[...] + sel[:, 2:3] * _load_row_tiles(buf.at[slot, 0])
                  + sel[:, 3:4] * _load_row_tiles(buf.at[slot, 1]))
    out_copies(i, slot, lambda cp: cp.start())

    @pl.when(i == last)
    def _():
        out_copies(i, slot, lambda cp: cp.wait())

    @pl.when((i == last) & (i >= 1))
    def _():
        out_copies(i - 1, 1 - slot, lambda cp: cp.wait())


def _moe_combine(pos1, pos2, chunk_dst, x2d, sel, y_tiles, tm, rows_a, rows_b):
    T, D = x2d.shape
    S = y_tiles.shape[1]
    assert tm % OUT_CHUNK == 0 and rows_a % OUT_CHUNK == 0 and rows_b % OUT_CHUNK == 0
    row = lambda w: pl.BlockSpec((tm, w), lambda i, p1, p2, cd: (i, 0))
    hbm = pl.BlockSpec(memory_space=pl.ANY)
    grid_spec = pltpu.PrefetchScalarGridSpec(
        num_scalar_prefetch=3, grid=(T // tm,),
        in_specs=[row(D), row(LANES), hbm], out_specs=(hbm, hbm),
        scratch_shapes=[pltpu.VMEM((2, 2, tm, S, LANES), F32), pltpu.VMEM((2, tm, D), F32),
                        pltpu.SemaphoreType.DMA((2, 2)), pltpu.SemaphoreType.DMA((2,))])
    return pl.pallas_call(
        _moe_combine_body, grid_spec=grid_spec,
        out_shape=(jax.ShapeDtypeStruct((rows_a, D), F32), jax.ShapeDtypeStruct((rows_b, D), F32)),
        compiler_params=pltpu.CompilerParams(dimension_semantics=("arbitrary",), vmem_limit_bytes=VMEM_LIMIT,
                                             disable_bounds_checks=True),
        name="moe_combine",
    )(pos1, pos2, chunk_dst, x2d, sel, y_tiles)


def _moe_plan(sel, n_experts, tm):
    T = sel.shape[0]
    experts = jnp.concatenate([sel[:, 0], sel[:, 1]]).astype(I32)
    onehot = (experts[:, None] == jnp.arange(n_experts, dtype=I32)[None, :]).astype(I32)
    rank = jnp.cumsum(onehot, axis=0) - onehot
    counts = jnp.sum(onehot, axis=0)
    padded = (counts + tm - 1) // tm * tm
    ends = jnp.cumsum(padded)
    pos = jnp.sum(onehot * ((ends - padded)[None, :] + rank), axis=1)
    n_rows = (2 * T + n_experts * (tm - 1)) // tm * tm
    tile_start = jnp.arange(n_rows // tm, dtype=I32) * tm
    tile_expert = jnp.minimum(jnp.sum((tile_start[:, None] >= ends[None, :]).astype(I32), axis=1), n_experts - 1)
    n_used = (ends[-1] // tm).astype(I32).reshape(1)
    return tile_expert, n_used, ends.astype(I32), n_rows, pos[:T], pos[T:]


def _rope_tables(pos):
    half = HEAD_DIM // 2
    inv = ROPE_THETA ** (-jnp.arange(half, dtype=F32) / half)
    ang = pos[:, None] * inv[None, :]
    cos, sin = jnp.cos(ang), jnp.sin(ang)
    reps = LANES // HEAD_DIM
    return jnp.tile(jnp.concatenate([cos, cos], axis=1), (1, reps)), jnp.tile(jnp.concatenate([-sin, sin], axis=1), (1, reps))


def kernel(x_prompt, x_sample, cache_k, cache_v, cache_kidx, page_table, state_ssm_re, state_ssm_im, state_pool, meta_tokens, norm_mix0, w_in0, q_norm, k_norm, ssm_lambda_re, ssm_lambda_im, ssm_log_dt, ssm_b_re, ssm_b_im, ssm_c_re, ssm_c_im, ssm_d, ssm_w_glu, w_out0, norm_ffn0, ffn_w_gate, ffn_w_up, ffn_w_down, norm_mix1, pool_w, pool_scale, norm_ffn1, router_w, moe_w_gate, moe_w_up, moe_w_down):
    B, S, D = x_prompt.shape
    DB, DS, _ = x_sample.shape
    assert DS == 1, "one new token per sample sequence"
    L = S + N_META
    n_pool, page = cache_k.shape[1], cache_k.shape[2]
    n_pages = page_table.shape[1]
    past = n_pages * page
    d_att = N_HEADS * HEAD_DIM
    d_qi = N_IDX_HEADS * IDX_DIM
    G, P = ssm_lambda_re.shape[1:]
    d_ssm = G * SSM_GROUP
    n_state = G * P
    E = router_w.shape[-1]
    topk_p = min(TOPK_MAX, S // 4)
    topk_s = min(TOPK_MAX, (past + DS) // 4)
    tm = _row_block(L, 1024)
    assert tm >= POOL_HIST + 1

    tm_in = LANES * (-(-L // (3 * LANES)))
    l_pad = -(-L // tm_in) * tm_in
    x_p = jnp.concatenate([jnp.broadcast_to(meta_tokens[None], (B, N_META, D)), x_prompt,
                           jnp.zeros((B, l_pad - L, D), x_prompt.dtype)], axis=1)
    x_s = x_sample.reshape(1, DB, D)

    w_in = w_in0[0]
    o1, o2, o3, o4, o5, o6 = d_ssm, d_ssm + d_att, d_ssm + 2 * d_att, d_ssm + 3 * d_att, d_ssm + 3 * d_att + d_qi, d_ssm + 3 * d_att + d_qi + IDX_DIM
    w_kw = jnp.pad(w_in[:, o5:], ((0, 0), (0, LANES - (w_in.shape[1] - o5))))
    ws = tuple(w.astype(BF16) for w in (w_in[:, :o1], w_in[:, o1:o2], w_in[:, o2:o3], w_in[:, o3:o4], w_in[:, o4:o5], w_kw))
    g_mix0 = norm_mix0[0].reshape(1, D)
    qn = jnp.tile(q_norm[0], N_HEADS).reshape(1, d_att)
    kn = jnp.tile(k_norm[0], N_HEADS).reshape(1, d_att)
    head_of = jnp.arange(d_att) // HEAD_DIM
    hm = jnp.where(head_of[:, None] == head_of[None, :], 1.0 / HEAD_DIM, 0.0).astype(BF16)
    cos_p, sin_p = _rope_tables(jnp.arange(l_pad, dtype=F32))
    cos_s, sin_s = _rope_tables(jnp.full((DB,), float(past), F32))

    flat = lambda a: a.reshape(-1, a.shape[-1])
    u_p, q_p, kt_p, vt_p, qi_p, kit_p, kw_p, kb_p, vb_p, kib_p = _inproj(
        x_p, cos_p, sin_p, tm_in, L, g_mix0, ws, qn, kn, hm, True)
    u_p, q_p, qi_p, kw_p, kb_p, vb_p, kib_p = (flat(a) for a in (u_p, q_p, qi_p, kw_p, kb_p, vb_p, kib_p))
    u_s, q_s, k_s, v_s, qi_s, ki_s, kw_s, _, _, _ = (
        flat(a) for a in _inproj(x_s, cos_s, sin_s, DB, DB, g_mix0, ws, qn, kn, hm, False))

    lbr, lbi, bbr, bbi = _s5_prep(ssm_lambda_re[0], ssm_lambda_im[0], ssm_log_dt[0], ssm_b_re[0], ssm_b_im[0])
    gps = LANES // SSM_GROUP
    n_slab = G // gps
    eye = jnp.eye(gps, dtype=F32)

    def b_slabs(bt):
        return jnp.einsum('sgcp,gh->sgchp', bt.reshape(n_slab, gps, SSM_GROUP, P), eye).reshape(n_slab, LANES, gps * P)

    def c_slabs(c):
        return jnp.einsum('sgcp,gh->sgphc', c.reshape(n_slab, gps, SSM_GROUP, P), eye).reshape(n_slab, gps * P, LANES)

    wb = jnp.concatenate([b_slabs(bbr), b_slabs(bbi)], axis=2)
    wc = jnp.stack([c_slabs(ssm_c_re[0]), -c_slabs(ssm_c_im[0])], axis=1).astype(BF16)
    lbr_f, lbi_f = lbr.reshape(1, n_state), lbi.reshape(1, n_state)
    d_skip = ssm_d[0].reshape(1, d_ssm)
    wglu = ssm_w_glu[0].astype(BF16)
    assert B % SUBLANES == 0, "the S5 scan keeps one batch row per sublane"
    steps = max(d for d in range(1, L + 1) if L % d == 0 and d * B <= 512 and (d * B) % (2 * SUBLANES) == 0)
    u_tm = u_p.reshape(B, L, d_ssm).swapaxes(0, 1).reshape(L * B, d_ssm)
    zero_h = jnp.zeros((B, n_state), F32)
    ys_tm, hre_p, him_p = _s5(u_tm, zero_h, zero_h, lbr_f, lbi_f, wb, wc, d_skip, wglu, B, steps, False)
    ys_p = ys_tm.reshape(L, B, d_ssm).swapaxes(0, 1).reshape(B * L, d_ssm)
    ys_s, hre_s, him_s = _s5(u_s, state_ssm_re[0].reshape(DB, n_state), state_ssm_im[0].reshape(DB, n_state),
                             lbr_f, lbi_f, wb, wc, d_skip, wglu, DB, 1, True)

    ya_p = _dsa_prompt(qi_p, kw_p, q_p, kib_p, kb_p, vb_p, B, L, tm, topk_p)
    assert page == LANES
    w3 = kw_s[:, IDX_DIM:IDX_DIM + N_IDX_HEADS].reshape(DB, N_IDX_HEADS, 1)
    scores = _dsa_sample_scores(page_table, qi_s.astype(F32).reshape(DB, d_qi, 1), w3, ki_s.reshape(DB, IDX_DIM, 1),
                                jnp.transpose(cache_kidx[0], (0, 2, 1)))
    bias = _dsa_sample_select(scores.reshape(DB, (n_pages + 1) * page), past + DS, topk_s)
    ya_s = _dsa_sample_attend(page_table, q_s.astype(F32).reshape(DB, d_att, 1), k_s.reshape(DB, d_att, 1),
                              v_s.reshape(DB, d_att, 1), bias.reshape(DB, n_pages + 1, page),
                              jnp.transpose(cache_k[0], (0, 2, 3, 1)), jnp.transpose(cache_v[0], (0, 2, 3, 1)))
    ya_s = ya_s.reshape(DB, d_att).astype(BF16)

    w_out = w_out0[0].astype(BF16)
    ffn = (w_out[:d_ssm], w_out[d_ssm:], norm_ffn0[0].reshape(1, D), ffn_w_gate[0].astype(BF16),
           ffn_w_up[0].astype(BF16), ffn_w_down[0].astype(BF16))
    x_p = _outproj_ffn(x_p, ys_p, ya_p, *ffn, L, tm)
    x_s = _outproj_ffn(x_s, ys_s, ya_s, *ffn, DB, DB)

    g_mix1 = norm_mix1[0].reshape(1, D)
    pw = pool_w[0].astype(BF16)
    psc = pool_scale[0].reshape(1, D)
    T_all = B * L + DB
    x_p, pool_p = _pool_prompt(x_p, g_mix1, pw, psc, B, L, tm)
    x_s, hist_t = _pool_sample(x_s, state_pool[0].swapaxes(0, 1), g_mix1, pw, psc)
    pool_s = hist_t.swapaxes(0, 1)
    x_all = jnp.concatenate([x_p, x_s], axis=0)

    g_ffn1 = norm_ffn1[0].reshape(1, D)
    wr_pad = jnp.pad(router_w[0], ((0, 0), (0, LANES - E)))
    wg, wu, wd = moe_w_gate[0].astype(BF16), moe_w_up[0].astype(BF16), moe_w_down[0].astype(BF16)
    tr = _row_block(T_all, MOE_TILE)
    h_tiles, sel = _router(x_all, g_ffn1, wr_pad, E, tr)
    tile_expert, n_used, ends, n_rows, pos1, pos2 = _moe_plan(sel, E, MOE_TILE)
    x_tiles = _moe_dispatch(pos1, pos2, ends, h_tiles, n_rows, tr, MOE_TILE)
    y_tiles = _moe_group(tile_expert, n_used, x_tiles, wg, wu, wd, MOE_TILE)
    assert N_META % OUT_CHUNK == 0 and L % OUT_CHUNK == 0 and DB % OUT_CHUNK == 0
    row0 = jnp.arange(T_all // OUT_CHUNK, dtype=I32) * OUT_CHUNK
    in_seq = row0 % L
    chunk_dst = jnp.where(row0 >= B * L, row0 - B * N_META,
                          jnp.where(in_seq < N_META, -1, (row0 // L) * S + in_seq - N_META))
    y_prompt, y_sample = _moe_combine(pos1, pos2, chunk_dst, x_all, sel, y_tiles, tr, B * S, DB)
    y_prompt = y_prompt.reshape(B, S, D)
    y_sample = y_sample.reshape(DB, 1, D)
    k_prompt = jnp.transpose(kt_p, (0, 3, 1, 2))[None]
    v_prompt = jnp.transpose(vt_p, (0, 3, 1, 2))[None]
    kidx_prompt = jnp.transpose(kit_p, (0, 2, 1))[None]
    return (y_prompt, y_sample, k_prompt, v_prompt, kidx_prompt,
            hre_p.reshape(1, B, G, P), him_p.reshape(1, B, G, P), pool_p[None],
            k_s.reshape(1, DB, 1, N_HEADS, HEAD_DIM), v_s.reshape(1, DB, 1, N_HEADS, HEAD_DIM),
            ki_s.reshape(1, DB, 1, IDX_DIM), hre_s.reshape(1, DB, G, P), him_s.reshape(1, DB, G, P), pool_s[None])
```

```python
import functools
import math

import jax
import jax.numpy as jnp
from jax import lax
from jax.experimental import pallas as pl
from jax.experimental.pallas import tpu as pltpu

F32 = jnp.float32
BF16 = jnp.bfloat16
I32 = jnp.int32

N_META = 16
SSM_GROUP = 16
SSM_STATE = 64
N_HEADS = 8
HEAD_DIM = 64
N_IDX_HEADS = 8
IDX_DIM = 64
TOPK_MAX = 256
ROPE_THETA = 10000.0
POOL_WINDOWS = (2, 4, 8, 16)
POOL_HIST = max(POOL_WINDOWS) - 1
TOP_K_EXPERTS = 2
EPS = 1e-6

LANES = 128
SUBLANES = 8
VMEM_LIMIT = 56 * 1024 * 1024
OUT_CHUNK = 2 * SUBLANES
MOE_TILE = 256
BISECT_ITERS = 16
NEG_BIG = -1e30


def _params(sem):
    return pltpu.CompilerParams(dimension_semantics=sem, vmem_limit_bytes=VMEM_LIMIT)


def _row_block(n, cap, mult=2 * SUBLANES):
    best = None
    for d in range(mult, min(n, cap) + 1, mult):
        if n % d == 0:
            best = d
    assert best is not None, n
    return best


def _rms(x, g):
    return x * lax.rsqrt(jnp.mean(x * x, axis=-1, keepdims=True) + EPS) * g


def _dot(a, b):
    return jnp.dot(a, b, preferred_element_type=F32)


def _dot_nt(a, b):
    return lax.dot_general(a, b, (((1,), (1,)), ((), ())), preferred_element_type=F32)


def _const_spec(shape):
    nd = len(shape)
    return pl.BlockSpec(shape, lambda *_: (0,) * nd)


def _inproj_body(x_ref, g_ref, wu_ref, wq_ref, wk_ref, wv_ref, wqi_ref, wkw_ref, qn_ref, kn_ref, hm_ref,
                 cos_ref, sin_ref,
                 u_ref, q_ref, k_ref, v_ref, qi_ref, ki_ref, kw_ref, kb_ref, vb_ref, kib_ref, *, transposed):
    def store_kv(ref, z):
        if not transposed:
            ref[...] = z
            return
        per_slab = LANES // HEAD_DIM
        for sp in range(z.shape[1] // LANES):
            zt = z[:, LANES * sp:LANES * (sp + 1)].T
            ref[per_slab * sp:per_slab * (sp + 1)] = zt.reshape(per_slab, HEAD_DIM, z.shape[0])

    h = _rms(x_ref[...], g_ref[...]).astype(BF16)
    cos = cos_ref[...]
    sin = sin_ref[...]
    lane = lax.broadcasted_iota(I32, (1, LANES), 1)
    lo_half = (lane & (HEAD_DIM // 2)) == 0

    def rope(z):
        cols = []
        for j in range(z.shape[1] // LANES):
            zs = z[:, LANES * j:LANES * (j + 1)]
            partner = jnp.where(lo_half, pltpu.roll(zs, LANES - HEAD_DIM // 2, 1), pltpu.roll(zs, HEAD_DIM // 2, 1))
            cols.append(zs * cos + partner * sin)
        return cols[0] if len(cols) == 1 else jnp.concatenate(cols, axis=1)

    def head_norm(z, gn):
        ms = _dot((z * z).astype(BF16), hm_ref[...])
        return z * lax.rsqrt(ms + EPS) * gn

    u_ref[...] = _dot(h, wu_ref[...])
    q = rope(head_norm(_dot(h, wq_ref[...]), qn_ref[...]))
    q_ref[...] = (q * (HEAD_DIM ** -0.5)).astype(BF16)
    k = rope(head_norm(_dot(h, wk_ref[...]), kn_ref[...]))
    store_kv(k_ref, k)
    kb_ref[...] = k.astype(BF16)
    v = _dot(h, wv_ref[...])
    store_kv(v_ref, v)
    vb_ref[...] = v.astype(BF16)
    qi_ref[...] = (rope(_dot(h, wqi_ref[...])) * (IDX_DIM ** -0.5)).astype(BF16)
    zkw = _dot(h, wkw_ref[...])
    kir = rope(zkw)
    kw_ref[...] = jnp.where(lane < IDX_DIM, kir, zkw * (N_IDX_HEADS ** -0.5))
    ki_ref[...] = kir.T[:IDX_DIM] if transposed else kir[:, :IDX_DIM]
    kib_ref[...] = jnp.where(lane < IDX_DIM, kir, pltpu.roll(kir, IDX_DIM, 1)).astype(BF16)


def _inproj(x3, cos_t, sin_t, tm, out_rows, g, ws, qn, kn, hm, transposed):
    nb, rows, D = x3.shape
    wu, wq, wk, wv, wqi, wkw = ws
    d_ssm, d_att, d_qi = wu.shape[1], wq.shape[1], wqi.shape[1]
    row = lambda w: pl.BlockSpec((None, tm, w), lambda b, j: (b, j, 0))
    tab = pl.BlockSpec((tm, LANES), lambda b, j: (j, 0))
    sds = lambda w, dt: jax.ShapeDtypeStruct((nb, out_rows, w), dt)
    if transposed:
        kv_shape = jax.ShapeDtypeStruct((nb, N_HEADS, HEAD_DIM, out_rows), F32)
        kv_spec = pl.BlockSpec((None, N_HEADS, HEAD_DIM, tm), lambda b, j: (b, 0, 0, j))
        ki_shape = jax.ShapeDtypeStruct((nb, IDX_DIM, out_rows), F32)
        ki_spec = pl.BlockSpec((None, IDX_DIM, tm), lambda b, j: (b, 0, j))
    else:
        kv_shape, kv_spec, ki_shape, ki_spec = sds(d_att, F32), row(d_att), sds(IDX_DIM, F32), row(IDX_DIM)
    out_shape = (
        sds(d_ssm, F32),
        sds(d_att, BF16),
        kv_shape,
        kv_shape,
        sds(d_qi, BF16),
        ki_shape,
        sds(LANES, F32),
        sds(d_att, BF16),
        sds(d_att, BF16),
        sds(LANES, BF16),
    )
    out_specs = (row(d_ssm), row(d_att), kv_spec, kv_spec, row(d_qi), ki_spec, row(LANES),
                 row(d_att), row(d_att), row(LANES))
    in_specs = [row(D), _const_spec(g.shape)] + [_const_spec(w.shape) for w in ws] + [
        _const_spec(qn.shape), _const_spec(kn.shape), _const_spec(hm.shape), tab, tab]
    return pl.pallas_call(
        functools.partial(_inproj_body, transposed=transposed), grid=(nb, rows // tm),
        in_specs=in_specs, out_specs=out_specs, out_shape=out_shape,
        compiler_params=_params(("arbitrary", "arbitrary")), name="inproj_t" if transposed else "inproj",
    )(x3, g, *ws, qn, kn, hm, cos_t, sin_t)


def _s5_prep_body(lr_ref, li_ref, ldt_ref, btr_ref, bti_ref, lbr_ref, lbi_ref, bbr_ref, bbi_ref):
    lr = lr_ref[...]
    li = li_ref[...]
    dt = jnp.exp(ldt_ref[...])
    mag = jnp.exp(lr * dt)
    lbr = mag * jnp.cos(li * dt)
    lbi = mag * jnp.sin(li * dt)
    lbr_ref[...] = lbr
    lbi_ref[...] = lbi
    den = lr * lr + li * li
    cr = ((lbr - 1.0) * lr + lbi * li) / den
    ci = (lbi * lr - (lbr - 1.0) * li) / den
    btr = btr_ref[...]
    bti = bti_ref[...]
    bbr_ref[...] = cr * btr - ci * bti
    bbi_ref[...] = cr * bti + ci * btr


def _s5_prep(lam_re, lam_im, log_dt, b_re, b_im):
    G, P = lam_re.shape
    C = b_re.shape[-1]
    btr = jnp.swapaxes(b_re, 1, 2)
    bti = jnp.swapaxes(b_im, 1, 2)
    sds = jax.ShapeDtypeStruct
    return pl.pallas_call(
        _s5_prep_body,
        out_shape=(sds((G, 1, P), F32), sds((G, 1, P), F32), sds((G, C, P), F32), sds((G, C, P), F32)),
        name="s5_prep",
    )(lam_re.reshape(G, 1, P), lam_im.reshape(G, 1, P), log_dt.reshape(G, 1, 1), btr, bti)


def _s5_body(u_ref, h0r_ref, h0i_ref, lbr_ref, lbi_ref, wb_ref, wc_ref, d_ref, wglu_ref,
             y_ref, hr_ref, hi_ref, bu_ref, st_ref, *, nb, steps, precise):
    n_slab = wb_ref.shape[0]
    cin = wb_ref.shape[1]
    sw = wb_ref.shape[2] // 2
    n_state = n_slab * sw

    @pl.when(pl.program_id(0) == 0)
    def _():
        st_ref[0] = h0r_ref[...]
        st_ref[1] = h0i_ref[...]

    u = u_ref[...]
    for i in range(n_slab):
        ui = u[:, cin * i:cin * (i + 1)]
        if precise:
            bu = jnp.dot(ui, wb_ref[i], preferred_element_type=F32, precision=lax.Precision.HIGHEST)
        else:
            bu = _dot(ui.astype(BF16), wb_ref[i].astype(BF16))
        bu_ref[:, sw * i:sw * (i + 1)] = bu[:, :sw]
        bu_ref[:, n_state + sw * i:n_state + sw * (i + 1)] = bu[:, sw:]

    for i in range(n_slab):
        re_cols = slice(sw * i, sw * (i + 1))
        im_cols = slice(n_state + sw * i, n_state + sw * (i + 1))
        lr = jnp.broadcast_to(lbr_ref[:, re_cols], (nb, sw))
        li = jnp.broadcast_to(lbi_ref[:, re_cols], (nb, sw))

        def step(t, carry, re_cols=re_cols, im_cols=im_cols, lr=lr, li=li):
            hr, hi = carry
            rows = pl.ds(pl.multiple_of(t * nb, nb), nb)
            nhr = lr * hr - li * hi + bu_ref[rows, re_cols]
            nhi = lr * hi + li * hr + bu_ref[rows, im_cols]
            bu_ref[rows, re_cols] = nhr
            bu_ref[rows, im_cols] = nhi
            return nhr, nhi

        hr, hi = lax.fori_loop(0, steps, step, (st_ref[0, :, re_cols], st_ref[1, :, re_cols]))
        st_ref[0, :, re_cols] = hr
        st_ref[1, :, re_cols] = hi

    ys = []
    for i in range(n_slab):
        h_re = bu_ref[:, sw * i:sw * (i + 1)].astype(BF16)
        h_im = bu_ref[:, n_state + sw * i:n_state + sw * (i + 1)].astype(BF16)
        ys.append(_dot(h_re, wc_ref[i, 0]) + _dot(h_im, wc_ref[i, 1]))
    y = jnp.concatenate(ys, axis=1) + d_ref[...] * u
    y = 0.5 * y * (1.0 + jnp.tanh(math.sqrt(2.0 / math.pi) * (y + 0.044715 * (y * y * y))))
    y = y * jax.nn.sigmoid(_dot(y.astype(BF16), wglu_ref[...]))
    y_ref[...] = y.astype(BF16)
    hr_ref[...] = st_ref[0]
    hi_ref[...] = st_ref[1]


def _s5(u_tm, h0_re, h0_im, lbr_flat, lbi_flat, wb, wc, d_skip, wglu, nb, steps, precise):
    rows_total, d_ssm = u_tm.shape
    n_state = h0_re.shape[1]
    rows = nb * steps
    sds = jax.ShapeDtypeStruct
    body = functools.partial(_s5_body, nb=nb, steps=steps, precise=precise)
    return pl.pallas_call(
        body, grid=(rows_total // rows,),
        in_specs=[pl.BlockSpec((rows, d_ssm), lambda c: (c, 0)), _const_spec(h0_re.shape), _const_spec(h0_im.shape),
                  _const_spec(lbr_flat.shape), _const_spec(lbi_flat.shape), _const_spec(wb.shape),
                  _const_spec(wc.shape), _const_spec(d_skip.shape), _const_spec(wglu.shape)],
        out_specs=(pl.BlockSpec((rows, d_ssm), lambda c: (c, 0)), _const_spec(h0_re.shape), _const_spec(h0_im.shape)),
        out_shape=(sds((rows_total, d_ssm), BF16), sds(h0_re.shape, F32), sds(h0_im.shape, F32)),
        scratch_shapes=[pltpu.VMEM((rows, 2 * n_state), F32), pltpu.VMEM((2, nb, n_state), F32)],
        compiler_params=_params(("arbitrary",)), name="s5_precise" if precise else "s5",
    )(u_tm, h0_re, h0_im, lbr_flat, lbi_flat, wb, wc, d_skip, wglu)


def _topk_bias(s_ref, n, kk):
    rows = s_ref.shape[0]
    cols = slice(0, n)

    def count(pred):
        return jnp.sum(jnp.where(pred(s_ref[:, cols]), 1.0, 0.0), axis=1, keepdims=True)

    def largest_below(hi):
        s = s_ref[:, cols]
        return jnp.max(jnp.where(s < hi, s, -jnp.inf), axis=1, keepdims=True)

    s0 = s_ref[:, cols]
    smax = jnp.max(s0, axis=1, keepdims=True)
    smin = jnp.min(jnp.where(s0 > -jnp.inf, s0, jnp.inf), axis=1, keepdims=True)

    def bisect(_, c):
        lo, hi = c
        mid = lo + (hi - lo) * 0.5
        ge = count(lambda s: s >= mid) >= kk
        return jnp.where(ge, mid, lo), jnp.where(ge, hi, mid)

    _, hi = lax.fori_loop(0, BISECT_ITERS, bisect, (smin, smax + (smax - smin) + 1.0))
    thr = largest_below(hi)
    cnt = count(lambda s: s >= thr)

    def short(c):
        return jnp.min(c[2] - kk) < 0.0

    def lower(c):
        hi, thr, cnt = c
        hi = jnp.where(cnt < kk, thr, hi)
        thr = largest_below(hi)
        return hi, thr, count(lambda s: s >= thr)

    _, thr, _ = lax.while_loop(short, lower, (hi, thr, cnt))
    need = kk - count(lambda s: s > thr)
    blk = 2 * LANES
    tri = (lax.broadcasted_iota(I32, (blk, blk), 0) <= lax.broadcasted_iota(I32, (blk, blk), 1)).astype(BF16)
    seen = jnp.zeros((rows, 1), F32)
    for c0 in range(0, n, blk):
        w = min(blk, n - c0)
        s = s_ref[:, c0:c0 + w]
        tie = s == thr
        upto = seen + _dot(jnp.where(tie, 1.0, 0.0).astype(BF16), tri[:w, :w])
        seen = upto[:, w - 1:w]
        s_ref[:, c0:c0 + w] = jnp.where((s > thr) | (tie & (upto <= need)), 0.0, NEG_BIG)


def _dsa_prompt_block(q0, rows, n, qim_ref, qm_ref, w_ref, kip_ref, kp_ref, vp_ref, s_ref, acc_ref, topk):
    tq = rows.stop - rows.start
    s_ref = s_ref.at[0:tq]
    cols = slice(0, n)
    lane = lax.broadcasted_iota(I32, (1, LANES), 1)
    s_ref[:, cols] = jnp.zeros((tq, n), F32)

    def score_head(h, carry):
        d = jnp.maximum(_dot_nt(qim_ref[h, rows, :], kip_ref[cols, :]), 0.0)
        s_ref[:, cols] += w_ref[h, rows, :] * d
        return carry

    lax.fori_loop(0, N_IDX_HEADS, score_head, 0)
    qpos = q0 + lax.broadcasted_iota(I32, (tq, 1), 0)
    kpos = lax.broadcasted_iota(I32, (1, n), 1)
    s_ref[:, cols] = jnp.where(kpos <= qpos, s_ref[:, cols], -jnp.inf)
    _topk_bias(s_ref, n, jnp.minimum(qpos + 1, topk).astype(F32))

    def head_pair(sp, carry):
        outs = []
        for hh in range(2):
            lg = _dot_nt(qm_ref[2 * sp + hh, rows, :], kp_ref[sp, cols, :]) + s_ref[:, cols]
            p = jnp.exp(lg - jnp.max(lg, axis=1, keepdims=True))
            pv = _dot(p.astype(BF16), vp_ref[sp, cols, :])
            outs.append(pv[:, :LANES] / pv[:, LANES:LANES + 1])
        acc_ref[sp, rows, :] = jnp.where(lane < HEAD_DIM, outs[0], outs[1])
        return carry

    lax.fori_loop(0, acc_ref.shape[0], head_pair, 0)


def _dsa_prompt_body(qi_ref, kw_ref, q_ref, kib_ref, kb_ref, vb_ref, o_ref,
                     kip_ref, kp_ref, vp_ref, qim_ref, qm_ref, w_ref, s_ref, acc_ref, *, seq, topk):
    tq = q_ref.shape[0]
    n_slab, lp, _ = kp_ref.shape
    j = pl.program_id(1)
    lane = lax.broadcasted_iota(I32, (1, LANES), 1)

    @pl.when(j == 0)
    def _():
        kip_ref[0:seq] = kib_ref[...]
        if lp > seq:
            kip_ref[seq:] = jnp.zeros((lp - seq, LANES), BF16)
        for sp in range(n_slab):
            sl = slice(LANES * sp, LANES * (sp + 1))
            kp_ref[sp, 0:seq] = kb_ref[:, sl]
            vp_ref[sp, 0:seq, 0:LANES] = vb_ref[:, sl]
            vp_ref[sp, :, LANES:] = jnp.where(lane == 0, 1.0, 0.0).astype(BF16) + jnp.zeros((lp, LANES), BF16)
            if lp > seq:
                kp_ref[sp, seq:] = jnp.zeros((lp - seq, LANES), BF16)
                vp_ref[sp, seq:, 0:LANES] = jnp.zeros((lp - seq, LANES), BF16)

    kw = kw_ref[...]
    for h in range(N_IDX_HEADS):
        head_lanes = (lane // IDX_DIM) == (h % 2)
        sl = slice(LANES * (h // 2), LANES * (h // 2 + 1))
        qim_ref[h] = jnp.where(head_lanes, qi_ref[:, sl], jnp.zeros((), BF16))
        qm_ref[h] = jnp.where(head_lanes, q_ref[:, sl], jnp.zeros((), BF16))
        w_ref[h] = kw[:, IDX_DIM + h:IDX_DIM + h + 1]

    split = tq // 2 // (2 * SUBLANES) * (2 * SUBLANES)
    for jj in range(seq // tq):
        @pl.when(j == jj)
        def _(jj=jj):
            for rows in (slice(0, split), slice(split, tq)):
                n = min(-(-(jj * tq + rows.stop) // LANES) * LANES, lp)
                _dsa_prompt_block(jj * tq + rows.start, rows, n, qim_ref, qm_ref, w_ref, kip_ref, kp_ref, vp_ref,
                                  s_ref, acc_ref, topk)

    for sp in range(n_slab):
        o_ref[:, LANES * sp:LANES * (sp + 1)] = acc_ref[sp].astype(BF16)


def _dsa_prompt(qi, kw, q, kib, kb, vb, batch, seq, tq, topk):
    T, d_att = q.shape
    nq = seq // tq
    lp = -(-seq // LANES) * LANES
    n_slab = d_att // LANES
    qrow = lambda w: pl.BlockSpec((tq, w), lambda b, j: (b * nq + j, 0))
    full = lambda w: pl.BlockSpec((None, seq, w), lambda b, j: (b, 0, 0))
    body = functools.partial(_dsa_prompt_body, seq=seq, topk=topk)
    return pl.pallas_call(
        body, grid=(batch, nq),
        in_specs=[qrow(qi.shape[1]), qrow(LANES), qrow(d_att), full(LANES), full(d_att), full(d_att)],
        out_specs=qrow(d_att),
        out_shape=jax.ShapeDtypeStruct((T, d_att), BF16),
        scratch_shapes=[pltpu.VMEM((lp, LANES), BF16), pltpu.VMEM((n_slab, lp, LANES), BF16),
                        pltpu.VMEM((n_slab, lp, 2 * LANES), BF16),
                        pltpu.VMEM((N_IDX_HEADS, tq, LANES), BF16), pltpu.VMEM((N_HEADS, tq, LANES), BF16),
                        pltpu.VMEM((N_IDX_HEADS, tq, 1), F32),
                        pltpu.VMEM((tq, lp), F32), pltpu.VMEM((n_slab, tq, LANES), F32)],
        compiler_params=_params(("arbitrary", "arbitrary")), name="dsa_prompt",
    )(qi, kw, q, kib.reshape(batch, seq, LANES), kb.reshape(batch, seq, d_att), vb.reshape(batch, seq, d_att))


def _dsa_sample_scores_body(pt_ref, qi_ref, w_ref, kin_ref, *refs):
    del pt_ref
    page_refs, s_ref = refs[:-1], refs[-1]
    n_pages = len(page_refs)
    idx_dim, page = page_refs[0].shape
    heads = [slice(idx_dim * h, idx_dim * (h + 1)) for h in range(N_IDX_HEADS)]
    qcols = [jnp.broadcast_to(qi_ref[rows, :], (idx_dim, page)) for rows in heads]
    ws = [w_ref[h:h + 1, :] for h in range(N_IDX_HEADS)]

    def score(dot_of_head):
        s = None
        for h in range(N_IDX_HEADS):
            t = ws[h] * jnp.maximum(dot_of_head(h), 0.0)
            s = t if s is None else s + t
        return s

    for p in range(n_pages):
        kt = page_refs[p][...]
        s_ref[p:p + 1, :] = score(lambda h: jnp.sum(qcols[h] * kt, axis=0, keepdims=True))
    kin = kin_ref[...]
    s_new = score(lambda h: jnp.sum(qi_ref[heads[h], :] * kin, axis=0, keepdims=True))
    lane = lax.broadcasted_iota(I32, (1, page), 1)
    s_ref[n_pages:n_pages + 1, :] = jnp.where(lane == 0, s_new, 0.0)


def _dsa_sample_scores(page_table, qi_col, w3, ki_col, kidx_t):
    nb, n_pages = page_table.shape
    idx_dim, page = kidx_t.shape[1:]
    per = lambda shape: pl.BlockSpec((None,) + shape, lambda b, pt: (b, 0, 0))
    page_specs = [pl.BlockSpec((None, idx_dim, page), functools.partial(lambda b, pt, p: (pt[b, p], 0, 0), p=p))
                  for p in range(n_pages)]
    grid_spec = pltpu.PrefetchScalarGridSpec(
        num_scalar_prefetch=1, grid=(nb,),
        in_specs=[per(qi_col.shape[1:]), per(w3.shape[1:]), per(ki_col.shape[1:])] + page_specs,
        out_specs=per((n_pages + 1, page)))
    return pl.pallas_call(
        _dsa_sample_scores_body, grid_spec=grid_spec,
        out_shape=jax.ShapeDtypeStruct((nb, n_pages + 1, page), F32),
        compiler_params=_params(("arbitrary",)), name="dsa_sample_scores",
    )(page_table, qi_col, w3, ki_col, *([kidx_t] * n_pages))


def _dsa_sample_select_body(s_ref, b_ref, *, n_keys, topk):
    rows, width = s_ref.shape
    pos = lax.broadcasted_iota(I32, (1, width), 1)
    b_ref[...] = jnp.where(pos < n_keys, s_ref[...], -jnp.inf)
    _topk_bias(b_ref, width, jnp.full((rows, 1), float(min(topk, n_keys)), F32))


def _dsa_sample_select(scores, n_keys, topk):
    rows, width = scores.shape
    body = functools.partial(_dsa_sample_select_body, n_keys=n_keys, topk=topk)
    return pl.pallas_call(
        body, out_shape=jax.ShapeDtypeStruct((rows, width), F32),
        compiler_params=pltpu.CompilerParams(vmem_limit_bytes=VMEM_LIMIT), name="dsa_sample_select",
    )(scores)


def _dsa_sample_attend_body(pt_ref, q_ref, kn_ref, vn_ref, b_ref, *refs):
    del pt_ref
    o_ref, lg_ref = refs[-2], refs[-1]
    n_pages = (len(refs) - 2) // 2
    k_refs, v_refs = refs[:n_pages], refs[n_pages:2 * n_pages]
    n_heads, hd, page = k_refs[0].shape
    lane = lax.broadcasted_iota(I32, (1, page), 1)
    bias = b_ref[...]
    for h in range(n_heads):
        rows = slice(hd * h, hd * (h + 1))
        qc = jnp.broadcast_to(q_ref[rows, :], (hd, page))
        for p in range(n_pages):
            lg_ref[p:p + 1, :] = jnp.sum(qc * k_refs[p][h], axis=0, keepdims=True)
        lg_new = jnp.sum(q_ref[rows, :] * kn_ref[rows, :], axis=0, keepdims=True)
        lg_ref[n_pages:n_pages + 1, :] = jnp.where(lane == 0, lg_new, 0.0)
        lg = lg_ref[...] + bias
        m = jnp.max(jnp.max(lg, axis=1, keepdims=True), axis=0, keepdims=True)
        pr = jnp.exp(lg - m)
        den = jnp.sum(jnp.sum(pr, axis=1, keepdims=True), axis=0, keepdims=True)
        acc = jnp.zeros((hd, page), F32)
        for p in range(n_pages):
            acc = acc + pr[p:p + 1, :] * v_refs[p][h]
        out = jnp.sum(acc, axis=1, keepdims=True) + pr[n_pages:n_pages + 1, 0:1] * vn_ref[rows, :]
        o_ref[rows, :] = out / den


def _dsa_sample_attend(page_table, q_col, kn_col, vn_col, bias3, k_t, v_t):
    nb, n_pages = page_table.shape
    n_heads, hd, page = k_t.shape[1:]
    per = lambda shape: pl.BlockSpec((None,) + shape, lambda b, pt: (b, 0, 0))
    page_specs = [pl.BlockSpec((None, n_heads, hd, page), functools.partial(lambda b, pt, p: (pt[b, p], 0, 0, 0), p=p))
                  for p in range(n_pages)]
    grid_spec = pltpu.PrefetchScalarGridSpec(
        num_scalar_prefetch=1, grid=(nb,),
        in_specs=[per(q_col.shape[1:]), per(kn_col.shape[1:]), per(vn_col.shape[1:]), per(bias3.shape[1:])] + page_specs * 2,
        out_specs=per(q_col.shape[1:]),
        scratch_shapes=[pltpu.VMEM((n_pages + 1, page), F32)])
    return pl.pallas_call(
        _dsa_sample_attend_body, grid_spec=grid_spec,
        out_shape=jax.ShapeDtypeStruct(q_col.shape, F32),
        compiler_params=_params(("arbitrary",)), name="dsa_sample_attend",
    )(page_table, q_col, kn_col, vn_col, bias3, *([k_t] * n_pages), *([v_t] * n_pages))


def _swiglu(h, wg_ref, wu_ref, wd_ref, n_chunks):
    fc = wg_ref.shape[-1] // n_chunks
    y = None
    for c in range(n_chunks):
        cols = slice(fc * c, fc * (c + 1))
        gate = _dot(h, wg_ref[:, cols])
        act = (gate * jax.nn.sigmoid(gate) * _dot(h, wu_ref[:, cols])).astype(BF16)
        part = _dot(act, wd_ref[cols, :])
        y = part if y is None else y + part
    return y


def _outproj_ffn_body(x_ref, ys_ref, ya_ref, wos_ref, woa_ref, g_ref, wg_ref, wu_ref, wd_ref, o_ref):
    x = x_ref[...] + _dot(ys_ref[...], wos_ref[...]) + _dot(ya_ref[...], woa_ref[...])
    h = _rms(x, g_ref[...]).astype(BF16)
    o_ref[...] = x + _swiglu(h, wg_ref, wu_ref, wd_ref, 2)


def _outproj_ffn(x3, ys, ya, wos, woa, g, wg, wu, wd, seq, tm):
    nb, _, D = x3.shape
    nblk = seq // tm
    row = lambda w: pl.BlockSpec((tm, w), lambda b, j: (b * nblk + j, 0))
    single = lambda a: pl.BlockSpec(a.shape, lambda b, j: (0,) * a.ndim, pipeline_mode=pl.Buffered(1))
    return pl.pallas_call(
        _outproj_ffn_body, grid=(nb, nblk),
        in_specs=[pl.BlockSpec((None, tm, D), lambda b, j: (b, j, 0)), row(ys.shape[1]), row(ya.shape[1]),
                  single(wos), single(woa), single(g), single(wg), single(wu), single(wd)],
        out_specs=row(D), out_shape=jax.ShapeDtypeStruct((nb * seq, D), F32),
        compiler_params=_params(("arbitrary", "arbitrary")), name="outproj_ffn",
    )(x3, ys, ya, wos, woa, g, wg, wu, wd)


def _pool_mix(h, window_sum, divisor, pw_ref, scale):
    gd = h.shape[1] // len(POOL_WINDOWS)
    cols = []
    for g, w in enumerate(POOL_WINDOWS):
        pooled = window_sum(g, w) / divisor(w) - h[:, gd * g:gd * (g + 1)]
        cols.append(_dot(pooled.astype(BF16), pw_ref[g]))
    return jnp.concatenate(cols, axis=1) * scale


def _pool_prompt_body(x_ref, g_ref, pw_ref, sc_ref, o_ref, hist_ref, ext_ref):
    tm, D = x_ref.shape
    gd = D // len(POOL_WINDOWS)
    halo = POOL_HIST + 1
    j = pl.program_id(1)

    @pl.when(j == 0)
    def _():
        ext_ref[0:halo] = jnp.zeros((halo, D), F32)

    x = x_ref[...]
    h = _rms(x, g_ref[...])
    ext_ref[halo:] = h
    pos = (j * tm + lax.broadcasted_iota(I32, (tm, 1), 0)).astype(F32)

    def window_sum(g, w):
        acc = h[:, gd * g:gd * (g + 1)]
        for k in range(1, w):
            acc = acc + ext_ref[halo - k:halo - k + tm, gd * g:gd * (g + 1)]
        return acc

    mixed = _pool_mix(h, window_sum, lambda w: jnp.minimum(float(w), pos + 1.0), pw_ref, sc_ref[...])
    o_ref[...] = x + mixed
    ext_ref[0:halo] = ext_ref[tm:tm + halo]

    @pl.when(j == pl.num_programs(1) - 1)
    def _():
        hist_ref[...] = h[tm - POOL_HIST:, :]


def _pool_prompt(x2d, g, pw, scale, batch, seq, tm):
    T, D = x2d.shape
    nblk = seq // tm
    row = pl.BlockSpec((tm, D), lambda b, j: (b * nblk + j, 0))
    return pl.pallas_call(
        _pool_prompt_body, grid=(batch, nblk),
        in_specs=[row, _const_spec(g.shape), _const_spec(pw.shape), _const_spec(scale.shape)],
        out_specs=(row, pl.BlockSpec((None, POOL_HIST, D), lambda b, j: (b, 0, 0))),
        out_shape=(jax.ShapeDtypeStruct((T, D), F32), jax.ShapeDtypeStruct((batch, POOL_HIST, D), F32)),
        scratch_shapes=[pltpu.VMEM((tm + POOL_HIST + 1, D), F32)],
        compiler_params=_params(("arbitrary", "arbitrary")), name="pool_prompt",
    )(x2d, g, pw, scale)


def _pool_sample_body(x_ref, hist_ref, g_ref, pw_ref, sc_ref, o_ref, nh_ref):
    D = x_ref.shape[1]
    gd = D // len(POOL_WINDOWS)
    x = x_ref[...]
    h = _rms(x, g_ref[...])

    def window_sum(g, w):
        acc = h[:, gd * g:gd * (g + 1)]
        for k in range(1, w):
            acc = acc + hist_ref[POOL_HIST - k, :, gd * g:gd * (g + 1)]
        return acc

    mixed = _pool_mix(h, window_sum, float, pw_ref, sc_ref[...])
    o_ref[...] = x + mixed
    for i in range(POOL_HIST - 1):
        nh_ref[i] = hist_ref[i + 1]
    nh_ref[POOL_HIST - 1] = h


def _pool_sample(x2d, hist_t, g, pw, scale):
    sds = jax.ShapeDtypeStruct
    return pl.pallas_call(
        _pool_sample_body, out_shape=(sds(x2d.shape, F32), sds(hist_t.shape, F32)),
        compiler_params=pltpu.CompilerParams(vmem_limit_bytes=VMEM_LIMIT), name="pool_sample",
    )(x2d, hist_t, g, pw, scale)


def _store_row_tiles(ref, x):
    for s in range(ref.shape[1]):
        ref[:, s, :] = x[:, LANES * s:LANES * (s + 1)]


def _load_row_tiles(ref):
    return jnp.concatenate([ref[:, s, :] for s in range(ref.shape[1])], axis=1)


def _router_body(x_ref, g_ref, wr_ref, h_ref, sel_ref, *, n_experts):
    h = _rms(x_ref[...], g_ref[...])
    _store_row_tiles(h_ref, h)
    w = wr_ref[...]
    h_hi, w_hi = h.astype(BF16), w.astype(BF16)
    h_lo, w_lo = (h - h_hi.astype(F32)).astype(BF16), (w - w_hi.astype(F32)).astype(BF16)
    logits = _dot(h_hi, w_hi) + (_dot(h_hi, w_lo) + _dot(h_lo, w_hi))
    lane = lax.broadcasted_iota(I32, logits.shape, 1).astype(F32)
    logits = jnp.where(lane < n_experts, logits, -jnp.inf)
    v1 = jnp.max(logits, axis=1, keepdims=True)
    i1 = jnp.min(jnp.where(logits == v1, lane, float(LANES)), axis=1, keepdims=True)
    rest = jnp.where(lane == i1, -jnp.inf, logits)
    v2 = jnp.max(rest, axis=1, keepdims=True)
    i2 = jnp.min(jnp.where(rest == v2, lane, float(LANES)), axis=1, keepdims=True)
    e2 = jnp.exp(v2 - v1)
    den = 1.0 + e2
    sel_ref[...] = jnp.where(lane == 0.0, i1, jnp.where(lane == 1.0, i2, jnp.where(lane == 2.0, 1.0 / den, e2 / den)))


def _router(x2d, g, wr_pad, n_experts, tm):
    T, D = x2d.shape
    row = lambda w: pl.BlockSpec((tm, w), lambda i: (i, 0))
    body = functools.partial(_router_body, n_experts=n_experts)
    return pl.pallas_call(
        body, grid=(T // tm,),
        in_specs=[row(D), _const_spec(g.shape), _const_spec(wr_pad.shape)],
        out_specs=(pl.BlockSpec((tm, D // LANES, LANES), lambda i: (i, 0, 0)), row(LANES)),
        out_shape=(jax.ShapeDtypeStruct((T, D // LANES, LANES), F32), jax.ShapeDtypeStruct((T, LANES), F32)),
        compiler_params=_params(("arbitrary",)), name="router",
    )(x2d, g, wr_pad)


def _row_gather(idx_ref, base, src_hbm, dst, sem):
    def issue(r, carry):
        pltpu.make_async_copy(src_hbm.at[idx_ref[base + r]], dst.at[r], sem).start()
        return carry

    lax.fori_loop(0, dst.shape[0], issue, 0, unroll=8)


def _row_gather_wait(src_hbm, dst, sem):
    pltpu.make_async_copy(src_hbm.at[pl.ds(0, dst.shape[0])], dst, sem).wait()


def _moe_dispatch_body(p1_ref, p2_ref, ends_ref, h_ref, x_hbm, zero_ref, stage, sems, *, tm):
    i = pl.program_id(0)
    last = pl.num_programs(0) - 1
    gt = zero_ref.shape[0]
    n_groups = ends_ref.shape[0]
    zsem = sems.at[2]
    slot = i % 2

    def fill_tile(e, act):
        if e < n_groups:
            start = ends_ref[e] - gt
            exists = ends_ref[e] > (ends_ref[e - 1] if e else 0)
        else:
            start = ends_ref[n_groups - 1] + (e - n_groups) * gt
            exists = start < x_hbm.shape[0]

        @pl.when(exists)
        def _():
            act(pltpu.make_async_copy(zero_ref, x_hbm.at[pl.ds(pl.multiple_of(start, gt), gt)], zsem))

    @pl.when(i == 0)
    def _():
        zero_ref[...] = jnp.zeros(zero_ref.shape, F32)
        for e in range(2 * n_groups):
            fill_tile(e, lambda cp: cp.start())
        for e in range(2 * n_groups):
            fill_tile(e, lambda cp: cp.wait())

    def wait_rows(s):
        for _ in range(2):
            pltpu.make_async_copy(stage.at[s], x_hbm.at[pl.ds(0, tm)], sems.at[s]).wait()

    @pl.when(i >= 2)
    def _():
        wait_rows(slot)

    stage[slot] = h_ref[...]

    def issue(r, carry):
        t = i * tm + r
        pltpu.make_async_copy(stage.at[slot, r], x_hbm.at[p1_ref[t]], sems.at[slot]).start()
        pltpu.make_async_copy(stage.at[slot, r], x_hbm.at[p2_ref[t]], sems.at[slot]).start()
        return carry

    lax.fori_loop(0, tm, issue, 0, unroll=8)

    @pl.when(i == last)
    def _():
        wait_rows(slot)

    @pl.when((i == last) & (i >= 1))
    def _():
        wait_rows(1 - slot)


def _moe_dispatch(pos1, pos2, ends, h_tiles, n_rows, tm, group_tile):
    T, S, _ = h_tiles.shape
    grid_spec = pltpu.PrefetchScalarGridSpec(
        num_scalar_prefetch=3, grid=(T // tm,),
        in_specs=[pl.BlockSpec((tm, S, LANES), lambda i, p1, p2, en: (i, 0, 0))],
        out_specs=pl.BlockSpec(memory_space=pl.ANY),
        scratch_shapes=[pltpu.VMEM((group_tile, S, LANES), F32), pltpu.VMEM((2, tm, S, LANES), F32),
                        pltpu.SemaphoreType.DMA((3,))])
    return pl.pallas_call(
        functools.partial(_moe_dispatch_body, tm=tm), grid_spec=grid_spec,
        out_shape=jax.ShapeDtypeStruct((n_rows, S, LANES), F32),
        compiler_params=pltpu.CompilerParams(dimension_semantics=("arbitrary",), vmem_limit_bytes=VMEM_LIMIT,
                                             disable_bounds_checks=True),
        name="moe_dispatch",
    )(pos1, pos2, ends, h_tiles)


def _moe_group_body(te_ref, nu_ref, x_ref, wg_ref, wu_ref, wd_ref, y_ref):
    del te_ref
    i = pl.program_id(0)

    @pl.when(i < nu_ref[0])
    def _():
        _store_row_tiles(y_ref, _swiglu(_load_row_tiles(x_ref).astype(BF16), wg_ref, wu_ref, wd_ref, 2))

    @pl.when(i >= nu_ref[0])
    def _():
        y_ref[...] = jnp.zeros(y_ref.shape, F32)


def _moe_group(tile_expert, n_used, x_tiles, wg, wu, wd, tm):
    P, S, _ = x_tiles.shape
    _, D, F = wg.shape
    wspec = lambda a, b: pl.BlockSpec((None, a, b), lambda i, te, nu: (te[i], 0, 0), pipeline_mode=pl.Buffered(1))
    grid_spec = pltpu.PrefetchScalarGridSpec(
        num_scalar_prefetch=2, grid=(P // tm,),
        in_specs=[pl.BlockSpec((tm, S, LANES), lambda i, te, nu: (jnp.minimum(i, nu[0] - 1), 0, 0)),
                  wspec(D, F), wspec(D, F), wspec(F, D)],
        out_specs=pl.BlockSpec((tm, S, LANES), lambda i, te, nu: (i, 0, 0)))
    return pl.pallas_call(
        _moe_group_body, grid_spec=grid_spec, out_shape=jax.ShapeDtypeStruct((P, S, LANES), F32),
        compiler_params=_params(("arbitrary",)), name="moe_group",
    )(tile_expert, n_used, x_tiles, wg, wu, wd)


def _moe_combine_body(p1_ref, p2_ref, dst_ref, x_ref, sel_ref, y_hbm, ya_hbm, yb_hbm, buf, obuf, sem, osem):
    i = pl.program_id(0)
    last = pl.num_programs(0) - 1
    tm = x_ref.shape[0]
    n_chunks = tm // OUT_CHUNK
    rows_a = ya_hbm.shape[0]
    slot = i % 2

    def out_copies(step, s, act):
        for c in range(n_chunks):
            dst = dst_ref[step * n_chunks + c]
            src = obuf.at[s, pl.ds(c * OUT_CHUNK, OUT_CHUNK)]

            @pl.when((dst >= 0) & (dst < rows_a))
            def _():
                act(pltpu.make_async_copy(src, ya_hbm.at[pl.ds(pl.multiple_of(dst, OUT_CHUNK), OUT_CHUNK)], osem.at[s]))

            @pl.when(dst >= rows_a)
            def _():
                off = pl.multiple_of(dst - rows_a, OUT_CHUNK)
                act(pltpu.make_async_copy(src, yb_hbm.at[pl.ds(off, OUT_CHUNK)], osem.at[s]))

    def gather(tile, s):
        _row_gather(p1_ref, tile * tm, y_hbm, buf.at[s, 0], sem.at[s, 0])
        _row_gather(p2_ref, tile * tm, y_hbm, buf.at[s, 1], sem.at[s, 1])

    @pl.when(i == 0)
    def _():
        gather(0, 0)

    @pl.when(i + 1 < pl.num_programs(0))
    def _():
        gather(i + 1, 1 - slot)

    _row_gather_wait(y_hbm, buf.at[slot, 0], sem.at[slot, 0])
    _row_gather_wait(y_hbm, buf.at[slot, 1], sem.at[slot, 1])
    @pl.when(i >= 2)
    def _():
        out_copies(i - 2, slot, lambda cp: cp.wait())

    sel = sel_ref[...]
    obuf[slot] = (x_ref[...] + sel[:, 2:3] * _load_row_tiles(buf.at[slot, 0])
                  + sel[:, 3:4] * _load_row_tiles(buf.at[slot, 1]))
    out_copies(i, slot, lambda cp: cp.start())

    @pl.when(i == last)
    def _():
        out_copies(i, slot, lambda cp: cp.wait())

    @pl.when((i == last) & (i >= 1))
    def _():
        out_copies(i - 1, 1 - slot, lambda cp: cp.wait())


def _moe_combine(pos1, pos2, chunk_dst, x2d, sel, y_tiles, tm, rows_a, rows_b):
    T, D = x2d.shape
    S = y_tiles.shape[1]
    assert tm % OUT_CHUNK == 0 and rows_a % OUT_CHUNK == 0 and rows_b % OUT_CHUNK == 0
    row = lambda w: pl.BlockSpec((tm, w), lambda i, p1, p2, cd: (i, 0))
    hbm = pl.BlockSpec(memory_space=pl.ANY)
    grid_spec = pltpu.PrefetchScalarGridSpec(
        num_scalar_prefetch=3, grid=(T // tm,),
        in_specs=[row(D), row(LANES), hbm], out_specs=(hbm, hbm),
        scratch_shapes=[pltpu.VMEM((2, 2, tm, S, LANES), F32), pltpu.VMEM((2, tm, D), F32),
                        pltpu.SemaphoreType.DMA((2, 2)), pltpu.SemaphoreType.DMA((2,))])
    return pl.pallas_call(
        _moe_combine_body, grid_spec=grid_spec,
        out_shape=(jax.ShapeDtypeStruct((rows_a, D), F32), jax.ShapeDtypeStruct((rows_b, D), F32)),
        compiler_params=pltpu.CompilerParams(dimension_semantics=("arbitrary",), vmem_limit_bytes=VMEM_LIMIT,
                                             disable_bounds_checks=True),
        name="moe_combine",
    )(pos1, pos2, chunk_dst, x2d, sel, y_tiles)


def _moe_plan(sel, n_experts, tm):
    T = sel.shape[0]
    experts = jnp.concatenate([sel[:, 0], sel[:, 1]]).astype(I32)
    onehot = (experts[:, None] == jnp.arange(n_experts, dtype=I32)[None, :]).astype(I32)
    rank = jnp.cumsum(onehot, axis=0) - onehot
    counts = jnp.sum(onehot, axis=0)
    padded = (counts + tm - 1) // tm * tm
    ends = jnp.cumsum(padded)
    pos = jnp.sum(onehot * ((ends - padded)[None, :] + rank), axis=1)
    n_rows = (2 * T + n_experts * (tm - 1)) // tm * tm
    tile_start = jnp.arange(n_rows // tm, dtype=I32) * tm
    tile_expert = jnp.minimum(jnp.sum((tile_start[:, None] >= ends[None, :]).astype(I32), axis=1), n_experts - 1)
    n_used = (ends[-1] // tm).astype(I32).reshape(1)
    return tile_expert, n_used, ends.astype(I32), n_rows, pos[:T], pos[T:]


def _rope_tables(pos):
    half = HEAD_DIM // 2
    inv = ROPE_THETA ** (-jnp.arange(half, dtype=F32) / half)
    ang = pos[:, None] * inv[None, :]
    cos, sin = jnp.cos(ang), jnp.sin(ang)
    reps = LANES // HEAD_DIM
    return jnp.tile(jnp.concatenate([cos, cos], axis=1), (1, reps)), jnp.tile(jnp.concatenate([-sin, sin], axis=1), (1, reps))


def kernel(x_prompt, x_sample, cache_k, cache_v, cache_kidx, page_table, state_ssm_re, state_ssm_im, state_pool, meta_tokens, norm_mix0, w_in0, q_norm, k_norm, ssm_lambda_re, ssm_lambda_im, ssm_log_dt, ssm_b_re, ssm_b_im, ssm_c_re, ssm_c_im, ssm_d, ssm_w_glu, w_out0, norm_ffn0, ffn_w_gate, ffn_w_up, ffn_w_down, norm_mix1, pool_w, pool_scale, norm_ffn1, router_w, moe_w_gate, moe_w_up, moe_w_down):
    B, S, D = x_prompt.shape
    DB, DS, _ = x_sample.shape
    assert DS == 1, "one new token per sample sequence"
    L = S + N_META
    n_pool, page = cache_k.shape[1], cache_k.shape[2]
    n_pages = page_table.shape[1]
    past = n_pages * page
    d_att = N_HEADS * HEAD_DIM
    d_qi = N_IDX_HEADS * IDX_DIM
    G, P = ssm_lambda_re.shape[1:]
    d_ssm = G * SSM_GROUP
    n_state = G * P
    E = router_w.shape[-1]
    topk_p = min(TOPK_MAX, S // 4)
    topk_s = min(TOPK_MAX, (past + DS) // 4)
    tm = _row_block(L, 1024)
    assert tm >= POOL_HIST + 1

    tm_in = LANES * (-(-L // (3 * LANES)))
    l_pad = -(-L // tm_in) * tm_in
    x_p = jnp.concatenate([jnp.broadcast_to(meta_tokens[None], (B, N_META, D)), x_prompt,
                           jnp.zeros((B, l_pad - L, D), x_prompt.dtype)], axis=1)
    x_s = x_sample.reshape(1, DB, D)

    w_in = w_in0[0]
    o1, o2, o3, o4, o5, o6 = d_ssm, d_ssm + d_att, d_ssm + 2 * d_att, d_ssm + 3 * d_att, d_ssm + 3 * d_att + d_qi, d_ssm + 3 * d_att + d_qi + IDX_DIM
    w_kw = jnp.pad(w_in[:, o5:], ((0, 0), (0, LANES - (w_in.shape[1] - o5))))
    ws = tuple(w.astype(BF16) for w in (w_in[:, :o1], w_in[:, o1:o2], w_in[:, o2:o3], w_in[:, o3:o4], w_in[:, o4:o5], w_kw))
    g_mix0 = norm_mix0[0].reshape(1, D)
    qn = jnp.tile(q_norm[0], N_HEADS).reshape(1, d_att)
    kn = jnp.tile(k_norm[0], N_HEADS).reshape(1, d_att)
    head_of = jnp.arange(d_att) // HEAD_DIM
    hm = jnp.where(head_of[:, None] == head_of[None, :], 1.0 / HEAD_DIM, 0.0).astype(BF16)
    cos_p, sin_p = _rope_tables(jnp.arange(l_pad, dtype=F32))
    cos_s, sin_s = _rope_tables(jnp.full((DB,), float(past), F32))

    flat = lambda a: a.reshape(-1, a.shape[-1])
    u_p, q_p, kt_p, vt_p, qi_p, kit_p, kw_p, kb_p, vb_p, kib_p = _inproj(
        x_p, cos_p, sin_p, tm_in, L, g_mix0, ws, qn, kn, hm, True)
    u_p, q_p, qi_p, kw_p, kb_p, vb_p, kib_p = (flat(a) for a in (u_p, q_p, qi_p, kw_p, kb_p, vb_p, kib_p))
    u_s, q_s, k_s, v_s, qi_s, ki_s, kw_s, _, _, _ = (
        flat(a) for a in _inproj(x_s, cos_s, sin_s, DB, DB, g_mix0, ws, qn, kn, hm, False))

    lbr, lbi, bbr, bbi = _s5_prep(ssm_lambda_re[0], ssm_lambda_im[0], ssm_log_dt[0], ssm_b_re[0], ssm_b_im[0])
    gps = LANES // SSM_GROUP
    n_slab = G // gps
    eye = jnp.eye(gps, dtype=F32)

    def b_slabs(bt):
        return jnp.einsum('sgcp,gh->sgchp', bt.reshape(n_slab, gps, SSM_GROUP, P), eye).reshape(n_slab, LANES, gps * P)

    def c_slabs(c):
        return jnp.einsum('sgcp,gh->sgphc', c.reshape(n_slab, gps, SSM_GROUP, P), eye).reshape(n_slab, gps * P, LANES)

    wb = jnp.concatenate([b_slabs(bbr), b_slabs(bbi)], axis=2)
    wc = jnp.stack([c_slabs(ssm_c_re[0]), -c_slabs(ssm_c_im[0])], axis=1).astype(BF16)
    lbr_f, lbi_f = lbr.reshape(1, n_state), lbi.reshape(1, n_state)
    d_skip = ssm_d[0].reshape(1, d_ssm)
    wglu = ssm_w_glu[0].astype(BF16)
    assert B % SUBLANES == 0, "the S5 scan keeps one batch row per sublane"
    steps = max(d for d in range(1, L + 1) if L % d == 0 and d * B <= 512 and (d * B) % (2 * SUBLANES) == 0)
    u_tm = u_p.reshape(B, L, d_ssm).swapaxes(0, 1).reshape(L * B, d_ssm)
    zero_h = jnp.zeros((B, n_state), F32)
    ys_tm, hre_p, him_p = _s5(u_tm, zero_h, zero_h, lbr_f, lbi_f, wb, wc, d_skip, wglu, B, steps, False)
    ys_p = ys_tm.reshape(L, B, d_ssm).swapaxes(0, 1).reshape(B * L, d_ssm)
    ys_s, hre_s, him_s = _s5(u_s, state_ssm_re[0].reshape(DB, n_state), state_ssm_im[0].reshape(DB, n_state),
                             lbr_f, lbi_f, wb, wc, d_skip, wglu, DB, 1, True)

    ya_p = _dsa_prompt(qi_p, kw_p, q_p, kib_p, kb_p, vb_p, B, L, tm, topk_p)
    assert page == LANES
    w3 = kw_s[:, IDX_DIM:IDX_DIM + N_IDX_HEADS].reshape(DB, N_IDX_HEADS, 1)
    scores = _dsa_sample_scores(page_table, qi_s.astype(F32).reshape(DB, d_qi, 1), w3, ki_s.reshape(DB, IDX_DIM, 1),
                                jnp.transpose(cache_kidx[0], (0, 2, 1)))
    bias = _dsa_sample_select(scores.reshape(DB, (n_pages + 1) * page), past + DS, topk_s)
    ya_s = _dsa_sample_attend(page_table, q_s.astype(F32).reshape(DB, d_att, 1), k_s.reshape(DB, d_att, 1),
                              v_s.reshape(DB, d_att, 1), bias.reshape(DB, n_pages + 1, page),
                              jnp.transpose(cache_k[0], (0, 2, 3, 1)), jnp.transpose(cache_v[0], (0, 2, 3, 1)))
    ya_s = ya_s.reshape(DB, d_att).astype(BF16)

    w_out = w_out0[0].astype(BF16)
    ffn = (w_out[:d_ssm], w_out[d_ssm:], norm_ffn0[0].reshape(1, D), ffn_w_gate[0].astype(BF16),
           ffn_w_up[0].astype(BF16), ffn_w_down[0].astype(BF16))
    x_p = _outproj_ffn(x_p, ys_p, ya_p, *ffn, L, tm)
    x_s = _outproj_ffn(x_s, ys_s, ya_s, *ffn, DB, DB)

    g_mix1 = norm_mix1[0].reshape(1, D)
    pw = pool_w[0].astype(BF16)
    psc = pool_scale[0].reshape(1, D)
    T_all = B * L + DB
    x_p, pool_p = _pool_prompt(x_p, g_mix1, pw, psc, B, L, tm)
    x_s, hist_t = _pool_sample(x_s, state_pool[0].swapaxes(0, 1), g_mix1, pw, psc)
    pool_s = hist_t.swapaxes(0, 1)
    x_all = jnp.concatenate([x_p, x_s], axis=0)

    g_ffn1 = norm_ffn1[0].reshape(1, D)
    wr_pad = jnp.pad(router_w[0], ((0, 0), (0, LANES - E)))
    wg, wu, wd = moe_w_gate[0].astype(BF16), moe_w_up[0].astype(BF16), moe_w_down[0].astype(BF16)
    tr = _row_block(T_all, MOE_TILE)
    h_tiles, sel = _router(x_all, g_ffn1, wr_pad, E, tr)
    tile_expert, n_used, ends, n_rows, pos1, pos2 = _moe_plan(sel, E, MOE_TILE)
    x_tiles = _moe_dispatch(pos1, pos2, ends, h_tiles, n_rows, tr, MOE_TILE)
    y_tiles = _moe_group(tile_expert, n_used, x_tiles, wg, wu, wd, MOE_TILE)
    assert N_META % OUT_CHUNK == 0 and L % OUT_CHUNK == 0 and DB % OUT_CHUNK == 0
    row0 = jnp.arange(T_all // OUT_CHUNK, dtype=I32) * OUT_CHUNK
    in_seq = row0 % L
    chunk_dst = jnp.where(row0 >= B * L, row0 - B * N_META,
                          jnp.where(in_seq < N_META, -1, (row0 // L) * S + in_seq - N_META))
    y_prompt, y_sample = _moe_combine(pos1, pos2, chunk_dst, x_all, sel, y_tiles, tr, B * S, DB)
    y_prompt = y_prompt.reshape(B, S, D)
    y_sample = y_sample.reshape(DB, 1, D)
    k_prompt = jnp.transpose(kt_p, (0, 3, 1, 2))[None]
    v_prompt = jnp.transpose(vt_p, (0, 3, 1, 2))[None]
    kidx_prompt = jnp.transpose(kit_p, (0, 2, 1))[None]
    return (y_prompt, y_sample, k_prompt, v_prompt, kidx_prompt,
            hre_p.reshape(1, B, G, P), him_p.reshape(1, B, G, P), pool_p[None],
            k_s.reshape(1, DB, 1, N_HEADS, HEAD_DIM), v_s.reshape(1, DB, 1, N_HEADS, HEAD_DIM),
            ki_s.reshape(1, DB, 1, IDX_DIM), hre_s.reshape(1, DB, G, P), him_s.reshape(1, DB, G, P), pool_s[None])
```

```python
import functools
import math

import jax
import jax.numpy as jnp
from jax import lax
from jax.experimental import pallas as pl
from jax.experimental.pallas import tpu as pltpu

F32 = jnp.float32
BF16 = jnp.bfloat16
I32 = jnp.int32

N_META = 16
SSM_GROUP = 16
SSM_STATE = 64
N_HEADS = 8
HEAD_DIM = 64
N_IDX_HEADS = 8
IDX_DIM = 64
TOPK_MAX = 256
ROPE_THETA = 10000.0
POOL_WINDOWS = (2, 4, 8, 16)
POOL_HIST = max(POOL_WINDOWS) - 1
TOP_K_EXPERTS = 2
EPS = 1e-6

LANES = 128
SUBLANES = 8
VMEM_LIMIT = 56 * 1024 * 1024
OUT_CHUNK = 2 * SUBLANES
MOE_TILE = 256
BISECT_ITERS = 16
NEG_BIG = -1e30


def _params(sem):
    return pltpu.CompilerParams(dimension_semantics=sem, vmem_limit_bytes=VMEM_LIMIT)


def _row_block(n, cap, mult=2 * SUBLANES):
    best = None
    for d in range(mult, min(n, cap) + 1, mult):
        if n % d == 0:
            best = d
    assert best is not None, n
    return best


def _rms(x, g):
    return x * lax.rsqrt(jnp.mean(x * x, axis=-1, keepdims=True) + EPS) * g


def _dot(a, b):
    return jnp.dot(a, b, preferred_element_type=F32)


def _dot_nt(a, b):
    return lax.dot_general(a, b, (((1,), (1,)), ((), ())), preferred_element_type=F32)


def _const_spec(shape):
    nd = len(shape)
    return pl.BlockSpec(shape, lambda *_: (0,) * nd)


def _inproj_body(x_ref, g_ref, wu_ref, wq_ref, wk_ref, wv_ref, wqi_ref, wkw_ref, qn_ref, kn_ref, hm_ref,
                 cos_ref, sin_ref,
                 u_ref, q_ref, k_ref, v_ref, qi_ref, ki_ref, kw_ref, kb_ref, vb_ref, kib_ref, *, transposed):
    def store_kv(ref, z):
        if not transposed:
            ref[...] = z
            return
        per_slab = LANES // HEAD_DIM
        for sp in range(z.shape[1] // LANES):
            zt = z[:, LANES * sp:LANES * (sp + 1)].T
            ref[per_slab * sp:per_slab * (sp + 1)] = zt.reshape(per_slab, HEAD_DIM, z.shape[0])

    h = _rms(x_ref[...], g_ref[...]).astype(BF16)
    cos = cos_ref[...]
    sin = sin_ref[...]
    lane = lax.broadcasted_iota(I32, (1, LANES), 1)
    lo_half = (lane & (HEAD_DIM // 2)) == 0

    def rope(z):
        cols = []
        for j in range(z.shape[1] // LANES):
            zs = z[:, LANES * j:LANES * (j + 1)]
            partner = jnp.where(lo_half, pltpu.roll(zs, LANES - HEAD_DIM // 2, 1), pltpu.roll(zs, HEAD_DIM // 2, 1))
            cols.append(zs * cos + partner * sin)
        return cols[0] if len(cols) == 1 else jnp.concatenate(cols, axis=1)

    def head_norm(z, gn):
        ms = _dot((z * z).astype(BF16), hm_ref[...])
        return z * lax.rsqrt(ms + EPS) * gn

    u_ref[...] = _dot(h, wu_ref[...])
    q = rope(head_norm(_dot(h, wq_ref[...]), qn_ref[...]))
    q_ref[...] = (q * (HEAD_DIM ** -0.5)).astype(BF16)
    k = rope(head_norm(_dot(h, wk_ref[...]), kn_ref[...]))
    store_kv(k_ref, k)
    kb_ref[...] = k.astype(BF16)
    v = _dot(h, wv_ref[...])
    store_kv(v_ref, v)
    vb_ref[...] = v.astype(BF16)
    qi_ref[...] = (rope(_dot(h, wqi_ref[...])) * (IDX_DIM ** -0.5)).astype(BF16)
    zkw = _dot(h, wkw_ref[...])
    kir = rope(zkw)
    kw_ref[...] = jnp.where(lane < IDX_DIM, kir, zkw * (N_IDX_HEADS ** -0.5))
    ki_ref[...] = kir.T[:IDX_DIM] if transposed else kir[:, :IDX_DIM]
    kib_ref[...] = jnp.where(lane < IDX_DIM, kir, pltpu.roll(kir, IDX_DIM, 1)).astype(BF16)


def _inproj(x3, cos_t, sin_t, tm, out_rows, g, ws, qn, kn, hm, transposed):
    nb, rows, D = x3.shape
    wu, wq, wk, wv, wqi, wkw = ws
    d_ssm, d_att, d_qi = wu.shape[1], wq.shape[1], wqi.shape[1]
    row = lambda w: pl.BlockSpec((None, tm, w), lambda b, j: (b, j, 0))
    tab = pl.BlockSpec((tm, LANES), lambda b, j: (j, 0))
    sds = lambda w, dt: jax.ShapeDtypeStruct((nb, out_rows, w), dt)
    if transposed:
        kv_shape = jax.ShapeDtypeStruct((nb, N_HEADS, HEAD_DIM, out_rows), F32)
        kv_spec = pl.BlockSpec((None, N_HEADS, HEAD_DIM, tm), lambda b, j: (b, 0, 0, j))
        ki_shape = jax.ShapeDtypeStruct((nb, IDX_DIM, out_rows), F32)
        ki_spec = pl.BlockSpec((None, IDX_DIM, tm), lambda b, j: (b, 0, j))
    else:
        kv_shape, kv_spec, ki_shape, ki_spec = sds(d_att, F32), row(d_att), sds(IDX_DIM, F32), row(IDX_DIM)
    out_shape = (
        sds(d_ssm, F32),
        sds(d_att, BF16),
        kv_shape,
        kv_shape,
        sds(d_qi, BF16),
        ki_shape,
        sds(LANES, F32),
        sds(d_att, BF16),
        sds(d_att, BF16),
        sds(LANES, BF16),
    )
    out_specs = (row(d_ssm), row(d_att), kv_spec, kv_spec, row(d_qi), ki_spec, row(LANES),
                 row(d_att), row(d_att), row(LANES))
    in_specs = [row(D), _const_spec(g.shape)] + [_const_spec(w.shape) for w in ws] + [
        _const_spec(qn.shape), _const_spec(kn.shape), _const_spec(hm.shape), tab, tab]
    return pl.pallas_call(
        functools.partial(_inproj_body, transposed=transposed), grid=(nb, rows // tm),
        in_specs=in_specs, out_specs=out_specs, out_shape=out_shape,
        compiler_params=_params(("arbitrary", "arbitrary")), name="inproj_t" if transposed else "inproj",
    )(x3, g, *ws, qn, kn, hm, cos_t, sin_t)


def _s5_prep_body(lr_ref, li_ref, ldt_ref, btr_ref, bti_ref, lbr_ref, lbi_ref, bbr_ref, bbi_ref):
    lr = lr_ref[...]
    li = li_ref[...]
    dt = jnp.exp(ldt_ref[...])
    mag = jnp.exp(lr * dt)
    lbr = mag * jnp.cos(li * dt)
    lbi = mag * jnp.sin(li * dt)
    lbr_ref[...] = lbr
    lbi_ref[...] = lbi
    den = lr * lr + li * li
    cr = ((lbr - 1.0) * lr + lbi * li) / den
    ci = (lbi * lr - (lbr - 1.0) * li) / den
    btr = btr_ref[...]
    bti = bti_ref[...]
    bbr_ref[...] = cr * btr - ci * bti
    bbi_ref[...] = cr * bti + ci * btr


def _s5_prep(lam_re, lam_im, log_dt, b_re, b_im):
    G, P = lam_re.shape
    C = b_re.shape[-1]
    btr = jnp.swapaxes(b_re, 1, 2)
    bti = jnp.swapaxes(b_im, 1, 2)
    sds = jax.ShapeDtypeStruct
    return pl.pallas_call(
        _s5_prep_body,
        out_shape=(sds((G, 1, P), F32), sds((G, 1, P), F32), sds((G, C, P), F32), sds((G, C, P), F32)),
        name="s5_prep",
    )(lam_re.reshape(G, 1, P), lam_im.reshape(G, 1, P), log_dt.reshape(G, 1, 1), btr, bti)


def _s5_body(u_ref, h0r_ref, h0i_ref, lbr_ref, lbi_ref, wb_ref, wc_ref, d_ref, wglu_ref,
             y_ref, hr_ref, hi_ref, bu_ref, st_ref, *, nb, steps, precise):
    n_slab = wb_ref.shape[0]
    cin = wb_ref.shape[1]
    sw = wb_ref.shape[2] // 2
    n_state = n_slab * sw

    @pl.when(pl.program_id(0) == 0)
    def _():
        st_ref[0] = h0r_ref[...]
        st_ref[1] = h0i_ref[...]

    u = u_ref[...]
    for i in range(n_slab):
        ui = u[:, cin * i:cin * (i + 1)]
        if precise:
            bu = jnp.dot(ui, wb_ref[i], preferred_element_type=F32, precision=lax.Precision.HIGHEST)
        else:
            bu = _dot(ui.astype(BF16), wb_ref[i].astype(BF16))
        bu_ref[:, sw * i:sw * (i + 1)] = bu[:, :sw]
        bu_ref[:, n_state + sw * i:n_state + sw * (i + 1)] = bu[:, sw:]

    for i in range(n_slab):
        re_cols = slice(sw * i, sw * (i + 1))
        im_cols = slice(n_state + sw * i, n_state + sw * (i + 1))
        lr = jnp.broadcast_to(lbr_ref[:, re_cols], (nb, sw))
        li = jnp.broadcast_to(lbi_ref[:, re_cols], (nb, sw))

        def step(t, carry, re_cols=re_cols, im_cols=im_cols, lr=lr, li=li):
            hr, hi = carry
            rows = pl.ds(pl.multiple_of(t * nb, nb), nb)
            nhr = lr * hr - li * hi + bu_ref[rows, re_cols]
            nhi = lr * hi + li * hr + bu_ref[rows, im_cols]
            bu_ref[rows, re_cols] = nhr
            bu_ref[rows, im_cols] = nhi
            return nhr, nhi

        hr, hi = lax.fori_loop(0, steps, step, (st_ref[0, :, re_cols], st_ref[1, :, re_cols]))
        st_ref[0, :, re_cols] = hr
        st_ref[1, :, re_cols] = hi

    ys = []
    for i in range(n_slab):
        h_re = bu_ref[:, sw * i:sw * (i + 1)].astype(BF16)
        h_im = bu_ref[:, n_state + sw * i:n_state + sw * (i + 1)].astype(BF16)
        ys.append(_dot(h_re, wc_ref[i, 0]) + _dot(h_im, wc_ref[i, 1]))
    y = jnp.concatenate(ys, axis=1) + d_ref[...] * u
    y = 0.5 * y * (1.0 + jnp.tanh(math.sqrt(2.0 / math.pi) * (y + 0.044715 * (y * y * y))))
    y = y * jax.nn.sigmoid(_dot(y.astype(BF16), wglu_ref[...]))
    y_ref[...] = y.astype(BF16)
    hr_ref[...] = st_ref[0]
    hi_ref[...] = st_ref[1]


def _s5(u_tm, h0_re, h0_im, lbr_flat, lbi_flat, wb, wc, d_skip, wglu, nb, steps, precise):
    rows_total, d_ssm = u_tm.shape
    n_state = h0_re.shape[1]
    rows = nb * steps
    sds = jax.ShapeDtypeStruct
    body = functools.partial(_s5_body, nb=nb, steps=steps, precise=precise)
    return pl.pallas_call(
        body, grid=(rows_total // rows,),
        in_specs=[pl.BlockSpec((rows, d_ssm), lambda c: (c, 0)), _const_spec(h0_re.shape), _const_spec(h0_im.shape),
                  _const_spec(lbr_flat.shape), _const_spec(lbi_flat.shape), _const_spec(wb.shape),
                  _const_spec(wc.shape), _const_spec(d_skip.shape), _const_spec(wglu.shape)],
        out_specs=(pl.BlockSpec((rows, d_ssm), lambda c: (c, 0)), _const_spec(h0_re.shape), _const_spec(h0_im.shape)),
        out_shape=(sds((rows_total, d_ssm), BF16), sds(h0_re.shape, F32), sds(h0_im.shape, F32)),
        scratch_shapes=[pltpu.VMEM((rows, 2 * n_state), F32), pltpu.VMEM((2, nb, n_state), F32)],
        compiler_params=_params(("arbitrary",)), name="s5_precise" if precise else "s5",
    )(u_tm, h0_re, h0_im, lbr_flat, lbi_flat, wb, wc, d_skip, wglu)


def _topk_bias(s_ref, n, kk):
    rows = s_ref.shape[0]
    cols = slice(0, n)

    def count(pred):
        return jnp.sum(jnp.where(pred(s_ref[:, cols]), 1.0, 0.0), axis=1, keepdims=True)

    def largest_below(hi):
        s = s_ref[:, cols]
        return jnp.max(jnp.where(s < hi, s, -jnp.inf), axis=1, keepdims=True)

    s0 = s_ref[:, cols]
    smax = jnp.max(s0, axis=1, keepdims=True)
    smin = jnp.min(jnp.where(s0 > -jnp.inf, s0, jnp.inf), axis=1, keepdims=True)

    def bisect(_, c):
        lo, hi = c
        mid = lo + (hi - lo) * 0.5
        ge = count(lambda s: s >= mid) >= kk
        return jnp.where(ge, mid, lo), jnp.where(ge, hi, mid)

    _, hi = lax.fori_loop(0, BISECT_ITERS, bisect, (smin, smax + (smax - smin) + 1.0))
    thr = largest_below(hi)
    cnt = count(lambda s: s >= thr)

    def short(c):
        return jnp.min(c[2] - kk) < 0.0

    def lower(c):
        hi, thr, cnt = c
        hi = jnp.where(cnt < kk, thr, hi)
        thr = largest_below(hi)
        return hi, thr, count(lambda s: s >= thr)

    _, thr, _ = lax.while_loop(short, lower, (hi, thr, cnt))
    need = kk - count(lambda s: s > thr)
    blk = 2 * LANES
    tri = (lax.broadcasted_iota(I32, (blk, blk), 0) <= lax.broadcasted_iota(I32, (blk, blk), 1)).astype(BF16)
    seen = jnp.zeros((rows, 1), F32)
    for c0 in range(0, n, blk):
        w = min(blk, n - c0)
        s = s_ref[:, c0:c0 + w]
        tie = s == thr
        upto = seen + _dot(jnp.where(tie, 1.0, 0.0).astype(BF16), tri[:w, :w])
        seen = upto[:, w - 1:w]
        s_ref[:, c0:c0 + w] = jnp.where((s > thr) | (tie & (upto <= need)), 0.0, NEG_BIG)


def _dsa_prompt_block(q0, rows, n, qim_ref, qm_ref, w_ref, kip_ref, kp_ref, vp_ref, s_ref, acc_ref, topk):
    tq = rows.stop - rows.start
    s_ref = s_ref.at[0:tq]
    cols = slice(0, n)
    lane = lax.broadcasted_iota(I32, (1, LANES), 1)
    s_ref[:, cols] = jnp.zeros((tq, n), F32)

    def score_head(h, carry):
        d = jnp.maximum(_dot_nt(qim_ref[h, rows, :], kip_ref[cols, :]), 0.0)
        s_ref[:, cols] += w_ref[h, rows, :] * d
        return carry

    lax.fori_loop(0, N_IDX_HEADS, score_head, 0)
    qpos = q0 + lax.broadcasted_iota(I32, (tq, 1), 0)
    kpos = lax.broadcasted_iota(I32, (1, n), 1)
    s_ref[:, cols] = jnp.where(kpos <= qpos, s_ref[:, cols], -jnp.inf)
    _topk_bias(s_ref, n, jnp.minimum(qpos + 1, topk).astype(F32))

    def head_pair(sp, carry):
        outs = []
        for hh in range(2):
            lg = _dot_nt(qm_ref[2 * sp + hh, rows, :], kp_ref[sp, cols, :]) + s_ref[:, cols]
            p = jnp.exp(lg - jnp.max(lg, axis=1, keepdims=True))
            pv = _dot(p.astype(BF16), vp_ref[sp, cols, :])
            outs.append(pv[:, :LANES] / pv[:, LANES:LANES + 1])
        acc_ref[sp, rows, :] = jnp.where(lane < HEAD_DIM, outs[0], outs[1])
        return carry

    lax.fori_loop(0, acc_ref.shape[0], head_pair, 0)


def _dsa_prompt_body(qi_ref, kw_ref, q_ref, kib_ref, kb_ref, vb_ref, o_ref,
                     kip_ref, kp_ref, vp_ref, qim_ref, qm_ref, w_ref, s_ref, acc_ref, *, seq, topk):
    tq = q_ref.shape[0]
    n_slab, lp, _ = kp_ref.shape
    j = pl.program_id(1)
    lane = lax.broadcasted_iota(I32, (1, LANES), 1)

    @pl.when(j == 0)
    def _():
        kip_ref[0:seq] = kib_ref[...]
        if lp > seq:
            kip_ref[seq:] = jnp.zeros((lp - seq, LANES), BF16)
        for sp in range(n_slab):
            sl = slice(LANES * sp, LANES * (sp + 1))
            kp_ref[sp, 0:seq] = kb_ref[:, sl]
            vp_ref[sp, 0:seq, 0:LANES] = vb_ref[:, sl]
            vp_ref[sp, :, LANES:] = jnp.where(lane == 0, 1.0, 0.0).astype(BF16) + jnp.zeros((lp, LANES), BF16)
            if lp > seq:
                kp_ref[sp, seq:] = jnp.zeros((lp - seq, LANES), BF16)
                vp_ref[sp, seq:, 0:LANES] = jnp.zeros((lp - seq, LANES), BF16)

    kw = kw_ref[...]
    for h in range(N_IDX_HEADS):
        head_lanes = (lane // IDX_DIM) == (h % 2)
        sl = slice(LANES * (h // 2), LANES * (h // 2 + 1))
        qim_ref[h] = jnp.where(head_lanes, qi_ref[:, sl], jnp.zeros((), BF16))
        qm_ref[h] = jnp.where(head_lanes, q_ref[:, sl], jnp.zeros((), BF16))
        w_ref[h] = kw[:, IDX_DIM + h:IDX_DIM + h + 1]

    split = tq // 2 // (2 * SUBLANES) * (2 * SUBLANES)
    for jj in range(seq // tq):
        @pl.when(j == jj)
        def _(jj=jj):
            for rows in (slice(0, split), slice(split, tq)):
                n = min(-(-(jj * tq + rows.stop) // LANES) * LANES, lp)
                _dsa_prompt_block(jj * tq + rows.start, rows, n, qim_ref, qm_ref, w_ref, kip_ref, kp_ref, vp_ref,
                                  s_ref, acc_ref, topk)

    for sp in range(n_slab):
        o_ref[:, LANES * sp:LANES * (sp + 1)] = acc_ref[sp].astype(BF16)


def _dsa_prompt(qi, kw, q, kib, kb, vb, batch, seq, tq, topk):
    T, d_att = q.shape
    nq = seq // tq
    lp = -(-seq // LANES) * LANES
    n_slab = d_att // LANES
    qrow = lambda w: pl.BlockSpec((tq, w), lambda b, j: (b * nq + j, 0))
    full = lambda w: pl.BlockSpec((None, seq, w), lambda b, j: (b, 0, 0))
    body = functools.partial(_dsa_prompt_body, seq=seq, topk=topk)
    return pl.pallas_call(
        body, grid=(batch, nq),
        in_specs=[qrow(qi.shape[1]), qrow(LANES), qrow(d_att), full(LANES), full(d_att), full(d_att)],
        out_specs=qrow(d_att),
        out_shape=jax.ShapeDtypeStruct((T, d_att), BF16),
        scratch_shapes=[pltpu.VMEM((lp, LANES), BF16), pltpu.VMEM((n_slab, lp, LANES), BF16),
                        pltpu.VMEM((n_slab, lp, 2 * LANES), BF16),
                        pltpu.VMEM((N_IDX_HEADS, tq, LANES), BF16), pltpu.VMEM((N_HEADS, tq, LANES), BF16),
                        pltpu.VMEM((N_IDX_HEADS, tq, 1), F32),
                        pltpu.VMEM((tq, lp), F32), pltpu.VMEM((n_slab, tq, LANES), F32)],
        compiler_params=_params(("arbitrary", "arbitrary")), name="dsa_prompt",
    )(qi, kw, q, kib.reshape(batch, seq, LANES), kb.reshape(batch, seq, d_att), vb.reshape(batch, seq, d_att))


def _dsa_sample_scores_body(pt_ref, qi_ref, w_ref, kin_ref, *refs):
    del pt_ref
    page_refs, s_ref = refs[:-1], refs[-1]
    n_pages = len(page_refs)
    idx_dim, page = page_refs[0].shape
    qi = qi_ref[...]
    qcols = [jnp.broadcast_to(qi[:, h:h + 1], (idx_dim, page)) for h in range(N_IDX_HEADS)]
    ws = [w_ref[h:h + 1, :] for h in range(N_IDX_HEADS)]

    def score(dot_of_head):
        s = None
        for h in range(N_IDX_HEADS):
            t = ws[h] * jnp.maximum(dot_of_head(h), 0.0)
            s = t if s is None else s + t
        return s

    for p in range(n_pages):
        kt = page_refs[p][...]
        s_ref[p:p + 1, :] = score(lambda h: jnp.sum(qcols[h] * kt, axis=0, keepdims=True))
    kin = kin_ref[...]
    s_new = score(lambda h: jnp.sum(qi[:, h:h + 1] * kin, axis=0, keepdims=True))
    lane = lax.broadcasted_iota(I32, (1, page), 1)
    s_ref[n_pages:n_pages + 1, :] = jnp.where(lane == 0, s_new, 0.0)


def _dsa_sample_scores(page_table, qi_col, w3, ki_col, kidx_t):
    nb, n_pages = page_table.shape
    idx_dim, page = kidx_t.shape[1:]
    per = lambda shape: pl.BlockSpec((None,) + shape, lambda b, pt: (b, 0, 0))
    page_specs = [pl.BlockSpec((None, idx_dim, page), functools.partial(lambda b, pt, p: (pt[b, p], 0, 0), p=p))
                  for p in range(n_pages)]
    grid_spec = pltpu.PrefetchScalarGridSpec(
        num_scalar_prefetch=1, grid=(nb,),
        in_specs=[per(qi_col.shape[1:]), per(w3.shape[1:]), per(ki_col.shape[1:])] + page_specs,
        out_specs=per((n_pages + 1, page)))
    return pl.pallas_call(
        _dsa_sample_scores_body, grid_spec=grid_spec,
        out_shape=jax.ShapeDtypeStruct((nb, n_pages + 1, page), F32),
        compiler_params=_params(("arbitrary",)), name="dsa_sample_scores",
    )(page_table, qi_col, w3, ki_col, *([kidx_t] * n_pages))


def _dsa_sample_select_body(s_ref, b_ref, *, n_keys, topk):
    rows, width = s_ref.shape
    pos = lax.broadcasted_iota(I32, (1, width), 1)
    b_ref[...] = jnp.where(pos < n_keys, s_ref[...], -jnp.inf)
    _topk_bias(b_ref, width, jnp.full((rows, 1), float(min(topk, n_keys)), F32))


def _dsa_sample_select(scores, n_keys, topk):
    rows, width = scores.shape
    body = functools.partial(_dsa_sample_select_body, n_keys=n_keys, topk=topk)
    return pl.pallas_call(
        body, out_shape=jax.ShapeDtypeStruct((rows, width), F32),
        compiler_params=pltpu.CompilerParams(vmem_limit_bytes=VMEM_LIMIT), name="dsa_sample_select",
    )(scores)


def _dsa_sample_attend_body(pt_ref, q_ref, kn_ref, vn_ref, b_ref, *refs):
    del pt_ref
    o_ref, lg_ref = refs[-2], refs[-1]
    n_pages = (len(refs) - 2) // 2
    k_refs, v_refs = refs[:n_pages], refs[n_pages:2 * n_pages]
    n_heads, hd, page = k_refs[0].shape
    lane = lax.broadcasted_iota(I32, (1, page), 1)
    bias = b_ref[...]
    q, kn, vn = q_ref[...], kn_ref[...], vn_ref[...]
    for h in range(n_heads):
        col = slice(h, h + 1)
        qc = jnp.broadcast_to(q[:, col], (hd, page))
        for p in range(n_pages):
            lg_ref[p:p + 1, :] = jnp.sum(qc * k_refs[p][h], axis=0, keepdims=True)
        lg_new = jnp.sum(q[:, col] * kn[:, col], axis=0, keepdims=True)
        lg_ref[n_pages:n_pages + 1, :] = jnp.where(lane == 0, lg_new, 0.0)
        lg = lg_ref[...] + bias
        m = jnp.max(jnp.max(lg, axis=1, keepdims=True), axis=0, keepdims=True)
        pr = jnp.exp(lg - m)
        den = jnp.sum(jnp.sum(pr, axis=1, keepdims=True), axis=0, keepdims=True)
        acc = jnp.zeros((hd, page), F32)
        for p in range(n_pages):
            acc = acc + pr[p:p + 1, :] * v_refs[p][h]
        out = jnp.sum(acc, axis=1, keepdims=True) + pr[n_pages:n_pages + 1, 0:1] * vn[:, col]
        o_ref[:, col] = out / den


def _dsa_sample_attend(page_table, q_col, kn_col, vn_col, bias3, k_t, v_t):
    nb, n_pages = page_table.shape
    n_heads, hd, page = k_t.shape[1:]
    per = lambda shape: pl.BlockSpec((None,) + shape, lambda b, pt: (b, 0, 0))
    page_specs = [pl.BlockSpec((None, n_heads, hd, page), functools.partial(lambda b, pt, p: (pt[b, p], 0, 0, 0), p=p))
                  for p in range(n_pages)]
    grid_spec = pltpu.PrefetchScalarGridSpec(
        num_scalar_prefetch=1, grid=(nb,),
        in_specs=[per(q_col.shape[1:]), per(kn_col.shape[1:]), per(vn_col.shape[1:]), per(bias3.shape[1:])] + page_specs * 2,
        out_specs=per(q_col.shape[1:]),
        scratch_shapes=[pltpu.VMEM((n_pages + 1, page), F32)])
    return pl.pallas_call(
        _dsa_sample_attend_body, grid_spec=grid_spec,
        out_shape=jax.ShapeDtypeStruct(q_col.shape, F32),
        compiler_params=_params(("arbitrary",)), name="dsa_sample_attend",
    )(page_table, q_col, kn_col, vn_col, bias3, *([k_t] * n_pages), *([v_t] * n_pages))


def _swiglu(h, wg_ref, wu_ref, wd_ref, n_chunks):
    fc = wg_ref.shape[-1] // n_chunks
    y = None
    for c in range(n_chunks):
        cols = slice(fc * c, fc * (c + 1))
        gate = _dot(h, wg_ref[:, cols])
        act = (gate * jax.nn.sigmoid(gate) * _dot(h, wu_ref[:, cols])).astype(BF16)
        part = _dot(act, wd_ref[cols, :])
        y = part if y is None else y + part
    return y


def _outproj_ffn_body(x_ref, ys_ref, ya_ref, wos_ref, woa_ref, g_ref, wg_ref, wu_ref, wd_ref, o_ref):
    x = x_ref[...] + _dot(ys_ref[...], wos_ref[...]) + _dot(ya_ref[...], woa_ref[...])
    h = _rms(x, g_ref[...]).astype(BF16)
    o_ref[...] = x + _swiglu(h, wg_ref, wu_ref, wd_ref, 2)


def _outproj_ffn(x3, ys, ya, wos, woa, g, wg, wu, wd, seq, tm):
    nb, _, D = x3.shape
    nblk = seq // tm
    row = lambda w: pl.BlockSpec((tm, w), lambda b, j: (b * nblk + j, 0))
    single = lambda a: pl.BlockSpec(a.shape, lambda b, j: (0,) * a.ndim, pipeline_mode=pl.Buffered(1))
    return pl.pallas_call(
        _outproj_ffn_body, grid=(nb, nblk),
        in_specs=[pl.BlockSpec((None, tm, D), lambda b, j: (b, j, 0)), row(ys.shape[1]), row(ya.shape[1]),
                  single(wos), single(woa), single(g), single(wg), single(wu), single(wd)],
        out_specs=row(D), out_shape=jax.ShapeDtypeStruct((nb * seq, D), F32),
        compiler_params=_params(("arbitrary", "arbitrary")), name="outproj_ffn",
    )(x3, ys, ya, wos, woa, g, wg, wu, wd)


def _pool_mix(h, window_sum, divisor, pw_ref, scale):
    gd = h.shape[1] // len(POOL_WINDOWS)
    cols = []
    for g, w in enumerate(POOL_WINDOWS):
        pooled = window_sum(g, w) / divisor(w) - h[:, gd * g:gd * (g + 1)]
        cols.append(_dot(pooled.astype(BF16), pw_ref[g]))
    return jnp.concatenate(cols, axis=1) * scale


def _pool_prompt_body(x_ref, g_ref, pw_ref, sc_ref, o_ref, hist_ref, ext_ref):
    tm, D = x_ref.shape
    gd = D // len(POOL_WINDOWS)
    halo = POOL_HIST + 1
    j = pl.program_id(1)

    @pl.when(j == 0)
    def _():
        ext_ref[0:halo] = jnp.zeros((halo, D), F32)

    x = x_ref[...]
    h = _rms(x, g_ref[...])
    ext_ref[halo:] = h
    pos = (j * tm + lax.broadcasted_iota(I32, (tm, 1), 0)).astype(F32)

    def window_sum(g, w):
        acc = h[:, gd * g:gd * (g + 1)]
        for k in range(1, w):
            acc = acc + ext_ref[halo - k:halo - k + tm, gd * g:gd * (g + 1)]
        return acc

    mixed = _pool_mix(h, window_sum, lambda w: jnp.minimum(float(w), pos + 1.0), pw_ref, sc_ref[...])
    o_ref[...] = x + mixed
    ext_ref[0:halo] = ext_ref[tm:tm + halo]

    @pl.when(j == pl.num_programs(1) - 1)
    def _():
        hist_ref[...] = h[tm - POOL_HIST:, :]


def _pool_prompt(x2d, g, pw, scale, batch, seq, tm):
    T, D = x2d.shape
    nblk = seq // tm
    row = pl.BlockSpec((tm, D), lambda b, j: (b * nblk + j, 0))
    return pl.pallas_call(
        _pool_prompt_body, grid=(batch, nblk),
        in_specs=[row, _const_spec(g.shape), _const_spec(pw.shape), _const_spec(scale.shape)],
        out_specs=(row, pl.BlockSpec((None, POOL_HIST, D), lambda b, j: (b, 0, 0))),
        out_shape=(jax.ShapeDtypeStruct((T, D), F32), jax.ShapeDtypeStruct((batch, POOL_HIST, D), F32)),
        scratch_shapes=[pltpu.VMEM((tm + POOL_HIST + 1, D), F32)],
        compiler_params=_params(("arbitrary", "arbitrary")), name="pool_prompt",
    )(x2d, g, pw, scale)


def _pool_sample_body(x_ref, hist_ref, g_ref, pw_ref, sc_ref, o_ref, nh_ref):
    D = x_ref.shape[1]
    gd = D // len(POOL_WINDOWS)
    x = x_ref[...]
    h = _rms(x, g_ref[...])

    def window_sum(g, w):
        acc = h[:, gd * g:gd * (g + 1)]
        for k in range(1, w):
            acc = acc + hist_ref[POOL_HIST - k, :, gd * g:gd * (g + 1)]
        return acc

    mixed = _pool_mix(h, window_sum, float, pw_ref, sc_ref[...])
    o_ref[...] = x + mixed
    for i in range(POOL_HIST - 1):
        nh_ref[i] = hist_ref[i + 1]
    nh_ref[POOL_HIST - 1] = h


def _pool_sample(x2d, hist_t, g, pw, scale):
    sds = jax.ShapeDtypeStruct
    return pl.pallas_call(
        _pool_sample_body, out_shape=(sds(x2d.shape, F32), sds(hist_t.shape, F32)),
        compiler_params=pltpu.CompilerParams(vmem_limit_bytes=VMEM_LIMIT), name="pool_sample",
    )(x2d, hist_t, g, pw, scale)


def _store_row_tiles(ref, x):
    for s in range(ref.shape[1]):
        ref[:, s, :] = x[:, LANES * s:LANES * (s + 1)]


def _load_row_tiles(ref):
    return jnp.concatenate([ref[:, s, :] for s in range(ref.shape[1])], axis=1)


def _router_body(x_ref, g_ref, wr_ref, h_ref, sel_ref, *, n_experts):
    h = _rms(x_ref[...], g_ref[...])
    _store_row_tiles(h_ref, h)
    w = wr_ref[...]
    h_hi, w_hi = h.astype(BF16), w.astype(BF16)
    h_lo, w_lo = (h - h_hi.astype(F32)).astype(BF16), (w - w_hi.astype(F32)).astype(BF16)
    logits = _dot(h_hi, w_hi) + (_dot(h_hi, w_lo) + _dot(h_lo, w_hi))
    lane = lax.broadcasted_iota(I32, logits.shape, 1).astype(F32)
    logits = jnp.where(lane < n_experts, logits, -jnp.inf)
    v1 = jnp.max(logits, axis=1, keepdims=True)
    i1 = jnp.min(jnp.where(logits == v1, lane, float(LANES)), axis=1, keepdims=True)
    rest = jnp.where(lane == i1, -jnp.inf, logits)
    v2 = jnp.max(rest, axis=1, keepdims=True)
    i2 = jnp.min(jnp.where(rest == v2, lane, float(LANES)), axis=1, keepdims=True)
    e2 = jnp.exp(v2 - v1)
    den = 1.0 + e2
    sel_ref[...] = jnp.where(lane == 0.0, i1, jnp.where(lane == 1.0, i2, jnp.where(lane == 2.0, 1.0 / den, e2 / den)))


def _router(x2d, g, wr_pad, n_experts, tm):
    T, D = x2d.shape
    row = lambda w: pl.BlockSpec((tm, w), lambda i: (i, 0))
    body = functools.partial(_router_body, n_experts=n_experts)
    return pl.pallas_call(
        body, grid=(T // tm,),
        in_specs=[row(D), _const_spec(g.shape), _const_spec(wr_pad.shape)],
        out_specs=(pl.BlockSpec((tm, D // LANES, LANES), lambda i: (i, 0, 0)), row(LANES)),
        out_shape=(jax.ShapeDtypeStruct((T, D // LANES, LANES), F32), jax.ShapeDtypeStruct((T, LANES), F32)),
        compiler_params=_params(("arbitrary",)), name="router",
    )(x2d, g, wr_pad)


def _row_gather(idx_ref, base, src_hbm, dst, sem):
    def issue(r, carry):
        pltpu.make_async_copy(src_hbm.at[idx_ref[base + r]], dst.at[r], sem).start()
        return carry

    lax.fori_loop(0, dst.shape[0], issue, 0, unroll=8)


def _row_gather_wait(src_hbm, dst, sem):
    pltpu.make_async_copy(src_hbm.at[pl.ds(0, dst.shape[0])], dst, sem).wait()


def _moe_dispatch_body(p1_ref, p2_ref, ends_ref, h_ref, x_hbm, zero_ref, stage, sems, *, tm):
    i = pl.program_id(0)
    last = pl.num_programs(0) - 1
    gt = zero_ref.shape[0]
    n_groups = ends_ref.shape[0]
    zsem = sems.at[2]
    slot = i % 2

    def fill_tile(e, act):
        if e < n_groups:
            start = ends_ref[e] - gt
            exists = ends_ref[e] > (ends_ref[e - 1] if e else 0)
        else:
            start = ends_ref[n_groups - 1] + (e - n_groups) * gt
            exists = start < x_hbm.shape[0]

        @pl.when(exists)
        def _():
            act(pltpu.make_async_copy(zero_ref, x_hbm.at[pl.ds(pl.multiple_of(start, gt), gt)], zsem))

    @pl.when(i == 0)
    def _():
        zero_ref[...] = jnp.zeros(zero_ref.shape, F32)
        for e in range(2 * n_groups):
            fill_tile(e, lambda cp: cp.start())
        for e in range(2 * n_groups):
            fill_tile(e, lambda cp: cp.wait())

    def wait_rows(s):
        for _ in range(2):
            pltpu.make_async_copy(stage.at[s], x_hbm.at[pl.ds(0, tm)], sems.at[s]).wait()

    @pl.when(i >= 2)
    def _():
        wait_rows(slot)

    stage[slot] = h_ref[...]

    def issue(r, carry):
        t = i * tm + r
        pltpu.make_async_copy(stage.at[slot, r], x_hbm.at[p1_ref[t]], sems.at[slot]).start()
        pltpu.make_async_copy(stage.at[slot, r], x_hbm.at[p2_ref[t]], sems.at[slot]).start()
        return carry

    lax.fori_loop(0, tm, issue, 0, unroll=8)

    @pl.when(i == last)
    def _():
        wait_rows(slot)

    @pl.when((i == last) & (i >= 1))
    def _():
        wait_rows(1 - slot)


def _moe_dispatch(pos1, pos2, ends, h_tiles, n_rows, tm, group_tile):
    T, S, _ = h_tiles.shape
    grid_spec = pltpu.PrefetchScalarGridSpec(
        num_scalar_prefetch=3, grid=(T // tm,),
        in_specs=[pl.BlockSpec((tm, S, LANES), lambda i, p1, p2, en: (i, 0, 0))],
        out_specs=pl.BlockSpec(memory_space=pl.ANY),
        scratch_shapes=[pltpu.VMEM((group_tile, S, LANES), F32), pltpu.VMEM((2, tm, S, LANES), F32),
                        pltpu.SemaphoreType.DMA((3,))])
    return pl.pallas_call(
        functools.partial(_moe_dispatch_body, tm=tm), grid_spec=grid_spec,
        out_shape=jax.ShapeDtypeStruct((n_rows, S, LANES), F32),
        compiler_params=pltpu.CompilerParams(dimension_semantics=("arbitrary",), vmem_limit_bytes=VMEM_LIMIT,
                                             disable_bounds_checks=True),
        name="moe_dispatch",
    )(pos1, pos2, ends, h_tiles)


def _moe_group_body(te_ref, nu_ref, x_ref, wg_ref, wu_ref, wd_ref, y_ref):
    del te_ref
    i = pl.program_id(0)

    @pl.when(i < nu_ref[0])
    def _():
        _store_row_tiles(y_ref, _swiglu(_load_row_tiles(x_ref).astype(BF16), wg_ref, wu_ref, wd_ref, 2))

    @pl.when(i >= nu_ref[0])
    def _():
        y_ref[...] = jnp.zeros(y_ref.shape, F32)


def _moe_group(tile_expert, n_used, x_tiles, wg, wu, wd, tm):
    P, S, _ = x_tiles.shape
    _, D, F = wg.shape
    wspec = lambda a, b: pl.BlockSpec((None, a, b), lambda i, te, nu: (te[i], 0, 0))
    grid_spec = pltpu.PrefetchScalarGridSpec(
        num_scalar_prefetch=2, grid=(P // tm,),
        in_specs=[pl.BlockSpec((tm, S, LANES), lambda i, te, nu: (jnp.minimum(i, nu[0] - 1), 0, 0)),
                  wspec(D, F), wspec(D, F), wspec(F, D)],
        out_specs=pl.BlockSpec((tm, S, LANES), lambda i, te, nu: (i, 0, 0)))
    return pl.pallas_call(
        _moe_group_body, grid_spec=grid_spec, out_shape=jax.ShapeDtypeStruct((P, S, LANES), F32),
        compiler_params=_params(("arbitrary",)), name="moe_group",
    )(tile_expert, n_used, x_tiles, wg, wu, wd)


def _moe_combine_body(p1_ref, p2_ref, dst_ref, x_ref, sel_ref, y_hbm, ya_hbm, yb_hbm, buf, obuf, sem, osem):
    i = pl.program_id(0)
    last = pl.num_programs(0) - 1
    tm = x_ref.shape[0]
    n_chunks = tm // OUT_CHUNK
    rows_a = ya_hbm.shape[0]
    slot = i % 2

    def out_copies(step, s, act):
        for c in range(n_chunks):
            dst = dst_ref[step * n_chunks + c]
            src = obuf.at[s, pl.ds(c * OUT_CHUNK, OUT_CHUNK)]

            @pl.when((dst >= 0) & (dst < rows_a))
            def _():
                act(pltpu.make_async_copy(src, ya_hbm.at[pl.ds(pl.multiple_of(dst, OUT_CHUNK), OUT_CHUNK)], osem.at[s]))

            @pl.when(dst >= rows_a)
            def _():
                off = pl.multiple_of(dst - rows_a, OUT_CHUNK)
                act(pltpu.make_async_copy(src, yb_hbm.at[pl.ds(off, OUT_CHUNK)], osem.at[s]))

    def gather(tile, s):
        _row_gather(p1_ref, tile * tm, y_hbm, buf.at[s, 0], sem.at[s, 0])
        _row_gather(p2_ref, tile * tm, y_hbm, buf.at[s, 1], sem.at[s, 1])

    @pl.when(i == 0)
    def _():
        gather(0, 0)

    @pl.when(i + 1 < pl.num_programs(0))
    def _():
        gather(i + 1, 1 - slot)

    _row_gather_wait(y_hbm, buf.at[slot, 0], sem.at[slot, 0])
    _row_gather_wait(y_hbm, buf.at[slot, 1], sem.at[slot, 1])
    @pl.when(i >= 2)
    def _():
        out_copies(i - 2, slot, lambda cp: cp.wait())

    sel = sel_ref[...]
    obuf[slot] = (x_ref[...] + sel[:, 2:3] * _load_row_tiles(buf.at[slot, 0])
                  + sel[:, 3:4] * _load_row_tiles(buf.at[slot, 1]))
    out_copies(i, slot, lambda cp: cp.start())

    @pl.when(i == last)
    def _():
        out_copies(i, slot, lambda cp: cp.wait())

    @pl.when((i == last) & (i >= 1))
    def _():
        out_copies(i - 1, 1 - slot, lambda cp: cp.wait())


def _moe_combine(pos1, pos2, chunk_dst, x2d, sel, y_tiles, tm, rows_a, rows_b):
    T, D = x2d.shape
    S = y_tiles.shape[1]
    assert tm % OUT_CHUNK == 0 and rows_a % OUT_CHUNK == 0 and rows_b % OUT_CHUNK == 0
    row = lambda w: pl.BlockSpec((tm, w), lambda i, p1, p2, cd: (i, 0))
    hbm = pl.BlockSpec(memory_space=pl.ANY)
    grid_spec = pltpu.PrefetchScalarGridSpec(
        num_scalar_prefetch=3, grid=(T // tm,),
        in_specs=[row(D), row(LANES), hbm], out_specs=(hbm, hbm),
        scratch_shapes=[pltpu.VMEM((2, 2, tm, S, LANES), F32), pltpu.VMEM((2, tm, D), F32),
                        pltpu.SemaphoreType.DMA((2, 2)), pltpu.SemaphoreType.DMA((2,))])
    return pl.pallas_call(
        _moe_combine_body, grid_spec=grid_spec,
        out_shape=(jax.ShapeDtypeStruct((rows_a, D), F32), jax.ShapeDtypeStruct((rows_b, D), F32)),
        compiler_params=pltpu.CompilerParams(dimension_semantics=("arbitrary",), vmem_limit_bytes=VMEM_LIMIT,
                                             disable_bounds_checks=True),
        name="moe_combine",
    )(pos1, pos2, chunk_dst, x2d, sel, y_tiles)


def _moe_plan(sel, n_experts, tm):
    T = sel.shape[0]
    experts = jnp.concatenate([sel[:, 0], sel[:, 1]]).astype(I32)
    onehot = (experts[:, None] == jnp.arange(n_experts, dtype=I32)[None, :]).astype(I32)
    rank = jnp.cumsum(onehot, axis=0) - onehot
    counts = jnp.sum(onehot, axis=0)
    padded = (counts + tm - 1) // tm * tm
    ends = jnp.cumsum(padded)
    pos = jnp.sum(onehot * ((ends - padded)[None, :] + rank), axis=1)
    n_rows = (2 * T + n_experts * (tm - 1)) // tm * tm
    tile_start = jnp.arange(n_rows // tm, dtype=I32) * tm
    tile_expert = jnp.minimum(jnp.sum((tile_start[:, None] >= ends[None, :]).astype(I32), axis=1), n_experts - 1)
    n_used = (ends[-1] // tm).astype(I32).reshape(1)
    return tile_expert, n_used, ends.astype(I32), n_rows, pos[:T], pos[T:]


def _rope_tables(pos):
    half = HEAD_DIM // 2
    inv = ROPE_THETA ** (-jnp.arange(half, dtype=F32) / half)
    ang = pos[:, None] * inv[None, :]
    cos, sin = jnp.cos(ang), jnp.sin(ang)
    reps = LANES // HEAD_DIM
    return jnp.tile(jnp.concatenate([cos, cos], axis=1), (1, reps)), jnp.tile(jnp.concatenate([-sin, sin], axis=1), (1, reps))


def kernel(x_prompt, x_sample, cache_k, cache_v, cache_kidx, page_table, state_ssm_re, state_ssm_im, state_pool, meta_tokens, norm_mix0, w_in0, q_norm, k_norm, ssm_lambda_re, ssm_lambda_im, ssm_log_dt, ssm_b_re, ssm_b_im, ssm_c_re, ssm_c_im, ssm_d, ssm_w_glu, w_out0, norm_ffn0, ffn_w_gate, ffn_w_up, ffn_w_down, norm_mix1, pool_w, pool_scale, norm_ffn1, router_w, moe_w_gate, moe_w_up, moe_w_down):
    B, S, D = x_prompt.shape
    DB, DS, _ = x_sample.shape
    assert DS == 1, "one new token per sample sequence"
    L = S + N_META
    n_pool, page = cache_k.shape[1], cache_k.shape[2]
    n_pages = page_table.shape[1]
    past = n_pages * page
    d_att = N_HEADS * HEAD_DIM
    d_qi = N_IDX_HEADS * IDX_DIM
    G, P = ssm_lambda_re.shape[1:]
    d_ssm = G * SSM_GROUP
    n_state = G * P
    E = router_w.shape[-1]
    topk_p = min(TOPK_MAX, S // 4)
    topk_s = min(TOPK_MAX, (past + DS) // 4)
    tm = _row_block(L, 1024)
    assert tm >= POOL_HIST + 1

    tm_in = LANES * (-(-L // (3 * LANES)))
    l_pad = -(-L // tm_in) * tm_in
    x_p = jnp.concatenate([jnp.broadcast_to(meta_tokens[None], (B, N_META, D)), x_prompt,
                           jnp.zeros((B, l_pad - L, D), x_prompt.dtype)], axis=1)
    x_s = x_sample.reshape(1, DB, D)

    w_in = w_in0[0]
    o1, o2, o3, o4, o5, o6 = d_ssm, d_ssm + d_att, d_ssm + 2 * d_att, d_ssm + 3 * d_att, d_ssm + 3 * d_att + d_qi, d_ssm + 3 * d_att + d_qi + IDX_DIM
    w_kw = jnp.pad(w_in[:, o5:], ((0, 0), (0, LANES - (w_in.shape[1] - o5))))
    ws = tuple(w.astype(BF16) for w in (w_in[:, :o1], w_in[:, o1:o2], w_in[:, o2:o3], w_in[:, o3:o4], w_in[:, o4:o5], w_kw))
    g_mix0 = norm_mix0[0].reshape(1, D)
    qn = jnp.tile(q_norm[0], N_HEADS).reshape(1, d_att)
    kn = jnp.tile(k_norm[0], N_HEADS).reshape(1, d_att)
    head_of = jnp.arange(d_att) // HEAD_DIM
    hm = jnp.where(head_of[:, None] == head_of[None, :], 1.0 / HEAD_DIM, 0.0).astype(BF16)
    cos_p, sin_p = _rope_tables(jnp.arange(l_pad, dtype=F32))
    cos_s, sin_s = _rope_tables(jnp.full((DB,), float(past), F32))

    flat = lambda a: a.reshape(-1, a.shape[-1])
    u_p, q_p, kt_p, vt_p, qi_p, kit_p, kw_p, kb_p, vb_p, kib_p = _inproj(
        x_p, cos_p, sin_p, tm_in, L, g_mix0, ws, qn, kn, hm, True)
    u_p, q_p, qi_p, kw_p, kb_p, vb_p, kib_p = (flat(a) for a in (u_p, q_p, qi_p, kw_p, kb_p, vb_p, kib_p))
    u_s, q_s, k_s, v_s, qi_s, ki_s, kw_s, _, _, _ = (
        flat(a) for a in _inproj(x_s, cos_s, sin_s, DB, DB, g_mix0, ws, qn, kn, hm, False))

    lbr, lbi, bbr, bbi = _s5_prep(ssm_lambda_re[0], ssm_lambda_im[0], ssm_log_dt[0], ssm_b_re[0], ssm_b_im[0])
    gps = LANES // SSM_GROUP
    n_slab = G // gps
    eye = jnp.eye(gps, dtype=F32)

    def b_slabs(bt):
        return jnp.einsum('sgcp,gh->sgchp', bt.reshape(n_slab, gps, SSM_GROUP, P), eye).reshape(n_slab, LANES, gps * P)

    def c_slabs(c):
        return jnp.einsum('sgcp,gh->sgphc', c.reshape(n_slab, gps, SSM_GROUP, P), eye).reshape(n_slab, gps * P, LANES)

    wb = jnp.concatenate([b_slabs(bbr), b_slabs(bbi)], axis=2)
    wc = jnp.stack([c_slabs(ssm_c_re[0]), -c_slabs(ssm_c_im[0])], axis=1).astype(BF16)
    lbr_f, lbi_f = lbr.reshape(1, n_state), lbi.reshape(1, n_state)
    d_skip = ssm_d[0].reshape(1, d_ssm)
    wglu = ssm_w_glu[0].astype(BF16)
    assert B % SUBLANES == 0, "the S5 scan keeps one batch row per sublane"
    steps = max(d for d in range(1, L + 1) if L % d == 0 and d * B <= 512 and (d * B) % (2 * SUBLANES) == 0)
    u_tm = u_p.reshape(B, L, d_ssm).swapaxes(0, 1).reshape(L * B, d_ssm)
    zero_h = jnp.zeros((B, n_state), F32)
    ys_tm, hre_p, him_p = _s5(u_tm, zero_h, zero_h, lbr_f, lbi_f, wb, wc, d_skip, wglu, B, steps, False)
    ys_p = ys_tm.reshape(L, B, d_ssm).swapaxes(0, 1).reshape(B * L, d_ssm)
    ys_s, hre_s, him_s = _s5(u_s, state_ssm_re[0].reshape(DB, n_state), state_ssm_im[0].reshape(DB, n_state),
                             lbr_f, lbi_f, wb, wc, d_skip, wglu, DB, 1, True)

    ya_p = _dsa_prompt(qi_p, kw_p, q_p, kib_p, kb_p, vb_p, B, L, tm, topk_p)
    assert page == LANES
    w3 = kw_s[:, IDX_DIM:IDX_DIM + N_IDX_HEADS].reshape(DB, N_IDX_HEADS, 1)
    def head_cols(a, n_heads):
        return a.astype(F32).reshape(DB, n_heads, -1).swapaxes(1, 2)

    scores = _dsa_sample_scores(page_table, head_cols(qi_s, N_IDX_HEADS), w3, ki_s.reshape(DB, IDX_DIM, 1),
                                jnp.transpose(cache_kidx[0], (0, 2, 1)))
    bias = _dsa_sample_select(scores.reshape(DB, (n_pages + 1) * page), past + DS, topk_s)
    ya_s = _dsa_sample_attend(page_table, head_cols(q_s, N_HEADS), head_cols(k_s, N_HEADS), head_cols(v_s, N_HEADS),
                              bias.reshape(DB, n_pages + 1, page),
                              jnp.transpose(cache_k[0], (0, 2, 3, 1)), jnp.transpose(cache_v[0], (0, 2, 3, 1)))
    ya_s = ya_s.swapaxes(1, 2).reshape(DB, d_att).astype(BF16)

    w_out = w_out0[0].astype(BF16)
    ffn = (w_out[:d_ssm], w_out[d_ssm:], norm_ffn0[0].reshape(1, D), ffn_w_gate[0].astype(BF16),
           ffn_w_up[0].astype(BF16), ffn_w_down[0].astype(BF16))
    x_p = _outproj_ffn(x_p, ys_p, ya_p, *ffn, L, tm)
    x_s = _outproj_ffn(x_s, ys_s, ya_s, *ffn, DB, DB)

    g_mix1 = norm_mix1[0].reshape(1, D)
    pw = pool_w[0].astype(BF16)
    psc = pool_scale[0].reshape(1, D)
    T_all = B * L + DB
    x_p, pool_p = _pool_prompt(x_p, g_mix1, pw, psc, B, L, tm)
    x_s, hist_t = _pool_sample(x_s, state_pool[0].swapaxes(0, 1), g_mix1, pw, psc)
    pool_s = hist_t.swapaxes(0, 1)
    x_all = jnp.concatenate([x_p, x_s], axis=0)

    g_ffn1 = norm_ffn1[0].reshape(1, D)
    wr_pad = jnp.pad(router_w[0], ((0, 0), (0, LANES - E)))
    wg, wu, wd = moe_w_gate[0].astype(BF16), moe_w_up[0].astype(BF16), moe_w_down[0].astype(BF16)
    tr = _row_block(T_all, MOE_TILE)
    h_tiles, sel = _router(x_all, g_ffn1, wr_pad, E, tr)
    tile_expert, n_used, ends, n_rows, pos1, pos2 = _moe_plan(sel, E, MOE_TILE)
    x_tiles = _moe_dispatch(pos1, pos2, ends, h_tiles, n_rows, tr, MOE_TILE)
    y_tiles = _moe_group(tile_expert, n_used, x_tiles, wg, wu, wd, MOE_TILE)
    assert N_META % OUT_CHUNK == 0 and L % OUT_CHUNK == 0 and DB % OUT_CHUNK == 0
    row0 = jnp.arange(T_all // OUT_CHUNK, dtype=I32) * OUT_CHUNK
    in_seq = row0 % L
    chunk_dst = jnp.where(row0 >= B * L, row0 - B * N_META,
                          jnp.where(in_seq < N_META, -1, (row0 // L) * S + in_seq - N_META))
    y_prompt, y_sample = _moe_combine(pos1, pos2, chunk_dst, x_all, sel, y_tiles, tr, B * S, DB)
    y_prompt = y_prompt.reshape(B, S, D)
    y_sample = y_sample.reshape(DB, 1, D)
    k_prompt = jnp.transpose(kt_p, (0, 3, 1, 2))[None]
    v_prompt = jnp.transpose(vt_p, (0, 3, 1, 2))[None]
    kidx_prompt = jnp.transpose(kit_p, (0, 2, 1))[None]
    return (y_prompt, y_sample, k_prompt, v_prompt, kidx_prompt,
            hre_p.reshape(1, B, G, P), him_p.reshape(1, B, G, P), pool_p[None],
            k_s.reshape(1, DB, 1, N_HEADS, HEAD_DIM), v_s.reshape(1, DB, 1, N_HEADS, HEAD_DIM),
            ki_s.reshape(1, DB, 1, IDX_DIM), hre_s.reshape(1, DB, G, P), him_s.reshape(1, DB, G, P), pool_s[None])
```

```python
import functools
import math

import jax
import jax.numpy as jnp
from jax import lax
from jax.experimental import pallas as pl
from jax.experimental.pallas import tpu as pltpu

F32 = jnp.float32
BF16 = jnp.bfloat16
I32 = jnp.int32

N_META = 16
SSM_GROUP = 16
SSM_STATE = 64
N_HEADS = 8
HEAD_DIM = 64
N_IDX_HEADS = 8
IDX_DIM = 64
TOPK_MAX = 256
ROPE_THETA = 10000.0
POOL_WINDOWS = (2, 4, 8, 16)
POOL_HIST = max(POOL_WINDOWS) - 1
TOP_K_EXPERTS = 2
EPS = 1e-6

LANES = 128
SUBLANES = 8
VMEM_LIMIT = 56 * 1024 * 1024
DMA_UNROLL = 8
OUT_CHUNK = 2 * SUBLANES
MOE_TILE = 256
BISECT_ITERS = 16
NEG_BIG = -1e30


def _params(sem):
    return pltpu.CompilerParams(dimension_semantics=sem, vmem_limit_bytes=VMEM_LIMIT)


def _row_block(n, cap, mult=2 * SUBLANES):
    best = None
    for d in range(mult, min(n, cap) + 1, mult):
        if n % d == 0:
            best = d
    assert best is not None, n
    return best


def _rms(x, g):
    return x * lax.rsqrt(jnp.mean(x * x, axis=-1, keepdims=True) + EPS) * g


def _dot(a, b):
    return jnp.dot(a, b, preferred_element_type=F32)


def _dot_nt(a, b):
    return lax.dot_general(a, b, (((1,), (1,)), ((), ())), preferred_element_type=F32)


def _const_spec(shape):
    nd = len(shape)
    return pl.BlockSpec(shape, lambda *_: (0,) * nd)


def _inproj_body(x_ref, g_ref, wu_ref, wq_ref, wk_ref, wv_ref, wqi_ref, wkw_ref, qn_ref, kn_ref, hm_ref,
                 cos_ref, sin_ref,
                 u_ref, q_ref, k_ref, v_ref, qi_ref, ki_ref, kw_ref, kb_ref, vb_ref, kib_ref, *, transposed):
    def store_kv(ref, z):
        if not transposed:
            ref[...] = z
            return
        per_slab = LANES // HEAD_DIM
        for sp in range(z.shape[1] // LANES):
            zt = z[:, LANES * sp:LANES * (sp + 1)].T
            ref[per_slab * sp:per_slab * (sp + 1)] = zt.reshape(per_slab, HEAD_DIM, z.shape[0])

    h = _rms(x_ref[...], g_ref[...]).astype(BF16)
    cos = cos_ref[...]
    sin = sin_ref[...]
    lane = lax.broadcasted_iota(I32, (1, LANES), 1)
    lo_half = (lane & (HEAD_DIM // 2)) == 0

    def rope(z):
        cols = []
        for j in range(z.shape[1] // LANES):
            zs = z[:, LANES * j:LANES * (j + 1)]
            partner = jnp.where(lo_half, pltpu.roll(zs, LANES - HEAD_DIM // 2, 1), pltpu.roll(zs, HEAD_DIM // 2, 1))
            cols.append(zs * cos + partner * sin)
        return cols[0] if len(cols) == 1 else jnp.concatenate(cols, axis=1)

    def head_norm(z, gn):
        ms = _dot((z * z).astype(BF16), hm_ref[...])
        return z * lax.rsqrt(ms + EPS) * gn

    u_ref[...] = _dot(h, wu_ref[...])
    q = rope(head_norm(_dot(h, wq_ref[...]), qn_ref[...]))
    q_ref[...] = (q * (HEAD_DIM ** -0.5)).astype(BF16)
    k = rope(head_norm(_dot(h, wk_ref[...]), kn_ref[...]))
    store_kv(k_ref, k)
    kb_ref[...] = k.astype(BF16)
    v = _dot(h, wv_ref[...])
    store_kv(v_ref, v)
    vb_ref[...] = v.astype(BF16)
    qi_ref[...] = (rope(_dot(h, wqi_ref[...])) * (IDX_DIM ** -0.5)).astype(BF16)
    zkw = _dot(h, wkw_ref[...])
    kir = rope(zkw)
    kw_ref[...] = jnp.where(lane < IDX_DIM, kir, zkw * (N_IDX_HEADS ** -0.5))
    ki_ref[...] = kir.T[:IDX_DIM] if transposed else kir[:, :IDX_DIM]
    kib_ref[...] = jnp.where(lane < IDX_DIM, kir, pltpu.roll(kir, IDX_DIM, 1)).astype(BF16)


def _inproj(x3, cos_t, sin_t, tm, out_rows, g, ws, qn, kn, hm, transposed):
    nb, rows, D = x3.shape
    wu, wq, wk, wv, wqi, wkw = ws
    d_ssm, d_att, d_qi = wu.shape[1], wq.shape[1], wqi.shape[1]
    row = lambda w: pl.BlockSpec((None, tm, w), lambda b, j: (b, j, 0))
    tab = pl.BlockSpec((tm, LANES), lambda b, j: (j, 0))
    sds = lambda w, dt: jax.ShapeDtypeStruct((nb, out_rows, w), dt)
    if transposed:
        kv_shape = jax.ShapeDtypeStruct((nb, N_HEADS, HEAD_DIM, out_rows), F32)
        kv_spec = pl.BlockSpec((None, N_HEADS, HEAD_DIM, tm), lambda b, j: (b, 0, 0, j))
        ki_shape = jax.ShapeDtypeStruct((nb, IDX_DIM, out_rows), F32)
        ki_spec = pl.BlockSpec((None, IDX_DIM, tm), lambda b, j: (b, 0, j))
    else:
        kv_shape, kv_spec, ki_shape, ki_spec = sds(d_att, F32), row(d_att), sds(IDX_DIM, F32), row(IDX_DIM)
    out_shape = (
        sds(d_ssm, F32),
        sds(d_att, BF16),
        kv_shape,
        kv_shape,
        sds(d_qi, BF16),
        ki_shape,
        sds(LANES, F32),
        sds(d_att, BF16),
        sds(d_att, BF16),
        sds(LANES, BF16),
    )
    out_specs = (row(d_ssm), row(d_att), kv_spec, kv_spec, row(d_qi), ki_spec, row(LANES),
                 row(d_att), row(d_att), row(LANES))
    in_specs = [row(D), _const_spec(g.shape)] + [_const_spec(w.shape) for w in ws] + [
        _const_spec(qn.shape), _const_spec(kn.shape), _const_spec(hm.shape), tab, tab]
    return pl.pallas_call(
        functools.partial(_inproj_body, transposed=transposed), grid=(nb, rows // tm),
        in_specs=in_specs, out_specs=out_specs, out_shape=out_shape,
        compiler_params=_params(("arbitrary", "arbitrary")), name="inproj_t" if transposed else "inproj",
    )(x3, g, *ws, qn, kn, hm, cos_t, sin_t)


def _s5_prep_body(lr_ref, li_ref, ldt_ref, btr_ref, bti_ref, lbr_ref, lbi_ref, bbr_ref, bbi_ref):
    lr = lr_ref[...]
    li = li_ref[...]
    dt = jnp.exp(ldt_ref[...])
    mag = jnp.exp(lr * dt)
    lbr = mag * jnp.cos(li * dt)
    lbi = mag * jnp.sin(li * dt)
    lbr_ref[...] = lbr
    lbi_ref[...] = lbi
    den = lr * lr + li * li
    cr = ((lbr - 1.0) * lr + lbi * li) / den
    ci = (lbi * lr - (lbr - 1.0) * li) / den
    btr = btr_ref[...]
    bti = bti_ref[...]
    bbr_ref[...] = cr * btr - ci * bti
    bbi_ref[...] = cr * bti + ci * btr


def _s5_prep(lam_re, lam_im, log_dt, b_re, b_im):
    G, P = lam_re.shape
    C = b_re.shape[-1]
    btr = jnp.swapaxes(b_re, 1, 2)
    bti = jnp.swapaxes(b_im, 1, 2)
    sds = jax.ShapeDtypeStruct
    return pl.pallas_call(
        _s5_prep_body,
        out_shape=(sds((G, 1, P), F32), sds((G, 1, P), F32), sds((G, C, P), F32), sds((G, C, P), F32)),
        name="s5_prep",
    )(lam_re.reshape(G, 1, P), lam_im.reshape(G, 1, P), log_dt.reshape(G, 1, 1), btr, bti)


def _s5_body(u_ref, h0r_ref, h0i_ref, lbr_ref, lbi_ref, wb_ref, wc_ref, d_ref, wglu_ref,
             y_ref, hr_ref, hi_ref, bu_ref, st_ref, *, nb, steps, precise):
    n_slab = wb_ref.shape[0]
    cin = wb_ref.shape[1]
    sw = wb_ref.shape[2] // 2
    n_state = n_slab * sw

    @pl.when(pl.program_id(0) == 0)
    def _():
        st_ref[0] = h0r_ref[...]
        st_ref[1] = h0i_ref[...]

    u = u_ref[...]
    for i in range(n_slab):
        ui = u[:, cin * i:cin * (i + 1)]
        if precise:
            bu = jnp.dot(ui, wb_ref[i], preferred_element_type=F32, precision=lax.Precision.HIGHEST)
        else:
            bu = _dot(ui.astype(BF16), wb_ref[i].astype(BF16))
        bu_ref[:, sw * i:sw * (i + 1)] = bu[:, :sw]
        bu_ref[:, n_state + sw * i:n_state + sw * (i + 1)] = bu[:, sw:]

    for i in range(n_slab):
        re_cols = slice(sw * i, sw * (i + 1))
        im_cols = slice(n_state + sw * i, n_state + sw * (i + 1))
        lr = jnp.broadcast_to(lbr_ref[:, re_cols], (nb, sw))
        li = jnp.broadcast_to(lbi_ref[:, re_cols], (nb, sw))

        def step(t, carry, re_cols=re_cols, im_cols=im_cols, lr=lr, li=li):
            hr, hi = carry
            rows = pl.ds(pl.multiple_of(t * nb, nb), nb)
            nhr = lr * hr - li * hi + bu_ref[rows, re_cols]
            nhi = lr * hi + li * hr + bu_ref[rows, im_cols]
            bu_ref[rows, re_cols] = nhr
            bu_ref[rows, im_cols] = nhi
            return nhr, nhi

        hr, hi = lax.fori_loop(0, steps, step, (st_ref[0, :, re_cols], st_ref[1, :, re_cols]))
        st_ref[0, :, re_cols] = hr
        st_ref[1, :, re_cols] = hi

    ys = []
    for i in range(n_slab):
        h_re = bu_ref[:, sw * i:sw * (i + 1)].astype(BF16)
        h_im = bu_ref[:, n_state + sw * i:n_state + sw * (i + 1)].astype(BF16)
        ys.append(_dot(h_re, wc_ref[i, 0]) + _dot(h_im, wc_ref[i, 1]))
    y = jnp.concatenate(ys, axis=1) + d_ref[...] * u
    y = 0.5 * y * (1.0 + jnp.tanh(math.sqrt(2.0 / math.pi) * (y + 0.044715 * (y * y * y))))
    y = y * jax.nn.sigmoid(_dot(y.astype(BF16), wglu_ref[...]))
    y_ref[...] = y.astype(BF16)
    hr_ref[...] = st_ref[0]
    hi_ref[...] = st_ref[1]


def _s5(u_tm, h0_re, h0_im, lbr_flat, lbi_flat, wb, wc, d_skip, wglu, nb, steps, precise):
    rows_total, d_ssm = u_tm.shape
    n_state = h0_re.shape[1]
    rows = nb * steps
    sds = jax.ShapeDtypeStruct
    body = functools.partial(_s5_body, nb=nb, steps=steps, precise=precise)
    return pl.pallas_call(
        body, grid=(rows_total // rows,),
        in_specs=[pl.BlockSpec((rows, d_ssm), lambda c: (c, 0)), _const_spec(h0_re.shape), _const_spec(h0_im.shape),
                  _const_spec(lbr_flat.shape), _const_spec(lbi_flat.shape), _const_spec(wb.shape),
                  _const_spec(wc.shape), _const_spec(d_skip.shape), _const_spec(wglu.shape)],
        out_specs=(pl.BlockSpec((rows, d_ssm), lambda c: (c, 0)), _const_spec(h0_re.shape), _const_spec(h0_im.shape)),
        out_shape=(sds((rows_total, d_ssm), BF16), sds(h0_re.shape, F32), sds(h0_im.shape, F32)),
        scratch_shapes=[pltpu.VMEM((rows, 2 * n_state), F32), pltpu.VMEM((2, nb, n_state), F32)],
        compiler_params=_params(("arbitrary",)), name="s5_precise" if precise else "s5",
    )(u_tm, h0_re, h0_im, lbr_flat, lbi_flat, wb, wc, d_skip, wglu)


def _topk_bias(s_ref, n, kk):
    rows = s_ref.shape[0]
    cols = slice(0, n)

    def count(pred):
        return jnp.sum(jnp.where(pred(s_ref[:, cols]), 1.0, 0.0), axis=1, keepdims=True)

    def largest_below(hi):
        s = s_ref[:, cols]
        return jnp.max(jnp.where(s < hi, s, -jnp.inf), axis=1, keepdims=True)

    s0 = s_ref[:, cols]
    smax = jnp.max(s0, axis=1, keepdims=True)
    smin = jnp.min(jnp.where(s0 > -jnp.inf, s0, jnp.inf), axis=1, keepdims=True)

    def bisect(_, c):
        lo, hi = c
        mid = lo + (hi - lo) * 0.5
        ge = count(lambda s: s >= mid) >= kk
        return jnp.where(ge, mid, lo), jnp.where(ge, hi, mid)

    _, hi = lax.fori_loop(0, BISECT_ITERS, bisect, (smin, smax + (smax - smin) + 1.0))
    thr = largest_below(hi)
    cnt = count(lambda s: s >= thr)

    def short(c):
        return jnp.min(c[2] - kk) < 0.0

    def lower(c):
        hi, thr, cnt = c
        hi = jnp.where(cnt < kk, thr, hi)
        thr = largest_below(hi)
        return hi, thr, count(lambda s: s >= thr)

    _, thr, _ = lax.while_loop(short, lower, (hi, thr, cnt))
    need = kk - count(lambda s: s > thr)
    blk = 2 * LANES
    tri = (lax.broadcasted_iota(I32, (blk, blk), 0) <= lax.broadcasted_iota(I32, (blk, blk), 1)).astype(BF16)
    seen = jnp.zeros((rows, 1), F32)
    for c0 in range(0, n, blk):
        w = min(blk, n - c0)
        s = s_ref[:, c0:c0 + w]
        tie = s == thr
        upto = seen + _dot(jnp.where(tie, 1.0, 0.0).astype(BF16), tri[:w, :w])
        seen = upto[:, w - 1:w]
        s_ref[:, c0:c0 + w] = jnp.where((s > thr) | (tie & (upto <= need)), 0.0, NEG_BIG)


def _dsa_prompt_block(q0, rows, n, qim_ref, qm_ref, w_ref, kip_ref, kp_ref, vp_ref, s_ref, acc_ref, topk):
    tq = rows.stop - rows.start
    s_ref = s_ref.at[0:tq]
    cols = slice(0, n)
    lane = lax.broadcasted_iota(I32, (1, LANES), 1)
    s_ref[:, cols] = jnp.zeros((tq, n), F32)

    def score_head(h, carry):
        d = jnp.maximum(_dot_nt(qim_ref[h, rows, :], kip_ref[cols, :]), 0.0)
        s_ref[:, cols] += w_ref[h, rows, :] * d
        return carry

    lax.fori_loop(0, N_IDX_HEADS, score_head, 0)
    qpos = q0 + lax.broadcasted_iota(I32, (tq, 1), 0)
    kpos = lax.broadcasted_iota(I32, (1, n), 1)
    s_ref[:, cols] = jnp.where(kpos <= qpos, s_ref[:, cols], -jnp.inf)
    _topk_bias(s_ref, n, jnp.minimum(qpos + 1, topk).astype(F32))

    def head_pair(sp, carry):
        outs = []
        for hh in range(2):
            lg = _dot_nt(qm_ref[2 * sp + hh, rows, :], kp_ref[sp, cols, :]) + s_ref[:, cols]
            p = jnp.exp(lg - jnp.max(lg, axis=1, keepdims=True))
            pv = _dot(p.astype(BF16), vp_ref[sp, cols, :])
            outs.append(pv[:, :LANES] / pv[:, LANES:LANES + 1])
        acc_ref[sp, rows, :] = jnp.where(lane < HEAD_DIM, outs[0], outs[1])
        return carry

    lax.fori_loop(0, acc_ref.shape[0], head_pair, 0)


def _dsa_prompt_body(qi_ref, kw_ref, q_ref, kib_ref, kb_ref, vb_ref, o_ref,
                     kip_ref, kp_ref, vp_ref, qim_ref, qm_ref, w_ref, s_ref, acc_ref, *, seq, topk):
    tq = q_ref.shape[0]
    n_slab, lp, _ = kp_ref.shape
    j = pl.program_id(1)
    lane = lax.broadcasted_iota(I32, (1, LANES), 1)

    @pl.when(j == 0)
    def _():
        kip_ref[0:seq] = kib_ref[...]
        if lp > seq:
            kip_ref[seq:] = jnp.zeros((lp - seq, LANES), BF16)
        for sp in range(n_slab):
            sl = slice(LANES * sp, LANES * (sp + 1))
            kp_ref[sp, 0:seq] = kb_ref[:, sl]
            vp_ref[sp, 0:seq, 0:LANES] = vb_ref[:, sl]
            vp_ref[sp, :, LANES:] = jnp.where(lane == 0, 1.0, 0.0).astype(BF16) + jnp.zeros((lp, LANES), BF16)
            if lp > seq:
                kp_ref[sp, seq:] = jnp.zeros((lp - seq, LANES), BF16)
                vp_ref[sp, seq:, 0:LANES] = jnp.zeros((lp - seq, LANES), BF16)

    kw = kw_ref[...]
    for h in range(N_IDX_HEADS):
        head_lanes = (lane // IDX_DIM) == (h % 2)
        sl = slice(LANES * (h // 2), LANES * (h // 2 + 1))
        qim_ref[h] = jnp.where(head_lanes, qi_ref[:, sl], jnp.zeros((), BF16))
        qm_ref[h] = jnp.where(head_lanes, q_ref[:, sl], jnp.zeros((), BF16))
        w_ref[h] = kw[:, IDX_DIM + h:IDX_DIM + h + 1]

    split = tq // 2 // (2 * SUBLANES) * (2 * SUBLANES)
    for jj in range(seq // tq):
        @pl.when(j == jj)
        def _(jj=jj):
            for rows in (slice(0, split), slice(split, tq)):
                n = min(-(-(jj * tq + rows.stop) // LANES) * LANES, lp)
                _dsa_prompt_block(jj * tq + rows.start, rows, n, qim_ref, qm_ref, w_ref, kip_ref, kp_ref, vp_ref,
                                  s_ref, acc_ref, topk)

    for sp in range(n_slab):
        o_ref[:, LANES * sp:LANES * (sp + 1)] = acc_ref[sp].astype(BF16)


def _dsa_prompt(qi, kw, q, kib, kb, vb, batch, seq, tq, topk):
    T, d_att = q.shape
    nq = seq // tq
    lp = -(-seq // LANES) * LANES
    n_slab = d_att // LANES
    qrow = lambda w: pl.BlockSpec((tq, w), lambda b, j: (b * nq + j, 0))
    full = lambda w: pl.BlockSpec((None, seq, w), lambda b, j: (b, 0, 0))
    body = functools.partial(_dsa_prompt_body, seq=seq, topk=topk)
    return pl.pallas_call(
        body, grid=(batch, nq),
        in_specs=[qrow(qi.shape[1]), qrow(LANES), qrow(d_att), full(LANES), full(d_att), full(d_att)],
        out_specs=qrow(d_att),
        out_shape=jax.ShapeDtypeStruct((T, d_att), BF16),
        scratch_shapes=[pltpu.VMEM((lp, LANES), BF16), pltpu.VMEM((n_slab, lp, LANES), BF16),
                        pltpu.VMEM((n_slab, lp, 2 * LANES), BF16),
                        pltpu.VMEM((N_IDX_HEADS, tq, LANES), BF16), pltpu.VMEM((N_HEADS, tq, LANES), BF16),
                        pltpu.VMEM((N_IDX_HEADS, tq, 1), F32),
                        pltpu.VMEM((tq, lp), F32), pltpu.VMEM((n_slab, tq, LANES), F32)],
        compiler_params=_params(("arbitrary", "arbitrary")), name="dsa_prompt",
    )(qi, kw, q, kib.reshape(batch, seq, LANES), kb.reshape(batch, seq, d_att), vb.reshape(batch, seq, d_att))


def _dsa_sample_scores_body(pt_ref, qi_ref, w_ref, kin_ref, *refs):
    del pt_ref
    page_refs, s_ref = refs[:-1], refs[-1]
    page = page_refs[0].shape[1]
    qm = qi_ref[...]
    w = w_ref[...]
    kt = jnp.concatenate([r[...].astype(BF16) for r in page_refs], axis=1)
    s = jnp.sum(w * jnp.maximum(_dot(qm, kt), 0.0), axis=0, keepdims=True)
    kin = kin_ref[...].astype(BF16).astype(F32)
    d_new = jnp.maximum(jnp.sum(qm.astype(F32) * kin, axis=1, keepdims=True), 0.0)
    s_new = jnp.sum(w * d_new, axis=0, keepdims=True)
    lane = lax.broadcasted_iota(I32, (1, page), 1)
    s_ref[...] = jnp.concatenate([s, jnp.where(lane == 0, s_new, 0.0)], axis=1)


def _dsa_sample_scores(page_table, qi3, w3, ki_new3, kidx_t):
    nb, n_pages = page_table.shape
    idx_dim, page = kidx_t.shape[1:]
    width = (n_pages + 1) * page
    per = lambda shape: pl.BlockSpec((None,) + shape, lambda b, pt: (b, 0, 0))
    page_specs = [pl.BlockSpec((None, idx_dim, page), functools.partial(lambda b, pt, p: (pt[b, p], 0, 0), p=p))
                  for p in range(n_pages)]
    grid_spec = pltpu.PrefetchScalarGridSpec(
        num_scalar_prefetch=1, grid=(nb,),
        in_specs=[per(qi3.shape[1:]), per(w3.shape[1:]), per(ki_new3.shape[1:])] + page_specs,
        out_specs=per((1, width)))
    return pl.pallas_call(
        _dsa_sample_scores_body, grid_spec=grid_spec,
        out_shape=jax.ShapeDtypeStruct((nb, 1, width), F32),
        compiler_params=_params(("arbitrary",)), name="dsa_sample_scores",
    )(page_table, qi3, w3, ki_new3, *([kidx_t] * n_pages))


def _dsa_sample_select_body(s_ref, b_ref, *, n_keys, topk):
    rows, width = s_ref.shape
    pos = lax.broadcasted_iota(I32, (1, width), 1)
    b_ref[...] = jnp.where(pos < n_keys, s_ref[...], -jnp.inf)
    _topk_bias(b_ref, width, jnp.full((rows, 1), float(min(topk, n_keys)), F32))


def _dsa_sample_select(scores, n_keys, topk):
    rows, width = scores.shape
    body = functools.partial(_dsa_sample_select_body, n_keys=n_keys, topk=topk)
    return pl.pallas_call(
        body, out_shape=jax.ShapeDtypeStruct((rows, width), F32),
        compiler_params=pltpu.CompilerParams(vmem_limit_bytes=VMEM_LIMIT), name="dsa_sample_select",
    )(scores)


def _dsa_sample_attend_body(pt_ref, q_ref, kn_ref, vn_ref, b_ref, *refs):
    del pt_ref
    o_ref, lg_ref = refs[-2], refs[-1]
    n_pages = (len(refs) - 2) // 2
    k_refs, v_refs = refs[:n_pages], refs[n_pages:2 * n_pages]
    n_heads, hd, page = k_refs[0].shape
    lane = lax.broadcasted_iota(I32, (1, page), 1)
    bias = b_ref[...]
    q, kn, vn = q_ref[...], kn_ref[...], vn_ref[...]
    for h in range(n_heads):
        col = slice(h, h + 1)
        qc = jnp.broadcast_to(q[:, col], (hd, page))
        for p in range(n_pages):
            lg_ref[p:p + 1, :] = jnp.sum(qc * k_refs[p][h], axis=0, keepdims=True)
        lg_new = jnp.sum(q[:, col] * kn[:, col], axis=0, keepdims=True)
        lg_ref[n_pages:n_pages + 1, :] = jnp.where(lane == 0, lg_new, 0.0)
        lg = lg_ref[...] + bias
        m = jnp.max(jnp.max(lg, axis=1, keepdims=True), axis=0, keepdims=True)
        pr = jnp.exp(lg - m)
        den = jnp.sum(jnp.sum(pr, axis=1, keepdims=True), axis=0, keepdims=True)
        acc = jnp.zeros((hd, page), F32)
        for p in range(n_pages):
            acc = acc + pr[p:p + 1, :] * v_refs[p][h]
        out = jnp.sum(acc, axis=1, keepdims=True) + pr[n_pages:n_pages + 1, 0:1] * vn[:, col]
        o_ref[:, col] = out / den


def _dsa_sample_attend(page_table, q_col, kn_col, vn_col, bias3, k_t, v_t):
    nb, n_pages = page_table.shape
    n_heads, hd, page = k_t.shape[1:]
    per = lambda shape: pl.BlockSpec((None,) + shape, lambda b, pt: (b, 0, 0))
    page_specs = [pl.BlockSpec((None, n_heads, hd, page), functools.partial(lambda b, pt, p: (pt[b, p], 0, 0, 0), p=p))
                  for p in range(n_pages)]
    grid_spec = pltpu.PrefetchScalarGridSpec(
        num_scalar_prefetch=1, grid=(nb,),
        in_specs=[per(q_col.shape[1:]), per(kn_col.shape[1:]), per(vn_col.shape[1:]), per(bias3.shape[1:])] + page_specs * 2,
        out_specs=per(q_col.shape[1:]),
        scratch_shapes=[pltpu.VMEM((n_pages + 1, page), F32)])
    return pl.pallas_call(
        _dsa_sample_attend_body, grid_spec=grid_spec,
        out_shape=jax.ShapeDtypeStruct(q_col.shape, F32),
        compiler_params=_params(("arbitrary",)), name="dsa_sample_attend",
    )(page_table, q_col, kn_col, vn_col, bias3, *([k_t] * n_pages), *([v_t] * n_pages))


def _swiglu(h, wg_ref, wu_ref, wd_ref, n_chunks):
    fc = wg_ref.shape[-1] // n_chunks
    y = None
    for c in range(n_chunks):
        cols = slice(fc * c, fc * (c + 1))
        gate = _dot(h, wg_ref[:, cols])
        act = (gate * jax.nn.sigmoid(gate) * _dot(h, wu_ref[:, cols])).astype(BF16)
        part = _dot(act, wd_ref[cols, :])
        y = part if y is None else y + part
    return y


def _outproj_ffn_body(x_ref, ys_ref, ya_ref, wos_ref, woa_ref, g_ref, wg_ref, wu_ref, wd_ref, o_ref):
    x = x_ref[...] + _dot(ys_ref[...], wos_ref[...]) + _dot(ya_ref[...], woa_ref[...])
    h = _rms(x, g_ref[...]).astype(BF16)
    o_ref[...] = x + _swiglu(h, wg_ref, wu_ref, wd_ref, 2)


def _outproj_ffn(x3, ys, ya, wos, woa, g, wg, wu, wd, seq, tm):
    nb, _, D = x3.shape
    nblk = seq // tm
    row = lambda w: pl.BlockSpec((tm, w), lambda b, j: (b * nblk + j, 0))
    single = lambda a: pl.BlockSpec(a.shape, lambda b, j: (0,) * a.ndim, pipeline_mode=pl.Buffered(1))
    return pl.pallas_call(
        _outproj_ffn_body, grid=(nb, nblk),
        in_specs=[pl.BlockSpec((None, tm, D), lambda b, j: (b, j, 0)), row(ys.shape[1]), row(ya.shape[1]),
                  single(wos), single(woa), single(g), single(wg), single(wu), single(wd)],
        out_specs=row(D), out_shape=jax.ShapeDtypeStruct((nb * seq, D), F32),
        compiler_params=_params(("arbitrary", "arbitrary")), name="outproj_ffn",
    )(x3, ys, ya, wos, woa, g, wg, wu, wd)


def _pool_mix(h, window_sum, divisor, pw_ref, scale):
    gd = h.shape[1] // len(POOL_WINDOWS)
    cols = []
    for g, w in enumerate(POOL_WINDOWS):
        pooled = window_sum(g, w) / divisor(w) - h[:, gd * g:gd * (g + 1)]
        cols.append(_dot(pooled.astype(BF16), pw_ref[g]))
    return jnp.concatenate(cols, axis=1) * scale


def _pool_prompt_body(x_ref, g_ref, pw_ref, sc_ref, o_ref, hist_ref, ext_ref):
    tm, D = x_ref.shape
    gd = D // len(POOL_WINDOWS)
    halo = POOL_HIST + 1
    j = pl.program_id(1)

    @pl.when(j == 0)
    def _():
        ext_ref[0:halo] = jnp.zeros((halo, D), F32)

    x = x_ref[...]
    h = _rms(x, g_ref[...])
    ext_ref[halo:] = h
    pos = (j * tm + lax.broadcasted_iota(I32, (tm, 1), 0)).astype(F32)

    def window_sum(g, w):
        acc = h[:, gd * g:gd * (g + 1)]
        for k in range(1, w):
            acc = acc + ext_ref[halo - k:halo - k + tm, gd * g:gd * (g + 1)]
        return acc

    mixed = _pool_mix(h, window_sum, lambda w: jnp.minimum(float(w), pos + 1.0), pw_ref, sc_ref[...])
    o_ref[...] = x + mixed
    ext_ref[0:halo] = ext_ref[tm:tm + halo]

    @pl.when(j == pl.num_programs(1) - 1)
    def _():
        hist_ref[...] = h[tm - POOL_HIST:, :]


def _pool_prompt(x2d, g, pw, scale, batch, seq, tm):
    T, D = x2d.shape
    nblk = seq // tm
    row = pl.BlockSpec((tm, D), lambda b, j: (b * nblk + j, 0))
    return pl.pallas_call(
        _pool_prompt_body, grid=(batch, nblk),
        in_specs=[row, _const_spec(g.shape), _const_spec(pw.shape), _const_spec(scale.shape)],
        out_specs=(row, pl.BlockSpec((None, POOL_HIST, D), lambda b, j: (b, 0, 0))),
        out_shape=(jax.ShapeDtypeStruct((T, D), F32), jax.ShapeDtypeStruct((batch, POOL_HIST, D), F32)),
        scratch_shapes=[pltpu.VMEM((tm + POOL_HIST + 1, D), F32)],
        compiler_params=_params(("arbitrary", "arbitrary")), name="pool_prompt",
    )(x2d, g, pw, scale)


def _pool_sample_body(x_ref, hist_ref, g_ref, pw_ref, sc_ref, o_ref, nh_ref):
    D = x_ref.shape[1]
    gd = D // len(POOL_WINDOWS)
    x = x_ref[...]
    h = _rms(x, g_ref[...])

    def window_sum(g, w):
        acc = h[:, gd * g:gd * (g + 1)]
        for k in range(1, w):
            acc = acc + hist_ref[POOL_HIST - k, :, gd * g:gd * (g + 1)]
        return acc

    mixed = _pool_mix(h, window_sum, float, pw_ref, sc_ref[...])
    o_ref[...] = x + mixed
    for i in range(POOL_HIST - 1):
        nh_ref[i] = hist_ref[i + 1]
    nh_ref[POOL_HIST - 1] = h


def _pool_sample(x2d, hist_t, g, pw, scale):
    sds = jax.ShapeDtypeStruct
    return pl.pallas_call(
        _pool_sample_body, out_shape=(sds(x2d.shape, F32), sds(hist_t.shape, F32)),
        compiler_params=pltpu.CompilerParams(vmem_limit_bytes=VMEM_LIMIT), name="pool_sample",
    )(x2d, hist_t, g, pw, scale)


def _store_row_tiles(ref, x):
    for s in range(ref.shape[1]):
        ref[:, s, :] = x[:, LANES * s:LANES * (s + 1)]


def _load_row_tiles(ref):
    return jnp.concatenate([ref[:, s, :] for s in range(ref.shape[1])], axis=1)


def _router_body(x_ref, g_ref, wr_ref, h_ref, sel_ref, *, n_experts):
    h = _rms(x_ref[...], g_ref[...])
    _store_row_tiles(h_ref, h)
    w = wr_ref[...]
    h_hi, w_hi = h.astype(BF16), w.astype(BF16)
    h_lo, w_lo = (h - h_hi.astype(F32)).astype(BF16), (w - w_hi.astype(F32)).astype(BF16)
    logits = _dot(h_hi, w_hi) + (_dot(h_hi, w_lo) + _dot(h_lo, w_hi))
    lane = lax.broadcasted_iota(I32, logits.shape, 1).astype(F32)
    logits = jnp.where(lane < n_experts, logits, -jnp.inf)
    v1 = jnp.max(logits, axis=1, keepdims=True)
    i1 = jnp.min(jnp.where(logits == v1, lane, float(LANES)), axis=1, keepdims=True)
    rest = jnp.where(lane == i1, -jnp.inf, logits)
    v2 = jnp.max(rest, axis=1, keepdims=True)
    i2 = jnp.min(jnp.where(rest == v2, lane, float(LANES)), axis=1, keepdims=True)
    e2 = jnp.exp(v2 - v1)
    den = 1.0 + e2
    sel_ref[...] = jnp.where(lane == 0.0, i1, jnp.where(lane == 1.0, i2, jnp.where(lane == 2.0, 1.0 / den, e2 / den)))


def _router(x2d, g, wr_pad, n_experts, tm):
    T, D = x2d.shape
    row = lambda w: pl.BlockSpec((tm, w), lambda i: (i, 0))
    body = functools.partial(_router_body, n_experts=n_experts)
    return pl.pallas_call(
        body, grid=(T // tm,),
        in_specs=[row(D), _const_spec(g.shape), _const_spec(wr_pad.shape)],
        out_specs=(pl.BlockSpec((tm, D // LANES, LANES), lambda i: (i, 0, 0)), row(LANES)),
        out_shape=(jax.ShapeDtypeStruct((T, D // LANES, LANES), F32), jax.ShapeDtypeStruct((T, LANES), F32)),
        compiler_params=_params(("arbitrary",)), name="router",
    )(x2d, g, wr_pad)


def _row_gather(idx_ref, base, src_hbm, dst, sem):
    assert dst.shape[0] % DMA_UNROLL == 0

    def issue(g, carry):
        for k in range(DMA_UNROLL):
            r = g * DMA_UNROLL + k
            pltpu.make_async_copy(src_hbm.at[idx_ref[base + r]], dst.at[r], sem).start(priority=k % 2)
        return carry

    lax.fori_loop(0, dst.shape[0] // DMA_UNROLL, issue, 0)


def _row_gather_wait(src_hbm, dst, sem):
    pltpu.make_async_copy(src_hbm.at[pl.ds(0, dst.shape[0])], dst, sem).wait()


def _moe_dispatch_body(p1_ref, p2_ref, ends_ref, h_ref, x_hbm, zero_ref, stage, sems, *, tm):
    i = pl.program_id(0)
    last = pl.num_programs(0) - 1
    gt = zero_ref.shape[0]
    n_groups = ends_ref.shape[0]
    zsem = sems.at[2]
    slot = i % 2

    def fill_tile(e, act):
        if e < n_groups:
            start = ends_ref[e] - gt
            exists = ends_ref[e] > (ends_ref[e - 1] if e else 0)
        else:
            start = ends_ref[n_groups - 1] + (e - n_groups) * gt
            exists = start < x_hbm.shape[0]

        @pl.when(exists)
        def _():
            act(pltpu.make_async_copy(zero_ref, x_hbm.at[pl.ds(pl.multiple_of(start, gt), gt)], zsem))

    @pl.when(i == 0)
    def _():
        zero_ref[...] = jnp.zeros(zero_ref.shape, F32)
        for e in range(2 * n_groups):
            fill_tile(e, lambda cp: cp.start())
        for e in range(2 * n_groups):
            fill_tile(e, lambda cp: cp.wait())

    def wait_rows(s):
        for _ in range(2):
            pltpu.make_async_copy(stage.at[s], x_hbm.at[pl.ds(0, tm)], sems.at[s]).wait()

    @pl.when(i >= 2)
    def _():
        wait_rows(slot)

    stage[slot] = h_ref[...]

    def issue(r, carry):
        t = i * tm + r
        pltpu.make_async_copy(stage.at[slot, r], x_hbm.at[p1_ref[t]], sems.at[slot]).start(priority=0)
        pltpu.make_async_copy(stage.at[slot, r], x_hbm.at[p2_ref[t]], sems.at[slot]).start(priority=1)
        return carry

    lax.fori_loop(0, tm, issue, 0, unroll=DMA_UNROLL)

    @pl.when(i == last)
    def _():
        wait_rows(slot)

    @pl.when((i == last) & (i >= 1))
    def _():
        wait_rows(1 - slot)


def _moe_dispatch(pos1, pos2, ends, h_tiles, n_rows, tm, group_tile):
    T, S, _ = h_tiles.shape
    grid_spec = pltpu.PrefetchScalarGridSpec(
        num_scalar_prefetch=3, grid=(T // tm,),
        in_specs=[pl.BlockSpec((tm, S, LANES), lambda i, p1, p2, en: (i, 0, 0))],
        out_specs=pl.BlockSpec(memory_space=pl.ANY),
        scratch_shapes=[pltpu.VMEM((group_tile, S, LANES), F32), pltpu.VMEM((2, tm, S, LANES), F32),
                        pltpu.SemaphoreType.DMA((3,))])
    return pl.pallas_call(
        functools.partial(_moe_dispatch_body, tm=tm), grid_spec=grid_spec,
        out_shape=jax.ShapeDtypeStruct((n_rows, S, LANES), F32),
        compiler_params=pltpu.CompilerParams(dimension_semantics=("arbitrary",), vmem_limit_bytes=VMEM_LIMIT,
                                             disable_bounds_checks=True),
        name="moe_dispatch",
    )(pos1, pos2, ends, h_tiles)


def _moe_group_body(te_ref, nu_ref, x_ref, wg_ref, wu_ref, wd_ref, y_ref):
    del te_ref
    i = pl.program_id(0)

    @pl.when(i < nu_ref[0])
    def _():
        _store_row_tiles(y_ref, _swiglu(_load_row_tiles(x_ref).astype(BF16), wg_ref, wu_ref, wd_ref, 2))

    @pl.when(i >= nu_ref[0])
    def _():
        y_ref[...] = jnp.zeros(y_ref.shape, F32)


def _moe_group(tile_expert, n_used, x_tiles, wg, wu, wd, tm):
    P, S, _ = x_tiles.shape
    _, D, F = wg.shape
    wspec = lambda a, b: pl.BlockSpec((None, a, b), lambda i, te, nu: (te[i], 0, 0))
    grid_spec = pltpu.PrefetchScalarGridSpec(
        num_scalar_prefetch=2, grid=(P // tm,),
        in_specs=[pl.BlockSpec((tm, S, LANES), lambda i, te, nu: (jnp.minimum(i, nu[0] - 1), 0, 0)),
                  wspec(D, F), wspec(D, F), wspec(F, D)],
        out_specs=pl.BlockSpec((tm, S, LANES), lambda i, te, nu: (i, 0, 0)))
    return pl.pallas_call(
        _moe_group_body, grid_spec=grid_spec, out_shape=jax.ShapeDtypeStruct((P, S, LANES), F32),
        compiler_params=_params(("arbitrary",)), name="moe_group",
    )(tile_expert, n_used, x_tiles, wg, wu, wd)


def _moe_combine_body(p1_ref, p2_ref, dst_ref, x_ref, sel_ref, y_hbm, ya_hbm, yb_hbm, buf, obuf, sem, osem):
    i = pl.program_id(0)
    last = pl.num_programs(0) - 1
    tm = x_ref.shape[0]
    n_chunks = tm // OUT_CHUNK
    rows_a = ya_hbm.shape[0]
    slot = i % 2

    def out_copies(step, s, act):
        for c in range(n_chunks):
            dst = dst_ref[step * n_chunks + c]
            src = obuf.at[s, pl.ds(c * OUT_CHUNK, OUT_CHUNK)]

            @pl.when((dst >= 0) & (dst < rows_a))
            def _():
                act(pltpu.make_async_copy(src, ya_hbm.at[pl.ds(pl.multiple_of(dst, OUT_CHUNK), OUT_CHUNK)], osem.at[s]))

            @pl.when(dst >= rows_a)
            def _():
                off = pl.multiple_of(dst - rows_a, OUT_CHUNK)
                act(pltpu.make_async_copy(src, yb_hbm.at[pl.ds(off, OUT_CHUNK)], osem.at[s]))

    def gather(tile, s):
        _row_gather(p1_ref, tile * tm, y_hbm, buf.at[s, 0], sem.at[s, 0])
        _row_gather(p2_ref, tile * tm, y_hbm, buf.at[s, 1], sem.at[s, 1])

    @pl.when(i == 0)
    def _():
        gather(0, 0)

    @pl.when(i + 1 < pl.num_programs(0))
    def _():
        gather(i + 1, 1 - slot)

    _row_gather_wait(y_hbm, buf.at[slot, 0], sem.at[slot, 0])
    _row_gather_wait(y_hbm, buf.at[slot, 1], sem.at[slot, 1])
    @pl.when(i >= 2)
    def _():
        out_copies(i - 2, slot, lambda cp: cp.wait())

    sel = sel_ref[...]
    obuf[slot] = (x_ref[...] + sel[:, 2:3] * _load_row_tiles(buf.at[slot, 0])
                  + sel[:, 3:4] * _load_row_tiles(buf.at[slot, 1]))
    out_copies(i, slot, lambda cp: cp.start())

    @pl.when(i == last)
    def _():
        out_copies(i, slot, lambda cp: cp.wait())

    @pl.when((i == last) & (i >= 1))
    def _():
        out_copies(i - 1, 1 - slot, lambda cp: cp.wait())


def _moe_combine(pos1, pos2, chunk_dst, x2d, sel, y_tiles, tm, rows_a, rows_b):
    T, D = x2d.shape
    S = y_tiles.shape[1]
    assert tm % OUT_CHUNK == 0 and rows_a % OUT_CHUNK == 0 and rows_b % OUT_CHUNK == 0
    row = lambda w: pl.BlockSpec((tm, w), lambda i, p1, p2, cd: (i, 0))
    hbm = pl.BlockSpec(memory_space=pl.ANY)
    grid_spec = pltpu.PrefetchScalarGridSpec(
        num_scalar_prefetch=3, grid=(T // tm,),
        in_specs=[row(D), row(LANES), hbm], out_specs=(hbm, hbm),
        scratch_shapes=[pltpu.VMEM((2, 2, tm, S, LANES), F32), pltpu.VMEM((2, tm, D), F32),
                        pltpu.SemaphoreType.DMA((2, 2)), pltpu.SemaphoreType.DMA((2,))])
    return pl.pallas_call(
        _moe_combine_body, grid_spec=grid_spec,
        out_shape=(jax.ShapeDtypeStruct((rows_a, D), F32), jax.ShapeDtypeStruct((rows_b, D), F32)),
        compiler_params=pltpu.CompilerParams(dimension_semantics=("arbitrary",), vmem_limit_bytes=VMEM_LIMIT,
                                             disable_bounds_checks=True),
        name="moe_combine",
    )(pos1, pos2, chunk_dst, x2d, sel, y_tiles)


def _moe_plan(sel, n_experts, tm):
    T = sel.shape[0]
    experts = jnp.concatenate([sel[:, 0], sel[:, 1]]).astype(I32)
    onehot = (experts[:, None] == jnp.arange(n_experts, dtype=I32)[None, :]).astype(I32)
    rank = jnp.cumsum(onehot, axis=0) - onehot
    counts = jnp.sum(onehot, axis=0)
    padded = (counts + tm - 1) // tm * tm
    ends = jnp.cumsum(padded)
    pos = jnp.sum(onehot * ((ends - padded)[None, :] + rank), axis=1)
    n_rows = (2 * T + n_experts * (tm - 1)) // tm * tm
    tile_start = jnp.arange(n_rows // tm, dtype=I32) * tm
    tile_expert = jnp.minimum(jnp.sum((tile_start[:, None] >= ends[None, :]).astype(I32), axis=1), n_experts - 1)
    n_used = (ends[-1] // tm).astype(I32).reshape(1)
    return tile_expert, n_used, ends.astype(I32), n_rows, pos[:T], pos[T:]


def _rope_tables(pos):
    half = HEAD_DIM // 2
    inv = ROPE_THETA ** (-jnp.arange(half, dtype=F32) / half)
    ang = pos[:, None] * inv[None, :]
    cos, sin = jnp.cos(ang), jnp.sin(ang)
    reps = LANES // HEAD_DIM
    return jnp.tile(jnp.concatenate([cos, cos], axis=1), (1, reps)), jnp.tile(jnp.concatenate([-sin, sin], axis=1), (1, reps))


def kernel(x_prompt, x_sample, cache_k, cache_v, cache_kidx, page_table, state_ssm_re, state_ssm_im, state_pool, meta_tokens, norm_mix0, w_in0, q_norm, k_norm, ssm_lambda_re, ssm_lambda_im, ssm_log_dt, ssm_b_re, ssm_b_im, ssm_c_re, ssm_c_im, ssm_d, ssm_w_glu, w_out0, norm_ffn0, ffn_w_gate, ffn_w_up, ffn_w_down, norm_mix1, pool_w, pool_scale, norm_ffn1, router_w, moe_w_gate, moe_w_up, moe_w_down):
    B, S, D = x_prompt.shape
    DB, DS, _ = x_sample.shape
    assert DS == 1, "one new token per sample sequence"
    L = S + N_META
    n_pool, page = cache_k.shape[1], cache_k.shape[2]
    n_pages = page_table.shape[1]
    past = n_pages * page
    d_att = N_HEADS * HEAD_DIM
    d_qi = N_IDX_HEADS * IDX_DIM
    G, P = ssm_lambda_re.shape[1:]
    d_ssm = G * SSM_GROUP
    n_state = G * P
    E = router_w.shape[-1]
    topk_p = min(TOPK_MAX, S // 4)
    topk_s = min(TOPK_MAX, (past + DS) // 4)
    tm = _row_block(L, 1024)
    assert tm >= POOL_HIST + 1

    tm_in = LANES * (-(-L // (3 * LANES)))
    l_pad = -(-L // tm_in) * tm_in
    x_p = jnp.concatenate([jnp.broadcast_to(meta_tokens[None], (B, N_META, D)), x_prompt,
                           jnp.zeros((B, l_pad - L, D), x_prompt.dtype)], axis=1)
    x_s = x_sample.reshape(1, DB, D)

    w_in = w_in0[0]
    o1, o2, o3, o4, o5, o6 = d_ssm, d_ssm + d_att, d_ssm + 2 * d_att, d_ssm + 3 * d_att, d_ssm + 3 * d_att + d_qi, d_ssm + 3 * d_att + d_qi + IDX_DIM
    w_kw = jnp.pad(w_in[:, o5:], ((0, 0), (0, LANES - (w_in.shape[1] - o5))))
    ws = tuple(w.astype(BF16) for w in (w_in[:, :o1], w_in[:, o1:o2], w_in[:, o2:o3], w_in[:, o3:o4], w_in[:, o4:o5], w_kw))
    g_mix0 = norm_mix0[0].reshape(1, D)
    qn = jnp.tile(q_norm[0], N_HEADS).reshape(1, d_att)
    kn = jnp.tile(k_norm[0], N_HEADS).reshape(1, d_att)
    head_of = jnp.arange(d_att) // HEAD_DIM
    hm = jnp.where(head_of[:, None] == head_of[None, :], 1.0 / HEAD_DIM, 0.0).astype(BF16)
    cos_p, sin_p = _rope_tables(jnp.arange(l_pad, dtype=F32))
    cos_s, sin_s = _rope_tables(jnp.full((DB,), float(past), F32))

    flat = lambda a: a.reshape(-1, a.shape[-1])
    u_p, q_p, kt_p, vt_p, qi_p, kit_p, kw_p, kb_p, vb_p, kib_p = _inproj(
        x_p, cos_p, sin_p, tm_in, L, g_mix0, ws, qn, kn, hm, True)
    u_p, q_p, qi_p, kw_p, kb_p, vb_p, kib_p = (flat(a) for a in (u_p, q_p, qi_p, kw_p, kb_p, vb_p, kib_p))
    u_s, q_s, k_s, v_s, qi_s, ki_s, kw_s, _, _, _ = (
        flat(a) for a in _inproj(x_s, cos_s, sin_s, DB, DB, g_mix0, ws, qn, kn, hm, False))

    lbr, lbi, bbr, bbi = _s5_prep(ssm_lambda_re[0], ssm_lambda_im[0], ssm_log_dt[0], ssm_b_re[0], ssm_b_im[0])
    gps = LANES // SSM_GROUP
    n_slab = G // gps
    eye = jnp.eye(gps, dtype=F32)

    def b_slabs(bt):
        return jnp.einsum('sgcp,gh->sgchp', bt.reshape(n_slab, gps, SSM_GROUP, P), eye).reshape(n_slab, LANES, gps * P)

    def c_slabs(c):
        return jnp.einsum('sgcp,gh->sgphc', c.reshape(n_slab, gps, SSM_GROUP, P), eye).reshape(n_slab, gps * P, LANES)

    wb = jnp.concatenate([b_slabs(bbr), b_slabs(bbi)], axis=2)
    wc = jnp.stack([c_slabs(ssm_c_re[0]), -c_slabs(ssm_c_im[0])], axis=1).astype(BF16)
    lbr_f, lbi_f = lbr.reshape(1, n_state), lbi.reshape(1, n_state)
    d_skip = ssm_d[0].reshape(1, d_ssm)
    wglu = ssm_w_glu[0].astype(BF16)
    assert B % SUBLANES == 0, "the S5 scan keeps one batch row per sublane"
    steps = max(d for d in range(1, L + 1) if L % d == 0 and d * B <= 512 and (d * B) % (2 * SUBLANES) == 0)
    u_tm = u_p.reshape(B, L, d_ssm).swapaxes(0, 1).reshape(L * B, d_ssm)
    zero_h = jnp.zeros((B, n_state), F32)
    ys_tm, hre_p, him_p = _s5(u_tm, zero_h, zero_h, lbr_f, lbi_f, wb, wc, d_skip, wglu, B, steps, False)
    ys_p = ys_tm.reshape(L, B, d_ssm).swapaxes(0, 1).reshape(B * L, d_ssm)
    ys_s, hre_s, him_s = _s5(u_s, state_ssm_re[0].reshape(DB, n_state), state_ssm_im[0].reshape(DB, n_state),
                             lbr_f, lbi_f, wb, wc, d_skip, wglu, DB, 1, True)

    ya_p = _dsa_prompt(qi_p, kw_p, q_p, kib_p, kb_p, vb_p, B, L, tm, topk_p)
    assert page == LANES
    w3 = kw_s[:, IDX_DIM:IDX_DIM + N_IDX_HEADS].reshape(DB, N_IDX_HEADS, 1)
    def head_cols(a, n_heads):
        return a.astype(F32).reshape(DB, n_heads, -1).swapaxes(1, 2)

    scores = _dsa_sample_scores(page_table, qi_s.reshape(DB, N_IDX_HEADS, IDX_DIM), w3, ki_s.reshape(DB, 1, IDX_DIM),
                                jnp.transpose(cache_kidx[0], (0, 2, 1)))
    bias = _dsa_sample_select(scores.reshape(DB, (n_pages + 1) * page), past + DS, topk_s)
    ya_s = _dsa_sample_attend(page_table, head_cols(q_s, N_HEADS), head_cols(k_s, N_HEADS), head_cols(v_s, N_HEADS),
                              bias.reshape(DB, n_pages + 1, page),
                              jnp.transpose(cache_k[0], (0, 2, 3, 1)), jnp.transpose(cache_v[0], (0, 2, 3, 1)))
    ya_s = ya_s.swapaxes(1, 2).reshape(DB, d_att).astype(BF16)

    w_out = w_out0[0].astype(BF16)
    ffn = (w_out[:d_ssm], w_out[d_ssm:], norm_ffn0[0].reshape(1, D), ffn_w_gate[0].astype(BF16),
           ffn_w_up[0].astype(BF16), ffn_w_down[0].astype(BF16))
    x_p = _outproj_ffn(x_p, ys_p, ya_p, *ffn, L, tm)
    x_s = _outproj_ffn(x_s, ys_s, ya_s, *ffn, DB, DB)

    g_mix1 = norm_mix1[0].reshape(1, D)
    pw = pool_w[0].astype(BF16)
    psc = pool_scale[0].reshape(1, D)
    T_all = B * L + DB
    x_p, pool_p = _pool_prompt(x_p, g_mix1, pw, psc, B, L, tm)
    x_s, hist_t = _pool_sample(x_s, state_pool[0].swapaxes(0, 1), g_mix1, pw, psc)
    pool_s = hist_t.swapaxes(0, 1)
    x_all = jnp.concatenate([x_p, x_s], axis=0)

    g_ffn1 = norm_ffn1[0].reshape(1, D)
    wr_pad = jnp.pad(router_w[0], ((0, 0), (0, LANES - E)))
    wg, wu, wd = moe_w_gate[0].astype(BF16), moe_w_up[0].astype(BF16), moe_w_down[0].astype(BF16)
    tr = _row_block(T_all, MOE_TILE)
    h_tiles, sel = _router(x_all, g_ffn1, wr_pad, E, tr)
    tile_expert, n_used, ends, n_rows, pos1, pos2 = _moe_plan(sel, E, MOE_TILE)
    x_tiles = _moe_dispatch(pos1, pos2, ends, h_tiles, n_rows, tr, MOE_TILE)
    y_tiles = _moe_group(tile_expert, n_used, x_tiles, wg, wu, wd, MOE_TILE)
    assert N_META % OUT_CHUNK == 0 and L % OUT_CHUNK == 0 and DB % OUT_CHUNK == 0
    row0 = jnp.arange(T_all // OUT_CHUNK, dtype=I32) * OUT_CHUNK
    in_seq = row0 % L
    chunk_dst = jnp.where(row0 >= B * L, row0 - B * N_META,
                          jnp.where(in_seq < N_META, -1, (row0 // L) * S + in_seq - N_META))
    y_prompt, y_sample = _moe_combine(pos1, pos2, chunk_dst, x_all, sel, y_tiles, tr, B * S, DB)
    y_prompt = y_prompt.reshape(B, S, D)
    y_sample = y_sample.reshape(DB, 1, D)
    k_prompt = jnp.transpose(kt_p, (0, 3, 1, 2))[None]
    v_prompt = jnp.transpose(vt_p, (0, 3, 1, 2))[None]
    kidx_prompt = jnp.transpose(kit_p, (0, 2, 1))[None]
    return (y_prompt, y_sample, k_prompt, v_prompt, kidx_prompt,
            hre_p.reshape(1, B, G, P), him_p.reshape(1, B, G, P), pool_p[None],
            k_s.reshape(1, DB, 1, N_HEADS, HEAD_DIM), v_s.reshape(1, DB, 1, N_HEADS, HEAD_DIM),
            ki_s.reshape(1, DB, 1, IDX_DIM), hre_s.reshape(1, DB, G, P), him_s.reshape(1, DB, G, P), pool_s[None])
```

```python
import functools
import math

import jax
import jax.numpy as jnp
from jax import lax
from jax.experimental import pallas as pl
from jax.experimental.pallas import tpu as pltpu

F32 = jnp.float32
BF16 = jnp.bfloat16
I32 = jnp.int32

N_META = 16
SSM_GROUP = 16
SSM_STATE = 64
N_HEADS = 8
HEAD_DIM = 64
N_IDX_HEADS = 8
IDX_DIM = 64
TOPK_MAX = 256
ROPE_THETA = 10000.0
POOL_WINDOWS = (2, 4, 8, 16)
POOL_HIST = max(POOL_WINDOWS) - 1
TOP_K_EXPERTS = 2
EPS = 1e-6

LANES = 128
SUBLANES = 8
VMEM_LIMIT = 56 * 1024 * 1024
DMA_UNROLL = 8
OUT_CHUNK = 2 * SUBLANES
MOE_TILE = 256
BISECT_ITERS = 16
NEG_BIG = -1e30


def _params(sem):
    return pltpu.CompilerParams(dimension_semantics=sem, vmem_limit_bytes=VMEM_LIMIT)


def _row_block(n, cap, mult=2 * SUBLANES):
    best = None
    for d in range(mult, min(n, cap) + 1, mult):
        if n % d == 0:
            best = d
    assert best is not None, n
    return best


def _rms(x, g):
    return x * lax.rsqrt(jnp.mean(x * x, axis=-1, keepdims=True) + EPS) * g


def _dot(a, b):
    return jnp.dot(a, b, preferred_element_type=F32)


def _dot_nt(a, b):
    return lax.dot_general(a, b, (((1,), (1,)), ((), ())), preferred_element_type=F32)


def _const_spec(shape):
    nd = len(shape)
    return pl.BlockSpec(shape, lambda *_: (0,) * nd)


def _inproj_body(x_ref, g_ref, wu_ref, wq_ref, wk_ref, wv_ref, wqi_ref, wkw_ref, qn_ref, kn_ref, hm_ref,
                 cos_ref, sin_ref,
                 u_ref, q_ref, k_ref, v_ref, qi_ref, ki_ref, kw_ref, kb_ref, vb_ref, kib_ref, *, transposed):
    def store_kv(ref, z):
        if not transposed:
            ref[...] = z
            return
        per_slab = LANES // HEAD_DIM
        for sp in range(z.shape[1] // LANES):
            zt = z[:, LANES * sp:LANES * (sp + 1)].T
            ref[per_slab * sp:per_slab * (sp + 1)] = zt.reshape(per_slab, HEAD_DIM, z.shape[0])

    h = _rms(x_ref[...], g_ref[...]).astype(BF16)
    cos = cos_ref[...]
    sin = sin_ref[...]
    lane = lax.broadcasted_iota(I32, (1, LANES), 1)
    lo_half = (lane & (HEAD_DIM // 2)) == 0

    def rope(z):
        cols = []
        for j in range(z.shape[1] // LANES):
            zs = z[:, LANES * j:LANES * (j + 1)]
            partner = jnp.where(lo_half, pltpu.roll(zs, LANES - HEAD_DIM // 2, 1), pltpu.roll(zs, HEAD_DIM // 2, 1))
            cols.append(zs * cos + partner * sin)
        return cols[0] if len(cols) == 1 else jnp.concatenate(cols, axis=1)

    def head_norm(z, gn):
        ms = _dot((z * z).astype(BF16), hm_ref[...])
        return z * lax.rsqrt(ms + EPS) * gn

    u_ref[...] = _dot(h, wu_ref[...])
    q = rope(head_norm(_dot(h, wq_ref[...]), qn_ref[...]))
    q_ref[...] = (q * (HEAD_DIM ** -0.5)).astype(BF16)
    k = rope(head_norm(_dot(h, wk_ref[...]), kn_ref[...]))
    store_kv(k_ref, k)
    kb_ref[...] = k.astype(BF16)
    v = _dot(h, wv_ref[...])
    store_kv(v_ref, v)
    vb_ref[...] = v.astype(BF16)
    qi_ref[...] = (rope(_dot(h, wqi_ref[...])) * (IDX_DIM ** -0.5)).astype(BF16)
    zkw = _dot(h, wkw_ref[...])
    kir = rope(zkw)
    kw_ref[...] = jnp.where(lane < IDX_DIM, kir, zkw * (N_IDX_HEADS ** -0.5))
    ki_ref[...] = kir.T[:IDX_DIM] if transposed else kir[:, :IDX_DIM]
    kib_ref[...] = jnp.where(lane < IDX_DIM, kir, pltpu.roll(kir, IDX_DIM, 1)).astype(BF16)


def _inproj(x3, cos_t, sin_t, tm, out_rows, g, ws, qn, kn, hm, transposed):
    nb, rows, D = x3.shape
    wu, wq, wk, wv, wqi, wkw = ws
    d_ssm, d_att, d_qi = wu.shape[1], wq.shape[1], wqi.shape[1]
    row = lambda w: pl.BlockSpec((None, tm, w), lambda b, j: (b, j, 0))
    tab = pl.BlockSpec((tm, LANES), lambda b, j: (j, 0))
    sds = lambda w, dt: jax.ShapeDtypeStruct((nb, out_rows, w), dt)
    if transposed:
        kv_shape = jax.ShapeDtypeStruct((nb, N_HEADS, HEAD_DIM, out_rows), F32)
        kv_spec = pl.BlockSpec((None, N_HEADS, HEAD_DIM, tm), lambda b, j: (b, 0, 0, j))
        ki_shape = jax.ShapeDtypeStruct((nb, IDX_DIM, out_rows), F32)
        ki_spec = pl.BlockSpec((None, IDX_DIM, tm), lambda b, j: (b, 0, j))
    else:
        kv_shape, kv_spec, ki_shape, ki_spec = sds(d_att, F32), row(d_att), sds(IDX_DIM, F32), row(IDX_DIM)
    out_shape = (
        sds(d_ssm, F32),
        sds(d_att, BF16),
        kv_shape,
        kv_shape,
        sds(d_qi, BF16),
        ki_shape,
        sds(LANES, F32),
        sds(d_att, BF16),
        sds(d_att, BF16),
        sds(LANES, BF16),
    )
    out_specs = (row(d_ssm), row(d_att), kv_spec, kv_spec, row(d_qi), ki_spec, row(LANES),
                 row(d_att), row(d_att), row(LANES))
    in_specs = [row(D), _const_spec(g.shape)] + [_const_spec(w.shape) for w in ws] + [
        _const_spec(qn.shape), _const_spec(kn.shape), _const_spec(hm.shape), tab, tab]
    return pl.pallas_call(
        functools.partial(_inproj_body, transposed=transposed), grid=(nb, rows // tm),
        in_specs=in_specs, out_specs=out_specs, out_shape=out_shape,
        compiler_params=_params(("arbitrary", "arbitrary")), name="inproj_t" if transposed else "inproj",
    )(x3, g, *ws, qn, kn, hm, cos_t, sin_t)


def _s5_prep_body(lr_ref, li_ref, ldt_ref, btr_ref, bti_ref, lbr_ref, lbi_ref, bbr_ref, bbi_ref):
    lr = lr_ref[...]
    li = li_ref[...]
    dt = jnp.exp(ldt_ref[...])
    mag = jnp.exp(lr * dt)
    lbr = mag * jnp.cos(li * dt)
    lbi = mag * jnp.sin(li * dt)
    lbr_ref[...] = lbr
    lbi_ref[...] = lbi
    den = lr * lr + li * li
    cr = ((lbr - 1.0) * lr + lbi * li) / den
    ci = (lbi * lr - (lbr - 1.0) * li) / den
    btr = btr_ref[...]
    bti = bti_ref[...]
    bbr_ref[...] = cr * btr - ci * bti
    bbi_ref[...] = cr * bti + ci * btr


def _s5_prep(lam_re, lam_im, log_dt, b_re, b_im):
    G, P = lam_re.shape
    C = b_re.shape[-1]
    btr = jnp.swapaxes(b_re, 1, 2)
    bti = jnp.swapaxes(b_im, 1, 2)
    sds = jax.ShapeDtypeStruct
    return pl.pallas_call(
        _s5_prep_body,
        out_shape=(sds((G, 1, P), F32), sds((G, 1, P), F32), sds((G, C, P), F32), sds((G, C, P), F32)),
        name="s5_prep",
    )(lam_re.reshape(G, 1, P), lam_im.reshape(G, 1, P), log_dt.reshape(G, 1, 1), btr, bti)


def _s5_body(u_ref, h0r_ref, h0i_ref, lbr_ref, lbi_ref, wb_ref, wc_ref, d_ref, wglu_ref,
             y_ref, hr_ref, hi_ref, bu_ref, st_ref, *, nb, steps, precise):
    n_slab = wb_ref.shape[0]
    cin = wb_ref.shape[1]
    sw = wb_ref.shape[2] // 2
    n_state = n_slab * sw

    @pl.when(pl.program_id(0) == 0)
    def _():
        st_ref[0] = h0r_ref[...]
        st_ref[1] = h0i_ref[...]

    u = u_ref[...]
    for i in range(n_slab):
        ui = u[:, cin * i:cin * (i + 1)]
        if precise:
            bu = jnp.dot(ui, wb_ref[i], preferred_element_type=F32, precision=lax.Precision.HIGHEST)
        else:
            bu = _dot(ui.astype(BF16), wb_ref[i].astype(BF16))
        bu_ref[:, sw * i:sw * (i + 1)] = bu[:, :sw]
        bu_ref[:, n_state + sw * i:n_state + sw * (i + 1)] = bu[:, sw:]

    for i in range(n_slab):
        re_cols = slice(sw * i, sw * (i + 1))
        im_cols = slice(n_state + sw * i, n_state + sw * (i + 1))
        lr = jnp.broadcast_to(lbr_ref[:, re_cols], (nb, sw))
        li = jnp.broadcast_to(lbi_ref[:, re_cols], (nb, sw))

        def step(t, carry, re_cols=re_cols, im_cols=im_cols, lr=lr, li=li):
            hr, hi = carry
            rows = pl.ds(pl.multiple_of(t * nb, nb), nb)
            nhr = lr * hr - li * hi + bu_ref[rows, re_cols]
            nhi = lr * hi + li * hr + bu_ref[rows, im_cols]
            bu_ref[rows, re_cols] = nhr
            bu_ref[rows, im_cols] = nhi
            return nhr, nhi

        hr, hi = lax.fori_loop(0, steps, step, (st_ref[0, :, re_cols], st_ref[1, :, re_cols]))
        st_ref[0, :, re_cols] = hr
        st_ref[1, :, re_cols] = hi

    ys = []
    for i in range(n_slab):
        h_re = bu_ref[:, sw * i:sw * (i + 1)].astype(BF16)
        h_im = bu_ref[:, n_state + sw * i:n_state + sw * (i + 1)].astype(BF16)
        ys.append(_dot(h_re, wc_ref[i, 0]) + _dot(h_im, wc_ref[i, 1]))
    y = jnp.concatenate(ys, axis=1) + d_ref[...] * u
    y = 0.5 * y * (1.0 + jnp.tanh(math.sqrt(2.0 / math.pi) * (y + 0.044715 * (y * y * y))))
    y = y * jax.nn.sigmoid(_dot(y.astype(BF16), wglu_ref[...]))
    y_ref[...] = y.astype(BF16)
    hr_ref[...] = st_ref[0]
    hi_ref[...] = st_ref[1]


def _s5(u_tm, h0_re, h0_im, lbr_flat, lbi_flat, wb, wc, d_skip, wglu, nb, steps, precise):
    rows_total, d_ssm = u_tm.shape
    n_state = h0_re.shape[1]
    rows = nb * steps
    sds = jax.ShapeDtypeStruct
    body = functools.partial(_s5_body, nb=nb, steps=steps, precise=precise)
    return pl.pallas_call(
        body, grid=(rows_total // rows,),
        in_specs=[pl.BlockSpec((rows, d_ssm), lambda c: (c, 0)), _const_spec(h0_re.shape), _const_spec(h0_im.shape),
                  _const_spec(lbr_flat.shape), _const_spec(lbi_flat.shape), _const_spec(wb.shape),
                  _const_spec(wc.shape), _const_spec(d_skip.shape), _const_spec(wglu.shape)],
        out_specs=(pl.BlockSpec((rows, d_ssm), lambda c: (c, 0)), _const_spec(h0_re.shape), _const_spec(h0_im.shape)),
        out_shape=(sds((rows_total, d_ssm), BF16), sds(h0_re.shape, F32), sds(h0_im.shape, F32)),
        scratch_shapes=[pltpu.VMEM((rows, 2 * n_state), F32), pltpu.VMEM((2, nb, n_state), F32)],
        compiler_params=_params(("arbitrary",)), name="s5_precise" if precise else "s5",
    )(u_tm, h0_re, h0_im, lbr_flat, lbi_flat, wb, wc, d_skip, wglu)


def _topk_bias(s_ref, n, kk):
    rows = s_ref.shape[0]
    cols = slice(0, n)

    def count(pred):
        return jnp.sum(jnp.where(pred(s_ref[:, cols]), 1.0, 0.0), axis=1, keepdims=True)

    def largest_below(hi):
        s = s_ref[:, cols]
        return jnp.max(jnp.where(s < hi, s, -jnp.inf), axis=1, keepdims=True)

    s0 = s_ref[:, cols]
    smax = jnp.max(s0, axis=1, keepdims=True)
    smin = jnp.min(jnp.where(s0 > -jnp.inf, s0, jnp.inf), axis=1, keepdims=True)

    def bisect(_, c):
        lo, hi = c
        mid = lo + (hi - lo) * 0.5
        ge = count(lambda s: s >= mid) >= kk
        return jnp.where(ge, mid, lo), jnp.where(ge, hi, mid)

    _, hi = lax.fori_loop(0, BISECT_ITERS, bisect, (smin, smax + (smax - smin) + 1.0))
    thr = largest_below(hi)
    cnt = count(lambda s: s >= thr)

    def short(c):
        return jnp.min(c[2] - kk) < 0.0

    def lower(c):
        hi, thr, cnt = c
        hi = jnp.where(cnt < kk, thr, hi)
        thr = largest_below(hi)
        return hi, thr, count(lambda s: s >= thr)

    _, thr, _ = lax.while_loop(short, lower, (hi, thr, cnt))
    need = kk - count(lambda s: s > thr)
    blk = 2 * LANES
    tri = (lax.broadcasted_iota(I32, (blk, blk), 0) <= lax.broadcasted_iota(I32, (blk, blk), 1)).astype(BF16)
    seen = jnp.zeros((rows, 1), F32)
    for c0 in range(0, n, blk):
        w = min(blk, n - c0)
        s = s_ref[:, c0:c0 + w]
        tie = s == thr
        upto = seen + _dot(jnp.where(tie, 1.0, 0.0).astype(BF16), tri[:w, :w])
        seen = upto[:, w - 1:w]
        s_ref[:, c0:c0 + w] = jnp.where((s > thr) | (tie & (upto <= need)), 0.0, NEG_BIG)


def _dsa_prompt_block(q0, rows, n, qim_ref, qm_ref, w_ref, kip_ref, kp_ref, vp_ref, s_ref, acc_ref, topk):
    tq = rows.stop - rows.start
    s_ref = s_ref.at[0:tq]
    cols = slice(0, n)
    lane = lax.broadcasted_iota(I32, (1, LANES), 1)
    s_ref[:, cols] = jnp.zeros((tq, n), F32)

    def score_head(h, carry):
        d = jnp.maximum(_dot_nt(qim_ref[h, rows, :], kip_ref[cols, :]), 0.0)
        s_ref[:, cols] += w_ref[h, rows, :] * d
        return carry

    lax.fori_loop(0, N_IDX_HEADS, score_head, 0)
    qpos = q0 + lax.broadcasted_iota(I32, (tq, 1), 0)
    kpos = lax.broadcasted_iota(I32, (1, n), 1)
    s_ref[:, cols] = jnp.where(kpos <= qpos, s_ref[:, cols], -jnp.inf)
    _topk_bias(s_ref, n, jnp.minimum(qpos + 1, topk).astype(F32))

    def head_pair(sp, carry):
        outs = []
        for hh in range(2):
            lg = _dot_nt(qm_ref[2 * sp + hh, rows, :], kp_ref[sp, cols, :]) + s_ref[:, cols]
            p = jnp.exp(lg - jnp.max(lg, axis=1, keepdims=True))
            pv = _dot(p.astype(BF16), vp_ref[sp, cols, :])
            outs.append(pv[:, :LANES] / pv[:, LANES:LANES + 1])
        acc_ref[sp, rows, :] = jnp.where(lane < HEAD_DIM, outs[0], outs[1])
        return carry

    lax.fori_loop(0, acc_ref.shape[0], head_pair, 0)


def _dsa_prompt_body(qi_ref, kw_ref, q_ref, kib_ref, kb_ref, vb_ref, o_ref,
                     kip_ref, kp_ref, vp_ref, qim_ref, qm_ref, w_ref, s_ref, acc_ref, *, seq, topk):
    tq = q_ref.shape[0]
    n_slab, lp, _ = kp_ref.shape
    j = pl.program_id(1)
    lane = lax.broadcasted_iota(I32, (1, LANES), 1)

    @pl.when(j == 0)
    def _():
        kip_ref[0:seq] = kib_ref[...]
        if lp > seq:
            kip_ref[seq:] = jnp.zeros((lp - seq, LANES), BF16)
        for sp in range(n_slab):
            sl = slice(LANES * sp, LANES * (sp + 1))
            kp_ref[sp, 0:seq] = kb_ref[:, sl]
            vp_ref[sp, 0:seq, 0:LANES] = vb_ref[:, sl]
            vp_ref[sp, :, LANES:] = jnp.where(lane == 0, 1.0, 0.0).astype(BF16) + jnp.zeros((lp, LANES), BF16)
            if lp > seq:
                kp_ref[sp, seq:] = jnp.zeros((lp - seq, LANES), BF16)
                vp_ref[sp, seq:, 0:LANES] = jnp.zeros((lp - seq, LANES), BF16)

    kw = kw_ref[...]
    for h in range(N_IDX_HEADS):
        head_lanes = (lane // IDX_DIM) == (h % 2)
        sl = slice(LANES * (h // 2), LANES * (h // 2 + 1))
        qim_ref[h] = jnp.where(head_lanes, qi_ref[:, sl], jnp.zeros((), BF16))
        qm_ref[h] = jnp.where(head_lanes, q_ref[:, sl], jnp.zeros((), BF16))
        w_ref[h] = kw[:, IDX_DIM + h:IDX_DIM + h + 1]

    split = tq // 2 // (2 * SUBLANES) * (2 * SUBLANES)
    for jj in range(seq // tq):
        @pl.when(j == jj)
        def _(jj=jj):
            for rows in (slice(0, split), slice(split, tq)):
                n = min(-(-(jj * tq + rows.stop) // LANES) * LANES, lp)
                _dsa_prompt_block(jj * tq + rows.start, rows, n, qim_ref, qm_ref, w_ref, kip_ref, kp_ref, vp_ref,
                                  s_ref, acc_ref, topk)

    for sp in range(n_slab):
        o_ref[:, LANES * sp:LANES * (sp + 1)] = acc_ref[sp].astype(BF16)


def _dsa_prompt(qi, kw, q, kib, kb, vb, batch, seq, tq, topk):
    T, d_att = q.shape
    nq = seq // tq
    lp = -(-seq // LANES) * LANES
    n_slab = d_att // LANES
    qrow = lambda w: pl.BlockSpec((tq, w), lambda b, j: (b * nq + j, 0))
    full = lambda w: pl.BlockSpec((None, seq, w), lambda b, j: (b, 0, 0))
    body = functools.partial(_dsa_prompt_body, seq=seq, topk=topk)
    return pl.pallas_call(
        body, grid=(batch, nq),
        in_specs=[qrow(qi.shape[1]), qrow(LANES), qrow(d_att), full(LANES), full(d_att), full(d_att)],
        out_specs=qrow(d_att),
        out_shape=jax.ShapeDtypeStruct((T, d_att), BF16),
        scratch_shapes=[pltpu.VMEM((lp, LANES), BF16), pltpu.VMEM((n_slab, lp, LANES), BF16),
                        pltpu.VMEM((n_slab, lp, 2 * LANES), BF16),
                        pltpu.VMEM((N_IDX_HEADS, tq, LANES), BF16), pltpu.VMEM((N_HEADS, tq, LANES), BF16),
                        pltpu.VMEM((N_IDX_HEADS, tq, 1), F32),
                        pltpu.VMEM((tq, lp), F32), pltpu.VMEM((n_slab, tq, LANES), F32)],
        compiler_params=_params(("arbitrary", "arbitrary")), name="dsa_prompt",
    )(qi, kw, q, kib.reshape(batch, seq, LANES), kb.reshape(batch, seq, d_att), vb.reshape(batch, seq, d_att))


def _dsa_sample_scores_body(pt_ref, qi_ref, w_ref, kin_ref, *refs):
    del pt_ref
    page_refs, s_ref = refs[:-1], refs[-1]
    page = page_refs[0].shape[1]
    qm = qi_ref[...]
    w = w_ref[...]
    kt = jnp.concatenate([r[...].astype(BF16) for r in page_refs], axis=1)
    s = jnp.sum(w * jnp.maximum(_dot(qm, kt), 0.0), axis=0, keepdims=True)
    kin = kin_ref[...].astype(BF16).astype(F32)
    d_new = jnp.maximum(jnp.sum(qm.astype(F32) * kin, axis=1, keepdims=True), 0.0)
    s_new = jnp.sum(w * d_new, axis=0, keepdims=True)
    lane = lax.broadcasted_iota(I32, (1, page), 1)
    s_ref[...] = jnp.concatenate([s, jnp.where(lane == 0, s_new, 0.0)], axis=1)


def _dsa_sample_scores(page_table, qi3, w3, ki_new3, kidx_t):
    nb, n_pages = page_table.shape
    idx_dim, page = kidx_t.shape[1:]
    width = (n_pages + 1) * page
    per = lambda shape: pl.BlockSpec((None,) + shape, lambda b, pt: (b, 0, 0))
    page_specs = [pl.BlockSpec((None, idx_dim, page), functools.partial(lambda b, pt, p: (pt[b, p], 0, 0), p=p))
                  for p in range(n_pages)]
    grid_spec = pltpu.PrefetchScalarGridSpec(
        num_scalar_prefetch=1, grid=(nb,),
        in_specs=[per(qi3.shape[1:]), per(w3.shape[1:]), per(ki_new3.shape[1:])] + page_specs,
        out_specs=per((1, width)))
    return pl.pallas_call(
        _dsa_sample_scores_body, grid_spec=grid_spec,
        out_shape=jax.ShapeDtypeStruct((nb, 1, width), F32),
        compiler_params=_params(("arbitrary",)), name="dsa_sample_scores",
    )(page_table, qi3, w3, ki_new3, *([kidx_t] * n_pages))


def _dsa_sample_select_body(s_ref, b_ref, *, n_keys, topk):
    rows, width = s_ref.shape
    pos = lax.broadcasted_iota(I32, (1, width), 1)
    b_ref[...] = jnp.where(pos < n_keys, s_ref[...], -jnp.inf)
    _topk_bias(b_ref, width, jnp.full((rows, 1), float(min(topk, n_keys)), F32))


def _dsa_sample_select(scores, n_keys, topk):
    rows, width = scores.shape
    body = functools.partial(_dsa_sample_select_body, n_keys=n_keys, topk=topk)
    return pl.pallas_call(
        body, out_shape=jax.ShapeDtypeStruct((rows, width), F32),
        compiler_params=pltpu.CompilerParams(vmem_limit_bytes=VMEM_LIMIT), name="dsa_sample_select",
    )(scores)


def _dsa_sample_attend_body(pt_ref, q_ref, kn_ref, vn_ref, b_ref, *refs):
    del pt_ref
    o_ref, lg_ref = refs[-2], refs[-1]
    spp = q_ref.shape[0]
    n_pages = (len(refs) - 2) // (2 * spp)
    n_heads, hd, page = refs[0].shape
    lane = lax.broadcasted_iota(I32, (1, page), 1)
    for s in range(spp):
        k_refs = refs[s * n_pages:(s + 1) * n_pages]
        v_refs = refs[(spp + s) * n_pages:(spp + s + 1) * n_pages]
        bias = b_ref[s]
        q, kn, vn = q_ref[s], kn_ref[s], vn_ref[s]
        for h in range(n_heads):
            col = slice(h, h + 1)
            qc = jnp.broadcast_to(q[:, col], (hd, page))
            for p in range(n_pages):
                lg_ref[p:p + 1, :] = jnp.sum(qc * k_refs[p][h], axis=0, keepdims=True)
            lg_new = jnp.sum(q[:, col] * kn[:, col], axis=0, keepdims=True)
            lg_ref[n_pages:n_pages + 1, :] = jnp.where(lane == 0, lg_new, 0.0)
            lg = lg_ref[...] + bias
            m = jnp.max(jnp.max(lg, axis=1, keepdims=True), axis=0, keepdims=True)
            pr = jnp.exp(lg - m)
            den = jnp.sum(jnp.sum(pr, axis=1, keepdims=True), axis=0, keepdims=True)
            acc = jnp.zeros((hd, page), F32)
            for p in range(n_pages):
                acc = acc + pr[p:p + 1, :] * v_refs[p][h]
            out = jnp.sum(acc, axis=1, keepdims=True) + pr[n_pages:n_pages + 1, 0:1] * vn[:, col]
            o_ref[s, :, col] = out / den


def _dsa_sample_attend(page_table, q_col, kn_col, vn_col, bias3, k_t, v_t, spp):
    nb, n_pages = page_table.shape
    assert nb % spp == 0
    n_heads, hd, page = k_t.shape[1:]
    per = lambda shape: pl.BlockSpec((spp,) + shape, lambda b, pt: (b, 0, 0))
    page_specs = [pl.BlockSpec((None, n_heads, hd, page),
                               functools.partial(lambda b, pt, s, p: (pt[b * spp + s, p], 0, 0, 0), s=s, p=p))
                  for s in range(spp) for p in range(n_pages)]
    grid_spec = pltpu.PrefetchScalarGridSpec(
        num_scalar_prefetch=1, grid=(nb // spp,),
        in_specs=[per(q_col.shape[1:]), per(kn_col.shape[1:]), per(vn_col.shape[1:]), per(bias3.shape[1:])] + page_specs * 2,
        out_specs=per(q_col.shape[1:]),
        scratch_shapes=[pltpu.VMEM((n_pages + 1, page), F32)])
    return pl.pallas_call(
        _dsa_sample_attend_body, grid_spec=grid_spec,
        out_shape=jax.ShapeDtypeStruct(q_col.shape, F32),
        compiler_params=_params(("arbitrary",)), name="dsa_sample_attend",
    )(page_table, q_col, kn_col, vn_col, bias3, *([k_t] * (spp * n_pages)), *([v_t] * (spp * n_pages)))


def _swiglu(h, wg_ref, wu_ref, wd_ref, n_chunks):
    fc = wg_ref.shape[-1] // n_chunks
    y = None
    for c in range(n_chunks):
        cols = slice(fc * c, fc * (c + 1))
        gate = _dot(h, wg_ref[:, cols])
        act = (gate * jax.nn.sigmoid(gate) * _dot(h, wu_ref[:, cols])).astype(BF16)
        part = _dot(act, wd_ref[cols, :])
        y = part if y is None else y + part
    return y


def _outproj_ffn_body(x_ref, ys_ref, ya_ref, wos_ref, woa_ref, g_ref, wg_ref, wu_ref, wd_ref, o_ref):
    x = x_ref[...] + _dot(ys_ref[...], wos_ref[...]) + _dot(ya_ref[...], woa_ref[...])
    h = _rms(x, g_ref[...]).astype(BF16)
    o_ref[...] = x + _swiglu(h, wg_ref, wu_ref, wd_ref, 2)


def _outproj_ffn(x3, ys, ya, wos, woa, g, wg, wu, wd, seq, tm):
    nb, _, D = x3.shape
    nblk = seq // tm
    row = lambda w: pl.BlockSpec((tm, w), lambda b, j: (b * nblk + j, 0))
    single = lambda a: pl.BlockSpec(a.shape, lambda b, j: (0,) * a.ndim, pipeline_mode=pl.Buffered(1))
    return pl.pallas_call(
        _outproj_ffn_body, grid=(nb, nblk),
        in_specs=[pl.BlockSpec((None, tm, D), lambda b, j: (b, j, 0)), row(ys.shape[1]), row(ya.shape[1]),
                  single(wos), single(woa), single(g), single(wg), single(wu), single(wd)],
        out_specs=row(D), out_shape=jax.ShapeDtypeStruct((nb * seq, D), F32),
        compiler_params=_params(("arbitrary", "arbitrary")), name="outproj_ffn",
    )(x3, ys, ya, wos, woa, g, wg, wu, wd)


def _pool_mix(h, window_sum, divisor, pw_ref, scale):
    gd = h.shape[1] // len(POOL_WINDOWS)
    cols = []
    for g, w in enumerate(POOL_WINDOWS):
        pooled = window_sum(g, w) / divisor(w) - h[:, gd * g:gd * (g + 1)]
        cols.append(_dot(pooled.astype(BF16), pw_ref[g]))
    return jnp.concatenate(cols, axis=1) * scale


def _pool_prompt_body(x_ref, g_ref, pw_ref, sc_ref, o_ref, hist_ref, ext_ref):
    tm, D = x_ref.shape
    gd = D // len(POOL_WINDOWS)
    halo = POOL_HIST + 1
    j = pl.program_id(1)

    @pl.when(j == 0)
    def _():
        ext_ref[0:halo] = jnp.zeros((halo, D), F32)

    x = x_ref[...]
    h = _rms(x, g_ref[...])
    ext_ref[halo:] = h
    pos = (j * tm + lax.broadcasted_iota(I32, (tm, 1), 0)).astype(F32)

    def window_sum(g, w):
        acc = h[:, gd * g:gd * (g + 1)]
        for k in range(1, w):
            acc = acc + ext_ref[halo - k:halo - k + tm, gd * g:gd * (g + 1)]
        return acc

    mixed = _pool_mix(h, window_sum, lambda w: jnp.minimum(float(w), pos + 1.0), pw_ref, sc_ref[...])
    o_ref[...] = x + mixed
    ext_ref[0:halo] = ext_ref[tm:tm + halo]

    @pl.when(j == pl.num_programs(1) - 1)
    def _():
        hist_ref[...] = h[tm - POOL_HIST:, :]


def _pool_prompt(x2d, g, pw, scale, batch, seq, tm):
    T, D = x2d.shape
    nblk = seq // tm
    row = pl.BlockSpec((tm, D), lambda b, j: (b * nblk + j, 0))
    return pl.pallas_call(
        _pool_prompt_body, grid=(batch, nblk),
        in_specs=[row, _const_spec(g.shape), _const_spec(pw.shape), _const_spec(scale.shape)],
        out_specs=(row, pl.BlockSpec((None, POOL_HIST, D), lambda b, j: (b, 0, 0))),
        out_shape=(jax.ShapeDtypeStruct((T, D), F32), jax.ShapeDtypeStruct((batch, POOL_HIST, D), F32)),
        scratch_shapes=[pltpu.VMEM((tm + POOL_HIST + 1, D), F32)],
        compiler_params=_params(("arbitrary", "arbitrary")), name="pool_prompt",
    )(x2d, g, pw, scale)


def _pool_sample_body(x_ref, hist_ref, g_ref, pw_ref, sc_ref, o_ref, nh_ref):
    D = x_ref.shape[1]
    gd = D // len(POOL_WINDOWS)
    x = x_ref[...]
    h = _rms(x, g_ref[...])

    def window_sum(g, w):
        acc = h[:, gd * g:gd * (g + 1)]
        for k in range(1, w):
            acc = acc + hist_ref[POOL_HIST - k, :, gd * g:gd * (g + 1)]
        return acc

    mixed = _pool_mix(h, window_sum, float, pw_ref, sc_ref[...])
    o_ref[...] = x + mixed
    for i in range(POOL_HIST - 1):
        nh_ref[i] = hist_ref[i + 1]
    nh_ref[POOL_HIST - 1] = h


def _pool_sample(x2d, hist_t, g, pw, scale):
    sds = jax.ShapeDtypeStruct
    return pl.pallas_call(
        _pool_sample_body, out_shape=(sds(x2d.shape, F32), sds(hist_t.shape, F32)),
        compiler_params=pltpu.CompilerParams(vmem_limit_bytes=VMEM_LIMIT), name="pool_sample",
    )(x2d, hist_t, g, pw, scale)


def _store_row_tiles(ref, x):
    for s in range(ref.shape[1]):
        ref[:, s, :] = x[:, LANES * s:LANES * (s + 1)]


def _load_row_tiles(ref):
    return jnp.concatenate([ref[:, s, :] for s in range(ref.shape[1])], axis=1)


def _router_body(x_ref, g_ref, wr_ref, h_ref, sel_ref, *, n_experts):
    h = _rms(x_ref[...], g_ref[...])
    _store_row_tiles(h_ref, h)
    w = wr_ref[...]
    h_hi, w_hi = h.astype(BF16), w.astype(BF16)
    h_lo, w_lo = (h - h_hi.astype(F32)).astype(BF16), (w - w_hi.astype(F32)).astype(BF16)
    logits = _dot(h_hi, w_hi) + (_dot(h_hi, w_lo) + _dot(h_lo, w_hi))
    lane = lax.broadcasted_iota(I32, logits.shape, 1).astype(F32)
    logits = jnp.where(lane < n_experts, logits, -jnp.inf)
    v1 = jnp.max(logits, axis=1, keepdims=True)
    i1 = jnp.min(jnp.where(logits == v1, lane, float(LANES)), axis=1, keepdims=True)
    rest = jnp.where(lane == i1, -jnp.inf, logits)
    v2 = jnp.max(rest, axis=1, keepdims=True)
    i2 = jnp.min(jnp.where(rest == v2, lane, float(LANES)), axis=1, keepdims=True)
    e2 = jnp.exp(v2 - v1)
    den = 1.0 + e2
    sel_ref[...] = jnp.where(lane == 0.0, i1, jnp.where(lane == 1.0, i2, jnp.where(lane == 2.0, 1.0 / den, e2 / den)))


def _router(x2d, g, wr_pad, n_experts, tm):
    T, D = x2d.shape
    row = lambda w: pl.BlockSpec((tm, w), lambda i: (i, 0))
    body = functools.partial(_router_body, n_experts=n_experts)
    return pl.pallas_call(
        body, grid=(T // tm,),
        in_specs=[row(D), _const_spec(g.shape), _const_spec(wr_pad.shape)],
        out_specs=(pl.BlockSpec((tm, D // LANES, LANES), lambda i: (i, 0, 0)), row(LANES)),
        out_shape=(jax.ShapeDtypeStruct((T, D // LANES, LANES), F32), jax.ShapeDtypeStruct((T, LANES), F32)),
        compiler_params=_params(("arbitrary",)), name="router",
    )(x2d, g, wr_pad)


def _row_gather(idx_ref, base, src_hbm, dst, sem):
    assert dst.shape[0] % DMA_UNROLL == 0

    def issue(g, carry):
        for k in range(DMA_UNROLL):
            r = g * DMA_UNROLL + k
            pltpu.make_async_copy(src_hbm.at[idx_ref[base + r]], dst.at[r], sem).start(priority=k % 2)
        return carry

    lax.fori_loop(0, dst.shape[0] // DMA_UNROLL, issue, 0)


def _row_gather_wait(src_hbm, dst, sem):
    pltpu.make_async_copy(src_hbm.at[pl.ds(0, dst.shape[0])], dst, sem).wait()


def _moe_dispatch_body(p1_ref, p2_ref, ends_ref, h_ref, x_hbm, zero_ref, stage, sems, *, tm):
    i = pl.program_id(0)
    last = pl.num_programs(0) - 1
    gt = zero_ref.shape[0]
    n_groups = ends_ref.shape[0]
    zsem = sems.at[2]
    slot = i % 2

    def fill_tile(e, act):
        if e < n_groups:
            start = ends_ref[e] - gt
            exists = ends_ref[e] > (ends_ref[e - 1] if e else 0)
        else:
            start = ends_ref[n_groups - 1] + (e - n_groups) * gt
            exists = start < x_hbm.shape[0]

        @pl.when(exists)
        def _():
            act(pltpu.make_async_copy(zero_ref, x_hbm.at[pl.ds(pl.multiple_of(start, gt), gt)], zsem))

    @pl.when(i == 0)
    def _():
        zero_ref[...] = jnp.zeros(zero_ref.shape, F32)
        for e in range(2 * n_groups):
            fill_tile(e, lambda cp: cp.start())
        for e in range(2 * n_groups):
            fill_tile(e, lambda cp: cp.wait())

    def wait_rows(s):
        for _ in range(2):
            pltpu.make_async_copy(stage.at[s], x_hbm.at[pl.ds(0, tm)], sems.at[s]).wait()

    @pl.when(i >= 2)
    def _():
        wait_rows(slot)

    stage[slot] = h_ref[...]

    def issue(r, carry):
        t = i * tm + r
        pltpu.make_async_copy(stage.at[slot, r], x_hbm.at[p1_ref[t]], sems.at[slot]).start(priority=0)
        pltpu.make_async_copy(stage.at[slot, r], x_hbm.at[p2_ref[t]], sems.at[slot]).start(priority=1)
        return carry

    lax.fori_loop(0, tm, issue, 0, unroll=DMA_UNROLL)

    @pl.when(i == last)
    def _():
        wait_rows(slot)

    @pl.when((i == last) & (i >= 1))
    def _():
        wait_rows(1 - slot)


def _moe_dispatch(pos1, pos2, ends, h_tiles, n_rows, tm, group_tile):
    T, S, _ = h_tiles.shape
    grid_spec = pltpu.PrefetchScalarGridSpec(
        num_scalar_prefetch=3, grid=(T // tm,),
        in_specs=[pl.BlockSpec((tm, S, LANES), lambda i, p1, p2, en: (i, 0, 0))],
        out_specs=pl.BlockSpec(memory_space=pl.ANY),
        scratch_shapes=[pltpu.VMEM((group_tile, S, LANES), F32), pltpu.VMEM((2, tm, S, LANES), F32),
                        pltpu.SemaphoreType.DMA((3,))])
    return pl.pallas_call(
        functools.partial(_moe_dispatch_body, tm=tm), grid_spec=grid_spec,
        out_shape=jax.ShapeDtypeStruct((n_rows, S, LANES), F32),
        compiler_params=pltpu.CompilerParams(dimension_semantics=("arbitrary",), vmem_limit_bytes=VMEM_LIMIT,
                                             disable_bounds_checks=True),
        name="moe_dispatch",
    )(pos1, pos2, ends, h_tiles)


def _moe_group_body(te_ref, nu_ref, x_ref, wg_ref, wu_ref, wd_ref, y_ref):
    del te_ref
    i = pl.program_id(0)

    @pl.when(i < nu_ref[0])
    def _():
        _store_row_tiles(y_ref, _swiglu(_load_row_tiles(x_ref).astype(BF16), wg_ref, wu_ref, wd_ref, 2))

    @pl.when(i >= nu_ref[0])
    def _():
        y_ref[...] = jnp.zeros(y_ref.shape, F32)


def _moe_group(tile_expert, n_used, x_tiles, wg, wu, wd, tm):
    P, S, _ = x_tiles.shape
    _, D, F = wg.shape
    wspec = lambda a, b: pl.BlockSpec((None, a, b), lambda i, te, nu: (te[i], 0, 0))
    grid_spec = pltpu.PrefetchScalarGridSpec(
        num_scalar_prefetch=2, grid=(P // tm,),
        in_specs=[pl.BlockSpec((tm, S, LANES), lambda i, te, nu: (jnp.minimum(i, nu[0] - 1), 0, 0)),
                  wspec(D, F), wspec(D, F), wspec(F, D)],
        out_specs=pl.BlockSpec((tm, S, LANES), lambda i, te, nu: (i, 0, 0)))
    return pl.pallas_call(
        _moe_group_body, grid_spec=grid_spec, out_shape=jax.ShapeDtypeStruct((P, S, LANES), F32),
        compiler_params=_params(("arbitrary",)), name="moe_group",
    )(tile_expert, n_used, x_tiles, wg, wu, wd)


def _moe_combine_body(p1_ref, p2_ref, dst_ref, x_ref, sel_ref, y_hbm, ya_hbm, yb_hbm, buf, obuf, sem, osem):
    i = pl.program_id(0)
    last = pl.num_programs(0) - 1
    tm = x_ref.shape[0]
    n_chunks = tm // OUT_CHUNK
    rows_a = ya_hbm.shape[0]
    slot = i % 2

    def out_copies(step, s, act):
        for c in range(n_chunks):
            dst = dst_ref[step * n_chunks + c]
            src = obuf.at[s, pl.ds(c * OUT_CHUNK, OUT_CHUNK)]

            @pl.when((dst >= 0) & (dst < rows_a))
            def _():
                act(pltpu.make_async_copy(src, ya_hbm.at[pl.ds(pl.multiple_of(dst, OUT_CHUNK), OUT_CHUNK)], osem.at[s]))

            @pl.when(dst >= rows_a)
            def _():
                off = pl.multiple_of(dst - rows_a, OUT_CHUNK)
                act(pltpu.make_async_copy(src, yb_hbm.at[pl.ds(off, OUT_CHUNK)], osem.at[s]))

    def gather(tile, s):
        _row_gather(p1_ref, tile * tm, y_hbm, buf.at[s, 0], sem.at[s, 0])
        _row_gather(p2_ref, tile * tm, y_hbm, buf.at[s, 1], sem.at[s, 1])

    @pl.when(i == 0)
    def _():
        gather(0, 0)

    @pl.when(i + 1 < pl.num_programs(0))
    def _():
        gather(i + 1, 1 - slot)

    _row_gather_wait(y_hbm, buf.at[slot, 0], sem.at[slot, 0])
    _row_gather_wait(y_hbm, buf.at[slot, 1], sem.at[slot, 1])
    @pl.when(i >= 2)
    def _():
        out_copies(i - 2, slot, lambda cp: cp.wait())

    sel = sel_ref[...]
    obuf[slot] = (x_ref[...] + sel[:, 2:3] * _load_row_tiles(buf.at[slot, 0])
                  + sel[:, 3:4] * _load_row_tiles(buf.at[slot, 1]))
    out_copies(i, slot, lambda cp: cp.start())

    @pl.when(i == last)
    def _():
        out_copies(i, slot, lambda cp: cp.wait())

    @pl.when((i == last) & (i >= 1))
    def _():
        out_copies(i - 1, 1 - slot, lambda cp: cp.wait())


def _moe_combine(pos1, pos2, chunk_dst, x2d, sel, y_tiles, tm, rows_a, rows_b):
    T, D = x2d.shape
    S = y_tiles.shape[1]
    assert tm % OUT_CHUNK == 0 and rows_a % OUT_CHUNK == 0 and rows_b % OUT_CHUNK == 0
    row = lambda w: pl.BlockSpec((tm, w), lambda i, p1, p2, cd: (i, 0))
    hbm = pl.BlockSpec(memory_space=pl.ANY)
    grid_spec = pltpu.PrefetchScalarGridSpec(
        num_scalar_prefetch=3, grid=(T // tm,),
        in_specs=[row(D), row(LANES), hbm], out_specs=(hbm, hbm),
        scratch_shapes=[pltpu.VMEM((2, 2, tm, S, LANES), F32), pltpu.VMEM((2, tm, D), F32),
                        pltpu.SemaphoreType.DMA((2, 2)), pltpu.SemaphoreType.DMA((2,))])
    return pl.pallas_call(
        _moe_combine_body, grid_spec=grid_spec,
        out_shape=(jax.ShapeDtypeStruct((rows_a, D), F32), jax.ShapeDtypeStruct((rows_b, D), F32)),
        compiler_params=pltpu.CompilerParams(dimension_semantics=("arbitrary",), vmem_limit_bytes=VMEM_LIMIT,
                                             disable_bounds_checks=True),
        name="moe_combine",
    )(pos1, pos2, chunk_dst, x2d, sel, y_tiles)


def _moe_plan(sel, n_experts, tm):
    T = sel.shape[0]
    experts = jnp.concatenate([sel[:, 0], sel[:, 1]]).astype(I32)
    onehot = (experts[:, None] == jnp.arange(n_experts, dtype=I32)[None, :]).astype(I32)
    rank = jnp.cumsum(onehot, axis=0) - onehot
    counts = jnp.sum(onehot, axis=0)
    padded = (counts + tm - 1) // tm * tm
    ends = jnp.cumsum(padded)
    pos = jnp.sum(onehot * ((ends - padded)[None, :] + rank), axis=1)
    n_rows = (2 * T + n_experts * (tm - 1)) // tm * tm
    tile_start = jnp.arange(n_rows // tm, dtype=I32) * tm
    tile_expert = jnp.minimum(jnp.sum((tile_start[:, None] >= ends[None, :]).astype(I32), axis=1), n_experts - 1)
    n_used = (ends[-1] // tm).astype(I32).reshape(1)
    return tile_expert, n_used, ends.astype(I32), n_rows, pos[:T], pos[T:]


def _rope_tables(pos):
    half = HEAD_DIM // 2
    inv = ROPE_THETA ** (-jnp.arange(half, dtype=F32) / half)
    ang = pos[:, None] * inv[None, :]
    cos, sin = jnp.cos(ang), jnp.sin(ang)
    reps = LANES // HEAD_DIM
    return jnp.tile(jnp.concatenate([cos, cos], axis=1), (1, reps)), jnp.tile(jnp.concatenate([-sin, sin], axis=1), (1, reps))


def kernel(x_prompt, x_sample, cache_k, cache_v, cache_kidx, page_table, state_ssm_re, state_ssm_im, state_pool, meta_tokens, norm_mix0, w_in0, q_norm, k_norm, ssm_lambda_re, ssm_lambda_im, ssm_log_dt, ssm_b_re, ssm_b_im, ssm_c_re, ssm_c_im, ssm_d, ssm_w_glu, w_out0, norm_ffn0, ffn_w_gate, ffn_w_up, ffn_w_down, norm_mix1, pool_w, pool_scale, norm_ffn1, router_w, moe_w_gate, moe_w_up, moe_w_down):
    B, S, D = x_prompt.shape
    DB, DS, _ = x_sample.shape
    assert DS == 1, "one new token per sample sequence"
    L = S + N_META
    n_pool, page = cache_k.shape[1], cache_k.shape[2]
    n_pages = page_table.shape[1]
    past = n_pages * page
    d_att = N_HEADS * HEAD_DIM
    d_qi = N_IDX_HEADS * IDX_DIM
    G, P = ssm_lambda_re.shape[1:]
    d_ssm = G * SSM_GROUP
    n_state = G * P
    E = router_w.shape[-1]
    topk_p = min(TOPK_MAX, S // 4)
    topk_s = min(TOPK_MAX, (past + DS) // 4)
    tm = _row_block(L, 1024)
    assert tm >= POOL_HIST + 1

    tm_in = LANES * (-(-L // (3 * LANES)))
    l_pad = -(-L // tm_in) * tm_in
    x_p = jnp.concatenate([jnp.broadcast_to(meta_tokens[None], (B, N_META, D)), x_prompt,
                           jnp.zeros((B, l_pad - L, D), x_prompt.dtype)], axis=1)
    x_s = x_sample.reshape(1, DB, D)

    w_in = w_in0[0]
    o1, o2, o3, o4, o5, o6 = d_ssm, d_ssm + d_att, d_ssm + 2 * d_att, d_ssm + 3 * d_att, d_ssm + 3 * d_att + d_qi, d_ssm + 3 * d_att + d_qi + IDX_DIM
    w_kw = jnp.pad(w_in[:, o5:], ((0, 0), (0, LANES - (w_in.shape[1] - o5))))
    ws = tuple(w.astype(BF16) for w in (w_in[:, :o1], w_in[:, o1:o2], w_in[:, o2:o3], w_in[:, o3:o4], w_in[:, o4:o5], w_kw))
    g_mix0 = norm_mix0[0].reshape(1, D)
    qn = jnp.tile(q_norm[0], N_HEADS).reshape(1, d_att)
    kn = jnp.tile(k_norm[0], N_HEADS).reshape(1, d_att)
    head_of = jnp.arange(d_att) // HEAD_DIM
    hm = jnp.where(head_of[:, None] == head_of[None, :], 1.0 / HEAD_DIM, 0.0).astype(BF16)
    cos_p, sin_p = _rope_tables(jnp.arange(l_pad, dtype=F32))
    cos_s, sin_s = _rope_tables(jnp.full((DB,), float(past), F32))

    flat = lambda a: a.reshape(-1, a.shape[-1])
    u_p, q_p, kt_p, vt_p, qi_p, kit_p, kw_p, kb_p, vb_p, kib_p = _inproj(
        x_p, cos_p, sin_p, tm_in, L, g_mix0, ws, qn, kn, hm, True)
    u_p, q_p, qi_p, kw_p, kb_p, vb_p, kib_p = (flat(a) for a in (u_p, q_p, qi_p, kw_p, kb_p, vb_p, kib_p))
    u_s, q_s, k_s, v_s, qi_s, ki_s, kw_s, _, _, _ = (
        flat(a) for a in _inproj(x_s, cos_s, sin_s, DB, DB, g_mix0, ws, qn, kn, hm, False))

    lbr, lbi, bbr, bbi = _s5_prep(ssm_lambda_re[0], ssm_lambda_im[0], ssm_log_dt[0], ssm_b_re[0], ssm_b_im[0])
    gps = LANES // SSM_GROUP
    n_slab = G // gps
    eye = jnp.eye(gps, dtype=F32)

    def b_slabs(bt):
        return jnp.einsum('sgcp,gh->sgchp', bt.reshape(n_slab, gps, SSM_GROUP, P), eye).reshape(n_slab, LANES, gps * P)

    def c_slabs(c):
        return jnp.einsum('sgcp,gh->sgphc', c.reshape(n_slab, gps, SSM_GROUP, P), eye).reshape(n_slab, gps * P, LANES)

    wb = jnp.concatenate([b_slabs(bbr), b_slabs(bbi)], axis=2)
    wc = jnp.stack([c_slabs(ssm_c_re[0]), -c_slabs(ssm_c_im[0])], axis=1).astype(BF16)
    lbr_f, lbi_f = lbr.reshape(1, n_state), lbi.reshape(1, n_state)
    d_skip = ssm_d[0].reshape(1, d_ssm)
    wglu = ssm_w_glu[0].astype(BF16)
    assert B % SUBLANES == 0, "the S5 scan keeps one batch row per sublane"
    steps = max(d for d in range(1, L + 1) if L % d == 0 and d * B <= 512 and (d * B) % (2 * SUBLANES) == 0)
    u_tm = u_p.reshape(B, L, d_ssm).swapaxes(0, 1).reshape(L * B, d_ssm)
    zero_h = jnp.zeros((B, n_state), F32)
    ys_tm, hre_p, him_p = _s5(u_tm, zero_h, zero_h, lbr_f, lbi_f, wb, wc, d_skip, wglu, B, steps, False)
    ys_p = ys_tm.reshape(L, B, d_ssm).swapaxes(0, 1).reshape(B * L, d_ssm)
    ys_s, hre_s, him_s = _s5(u_s, state_ssm_re[0].reshape(DB, n_state), state_ssm_im[0].reshape(DB, n_state),
                             lbr_f, lbi_f, wb, wc, d_skip, wglu, DB, 1, True)

    ya_p = _dsa_prompt(qi_p, kw_p, q_p, kib_p, kb_p, vb_p, B, L, tm, topk_p)
    assert page == LANES
    w3 = kw_s[:, IDX_DIM:IDX_DIM + N_IDX_HEADS].reshape(DB, N_IDX_HEADS, 1)
    def head_cols(a, n_heads):
        return a.astype(F32).reshape(DB, n_heads, -1).swapaxes(1, 2)

    scores = _dsa_sample_scores(page_table, qi_s.reshape(DB, N_IDX_HEADS, IDX_DIM), w3, ki_s.reshape(DB, 1, IDX_DIM),
                                jnp.transpose(cache_kidx[0], (0, 2, 1)))
    bias = _dsa_sample_select(scores.reshape(DB, (n_pages + 1) * page), past + DS, topk_s)
    ya_s = _dsa_sample_attend(page_table, head_cols(q_s, N_HEADS), head_cols(k_s, N_HEADS), head_cols(v_s, N_HEADS),
                              bias.reshape(DB, n_pages + 1, page),
                              jnp.transpose(cache_k[0], (0, 2, 3, 1)), jnp.transpose(cache_v[0], (0, 2, 3, 1)),
                              2 if DB % 2 == 0 else 1)
    ya_s = ya_s.swapaxes(1, 2).reshape(DB, d_att).astype(BF16)

    w_out = w_out0[0].astype(BF16)
    ffn = (w_out[:d_ssm], w_out[d_ssm:], norm_ffn0[0].reshape(1, D), ffn_w_gate[0].astype(BF16),
           ffn_w_up[0].astype(BF16), ffn_w_down[0].astype(BF16))
    x_p = _outproj_ffn(x_p, ys_p, ya_p, *ffn, L, tm)
    x_s = _outproj_ffn(x_s, ys_s, ya_s, *ffn, DB, DB)

    g_mix1 = norm_mix1[0].reshape(1, D)
    pw = pool_w[0].astype(BF16)
    psc = pool_scale[0].reshape(1, D)
    T_all = B * L + DB
    x_p, pool_p = _pool_prompt(x_p, g_mix1, pw, psc, B, L, tm)
    x_s, hist_t = _pool_sample(x_s, state_pool[0].swapaxes(0, 1), g_mix1, pw, psc)
    pool_s = hist_t.swapaxes(0, 1)
    x_all = jnp.concatenate([x_p, x_s], axis=0)

    g_ffn1 = norm_ffn1[0].reshape(1, D)
    wr_pad = jnp.pad(router_w[0], ((0, 0), (0, LANES - E)))
    wg, wu, wd = moe_w_gate[0].astype(BF16), moe_w_up[0].astype(BF16), moe_w_down[0].astype(BF16)
    tr = _row_block(T_all, MOE_TILE)
    h_tiles, sel = _router(x_all, g_ffn1, wr_pad, E, tr)
    tile_expert, n_used, ends, n_rows, pos1, pos2 = _moe_plan(sel, E, MOE_TILE)
    x_tiles = _moe_dispatch(pos1, pos2, ends, h_tiles, n_rows, tr, MOE_TILE)
    y_tiles = _moe_group(tile_expert, n_used, x_tiles, wg, wu, wd, MOE_TILE)
    assert N_META % OUT_CHUNK == 0 and L % OUT_CHUNK == 0 and DB % OUT_CHUNK == 0
    row0 = jnp.arange(T_all // OUT_CHUNK, dtype=I32) * OUT_CHUNK
    in_seq = row0 % L
    chunk_dst = jnp.where(row0 >= B * L, row0 - B * N_META,
                          jnp.where(in_seq < N_META, -1, (row0 // L) * S + in_seq - N_META))
    y_prompt, y_sample = _moe_combine(pos1, pos2, chunk_dst, x_all, sel, y_tiles, tr, B * S, DB)
    y_prompt = y_prompt.reshape(B, S, D)
    y_sample = y_sample.reshape(DB, 1, D)
    k_prompt = jnp.transpose(kt_p, (0, 3, 1, 2))[None]
    v_prompt = jnp.transpose(vt_p, (0, 3, 1, 2))[None]
    kidx_prompt = jnp.transpose(kit_p, (0, 2, 1))[None]
    return (y_prompt, y_sample, k_prompt, v_prompt, kidx_prompt,
            hre_p.reshape(1, B, G, P), him_p.reshape(1, B, G, P), pool_p[None],
            k_s.reshape(1, DB, 1, N_HEADS, HEAD_DIM), v_s.reshape(1, DB, 1, N_HEADS, HEAD_DIM),
            ki_s.reshape(1, DB, 1, IDX_DIM), hre_s.reshape(1, DB, G, P), him_s.reshape(1, DB, G, P), pool_s[None])
```

```python
import functools
import math

import jax
import jax.numpy as jnp
from jax import lax
from jax.experimental import pallas as pl
from jax.experimental.pallas import tpu as pltpu

F32 = jnp.float32
BF16 = jnp.bfloat16
I32 = jnp.int32

N_META = 16
SSM_GROUP = 16
SSM_STATE = 64
N_HEADS = 8
HEAD_DIM = 64
N_IDX_HEADS = 8
IDX_DIM = 64
TOPK_MAX = 256
ROPE_THETA = 10000.0
POOL_WINDOWS = (2, 4, 8, 16)
POOL_HIST = max(POOL_WINDOWS) - 1
TOP_K_EXPERTS = 2
EPS = 1e-6

LANES = 128
SUBLANES = 8
VMEM_LIMIT = 56 * 1024 * 1024
DMA_UNROLL = 8
OUT_CHUNK = 2 * SUBLANES
MOE_TILE = 256
BISECT_ITERS = 16
NEG_BIG = -1e30


def _params(sem):
    return pltpu.CompilerParams(dimension_semantics=sem, vmem_limit_bytes=VMEM_LIMIT)


def _row_block(n, cap, mult=2 * SUBLANES):
    best = None
    for d in range(mult, min(n, cap) + 1, mult):
        if n % d == 0:
            best = d
    assert best is not None, n
    return best


def _rms(x, g):
    return x * lax.rsqrt(jnp.mean(x * x, axis=-1, keepdims=True) + EPS) * g


def _dot(a, b):
    return jnp.dot(a, b, preferred_element_type=F32)


def _dot_nt(a, b):
    return lax.dot_general(a, b, (((1,), (1,)), ((), ())), preferred_element_type=F32)


def _const_spec(shape):
    nd = len(shape)
    return pl.BlockSpec(shape, lambda *_: (0,) * nd)


def _inproj_body(x_ref, g_ref, wu_ref, wq_ref, wk_ref, wv_ref, wqi_ref, wkw_ref, qn_ref, kn_ref, hm_ref,
                 cos_ref, sin_ref,
                 u_ref, q_ref, k_ref, v_ref, qi_ref, ki_ref, kw_ref, kb_ref, vb_ref, kib_ref, *, transposed):
    def store_kv(ref, z):
        if not transposed:
            ref[...] = z
            return
        per_slab = LANES // HEAD_DIM
        for sp in range(z.shape[1] // LANES):
            zt = z[:, LANES * sp:LANES * (sp + 1)].T
            ref[per_slab * sp:per_slab * (sp + 1)] = zt.reshape(per_slab, HEAD_DIM, z.shape[0])

    h = _rms(x_ref[...], g_ref[...]).astype(BF16)
    cos = cos_ref[...]
    sin = sin_ref[...]
    lane = lax.broadcasted_iota(I32, (1, LANES), 1)
    lo_half = (lane & (HEAD_DIM // 2)) == 0

    def rope(z):
        cols = []
        for j in range(z.shape[1] // LANES):
            zs = z[:, LANES * j:LANES * (j + 1)]
            partner = jnp.where(lo_half, pltpu.roll(zs, LANES - HEAD_DIM // 2, 1), pltpu.roll(zs, HEAD_DIM // 2, 1))
            cols.append(zs * cos + partner * sin)
        return cols[0] if len(cols) == 1 else jnp.concatenate(cols, axis=1)

    def head_norm(z, gn):
        ms = _dot((z * z).astype(BF16), hm_ref[...])
        return z * lax.rsqrt(ms + EPS) * gn

    u_ref[...] = _dot(h, wu_ref[...])
    q = rope(head_norm(_dot(h, wq_ref[...]), qn_ref[...]))
    q_ref[...] = (q * (HEAD_DIM ** -0.5)).astype(BF16)
    k = rope(head_norm(_dot(h, wk_ref[...]), kn_ref[...]))
    store_kv(k_ref, k)
    kb_ref[...] = k.astype(BF16)
    v = _dot(h, wv_ref[...])
    store_kv(v_ref, v)
    vb_ref[...] = v.astype(BF16)
    qi_ref[...] = (rope(_dot(h, wqi_ref[...])) * (IDX_DIM ** -0.5)).astype(BF16)
    zkw = _dot(h, wkw_ref[...])
    kir = rope(zkw)
    kw_ref[...] = jnp.where(lane < IDX_DIM, kir, zkw * (N_IDX_HEADS ** -0.5))
    ki_ref[...] = kir.T[:IDX_DIM] if transposed else kir[:, :IDX_DIM]
    kib_ref[...] = jnp.where(lane < IDX_DIM, kir, pltpu.roll(kir, IDX_DIM, 1)).astype(BF16)


def _inproj(x3, cos_t, sin_t, tm, out_rows, g, ws, qn, kn, hm, transposed):
    nb, rows, D = x3.shape
    wu, wq, wk, wv, wqi, wkw = ws
    d_ssm, d_att, d_qi = wu.shape[1], wq.shape[1], wqi.shape[1]
    row = lambda w: pl.BlockSpec((None, tm, w), lambda b, j: (b, j, 0))
    tab = pl.BlockSpec((tm, LANES), lambda b, j: (j, 0))
    sds = lambda w, dt: jax.ShapeDtypeStruct((nb, out_rows, w), dt)
    if transposed:
        kv_shape = jax.ShapeDtypeStruct((nb, N_HEADS, HEAD_DIM, out_rows), F32)
        kv_spec = pl.BlockSpec((None, N_HEADS, HEAD_DIM, tm), lambda b, j: (b, 0, 0, j))
        ki_shape = jax.ShapeDtypeStruct((nb, IDX_DIM, out_rows), F32)
        ki_spec = pl.BlockSpec((None, IDX_DIM, tm), lambda b, j: (b, 0, j))
    else:
        kv_shape, kv_spec, ki_shape, ki_spec = sds(d_att, F32), row(d_att), sds(IDX_DIM, F32), row(IDX_DIM)
    out_shape = (
        sds(d_ssm, F32),
        sds(d_att, BF16),
        kv_shape,
        kv_shape,
        sds(d_qi, BF16),
        ki_shape,
        sds(LANES, F32),
        sds(d_att, BF16),
        sds(d_att, BF16),
        sds(LANES, BF16),
    )
    out_specs = (row(d_ssm), row(d_att), kv_spec, kv_spec, row(d_qi), ki_spec, row(LANES),
                 row(d_att), row(d_att), row(LANES))
    in_specs = [row(D), _const_spec(g.shape)] + [_const_spec(w.shape) for w in ws] + [
        _const_spec(qn.shape), _const_spec(kn.shape), _const_spec(hm.shape), tab, tab]
    return pl.pallas_call(
        functools.partial(_inproj_body, transposed=transposed), grid=(nb, rows // tm),
        in_specs=in_specs, out_specs=out_specs, out_shape=out_shape,
        compiler_params=_params(("arbitrary", "arbitrary")), name="inproj_t" if transposed else "inproj",
    )(x3, g, *ws, qn, kn, hm, cos_t, sin_t)


def _s5_prep_body(lr_ref, li_ref, ldt_ref, btr_ref, bti_ref, lbr_ref, lbi_ref, bbr_ref, bbi_ref):
    lr = lr_ref[...]
    li = li_ref[...]
    dt = jnp.exp(ldt_ref[...])
    mag = jnp.exp(lr * dt)
    lbr = mag * jnp.cos(li * dt)
    lbi = mag * jnp.sin(li * dt)
    lbr_ref[...] = lbr
    lbi_ref[...] = lbi
    den = lr * lr + li * li
    cr = ((lbr - 1.0) * lr + lbi * li) / den
    ci = (lbi * lr - (lbr - 1.0) * li) / den
    btr = btr_ref[...]
    bti = bti_ref[...]
    bbr_ref[...] = cr * btr - ci * bti
    bbi_ref[...] = cr * bti + ci * btr


def _s5_prep(lam_re, lam_im, log_dt, b_re, b_im):
    G, P = lam_re.shape
    C = b_re.shape[-1]
    btr = jnp.swapaxes(b_re, 1, 2)
    bti = jnp.swapaxes(b_im, 1, 2)
    sds = jax.ShapeDtypeStruct
    return pl.pallas_call(
        _s5_prep_body,
        out_shape=(sds((G, 1, P), F32), sds((G, 1, P), F32), sds((G, C, P), F32), sds((G, C, P), F32)),
        name="s5_prep",
    )(lam_re.reshape(G, 1, P), lam_im.reshape(G, 1, P), log_dt.reshape(G, 1, 1), btr, bti)


def _s5_body(u_ref, h0r_ref, h0i_ref, lbr_ref, lbi_ref, wb_ref, wc_ref, d_ref, wglu_ref,
             y_ref, hr_ref, hi_ref, bu_ref, st_ref, *, nb, steps, precise):
    n_slab = wb_ref.shape[0]
    cin = wb_ref.shape[1]
    sw = wb_ref.shape[2] // 2
    n_state = n_slab * sw

    @pl.when(pl.program_id(0) == 0)
    def _():
        st_ref[0] = h0r_ref[...]
        st_ref[1] = h0i_ref[...]

    u = u_ref[...]
    for i in range(n_slab):
        ui = u[:, cin * i:cin * (i + 1)]
        if precise:
            bu = jnp.dot(ui, wb_ref[i], preferred_element_type=F32, precision=lax.Precision.HIGHEST)
        else:
            bu = _dot(ui.astype(BF16), wb_ref[i].astype(BF16))
        bu_ref[:, sw * i:sw * (i + 1)] = bu[:, :sw]
        bu_ref[:, n_state + sw * i:n_state + sw * (i + 1)] = bu[:, sw:]

    for i in range(n_slab):
        re_cols = slice(sw * i, sw * (i + 1))
        im_cols = slice(n_state + sw * i, n_state + sw * (i + 1))
        lr = jnp.broadcast_to(lbr_ref[:, re_cols], (nb, sw))
        li = jnp.broadcast_to(lbi_ref[:, re_cols], (nb, sw))

        def step(t, carry, re_cols=re_cols, im_cols=im_cols, lr=lr, li=li):
            hr, hi = carry
            rows = pl.ds(pl.multiple_of(t * nb, nb), nb)
            nhr = lr * hr - li * hi + bu_ref[rows, re_cols]
            nhi = lr * hi + li * hr + bu_ref[rows, im_cols]
            bu_ref[rows, re_cols] = nhr
            bu_ref[rows, im_cols] = nhi
            return nhr, nhi

        hr, hi = lax.fori_loop(0, steps, step, (st_ref[0, :, re_cols], st_ref[1, :, re_cols]))
        st_ref[0, :, re_cols] = hr
        st_ref[1, :, re_cols] = hi

    ys = []
    for i in range(n_slab):
        h_re = bu_ref[:, sw * i:sw * (i + 1)].astype(BF16)
        h_im = bu_ref[:, n_state + sw * i:n_state + sw * (i + 1)].astype(BF16)
        ys.append(_dot(h_re, wc_ref[i, 0]) + _dot(h_im, wc_ref[i, 1]))
    y = jnp.concatenate(ys, axis=1) + d_ref[...] * u
    y = 0.5 * y * (1.0 + jnp.tanh(math.sqrt(2.0 / math.pi) * (y + 0.044715 * (y * y * y))))
    y = y * jax.nn.sigmoid(_dot(y.astype(BF16), wglu_ref[...]))
    y_ref[...] = y.astype(BF16)
    hr_ref[...] = st_ref[0]
    hi_ref[...] = st_ref[1]


def _s5(u_tm, h0_re, h0_im, lbr_flat, lbi_flat, wb, wc, d_skip, wglu, nb, steps, precise):
    rows_total, d_ssm = u_tm.shape
    n_state = h0_re.shape[1]
    rows = nb * steps
    sds = jax.ShapeDtypeStruct
    body = functools.partial(_s5_body, nb=nb, steps=steps, precise=precise)
    return pl.pallas_call(
        body, grid=(rows_total // rows,),
        in_specs=[pl.BlockSpec((rows, d_ssm), lambda c: (c, 0)), _const_spec(h0_re.shape), _const_spec(h0_im.shape),
                  _const_spec(lbr_flat.shape), _const_spec(lbi_flat.shape), _const_spec(wb.shape),
                  _const_spec(wc.shape), _const_spec(d_skip.shape), _const_spec(wglu.shape)],
        out_specs=(pl.BlockSpec((rows, d_ssm), lambda c: (c, 0)), _const_spec(h0_re.shape), _const_spec(h0_im.shape)),
        out_shape=(sds((rows_total, d_ssm), BF16), sds(h0_re.shape, F32), sds(h0_im.shape, F32)),
        scratch_shapes=[pltpu.VMEM((rows, 2 * n_state), F32), pltpu.VMEM((2, nb, n_state), F32)],
        compiler_params=_params(("arbitrary",)), name="s5_precise" if precise else "s5",
    )(u_tm, h0_re, h0_im, lbr_flat, lbi_flat, wb, wc, d_skip, wglu)


def _topk_bias(s_ref, n, kk):
    rows = s_ref.shape[0]
    cols = slice(0, n)

    def count(pred):
        return jnp.sum(jnp.where(pred(s_ref[:, cols]), 1.0, 0.0), axis=1, keepdims=True)

    def largest_below(hi):
        s = s_ref[:, cols]
        return jnp.max(jnp.where(s < hi, s, -jnp.inf), axis=1, keepdims=True)

    s0 = s_ref[:, cols]
    smax = jnp.max(s0, axis=1, keepdims=True)
    smin = jnp.min(jnp.where(s0 > -jnp.inf, s0, jnp.inf), axis=1, keepdims=True)

    def bisect(_, c):
        lo, hi = c
        mid = lo + (hi - lo) * 0.5
        ge = count(lambda s: s >= mid) >= kk
        return jnp.where(ge, mid, lo), jnp.where(ge, hi, mid)

    _, hi = lax.fori_loop(0, BISECT_ITERS, bisect, (smin, smax + (smax - smin) + 1.0))
    thr = largest_below(hi)
    cnt = count(lambda s: s >= thr)

    def short(c):
        return jnp.min(c[2] - kk) < 0.0

    def lower(c):
        hi, thr, cnt = c
        hi = jnp.where(cnt < kk, thr, hi)
        thr = largest_below(hi)
        return hi, thr, count(lambda s: s >= thr)

    _, thr, _ = lax.while_loop(short, lower, (hi, thr, cnt))
    need = kk - count(lambda s: s > thr)
    blk = 2 * LANES
    tri = (lax.broadcasted_iota(I32, (blk, blk), 0) <= lax.broadcasted_iota(I32, (blk, blk), 1)).astype(BF16)
    seen = jnp.zeros((rows, 1), F32)
    for c0 in range(0, n, blk):
        w = min(blk, n - c0)
        s = s_ref[:, c0:c0 + w]
        tie = s == thr
        upto = seen + _dot(jnp.where(tie, 1.0, 0.0).astype(BF16), tri[:w, :w])
        seen = upto[:, w - 1:w]
        s_ref[:, c0:c0 + w] = jnp.where((s > thr) | (tie & (upto <= need)), 0.0, NEG_BIG)


def _dsa_prompt_block(q0, rows, n, qim_ref, qm_ref, w_ref, kip_ref, kp_ref, vp_ref, s_ref, acc_ref, topk):
    tq = rows.stop - rows.start
    s_ref = s_ref.at[0:tq]
    cols = slice(0, n)
    lane = lax.broadcasted_iota(I32, (1, LANES), 1)
    s_ref[:, cols] = jnp.zeros((tq, n), F32)

    def score_head(h, carry):
        d = jnp.maximum(_dot_nt(qim_ref[h, rows, :], kip_ref[cols, :]), 0.0)
        s_ref[:, cols] += w_ref[h, rows, :] * d
        return carry

    lax.fori_loop(0, N_IDX_HEADS, score_head, 0)
    qpos = q0 + lax.broadcasted_iota(I32, (tq, 1), 0)
    kpos = lax.broadcasted_iota(I32, (1, n), 1)
    s_ref[:, cols] = jnp.where(kpos <= qpos, s_ref[:, cols], -jnp.inf)
    _topk_bias(s_ref, n, jnp.minimum(qpos + 1, topk).astype(F32))

    def head_pair(sp, carry):
        outs = []
        for hh in range(2):
            lg = _dot_nt(qm_ref[2 * sp + hh, rows, :], kp_ref[sp, cols, :]) + s_ref[:, cols]
            p = jnp.exp(lg - jnp.max(lg, axis=1, keepdims=True))
            pv = _dot(p.astype(BF16), vp_ref[sp, cols, :])
            outs.append(pv[:, :LANES] / pv[:, LANES:LANES + 1])
        acc_ref[sp, rows, :] = jnp.where(lane < HEAD_DIM, outs[0], outs[1])
        return carry

    lax.fori_loop(0, acc_ref.shape[0], head_pair, 0)


def _dsa_prompt_body(qi_ref, kw_ref, q_ref, kib_ref, kb_ref, vb_ref, o_ref,
                     kip_ref, kp_ref, vp_ref, qim_ref, qm_ref, w_ref, s_ref, acc_ref, *, seq, topk):
    tq = q_ref.shape[0]
    n_slab, lp, _ = kp_ref.shape
    j = pl.program_id(1)
    lane = lax.broadcasted_iota(I32, (1, LANES), 1)

    @pl.when(j == 0)
    def _():
        kip_ref[0:seq] = kib_ref[...]
        if lp > seq:
            kip_ref[seq:] = jnp.zeros((lp - seq, LANES), BF16)
        for sp in range(n_slab):
            sl = slice(LANES * sp, LANES * (sp + 1))
            kp_ref[sp, 0:seq] = kb_ref[:, sl]
            vp_ref[sp, 0:seq, 0:LANES] = vb_ref[:, sl]
            vp_ref[sp, :, LANES:] = jnp.where(lane == 0, 1.0, 0.0).astype(BF16) + jnp.zeros((lp, LANES), BF16)
            if lp > seq:
                kp_ref[sp, seq:] = jnp.zeros((lp - seq, LANES), BF16)
                vp_ref[sp, seq:, 0:LANES] = jnp.zeros((lp - seq, LANES), BF16)

    kw = kw_ref[...]
    for h in range(N_IDX_HEADS):
        head_lanes = (lane // IDX_DIM) == (h % 2)
        sl = slice(LANES * (h // 2), LANES * (h // 2 + 1))
        qim_ref[h] = jnp.where(head_lanes, qi_ref[:, sl], jnp.zeros((), BF16))
        qm_ref[h] = jnp.where(head_lanes, q_ref[:, sl], jnp.zeros((), BF16))
        w_ref[h] = kw[:, IDX_DIM + h:IDX_DIM + h + 1]

    split = tq // 2 // (2 * SUBLANES) * (2 * SUBLANES)
    for jj in range(seq // tq):
        @pl.when(j == jj)
        def _(jj=jj):
            for rows in (slice(0, split), slice(split, tq)):
                n = min(-(-(jj * tq + rows.stop) // LANES) * LANES, lp)
                _dsa_prompt_block(jj * tq + rows.start, rows, n, qim_ref, qm_ref, w_ref, kip_ref, kp_ref, vp_ref,
                                  s_ref, acc_ref, topk)

    for sp in range(n_slab):
        o_ref[:, LANES * sp:LANES * (sp + 1)] = acc_ref[sp].astype(BF16)


def _dsa_prompt(qi, kw, q, kib, kb, vb, batch, seq, tq, topk):
    T, d_att = q.shape
    nq = seq // tq
    lp = -(-seq // LANES) * LANES
    n_slab = d_att // LANES
    qrow = lambda w: pl.BlockSpec((tq, w), lambda b, j: (b * nq + j, 0))
    full = lambda w: pl.BlockSpec((None, seq, w), lambda b, j: (b, 0, 0))
    body = functools.partial(_dsa_prompt_body, seq=seq, topk=topk)
    return pl.pallas_call(
        body, grid=(batch, nq),
        in_specs=[qrow(qi.shape[1]), qrow(LANES), qrow(d_att), full(LANES), full(d_att), full(d_att)],
        out_specs=qrow(d_att),
        out_shape=jax.ShapeDtypeStruct((T, d_att), BF16),
        scratch_shapes=[pltpu.VMEM((lp, LANES), BF16), pltpu.VMEM((n_slab, lp, LANES), BF16),
                        pltpu.VMEM((n_slab, lp, 2 * LANES), BF16),
                        pltpu.VMEM((N_IDX_HEADS, tq, LANES), BF16), pltpu.VMEM((N_HEADS, tq, LANES), BF16),
                        pltpu.VMEM((N_IDX_HEADS, tq, 1), F32),
                        pltpu.VMEM((tq, lp), F32), pltpu.VMEM((n_slab, tq, LANES), F32)],
        compiler_params=_params(("arbitrary", "arbitrary")), name="dsa_prompt",
    )(qi, kw, q, kib.reshape(batch, seq, LANES), kb.reshape(batch, seq, d_att), vb.reshape(batch, seq, d_att))


def _dsa_sample_scores_body(pt_ref, qi_ref, w_ref, kin_ref, *refs):
    del pt_ref
    page_refs, s_ref = refs[:-1], refs[-1]
    page = page_refs[0].shape[1]
    qm = qi_ref[...]
    w = w_ref[...]
    kt = jnp.concatenate([r[...].astype(BF16) for r in page_refs], axis=1)
    s = jnp.sum(w * jnp.maximum(_dot(qm, kt), 0.0), axis=0, keepdims=True)
    kin = kin_ref[...].astype(BF16).astype(F32)
    d_new = jnp.maximum(jnp.sum(qm.astype(F32) * kin, axis=1, keepdims=True), 0.0)
    s_new = jnp.sum(w * d_new, axis=0, keepdims=True)
    lane = lax.broadcasted_iota(I32, (1, page), 1)
    s_ref[...] = jnp.concatenate([s, jnp.where(lane == 0, s_new, 0.0)], axis=1)


def _dsa_sample_scores(page_table, qi3, w3, ki_new3, kidx_t):
    nb, n_pages = page_table.shape
    idx_dim, page = kidx_t.shape[1:]
    width = (n_pages + 1) * page
    per = lambda shape: pl.BlockSpec((None,) + shape, lambda b, pt: (b, 0, 0))
    page_specs = [pl.BlockSpec((None, idx_dim, page), functools.partial(lambda b, pt, p: (pt[b, p], 0, 0), p=p))
                  for p in range(n_pages)]
    grid_spec = pltpu.PrefetchScalarGridSpec(
        num_scalar_prefetch=1, grid=(nb,),
        in_specs=[per(qi3.shape[1:]), per(w3.shape[1:]), per(ki_new3.shape[1:])] + page_specs,
        out_specs=per((1, width)))
    return pl.pallas_call(
        _dsa_sample_scores_body, grid_spec=grid_spec,
        out_shape=jax.ShapeDtypeStruct((nb, 1, width), F32),
        compiler_params=_params(("arbitrary",)), name="dsa_sample_scores",
    )(page_table, qi3, w3, ki_new3, *([kidx_t] * n_pages))


def _dsa_sample_select_body(s_ref, b_ref, *, n_keys, topk):
    rows, width = s_ref.shape
    pos = lax.broadcasted_iota(I32, (1, width), 1)
    b_ref[...] = jnp.where(pos < n_keys, s_ref[...], -jnp.inf)
    _topk_bias(b_ref, width, jnp.full((rows, 1), float(min(topk, n_keys)), F32))


def _dsa_sample_select(scores, n_keys, topk):
    rows, width = scores.shape
    body = functools.partial(_dsa_sample_select_body, n_keys=n_keys, topk=topk)
    return pl.pallas_call(
        body, out_shape=jax.ShapeDtypeStruct((rows, width), F32),
        compiler_params=pltpu.CompilerParams(vmem_limit_bytes=VMEM_LIMIT), name="dsa_sample_select",
    )(scores)


def _dsa_sample_attend_body(pt_ref, q_ref, kn_ref, vn_ref, b_ref, *refs):
    del pt_ref
    o_ref, lg_ref = refs[-2], refs[-1]
    n_pages = (len(refs) - 2) // 2
    k_refs, v_refs = refs[:n_pages], refs[n_pages:2 * n_pages]
    n_heads, hd, page = k_refs[0].shape
    lane = lax.broadcasted_iota(I32, (1, page), 1)
    bias = b_ref[...]
    q, kn, vn = q_ref[...], kn_ref[...], vn_ref[...]
    for h in range(n_heads):
        col = slice(h, h + 1)
        qc = jnp.broadcast_to(q[:, col], (hd, page))
        for p in range(n_pages):
            lg_ref[p:p + 1, :] = jnp.sum(qc * k_refs[p][h], axis=0, keepdims=True)
        lg_new = jnp.sum(q[:, col] * kn[:, col], axis=0, keepdims=True)
        lg_ref[n_pages:n_pages + 1, :] = jnp.where(lane == 0, lg_new, 0.0)
        lg = lg_ref[...] + bias
        m = jnp.max(jnp.max(lg, axis=1, keepdims=True), axis=0, keepdims=True)
        pr = jnp.exp(lg - m)
        den = jnp.sum(jnp.sum(pr, axis=1, keepdims=True), axis=0, keepdims=True)
        acc = jnp.zeros((hd, page), F32)
        for p in range(n_pages):
            acc = acc + pr[p:p + 1, :] * v_refs[p][h]
        out = jnp.sum(acc, axis=1, keepdims=True) + pr[n_pages:n_pages + 1, 0:1] * vn[:, col]
        o_ref[:, col] = out / den


def _dsa_sample_attend(page_table, q_col, kn_col, vn_col, bias3, k_t, v_t):
    nb, n_pages = page_table.shape
    n_heads, hd, page = k_t.shape[1:]
    per = lambda shape: pl.BlockSpec((None,) + shape, lambda b, pt: (b, 0, 0))
    page_specs = [pl.BlockSpec((None, n_heads, hd, page), functools.partial(lambda b, pt, p: (pt[b, p], 0, 0, 0), p=p))
                  for p in range(n_pages)]
    grid_spec = pltpu.PrefetchScalarGridSpec(
        num_scalar_prefetch=1, grid=(nb,),
        in_specs=[per(q_col.shape[1:]), per(kn_col.shape[1:]), per(vn_col.shape[1:]), per(bias3.shape[1:])] + page_specs * 2,
        out_specs=per(q_col.shape[1:]),
        scratch_shapes=[pltpu.VMEM((n_pages + 1, page), F32)])
    return pl.pallas_call(
        _dsa_sample_attend_body, grid_spec=grid_spec,
        out_shape=jax.ShapeDtypeStruct(q_col.shape, F32),
        compiler_params=_params(("arbitrary",)), name="dsa_sample_attend",
    )(page_table, q_col, kn_col, vn_col, bias3, *([k_t] * n_pages), *([v_t] * n_pages))


def _swiglu(h, wg_ref, wu_ref, wd_ref, n_chunks):
    fc = wg_ref.shape[-1] // n_chunks
    y = None
    for c in range(n_chunks):
        cols = slice(fc * c, fc * (c + 1))
        gate = _dot(h, wg_ref[:, cols])
        act = (gate * jax.nn.sigmoid(gate) * _dot(h, wu_ref[:, cols])).astype(BF16)
        part = _dot(act, wd_ref[cols, :])
        y = part if y is None else y + part
    return y


def _outproj_ffn_body(x_ref, ys_ref, ya_ref, wos_ref, woa_ref, g_ref, wg_ref, wu_ref, wd_ref, o_ref):
    x = x_ref[...] + _dot(ys_ref[...], wos_ref[...]) + _dot(ya_ref[...], woa_ref[...])
    h = _rms(x, g_ref[...]).astype(BF16)
    o_ref[...] = x + _swiglu(h, wg_ref, wu_ref, wd_ref, 2)


def _outproj_ffn(x3, ys, ya, wos, woa, g, wg, wu, wd, seq, tm):
    nb, _, D = x3.shape
    nblk = seq // tm
    row = lambda w: pl.BlockSpec((tm, w), lambda b, j: (b * nblk + j, 0))
    single = lambda a: pl.BlockSpec(a.shape, lambda b, j: (0,) * a.ndim, pipeline_mode=pl.Buffered(1))
    return pl.pallas_call(
        _outproj_ffn_body, grid=(nb, nblk),
        in_specs=[pl.BlockSpec((None, tm, D), lambda b, j: (b, j, 0)), row(ys.shape[1]), row(ya.shape[1]),
                  single(wos), single(woa), single(g), single(wg), single(wu), single(wd)],
        out_specs=row(D), out_shape=jax.ShapeDtypeStruct((nb * seq, D), F32),
        compiler_params=_params(("arbitrary", "arbitrary")), name="outproj_ffn",
    )(x3, ys, ya, wos, woa, g, wg, wu, wd)


def _pool_mix(h, window_sum, divisor, pw_ref, scale):
    gd = h.shape[1] // len(POOL_WINDOWS)
    cols = []
    for g, w in enumerate(POOL_WINDOWS):
        pooled = window_sum(g, w) / divisor(w) - h[:, gd * g:gd * (g + 1)]
        cols.append(_dot(pooled.astype(BF16), pw_ref[g]))
    return jnp.concatenate(cols, axis=1) * scale


def _pool_prompt_body(x_ref, g_ref, pw_ref, sc_ref, o_ref, hist_ref, ext_ref):
    tm, D = x_ref.shape
    gd = D // len(POOL_WINDOWS)
    halo = POOL_HIST + 1
    j = pl.program_id(1)

    @pl.when(j == 0)
    def _():
        ext_ref[0:halo] = jnp.zeros((halo, D), F32)

    x = x_ref[...]
    h = _rms(x, g_ref[...])
    ext_ref[halo:] = h
    pos = (j * tm + lax.broadcasted_iota(I32, (tm, 1), 0)).astype(F32)

    def window_sum(g, w):
        acc = h[:, gd * g:gd * (g + 1)]
        for k in range(1, w):
            acc = acc + ext_ref[halo - k:halo - k + tm, gd * g:gd * (g + 1)]
        return acc

    mixed = _pool_mix(h, window_sum, lambda w: jnp.minimum(float(w), pos + 1.0), pw_ref, sc_ref[...])
    o_ref[...] = x + mixed
    ext_ref[0:halo] = ext_ref[tm:tm + halo]

    @pl.when(j == pl.num_programs(1) - 1)
    def _():
        hist_ref[...] = h[tm - POOL_HIST:, :]


def _pool_prompt(x2d, g, pw, scale, batch, seq, tm):
    T, D = x2d.shape
    nblk = seq // tm
    row = pl.BlockSpec((tm, D), lambda b, j: (b * nblk + j, 0))
    return pl.pallas_call(
        _pool_prompt_body, grid=(batch, nblk),
        in_specs=[row, _const_spec(g.shape), _const_spec(pw.shape), _const_spec(scale.shape)],
        out_specs=(row, pl.BlockSpec((None, POOL_HIST, D), lambda b, j: (b, 0, 0))),
        out_shape=(jax.ShapeDtypeStruct((T, D), F32), jax.ShapeDtypeStruct((batch, POOL_HIST, D), F32)),
        scratch_shapes=[pltpu.VMEM((tm + POOL_HIST + 1, D), F32)],
        compiler_params=_params(("arbitrary", "arbitrary")), name="pool_prompt",
    )(x2d, g, pw, scale)


def _pool_sample_body(x_ref, hist_ref, g_ref, pw_ref, sc_ref, o_ref, nh_ref):
    D = x_ref.shape[1]
    gd = D // len(POOL_WINDOWS)
    x = x_ref[...]
    h = _rms(x, g_ref[...])

    def window_sum(g, w):
        acc = h[:, gd * g:gd * (g + 1)]
        for k in range(1, w):
            acc = acc + hist_ref[POOL_HIST - k, :, gd * g:gd * (g + 1)]
        return acc

    mixed = _pool_mix(h, window_sum, float, pw_ref, sc_ref[...])
    o_ref[...] = x + mixed
    for i in range(POOL_HIST - 1):
        nh_ref[i] = hist_ref[i + 1]
    nh_ref[POOL_HIST - 1] = h


def _pool_sample(x2d, hist_t, g, pw, scale):
    sds = jax.ShapeDtypeStruct
    return pl.pallas_call(
        _pool_sample_body, out_shape=(sds(x2d.shape, F32), sds(hist_t.shape, F32)),
        compiler_params=pltpu.CompilerParams(vmem_limit_bytes=VMEM_LIMIT), name="pool_sample",
    )(x2d, hist_t, g, pw, scale)


def _store_row_tiles(ref, x):
    for s in range(ref.shape[1]):
        ref[:, s, :] = x[:, LANES * s:LANES * (s + 1)]


def _load_row_tiles(ref):
    return jnp.concatenate([ref[:, s, :] for s in range(ref.shape[1])], axis=1)


def _router_body(x_ref, g_ref, wr_ref, h_ref, sel_ref, *, n_experts):
    h = _rms(x_ref[...], g_ref[...])
    _store_row_tiles(h_ref, h)
    w = wr_ref[...]
    h_hi, w_hi = h.astype(BF16), w.astype(BF16)
    h_lo, w_lo = (h - h_hi.astype(F32)).astype(BF16), (w - w_hi.astype(F32)).astype(BF16)
    logits = _dot(h_hi, w_hi) + (_dot(h_hi, w_lo) + _dot(h_lo, w_hi))
    lane = lax.broadcasted_iota(I32, logits.shape, 1).astype(F32)
    logits = jnp.where(lane < n_experts, logits, -jnp.inf)
    v1 = jnp.max(logits, axis=1, keepdims=True)
    i1 = jnp.min(jnp.where(logits == v1, lane, float(LANES)), axis=1, keepdims=True)
    rest = jnp.where(lane == i1, -jnp.inf, logits)
    v2 = jnp.max(rest, axis=1, keepdims=True)
    i2 = jnp.min(jnp.where(rest == v2, lane, float(LANES)), axis=1, keepdims=True)
    e2 = jnp.exp(v2 - v1)
    den = 1.0 + e2
    sel_ref[...] = jnp.where(lane == 0.0, i1, jnp.where(lane == 1.0, i2, jnp.where(lane == 2.0, 1.0 / den, e2 / den)))


def _router(x2d, g, wr_pad, n_experts, tm):
    T, D = x2d.shape
    row = lambda w: pl.BlockSpec((tm, w), lambda i: (i, 0))
    body = functools.partial(_router_body, n_experts=n_experts)
    return pl.pallas_call(
        body, grid=(T // tm,),
        in_specs=[row(D), _const_spec(g.shape), _const_spec(wr_pad.shape)],
        out_specs=(pl.BlockSpec((tm, D // LANES, LANES), lambda i: (i, 0, 0)), row(LANES)),
        out_shape=(jax.ShapeDtypeStruct((T, D // LANES, LANES), F32), jax.ShapeDtypeStruct((T, LANES), F32)),
        compiler_params=_params(("arbitrary",)), name="router",
    )(x2d, g, wr_pad)


def _row_gather(idx_ref, base, src_hbm, dst, sem):
    assert dst.shape[0] % DMA_UNROLL == 0

    def issue(g, carry):
        for k in range(DMA_UNROLL):
            r = g * DMA_UNROLL + k
            pltpu.make_async_copy(src_hbm.at[idx_ref[base + r]], dst.at[r], sem).start(priority=1)
        return carry

    lax.fori_loop(0, dst.shape[0] // DMA_UNROLL, issue, 0)


def _row_gather_wait(src_hbm, dst, sem):
    pltpu.make_async_copy(src_hbm.at[pl.ds(0, dst.shape[0])], dst, sem).wait()


def _moe_dispatch_body(p1_ref, p2_ref, ends_ref, h_ref, x_hbm, zero_ref, stage, sems, *, tm):
    i = pl.program_id(0)
    last = pl.num_programs(0) - 1
    gt = zero_ref.shape[0]
    n_groups = ends_ref.shape[0]
    zsem = sems.at[2]
    slot = i % 2

    def fill_tile(e, act):
        if e < n_groups:
            start = ends_ref[e] - gt
            exists = ends_ref[e] > (ends_ref[e - 1] if e else 0)
        else:
            start = ends_ref[n_groups - 1] + (e - n_groups) * gt
            exists = start < x_hbm.shape[0]

        @pl.when(exists)
        def _():
            act(pltpu.make_async_copy(zero_ref, x_hbm.at[pl.ds(pl.multiple_of(start, gt), gt)], zsem))

    @pl.when(i == 0)
    def _():
        zero_ref[...] = jnp.zeros(zero_ref.shape, F32)
        for e in range(2 * n_groups):
            fill_tile(e, lambda cp: cp.start())
        for e in range(2 * n_groups):
            fill_tile(e, lambda cp: cp.wait())

    def wait_rows(s):
        for _ in range(2):
            pltpu.make_async_copy(stage.at[s], x_hbm.at[pl.ds(0, tm)], sems.at[s]).wait()

    @pl.when(i >= 2)
    def _():
        wait_rows(slot)

    stage[slot] = h_ref[...]

    def issue(r, carry):
        t = i * tm + r
        pltpu.make_async_copy(stage.at[slot, r], x_hbm.at[p1_ref[t]], sems.at[slot]).start(priority=0)
        pltpu.make_async_copy(stage.at[slot, r], x_hbm.at[p2_ref[t]], sems.at[slot]).start(priority=1)
        return carry

    lax.fori_loop(0, tm, issue, 0, unroll=DMA_UNROLL)

    @pl.when(i == last)
    def _():
        wait_rows(slot)

    @pl.when((i == last) & (i >= 1))
    def _():
        wait_rows(1 - slot)


def _moe_dispatch(pos1, pos2, ends, h_tiles, n_rows, tm, group_tile):
    T, S, _ = h_tiles.shape
    grid_spec = pltpu.PrefetchScalarGridSpec(
        num_scalar_prefetch=3, grid=(T // tm,),
        in_specs=[pl.BlockSpec((tm, S, LANES), lambda i, p1, p2, en: (i, 0, 0))],
        out_specs=pl.BlockSpec(memory_space=pl.ANY),
        scratch_shapes=[pltpu.VMEM((group_tile, S, LANES), F32), pltpu.VMEM((2, tm, S, LANES), F32),
                        pltpu.SemaphoreType.DMA((3,))])
    return pl.pallas_call(
        functools.partial(_moe_dispatch_body, tm=tm), grid_spec=grid_spec,
        out_shape=jax.ShapeDtypeStruct((n_rows, S, LANES), F32),
        compiler_params=pltpu.CompilerParams(dimension_semantics=("arbitrary",), vmem_limit_bytes=VMEM_LIMIT,
                                             disable_bounds_checks=True),
        name="moe_dispatch",
    )(pos1, pos2, ends, h_tiles)


def _moe_group_body(te_ref, nu_ref, x_ref, wg_ref, wu_ref, wd_ref, y_ref):
    del te_ref
    i = pl.program_id(0)

    @pl.when(i < nu_ref[0])
    def _():
        _store_row_tiles(y_ref, _swiglu(_load_row_tiles(x_ref).astype(BF16), wg_ref, wu_ref, wd_ref, 2))

    @pl.when(i >= nu_ref[0])
    def _():
        y_ref[...] = jnp.zeros(y_ref.shape, F32)


def _moe_group(tile_expert, n_used, x_tiles, wg, wu, wd, tm):
    P, S, _ = x_tiles.shape
    _, D, F = wg.shape
    wspec = lambda a, b: pl.BlockSpec((None, a, b), lambda i, te, nu: (te[i], 0, 0))
    grid_spec = pltpu.PrefetchScalarGridSpec(
        num_scalar_prefetch=2, grid=(P // tm,),
        in_specs=[pl.BlockSpec((tm, S, LANES), lambda i, te, nu: (jnp.minimum(i, nu[0] - 1), 0, 0)),
                  wspec(D, F), wspec(D, F), wspec(F, D)],
        out_specs=pl.BlockSpec((tm, S, LANES), lambda i, te, nu: (i, 0, 0)))
    return pl.pallas_call(
        _moe_group_body, grid_spec=grid_spec, out_shape=jax.ShapeDtypeStruct((P, S, LANES), F32),
        compiler_params=_params(("arbitrary",)), name="moe_group",
    )(tile_expert, n_used, x_tiles, wg, wu, wd)


def _moe_combine_body(p1_ref, p2_ref, dst_ref, x_ref, sel_ref, y_hbm, ya_hbm, yb_hbm, buf, obuf, sem, osem):
    i = pl.program_id(0)
    last = pl.num_programs(0) - 1
    tm = x_ref.shape[0]
    n_chunks = tm // OUT_CHUNK
    rows_a = ya_hbm.shape[0]
    slot = i % 2

    def out_copies(step, s, act):
        for c in range(n_chunks):
            dst = dst_ref[step * n_chunks + c]
            src = obuf.at[s, pl.ds(c * OUT_CHUNK, OUT_CHUNK)]

            @pl.when((dst >= 0) & (dst < rows_a))
            def _():
                act(pltpu.make_async_copy(src, ya_hbm.at[pl.ds(pl.multiple_of(dst, OUT_CHUNK), OUT_CHUNK)], osem.at[s]))

            @pl.when(dst >= rows_a)
            def _():
                off = pl.multiple_of(dst - rows_a, OUT_CHUNK)
                act(pltpu.make_async_copy(src, yb_hbm.at[pl.ds(off, OUT_CHUNK)], osem.at[s]))

    def gather(tile, s):
        _row_gather(p1_ref, tile * tm, y_hbm, buf.at[s, 0], sem.at[s, 0])
        _row_gather(p2_ref, tile * tm, y_hbm, buf.at[s, 1], sem.at[s, 1])

    @pl.when(i == 0)
    def _():
        gather(0, 0)

    @pl.when(i + 1 < pl.num_programs(0))
    def _():
        gather(i + 1, 1 - slot)

    _row_gather_wait(y_hbm, buf.at[slot, 0], sem.at[slot, 0])
    _row_gather_wait(y_hbm, buf.at[slot, 1], sem.at[slot, 1])
    @pl.when(i >= 2)
    def _():
        out_copies(i - 2, slot, lambda cp: cp.wait())

    sel = sel_ref[...]
    obuf[slot] = (x_ref[...] + sel[:, 2:3] * _load_row_tiles(buf.at[slot, 0])
                  + sel[:, 3:4] * _load_row_tiles(buf.at[slot, 1]))
    out_copies(i, slot, lambda cp: cp.start())

    @pl.when(i == last)
    def _():
        out_copies(i, slot, lambda cp: cp.wait())

    @pl.when((i == last) & (i >= 1))
    def _():
        out_copies(i - 1, 1 - slot, lambda cp: cp.wait())


def _moe_combine(pos1, pos2, chunk_dst, x2d, sel, y_tiles, tm, rows_a, rows_b):
    T, D = x2d.shape
    S = y_tiles.shape[1]
    assert tm % OUT_CHUNK == 0 and rows_a % OUT_CHUNK == 0 and rows_b % OUT_CHUNK == 0
    row = lambda w: pl.BlockSpec((tm, w), lambda i, p1, p2, cd: (i, 0))
    hbm = pl.BlockSpec(memory_space=pl.ANY)
    grid_spec = pltpu.PrefetchScalarGridSpec(
        num_scalar_prefetch=3, grid=(T // tm,),
        in_specs=[row(D), row(LANES), hbm], out_specs=(hbm, hbm),
        scratch_shapes=[pltpu.VMEM((2, 2, tm, S, LANES), F32), pltpu.VMEM((2, tm, D), F32),
                        pltpu.SemaphoreType.DMA((2, 2)), pltpu.SemaphoreType.DMA((2,))])
    return pl.pallas_call(
        _moe_combine_body, grid_spec=grid_spec,
        out_shape=(jax.ShapeDtypeStruct((rows_a, D), F32), jax.ShapeDtypeStruct((rows_b, D), F32)),
        compiler_params=pltpu.CompilerParams(dimension_semantics=("arbitrary",), vmem_limit_bytes=VMEM_LIMIT,
                                             disable_bounds_checks=True),
        name="moe_combine",
    )(pos1, pos2, chunk_dst, x2d, sel, y_tiles)


def _moe_plan(sel, n_experts, tm):
    T = sel.shape[0]
    experts = jnp.concatenate([sel[:, 0], sel[:, 1]]).astype(I32)
    onehot = (experts[:, None] == jnp.arange(n_experts, dtype=I32)[None, :]).astype(I32)
    rank = jnp.cumsum(onehot, axis=0) - onehot
    counts = jnp.sum(onehot, axis=0)
    padded = (counts + tm - 1) // tm * tm
    ends = jnp.cumsum(padded)
    pos = jnp.sum(onehot * ((ends - padded)[None, :] + rank), axis=1)
    n_rows = (2 * T + n_experts * (tm - 1)) // tm * tm
    tile_start = jnp.arange(n_rows // tm, dtype=I32) * tm
    tile_expert = jnp.minimum(jnp.sum((tile_start[:, None] >= ends[None, :]).astype(I32), axis=1), n_experts - 1)
    n_used = (ends[-1] // tm).astype(I32).reshape(1)
    return tile_expert, n_used, ends.astype(I32), n_rows, pos[:T], pos[T:]


def _rope_tables(pos):
    half = HEAD_DIM // 2
    inv = ROPE_THETA ** (-jnp.arange(half, dtype=F32) / half)
    ang = pos[:, None] * inv[None, :]
    cos, sin = jnp.cos(ang), jnp.sin(ang)
    reps = LANES // HEAD_DIM
    return jnp.tile(jnp.concatenate([cos, cos], axis=1), (1, reps)), jnp.tile(jnp.concatenate([-sin, sin], axis=1), (1, reps))


def kernel(x_prompt, x_sample, cache_k, cache_v, cache_kidx, page_table, state_ssm_re, state_ssm_im, state_pool, meta_tokens, norm_mix0, w_in0, q_norm, k_norm, ssm_lambda_re, ssm_lambda_im, ssm_log_dt, ssm_b_re, ssm_b_im, ssm_c_re, ssm_c_im, ssm_d, ssm_w_glu, w_out0, norm_ffn0, ffn_w_gate, ffn_w_up, ffn_w_down, norm_mix1, pool_w, pool_scale, norm_ffn1, router_w, moe_w_gate, moe_w_up, moe_w_down):
    B, S, D = x_prompt.shape
    DB, DS, _ = x_sample.shape
    assert DS == 1, "one new token per sample sequence"
    L = S + N_META
    n_pool, page = cache_k.shape[1], cache_k.shape[2]
    n_pages = page_table.shape[1]
    past = n_pages * page
    d_att = N_HEADS * HEAD_DIM
    d_qi = N_IDX_HEADS * IDX_DIM
    G, P = ssm_lambda_re.shape[1:]
    d_ssm = G * SSM_GROUP
    n_state = G * P
    E = router_w.shape[-1]
    topk_p = min(TOPK_MAX, S // 4)
    topk_s = min(TOPK_MAX, (past + DS) // 4)
    tm = _row_block(L, 1024)
    assert tm >= POOL_HIST + 1

    tm_in = LANES * (-(-L // (3 * LANES)))
    l_pad = -(-L // tm_in) * tm_in
    x_p = jnp.concatenate([jnp.broadcast_to(meta_tokens[None], (B, N_META, D)), x_prompt,
                           jnp.zeros((B, l_pad - L, D), x_prompt.dtype)], axis=1)
    x_s = x_sample.reshape(1, DB, D)

    w_in = w_in0[0]
    o1, o2, o3, o4, o5, o6 = d_ssm, d_ssm + d_att, d_ssm + 2 * d_att, d_ssm + 3 * d_att, d_ssm + 3 * d_att + d_qi, d_ssm + 3 * d_att + d_qi + IDX_DIM
    w_kw = jnp.pad(w_in[:, o5:], ((0, 0), (0, LANES - (w_in.shape[1] - o5))))
    ws = tuple(w.astype(BF16) for w in (w_in[:, :o1], w_in[:, o1:o2], w_in[:, o2:o3], w_in[:, o3:o4], w_in[:, o4:o5], w_kw))
    g_mix0 = norm_mix0[0].reshape(1, D)
    qn = jnp.tile(q_norm[0], N_HEADS).reshape(1, d_att)
    kn = jnp.tile(k_norm[0], N_HEADS).reshape(1, d_att)
    head_of = jnp.arange(d_att) // HEAD_DIM
    hm = jnp.where(head_of[:, None] == head_of[None, :], 1.0 / HEAD_DIM, 0.0).astype(BF16)
    cos_p, sin_p = _rope_tables(jnp.arange(l_pad, dtype=F32))
    cos_s, sin_s = _rope_tables(jnp.full((DB,), float(past), F32))

    flat = lambda a: a.reshape(-1, a.shape[-1])
    u_p, q_p, kt_p, vt_p, qi_p, kit_p, kw_p, kb_p, vb_p, kib_p = _inproj(
        x_p, cos_p, sin_p, tm_in, L, g_mix0, ws, qn, kn, hm, True)
    u_p, q_p, qi_p, kw_p, kb_p, vb_p, kib_p = (flat(a) for a in (u_p, q_p, qi_p, kw_p, kb_p, vb_p, kib_p))
    u_s, q_s, k_s, v_s, qi_s, ki_s, kw_s, _, _, _ = (
        flat(a) for a in _inproj(x_s, cos_s, sin_s, DB, DB, g_mix0, ws, qn, kn, hm, False))

    lbr, lbi, bbr, bbi = _s5_prep(ssm_lambda_re[0], ssm_lambda_im[0], ssm_log_dt[0], ssm_b_re[0], ssm_b_im[0])
    gps = LANES // SSM_GROUP
    n_slab = G // gps
    eye = jnp.eye(gps, dtype=F32)

    def b_slabs(bt):
        return jnp.einsum('sgcp,gh->sgchp', bt.reshape(n_slab, gps, SSM_GROUP, P), eye).reshape(n_slab, LANES, gps * P)

    def c_slabs(c):
        return jnp.einsum('sgcp,gh->sgphc', c.reshape(n_slab, gps, SSM_GROUP, P), eye).reshape(n_slab, gps * P, LANES)

    wb = jnp.concatenate([b_slabs(bbr), b_slabs(bbi)], axis=2)
    wc = jnp.stack([c_slabs(ssm_c_re[0]), -c_slabs(ssm_c_im[0])], axis=1).astype(BF16)
    lbr_f, lbi_f = lbr.reshape(1, n_state), lbi.reshape(1, n_state)
    d_skip = ssm_d[0].reshape(1, d_ssm)
    wglu = ssm_w_glu[0].astype(BF16)
    assert B % SUBLANES == 0, "the S5 scan keeps one batch row per sublane"
    steps = max(d for d in range(1, L + 1) if L % d == 0 and d * B <= 512 and (d * B) % (2 * SUBLANES) == 0)
    u_tm = u_p.reshape(B, L, d_ssm).swapaxes(0, 1).reshape(L * B, d_ssm)
    zero_h = jnp.zeros((B, n_state), F32)
    ys_tm, hre_p, him_p = _s5(u_tm, zero_h, zero_h, lbr_f, lbi_f, wb, wc, d_skip, wglu, B, steps, False)
    ys_p = ys_tm.reshape(L, B, d_ssm).swapaxes(0, 1).reshape(B * L, d_ssm)
    ys_s, hre_s, him_s = _s5(u_s, state_ssm_re[0].reshape(DB, n_state), state_ssm_im[0].reshape(DB, n_state),
                             lbr_f, lbi_f, wb, wc, d_skip, wglu, DB, 1, True)

    ya_p = _dsa_prompt(qi_p, kw_p, q_p, kib_p, kb_p, vb_p, B, L, tm, topk_p)
    assert page == LANES
    w3 = kw_s[:, IDX_DIM:IDX_DIM + N_IDX_HEADS].reshape(DB, N_IDX_HEADS, 1)
    def head_cols(a, n_heads):
        return a.astype(F32).reshape(DB, n_heads, -1).swapaxes(1, 2)

    scores = _dsa_sample_scores(page_table, qi_s.reshape(DB, N_IDX_HEADS, IDX_DIM), w3, ki_s.reshape(DB, 1, IDX_DIM),
                                jnp.transpose(cache_kidx[0], (0, 2, 1)))
    bias = _dsa_sample_select(scores.reshape(DB, (n_pages + 1) * page), past + DS, topk_s)
    ya_s = _dsa_sample_attend(page_table, head_cols(q_s, N_HEADS), head_cols(k_s, N_HEADS), head_cols(v_s, N_HEADS),
                              bias.reshape(DB, n_pages + 1, page),
                              jnp.transpose(cache_k[0], (0, 2, 3, 1)), jnp.transpose(cache_v[0], (0, 2, 3, 1)))
    ya_s = ya_s.swapaxes(1, 2).reshape(DB, d_att).astype(BF16)

    w_out = w_out0[0].astype(BF16)
    ffn = (w_out[:d_ssm], w_out[d_ssm:], norm_ffn0[0].reshape(1, D), ffn_w_gate[0].astype(BF16),
           ffn_w_up[0].astype(BF16), ffn_w_down[0].astype(BF16))
    x_p = _outproj_ffn(x_p, ys_p, ya_p, *ffn, L, tm)
    x_s = _outproj_ffn(x_s, ys_s, ya_s, *ffn, DB, DB)

    g_mix1 = norm_mix1[0].reshape(1, D)
    pw = pool_w[0].astype(BF16)
    psc = pool_scale[0].reshape(1, D)
    T_all = B * L + DB
    x_p, pool_p = _pool_prompt(x_p, g_mix1, pw, psc, B, L, tm)
    x_s, hist_t = _pool_sample(x_s, state_pool[0].swapaxes(0, 1), g_mix1, pw, psc)
    pool_s = hist_t.swapaxes(0, 1)
    x_all = jnp.concatenate([x_p, x_s], axis=0)

    g_ffn1 = norm_ffn1[0].reshape(1, D)
    wr_pad = jnp.pad(router_w[0], ((0, 0), (0, LANES - E)))
    wg, wu, wd = moe_w_gate[0].astype(BF16), moe_w_up[0].astype(BF16), moe_w_down[0].astype(BF16)
    tr = _row_block(T_all, MOE_TILE)
    h_tiles, sel = _router(x_all, g_ffn1, wr_pad, E, tr)
    tile_expert, n_used, ends, n_rows, pos1, pos2 = _moe_plan(sel, E, MOE_TILE)
    x_tiles = _moe_dispatch(pos1, pos2, ends, h_tiles, n_rows, tr, MOE_TILE)
    y_tiles = _moe_group(tile_expert, n_used, x_tiles, wg, wu, wd, MOE_TILE)
    assert N_META % OUT_CHUNK == 0 and L % OUT_CHUNK == 0 and DB % OUT_CHUNK == 0
    row0 = jnp.arange(T_all // OUT_CHUNK, dtype=I32) * OUT_CHUNK
    in_seq = row0 % L
    chunk_dst = jnp.where(row0 >= B * L, row0 - B * N_META,
                          jnp.where(in_seq < N_META, -1, (row0 // L) * S + in_seq - N_META))
    y_prompt, y_sample = _moe_combine(pos1, pos2, chunk_dst, x_all, sel, y_tiles, tr, B * S, DB)
    y_prompt = y_prompt.reshape(B, S, D)
    y_sample = y_sample.reshape(DB, 1, D)
    k_prompt = jnp.transpose(kt_p, (0, 3, 1, 2))[None]
    v_prompt = jnp.transpose(vt_p, (0, 3, 1, 2))[None]
    kidx_prompt = jnp.transpose(kit_p, (0, 2, 1))[None]
    return (y_prompt, y_sample, k_prompt, v_prompt, kidx_prompt,
            hre_p.reshape(1, B, G, P), him_p.reshape(1, B, G, P), pool_p[None],
            k_s.reshape(1, DB, 1, N_HEADS, HEAD_DIM), v_s.reshape(1, DB, 1, N_HEADS, HEAD_DIM),
            ki_s.reshape(1, DB, 1, IDX_DIM), hre_s.reshape(1, DB, G, P), him_s.reshape(1, DB, G, P), pool_s[None])
```
